```python
import jax, jax.numpy as jnp
from jax import lax
import numpy as np

D_MODEL = 1024
BATCH = 8
SEQ = 2048
DEPTH = 1

CTX_LEN = 256
GRID_W = 64
EPS = 1e-6

ATT_HEADS = 8
ATT_KV_HEADS = 2
ATT_GROUP = ATT_HEADS // ATT_KV_HEADS
ATT_HEAD_DIM = 64
WINDOW = 128
ATT_BLOCK = 128
ROPE_BASE = 10000.0
ROPE_AXIS_DIM = ATT_HEAD_DIM // 2

ML_HEADS = 4
ML_V_DIM = 128
ML_QK_DIM = 64
ML_CHUNK = 64
ML_CONV = 3

ATT_WIDTH = ATT_HEADS * ATT_HEAD_DIM
ATT_KV_WIDTH = ATT_KV_HEADS * ATT_HEAD_DIM
ML_WIDTH = ML_HEADS * ML_V_DIM
ML_QK_WIDTH = ML_HEADS * ML_QK_DIM
ML_GATES = 4 * ML_HEADS
MIX_WIDTH = ATT_WIDTH + ML_WIDTH
FFN_HIDDEN = -(-8 * D_MODEL // (3 * 256)) * 256

COLS = (ATT_WIDTH, ATT_KV_WIDTH, ATT_KV_WIDTH, ML_QK_WIDTH, ML_QK_WIDTH, ML_WIDTH, ML_WIDTH, ML_GATES)
IN_WIDTH = sum(COLS)
SPLIT_AT = tuple(sum(COLS[:i + 1]) for i in range(len(COLS) - 1))

kernel_name = 'hymba_style_mlstm_swa_sandwich_adaln_block'


def rms_norm(x, g):
    xf = x.astype(jnp.float32)
    y = xf * lax.rsqrt(jnp.mean(xf * xf, axis=-1, keepdims=True) + EPS)
    return (y * g.astype(jnp.float32)).astype(x.dtype)


def modulate(h, shift, scale):
    return h * (1.0 + scale) + shift


def axial_rope_tables(n_tokens, dtype):
    n_rows = n_tokens // GRID_W
    row = jnp.repeat(jnp.arange(n_rows, dtype=jnp.float32), GRID_W)
    col = jnp.tile(jnp.arange(GRID_W, dtype=jnp.float32), n_rows)
    half = ROPE_AXIS_DIM // 2
    inv_freq = jnp.power(ROPE_BASE, -jnp.arange(half, dtype=jnp.float32) / half)
    ang_r = row[:, None] * inv_freq
    ang_c = col[:, None] * inv_freq
    return tuple(t[:, None, :].astype(dtype) for t in (jnp.cos(ang_r), jnp.sin(ang_r), jnp.cos(ang_c), jnp.sin(ang_c)))


def axial_rope(x, cos_r, sin_r, cos_c, sin_c):
    def rot(u, cos, sin):
        u1, u2 = jnp.split(u, 2, axis=-1)
        return jnp.concatenate([u1 * cos - u2 * sin, u2 * cos + u1 * sin], axis=-1)
    xr, xc = jnp.split(x, 2, axis=-1)
    return jnp.concatenate([rot(xr, cos_r, sin_r), rot(xc, cos_c, sin_c)], axis=-1)


def window_attention(q, k, v, kc, vc, sink):
    B, S = q.shape[:2]
    nb = S // ATT_BLOCK
    scale = ATT_HEAD_DIM ** -0.5
    qb = q.reshape(B, nb, ATT_BLOCK, ATT_KV_HEADS, ATT_GROUP, ATT_HEAD_DIM)
    pad = ((0, 0), (ATT_BLOCK, ATT_BLOCK), (0, 0), (0, 0))

    def band(t):
        tp = jnp.pad(t, pad).reshape(B, nb + 2, ATT_BLOCK, ATT_KV_HEADS, ATT_HEAD_DIM)
        return jnp.concatenate([tp[:, :-2], tp[:, 1:-1], tp[:, 2:]], axis=2)

    kw, vw = band(k), band(v)
    qpos = jnp.arange(S).reshape(nb, ATT_BLOCK)
    kpos = (jnp.arange(nb)[:, None] - 1) * ATT_BLOCK + jnp.arange(3 * ATT_BLOCK)[None, :]
    valid = ((jnp.abs(qpos[:, :, None] - kpos[:, None, :]) <= WINDOW)
             & (kpos[:, None, :] >= 0) & (kpos[:, None, :] < S))
    s_loc = jnp.einsum('bnqhgd,bnkhd->bnhgqk', qb, kw).astype(jnp.float32) * scale
    s_loc = jnp.where(valid[None, :, None, None], s_loc, -jnp.inf)
    s_ctx = jnp.einsum('bnqhgd,bchd->bnhgqc', qb, kc).astype(jnp.float32) * scale
    sk = sink.astype(jnp.float32).reshape(1, 1, ATT_KV_HEADS, ATT_GROUP, 1, 1)
    m = jnp.maximum(jnp.maximum(s_loc.max(-1, keepdims=True), s_ctx.max(-1, keepdims=True)), sk)
    p_loc = jnp.exp(s_loc - m)
    p_ctx = jnp.exp(s_ctx - m)
    inv = 1.0 / (p_loc.sum(-1, keepdims=True) + p_ctx.sum(-1, keepdims=True) + jnp.exp(sk - m))
    o = (jnp.einsum('bnhgqk,bnkhd->bnqhgd', (p_loc * inv).astype(v.dtype), vw)
         + jnp.einsum('bnhgqc,bchd->bnqhgd', (p_ctx * inv).astype(v.dtype), vc))
    return o.reshape(B, S, ATT_WIDTH)


def context_self_attention(qc, kc, vc, sink):
    B, L = qc.shape[:2]
    scale = ATT_HEAD_DIM ** -0.5
    qg = qc.reshape(B, L, ATT_KV_HEADS, ATT_GROUP, ATT_HEAD_DIM)
    s = jnp.einsum('bqhgd,bkhd->bhgqk', qg, kc).astype(jnp.float32) * scale
    sk = sink.astype(jnp.float32).reshape(1, ATT_KV_HEADS, ATT_GROUP, 1, 1)
    m = jnp.maximum(s.max(-1, keepdims=True), sk)
    p = jnp.exp(s - m)
    p = p / (p.sum(-1, keepdims=True) + jnp.exp(sk - m))
    o = jnp.einsum('bhgqk,bkhd->bqhgd', p.astype(vc.dtype), vc)
    return o.reshape(B, L, ATT_WIDTH)


def centred_dwconv(t, w):
    pad = w.shape[0] // 2
    return lax.conv_general_dilated(t, w[:, None, :].astype(t.dtype), window_strides=(1,),
                                    padding=[(pad, pad)], dimension_numbers=('NWC', 'WIO', 'NWC'),
                                    feature_group_count=t.shape[-1])


def mlstm_heads(t, d):
    B, T = t.shape[:2]
    return t.reshape(B, T, ML_HEADS, d).transpose(0, 2, 1, 3).astype(jnp.float32)


def mlstm_prepare(mq, mk, mv, mg, w_conv, gate_bias):
    qk = jax.nn.silu(centred_dwconv(jnp.concatenate([mq, mk], axis=-1), w_conv))
    mq, mk = jnp.split(qk, 2, axis=-1)
    B, T = mq.shape[:2]
    g = (mg.astype(jnp.float32) + gate_bias.astype(jnp.float32)).reshape(B, T, 4, ML_HEADS).transpose(0, 3, 1, 2)
    return mlstm_heads(mq, ML_QK_DIM), mlstm_heads(mk, ML_QK_DIM), mlstm_heads(mv, ML_V_DIM), g


def mlstm_zero_state(B):
    return (jnp.zeros((B, ML_HEADS, ML_V_DIM, ML_QK_DIM), jnp.float32),
            jnp.zeros((B, ML_HEADS, ML_QK_DIM), jnp.float32),
            jnp.zeros((B, ML_HEADS), jnp.float32))


def mlstm_chunkwise(q, k, v, ig, lf, state):
    B, H, T, _ = q.shape
    nc = T // ML_CHUNK
    q = q * (ML_QK_DIM ** -0.5)

    def to_chunks(a):
        return jnp.moveaxis(a.reshape(B, H, nc, ML_CHUNK, *a.shape[3:]), 2, 0)

    xs = tuple(to_chunks(a) for a in (q, k, v, ig, lf))
    lower = jnp.tril(jnp.ones((ML_CHUNK, ML_CHUNK), dtype=bool))

    def step(carry, inp):
        C, n, m = carry
        qc, kc, vc, ic, fc = inp
        b = jnp.cumsum(fc, axis=-1)
        a_inter = b + m[..., None]
        d = jnp.where(lower, b[..., :, None] - b[..., None, :] + ic[..., None, :], -jnp.inf)
        m_t = jnp.maximum(a_inter, d.max(-1))
        w_intra = jnp.exp(d - m_t[..., None])
        w_inter = jnp.exp(a_inter - m_t)
        s = jnp.einsum('bhtd,bhsd->bhts', qc, kc) * w_intra
        num = jnp.einsum('bhts,bhsv->bhtv', s, vc) + w_inter[..., None] * jnp.einsum('bhvd,bhtd->bhtv', C, qc)
        den = s.sum(-1) + w_inter * jnp.einsum('bhd,bhtd->bht', n, qc)
        h = num / jnp.maximum(jnp.abs(den), jnp.exp(-m_t))[..., None]
        b_last = b[..., -1]
        g = b_last[..., None] - b + ic
        m_new = jnp.maximum(b_last + m, g.max(-1))
        decay = jnp.exp(b_last + m - m_new)
        wk = jnp.exp(g - m_new[..., None])
        C = decay[..., None, None] * C + jnp.einsum('bhs,bhsv,bhsd->bhvd', wk, vc, kc)
        n = decay[..., None] * n + jnp.einsum('bhs,bhsd->bhd', wk, kc)
        return (C, n, m_new), h

    state, hs = lax.scan(step, state, xs)
    return jnp.moveaxis(hs, 0, 2).reshape(B, H, T, ML_V_DIM), state


def mlstm_bidir(q, k, v, g, state_f, state_b):
    h_f, st_f = mlstm_chunkwise(q, k, v, g[..., 0], jax.nn.log_sigmoid(g[..., 1]), state_f)
    flip = lambda t: jnp.flip(t, axis=2)
    h_b, st_b = mlstm_chunkwise(flip(q), flip(k), flip(v), flip(g[..., 2]),
                                flip(jax.nn.log_sigmoid(g[..., 3])), state_b)
    return h_f + flip(h_b), st_f, st_b


def mlstm_output(h, o_pre, gain):
    B, H, T, DV = h.shape
    hf = h.transpose(0, 2, 1, 3)
    hf = hf * lax.rsqrt(jnp.mean(hf * hf, axis=-1, keepdims=True) + EPS) * gain.astype(jnp.float32).reshape(ML_HEADS, ML_V_DIM)
    return hf.reshape(B, T, ML_WIDTH).astype(o_pre.dtype) * jax.nn.sigmoid(o_pre)


def swiglu(h, w_in, w_out):
    gate, up = jnp.split(h @ w_in, 2, axis=-1)
    return (jax.nn.silu(gate) * up) @ w_out


def setup_inputs(seed: int = 0) -> dict:
    key = jax.random.key(seed)
    ks = jax.random.split(key, 20)
    nrm = lambda k, shape, s: jax.random.normal(k, shape, jnp.float32) * s
    fgt = jnp.linspace(3.0, 6.0, ML_HEADS, dtype=jnp.float32)
    zh = jnp.zeros((ML_HEADS,), jnp.float32)
    gate_base = jnp.concatenate([zh, fgt, zh, fgt])
    return {
        'x': nrm(ks[0], (BATCH, SEQ, D_MODEL), 1.0),
        'c': nrm(ks[1], (BATCH, D_MODEL), 1.0),
        'ctx': nrm(ks[2], (BATCH, CTX_LEN, D_MODEL), 1.0),
        'c_ctx': nrm(ks[3], (D_MODEL,), 1.0),
        'w_ada': nrm(ks[4], (DEPTH, D_MODEL, 6 * D_MODEL), 0.5 * D_MODEL ** -0.5),
        'b_ada': nrm(ks[5], (DEPTH, 6 * D_MODEL), 0.02),
        'g_pre_mix': 1.0 + nrm(ks[6], (DEPTH, D_MODEL), 0.05),
        'w_in': nrm(ks[7], (DEPTH, D_MODEL, IN_WIDTH), D_MODEL ** -0.5),
        'w_conv_qk': nrm(ks[8], (DEPTH, ML_CONV, 2 * ML_QK_WIDTH), ML_CONV ** -0.5),
        'b_gates': gate_base[None, :] + nrm(ks[9], (DEPTH, ML_GATES), 0.1),
        'attn_sink': nrm(ks[10], (DEPTH, ATT_HEADS), 0.5),
        'g_mlstm_out': 1.0 + nrm(ks[11], (DEPTH, ML_WIDTH), 0.05),
        'w_out': nrm(ks[12], (DEPTH, MIX_WIDTH, D_MODEL), MIX_WIDTH ** -0.5),
        'g_post_mix': 1.0 + nrm(ks[13], (DEPTH, D_MODEL), 0.05),
        'g_pre_ffn': 1.0 + nrm(ks[14], (DEPTH, D_MODEL), 0.05),
        'w_ffn_in': nrm(ks[15], (DEPTH, D_MODEL, 2 * FFN_HIDDEN), D_MODEL ** -0.5),
        'w_ffn_out': nrm(ks[16], (DEPTH, FFN_HIDDEN, D_MODEL), FFN_HIDDEN ** -0.5),
        'g_post_ffn': 1.0 + nrm(ks[17], (DEPTH, D_MODEL), 0.05),
    }


def reference(x, c, ctx, c_ctx, w_ada, b_ada, g_pre_mix, w_in, w_conv_qk, b_gates, attn_sink,
              g_mlstm_out, w_out, g_post_mix, g_pre_ffn, w_ffn_in, w_ffn_out, g_post_ffn):
    B, S, _ = x.shape
    L = ctx.shape[1]
    rope = axial_rope_tables(S, x.dtype)
    silu_c = jax.nn.silu(c)
    silu_cc = jax.nn.silu(c_ctx)
    for layer in range(DEPTH):
        advance_ctx = layer + 1 < DEPTH
        mod_x = (silu_c @ w_ada[layer] + b_ada[layer])[:, None, :]
        mod_c = silu_cc @ w_ada[layer] + b_ada[layer]
        sh_m, sc_m, gt_m, sh_f, sc_f, gt_f = jnp.split(mod_x, 6, axis=-1)
        csh_m, csc_m, cgt_m, csh_f, csc_f, cgt_f = jnp.split(mod_c, 6, axis=-1)

        px = modulate(rms_norm(x, g_pre_mix[layer]), sh_m, sc_m) @ w_in[layer]
        pc = modulate(rms_norm(ctx, g_pre_mix[layer]), csh_m, csc_m) @ w_in[layer]
        aqx, akx, avx, mqx, mkx, mvx, mox, mgx = jnp.split(px, SPLIT_AT, axis=-1)
        aqc, akc, avc, mqc, mkc, mvc, moc, mgc = jnp.split(pc, SPLIT_AT, axis=-1)

        kc_att = akc.reshape(B, L, ATT_KV_HEADS, ATT_HEAD_DIM)
        vc_att = avc.reshape(B, L, ATT_KV_HEADS, ATT_HEAD_DIM)
        qx_att = axial_rope(aqx.reshape(B, S, ATT_HEADS, ATT_HEAD_DIM), *rope)
        kx_att = axial_rope(akx.reshape(B, S, ATT_KV_HEADS, ATT_HEAD_DIM), *rope)
        att_x = window_attention(qx_att, kx_att, avx.reshape(B, S, ATT_KV_HEADS, ATT_HEAD_DIM),
                                 kc_att, vc_att, attn_sink[layer])

        cq, ck, cv, cg = mlstm_prepare(mqc, mkc, mvc, mgc, w_conv_qk[layer], b_gates[layer])
        h_ml_c, st_f, st_b = mlstm_bidir(cq, ck, cv, cg, mlstm_zero_state(B), mlstm_zero_state(B))
        xq, xk, xv, xg = mlstm_prepare(mqx, mkx, mvx, mgx, w_conv_qk[layer], b_gates[layer])
        h_ml_x, _, _ = mlstm_bidir(xq, xk, xv, xg, st_f, st_b)
        ml_x = mlstm_output(h_ml_x, mox, g_mlstm_out[layer])

        mix_x = jnp.concatenate([att_x, ml_x], axis=-1) @ w_out[layer]
        x = x + gt_m * rms_norm(mix_x, g_post_mix[layer])
        if advance_ctx:
            att_c = context_self_attention(aqc.reshape(B, L, ATT_HEADS, ATT_HEAD_DIM), kc_att, vc_att, attn_sink[layer])
            ml_c = mlstm_output(h_ml_c, moc, g_mlstm_out[layer])
            mix_c = jnp.concatenate([att_c, ml_c], axis=-1) @ w_out[layer]
            ctx = ctx + cgt_m * rms_norm(mix_c, g_post_mix[layer])

        fx = swiglu(modulate(rms_norm(x, g_pre_ffn[layer]), sh_f, sc_f), w_ffn_in[layer], w_ffn_out[layer])
        x = x + gt_f * rms_norm(fx, g_post_ffn[layer])
        if advance_ctx:
            fc = swiglu(modulate(rms_norm(ctx, g_pre_ffn[layer]), csh_f, csc_f), w_ffn_in[layer], w_ffn_out[layer])
            ctx = ctx + cgt_f * rms_norm(fc, g_post_ffn[layer])
    return x
```

```python
import functools

import jax
import jax.numpy as jnp
from jax import lax
from jax.experimental import pallas as pl
from jax.experimental.pallas import tpu as pltpu

F32 = jnp.float32
BF16 = jnp.bfloat16

EPS = 1e-6
GRID_W = 64
ROPE_BASE = 10000.0

ATT_HEADS = 8
ATT_KV_HEADS = 2
ATT_GROUP = ATT_HEADS // ATT_KV_HEADS
ATT_HEAD_DIM = 64
ATT_BLOCK = 128
ATT_WIDTH = ATT_HEADS * ATT_HEAD_DIM
ATT_KV_WIDTH = ATT_KV_HEADS * ATT_HEAD_DIM

ML_HEADS = 4
ML_V_DIM = 128
ML_QK_DIM = 64
ML_WIDTH = ML_HEADS * ML_V_DIM
ML_QK_WIDTH = ML_HEADS * ML_QK_DIM
ML_GATES = 4 * ML_HEADS
ML_CHUNK = 128

LANES = 128
SUBLANES = 8
VMEM_LIMIT = 56 * 1024 * 1024

INPROJ_TM = 512
FFN_TM = 256
ADA_TN = 1536
MOD_ROWS = 16

_HEAD_PERM = tuple(h * ATT_GROUP + g for g in range(ATT_GROUP) for h in range(ATT_KV_HEADS))


def _silu(v):
    return v * jax.nn.sigmoid(v)


def _log_sigmoid(v):
    return jnp.minimum(v, 0.0) - jnp.log1p(jnp.exp(-jnp.abs(v)))


def _rms(v, g):
    return v * lax.rsqrt(jnp.mean(v * v, axis=-1, keepdims=True) + EPS) * g


def _dot(a, b):
    return jnp.dot(a, b, preferred_element_type=F32)


def _dot_nt(a, b):
    return lax.dot_general(a, b, (((1,), (1,)), ((), ())), preferred_element_type=F32)


def _ada_kernel(cc_ref, w_ref, b_ref, o_ref):
    a = _silu(cc_ref[...])
    o_ref[...] = _dot(a.astype(BF16), w_ref[...].astype(BF16)) + b_ref[...]


def _ada(cc, w, b):
    d, n = w.shape
    return pl.pallas_call(
        _ada_kernel,
        grid=(n // ADA_TN,),
        in_specs=[pl.BlockSpec((MOD_ROWS, d), lambda j: (0, 0)),
                  pl.BlockSpec((d, ADA_TN), lambda j: (0, j)),
                  pl.BlockSpec((1, ADA_TN), lambda j: (0, j))],
        out_specs=pl.BlockSpec((MOD_ROWS, ADA_TN), lambda j: (0, j)),
        out_shape=jax.ShapeDtypeStruct((MOD_ROWS, n), F32),
        compiler_params=pltpu.CompilerParams(dimension_semantics=("arbitrary",),
                                             vmem_limit_bytes=VMEM_LIMIT),
        name="ada",
    )(cc, w, b)


def _rope(v, cos, sin_lo, sin_hi):
    return (v * cos + pltpu.roll(v, LANES - 16, axis=1) * sin_lo
            + pltpu.roll(v, 16, axis=1) * sin_hi)


def _inproj_kernel(*refs, tm, n_tiles, latent):
    if latent:
        (x_ref, xp_ref, xn_ref, sh_ref, sc_ref, g_ref, w_ref, wvt_ref, wgt_ref, bgr_ref, bgc_ref,
         wc_ref, cos_ref, sl_ref, shi_ref,
         q_ref, k_ref, vt_ref, qk_ref, v_ref, o_ref, gcol_ref, grow_ref) = refs
    else:
        (x_ref, xp_ref, xn_ref, sh_ref, sc_ref, g_ref, w_ref, wvt_ref, wgt_ref, bgr_ref, bgc_ref,
         wc_ref,
         k_ref, vt_ref, qk_ref, v_ref, gcol_ref, grow_ref) = refs
    i = pl.program_id(0)

    xt = jnp.concatenate([x_ref[0], xp_ref[0], xn_ref[0]], axis=0)
    ms = jnp.mean(xt * xt, axis=-1, keepdims=True)
    scale = g_ref[...] * (1.0 + sc_ref[0])
    hb = (xt * lax.rsqrt(ms + EPS) * scale + sh_ref[0]).astype(BF16)
    r = _dot(hb, w_ref[...])
    hm = hb[:tm]

    c = 0
    if latent:
        cos, sl, shi = cos_ref[...], sl_ref[...], shi_ref[...]
        for g in range(ATT_WIDTH // LANES):
            q_ref[0, :, g * LANES:(g + 1) * LANES] = _rope(
                r[:tm, c + g * LANES:c + (g + 1) * LANES], cos, sl, shi).astype(BF16)
        c += ATT_WIDTH
        k_ref[0] = _rope(r[:tm, c:c + LANES], cos, sl, shi).astype(BF16)
    else:
        k_ref[0] = r[:tm, c:c + LANES].astype(BF16)
    c += ATT_KV_WIDTH

    gcol_ref[0] = r[:tm, c:c + LANES] + bgr_ref[...]
    c += LANES
    g_t = _dot_nt(wgt_ref[...], hm) + bgc_ref[...]
    for j in range(tm // ML_CHUNK):
        grow_ref[0, j] = g_t[:, j * ML_CHUNK:(j + 1) * ML_CHUNK]

    vt_ref[0] = _dot_nt(wvt_ref[...], hm).astype(BF16)

    y = r[:, c:c + 2 * ML_QK_WIDTH]
    c += 2 * ML_QK_WIDTH
    ym = y[:tm]
    keep_prev = jnp.where(i == 0, 0.0, 1.0)
    keep_next = jnp.where(i == n_tiles - 1, 0.0, 1.0)
    row = lax.broadcasted_iota(jnp.int32, (tm, 1), 0)
    prev = jnp.where(row == 0, y[tm + SUBLANES - 1:tm + SUBLANES] * keep_prev,
                     pltpu.roll(ym, 1, axis=0))
    nxt = jnp.where(row == tm - 1, y[tm + SUBLANES:tm + SUBLANES + 1] * keep_next,
                    pltpu.roll(ym, tm - 1, axis=0))
    wc = wc_ref[...]
    act = _silu(prev * wc[0:1] + ym * wc[1:2] + nxt * wc[2:3])
    qk_ref[0, :, :ML_QK_WIDTH] = (act[:, :ML_QK_WIDTH] * (ML_QK_DIM ** -0.5)).astype(BF16)
    qk_ref[0, :, ML_QK_WIDTH:] = act[:, ML_QK_WIDTH:].astype(BF16)

    v_ref[0] = r[:tm, c:c + ML_WIDTH].astype(BF16)
    c += ML_WIDTH
    if latent:
        o_ref[0] = jax.nn.sigmoid(r[:tm, c:c + ML_WIDTH]).astype(BF16)


def _inproj(x, mod3, mod_row, g_pre, w_main, wvt, wgt, bg_row, bg_col, wc, rope_tabs, *, tm, latent):
    bsz, t, d = x.shape
    n_tiles = t // tm
    hb = tm // SUBLANES
    n_hblk = t // SUBLANES
    n = w_main.shape[1]

    def const(shape):
        return pl.BlockSpec(shape, lambda i, b: (0,) * len(shape))

    in_specs = [
        pl.BlockSpec((1, tm, d), lambda i, b: (b, i, 0)),
        pl.BlockSpec((1, SUBLANES, d), lambda i, b: (b, jnp.maximum(i * hb - 1, 0), 0)),
        pl.BlockSpec((1, SUBLANES, d), lambda i, b: (b, jnp.minimum((i + 1) * hb, n_hblk - 1), 0)),
        pl.BlockSpec((1, 1, d), lambda i, b: (mod_row(b), 0, 0)),
        pl.BlockSpec((1, 1, d), lambda i, b: (mod_row(b), 0, 1)),
        const((1, d)), const((d, n)), const((LANES, d)), const((ML_GATES, d)),
        const((1, LANES)), const((ML_GATES, 1)), const((3, 2 * ML_QK_WIDTH)),
    ]
    args = [x, x, x, mod3, mod3, g_pre, w_main, wvt, wgt, bg_row, bg_col, wc]
    tok = lambda width, dt: (pl.BlockSpec((1, tm, width), lambda i, b: (b, i, 0)),
                             jax.ShapeDtypeStruct((bsz, t, width), dt))
    outs = []
    if latent:
        in_specs += [pl.BlockSpec((tm, LANES), lambda i, b: (i, 0))] * 3
        args += list(rope_tabs)
        outs.append(tok(ATT_WIDTH, BF16))
    outs.append(tok(ATT_KV_WIDTH, BF16))
    outs.append((pl.BlockSpec((1, LANES, tm), lambda i, b: (b, 0, i)),
                 jax.ShapeDtypeStruct((bsz, ATT_KV_WIDTH, t), BF16)))
    outs.append(tok(2 * ML_QK_WIDTH, BF16))
    outs.append(tok(ML_WIDTH, BF16))
    if latent:
        outs.append(tok(ML_WIDTH, BF16))
    outs.append(tok(LANES, F32))
    outs.append((pl.BlockSpec((1, tm // ML_CHUNK, ML_GATES, ML_CHUNK), lambda i, b: (b, i, 0, 0)),
                 jax.ShapeDtypeStruct((bsz, t // ML_CHUNK, ML_GATES, ML_CHUNK), F32)))
    return pl.pallas_call(
        functools.partial(_inproj_kernel, tm=tm, n_tiles=n_tiles, latent=latent),
        grid=(n_tiles, bsz),
        in_specs=in_specs,
        out_specs=[o[0] for o in outs],
        out_shape=[o[1] for o in outs],
        compiler_params=pltpu.CompilerParams(dimension_semantics=("arbitrary", "arbitrary"),
                                             vmem_limit_bytes=VMEM_LIMIT),
        name="inproj_latent" if latent else "inproj_context",
    )(*args)


def _attn_kernel(q_ref, kp_ref, kc_ref, kn_ref, kx_ref, vp_ref, vc_ref, vn_ref, vx_ref, sink_ref,
                 o_ref, *, nb):
    n = pl.program_id(1)
    blk = ATT_BLOCK
    q = q_ref[0]
    lane = lax.broadcasted_iota(jnp.int32, (blk, LANES), 1)
    zero = jnp.zeros((blk, LANES), BF16)
    parts = []
    for g in range(ATT_GROUP):
        slab = q[:, g * LANES:(g + 1) * LANES]
        parts.append(jnp.where(lane < ATT_HEAD_DIM, slab, zero))
        parts.append(jnp.where(lane >= ATT_HEAD_DIM, slab, zero))
    qs = jnp.concatenate(parts, axis=0)
    ncol = ATT_HEADS * blk

    key = lax.broadcasted_iota(jnp.int32, (blk, ncol), 0)
    qry = lax.broadcasted_iota(jnp.int32, (blk, ncol), 1) % blk
    s_p = jnp.where((key >= qry) & (n > 0), _dot_nt(kp_ref[0], qs), -jnp.inf)
    s_c = _dot_nt(kc_ref[0], qs)
    s_n = jnp.where((key <= qry) & (n < nb - 1), _dot_nt(kn_ref[0], qs), -jnp.inf)
    s_x = _dot_nt(kx_ref[0], qs)
    sink = sink_ref[...]

    def cmax(s):
        return jnp.max(s, axis=0, keepdims=True)

    m = jnp.maximum(jnp.maximum(jnp.maximum(cmax(s_p), cmax(s_c)), jnp.maximum(cmax(s_n), cmax(s_x))),
                    sink)
    ps = [jnp.exp(s - m) for s in (s_p, s_c, s_n, s_x)]
    den = sum(jnp.sum(p, axis=0, keepdims=True) for p in ps) + jnp.exp(sink - m)
    p = jnp.concatenate([p.astype(BF16) for p in ps], axis=0)
    vt = jnp.concatenate([vp_ref[0], vc_ref[0], vn_ref[0], vx_ref[0]], axis=1)
    ot = _dot(vt, p) * (1.0 / den)

    dim = lax.broadcasted_iota(jnp.int32, (LANES, blk), 0)
    for g in range(ATT_GROUP):
        a = ot[:, (2 * g) * blk:(2 * g + 1) * blk]
        b = ot[:, (2 * g + 1) * blk:(2 * g + 2) * blk]
        o_ref[0, :, g * LANES:(g + 1) * LANES] = jnp.where(dim < ATT_HEAD_DIM, a, b).T.astype(BF16)


def _attention(q, k, vt, kx, vxt, sink_row):
    bsz, s, _ = q.shape
    l = kx.shape[1]
    nb = s // ATT_BLOCK
    blk = ATT_BLOCK
    prev = lambda n: jnp.maximum(n - 1, 0)
    nxt = lambda n: jnp.minimum(n + 1, nb - 1)
    kspec = lambda f: pl.BlockSpec((1, blk, LANES), lambda b, n: (b, f(n), 0))
    vspec = lambda f: pl.BlockSpec((1, LANES, blk), lambda b, n: (b, 0, f(n)))
    same = lambda n: n
    return pl.pallas_call(
        functools.partial(_attn_kernel, nb=nb),
        grid=(bsz, nb),
        in_specs=[pl.BlockSpec((1, blk, ATT_WIDTH), lambda b, n: (b, n, 0)),
                  kspec(prev), kspec(same), kspec(nxt),
                  pl.BlockSpec((1, l, LANES), lambda b, n: (b, 0, 0)),
                  vspec(prev), vspec(same), vspec(nxt),
                  pl.BlockSpec((1, LANES, l), lambda b, n: (b, 0, 0)),
                  pl.BlockSpec((1, ATT_HEADS * blk), lambda b, n: (0, 0))],
        out_specs=pl.BlockSpec((1, blk, ATT_WIDTH), lambda b, n: (b, n, 0)),
        out_shape=jax.ShapeDtypeStruct((bsz, s, ATT_WIDTH), BF16),
        compiler_params=pltpu.CompilerParams(dimension_semantics=("arbitrary", "arbitrary"),
                                             vmem_limit_bytes=VMEM_LIMIT),
        name="attention",
    )(q, k, k, k, kx, vt, vt, vt, vxt, sink_row)


def _mlstm_kernel(qkx_ref, vx_ref, ox_ref, gcx_ref, grx_ref, qkc_ref, vc_ref, gcc_ref, grc_ref,
                  gain_ref, out_ref, hf_ref, hb_ref, cst_ref, m_ref, *, ncx, ncc):
    lc = ML_CHUNK
    qkw = ML_QK_WIDTH
    cst_ref[...] = jnp.zeros(cst_ref.shape, F32)
    m_ref[...] = jnp.zeros(m_ref.shape, F32)

    rr = lax.broadcasted_iota(jnp.int32, (lc, lc), 0)
    cc = lax.broadcasted_iota(jnp.int32, (lc, lc), 1)
    tril = rr >= cc
    triu = rr <= cc
    tril_b = jnp.where(tril, 1.0, 0.0).astype(BF16)
    triu_b = jnp.where(triu, 1.0, 0.0).astype(BF16)
    head_of_lane = lax.broadcasted_iota(jnp.int32, (lc, qkw), 1) // ML_QK_DIM
    ones = jnp.ones((lc, ML_V_DIM), F32)

    def split(v):
        hi = v.astype(BF16)
        return hi, (v - hi.astype(F32)).astype(BF16)

    def step(dirn, qk, v, gcol, grow, h_ref, row0):
        valid = tril if dirn == 0 else triu
        tri_col = tril_b if dirn == 0 else triu_b
        tri_row = triu_b if dirn == 0 else tril_b
        hi, lo = split(_log_sigmoid(gcol))
        bc = _dot(tri_col, jnp.concatenate([hi, lo], axis=1))
        b_col = bc[:, :LANES] + bc[:, LANES:]
        hi, lo = split(_log_sigmoid(grow))
        br = _dot(jnp.concatenate([hi, lo], axis=0), tri_row)
        b_row = br[:ML_GATES] + br[ML_GATES:]
        b_tot = b_col[lc - 1:lc] if dirn == 0 else b_col[0:1]

        q4 = qk[:, :qkw]
        k4 = qk[:, qkw:]
        kt = k4.astype(F32).T.astype(BF16)
        cst = cst_ref[dirn]
        if h_ref is not None:
            zero = jnp.zeros_like(q4)
            qs = jnp.concatenate([jnp.where(head_of_lane == h, q4, zero) for h in range(ML_HEADS)],
                                 axis=0)
            qk_all = _dot_nt(qs, k4)
            inter_all = _dot(qs, cst.astype(BF16))

        for h in range(ML_HEADS):
            ci = 2 * ML_HEADS * dirn + h
            cf = ci + ML_HEADS
            r = ML_HEADS * dirn + h
            bcol = b_col[:, cf:cf + 1]
            icol = gcol[:, ci:ci + 1]
            blast = b_tot[:, cf:cf + 1]
            m_old = m_ref[r:r + 1, 0:1]
            v_aug = jnp.concatenate([v[:, h * ML_V_DIM:(h + 1) * ML_V_DIM].astype(F32), ones], axis=1)

            if h_ref is not None:
                brow = b_row[cf:cf + 1, :]
                irow = grow[ci:ci + 1, :]
                a_inter = bcol + m_old
                d = jnp.where(valid, bcol - brow + irow, -jnp.inf)
                m_t = jnp.maximum(a_inter, jnp.max(d, axis=1, keepdims=True))
                s = (qk_all[h * lc:(h + 1) * lc] * jnp.exp(d - m_t)).astype(BF16)
                haug = (_dot(s, v_aug.astype(BF16))
                        + jnp.exp(a_inter - m_t) * inter_all[h * lc:(h + 1) * lc])
                hval = haug[:, :ML_V_DIM] / jnp.maximum(jnp.abs(haug[:, ML_V_DIM:]), jnp.exp(-m_t))
                h_ref[pl.ds(row0, lc), h * ML_V_DIM:(h + 1) * ML_V_DIM] = hval

            g = blast - bcol + icol
            m_new = jnp.maximum(blast + m_old, jnp.max(g, axis=0, keepdims=True))
            decay = jnp.exp(blast + m_old - m_new)
            wv = (jnp.exp(g - m_new) * v_aug).astype(BF16)
            upd = _dot(kt[h * ML_QK_DIM:(h + 1) * ML_QK_DIM], wv)
            cst_ref[dirn, h * ML_QK_DIM:(h + 1) * ML_QK_DIM, :] = (
                decay * cst[h * ML_QK_DIM:(h + 1) * ML_QK_DIM] + upd)
            m_ref[r:r + 1, :] = jnp.broadcast_to(m_new, (1, LANES))

    for j in range(ncc):
        for dirn, cj in ((0, j), (1, ncc - 1 - j)):
            rows = slice(cj * lc, (cj + 1) * lc)
            step(dirn, qkc_ref[0, rows, :], vc_ref[0, rows, :], gcc_ref[0, rows, :], grc_ref[0, cj],
                 None, 0)

    def body(j, carry):
        for dirn, cj, h_ref in ((0, j, hf_ref), (1, ncx - 1 - j, hb_ref)):
            row0 = pl.multiple_of(cj * lc, lc)
            rows = pl.ds(row0, lc)
            step(dirn, qkx_ref[0, rows, :], vx_ref[0, rows, :], gcx_ref[0, rows, :], grx_ref[0, cj],
                 h_ref, row0)
        return carry

    lax.fori_loop(0, ncx, body, 0)

    gain = gain_ref[...]

    def finish(j, carry):
        rows = pl.ds(pl.multiple_of(j * lc, lc), lc)
        for h in range(ML_HEADS):
            cols = slice(h * ML_V_DIM, (h + 1) * ML_V_DIM)
            hs = hf_ref[rows, cols] + hb_ref[rows, cols]
            out_ref[0, rows, cols] = (_rms(hs, gain[:, cols]) * ox_ref[0, rows, cols].astype(F32)).astype(BF16)
        return carry

    lax.fori_loop(0, ncx, finish, 0)


def _mlstm(qkx, vx, ox, gcx, grx, qkc, vc, gcc, grc, gain):
    bsz, s, _ = qkx.shape
    l = qkc.shape[1]
    per_b = lambda a: pl.BlockSpec((1,) + a.shape[1:], lambda b: (b,) + (0,) * (a.ndim - 1))
    ins = [qkx, vx, ox, gcx, grx, qkc, vc, gcc, grc]
    return pl.pallas_call(
        functools.partial(_mlstm_kernel, ncx=s // ML_CHUNK, ncc=l // ML_CHUNK),
        grid=(bsz,),
        in_specs=[per_b(a) for a in ins] + [pl.BlockSpec((1, ML_WIDTH), lambda b: (0, 0))],
        out_specs=pl.BlockSpec((1, s, ML_WIDTH), lambda b: (b, 0, 0)),
        out_shape=jax.ShapeDtypeStruct((bsz, s, ML_WIDTH), BF16),
        scratch_shapes=[pltpu.VMEM((s, ML_WIDTH), F32), pltpu.VMEM((s, ML_WIDTH), F32),
                        pltpu.VMEM((2, ML_QK_WIDTH, 2 * ML_V_DIM), F32),
                        pltpu.VMEM((2 * ML_HEADS, LANES), F32)],
        compiler_params=pltpu.CompilerParams(dimension_semantics=("arbitrary",),
                                             vmem_limit_bytes=VMEM_LIMIT),
        name="mlstm",
    )(*ins, gain)


def _out_ffn_kernel(x_ref, att_ref, ml_ref, gtm_ref, shf_ref, scf_ref, gtf_ref,
                    gpm_ref, gpf_ref, gqf_ref, woa_ref, wom_ref, wfi_ref, wfo_ref, o_ref, *, hidden):
    mix = _dot(att_ref[...], woa_ref[...]) + _dot(ml_ref[...], wom_ref[...])
    x1 = x_ref[...] + gtm_ref[0] * _rms(mix, gpm_ref[...])
    h = (_rms(x1, gpf_ref[...]) * (1.0 + scf_ref[0]) + shf_ref[0]).astype(BF16)
    gu = _dot(h, wfi_ref[...])
    act = (_silu(gu[:, :hidden]) * gu[:, hidden:]).astype(BF16)
    fx = _dot(act, wfo_ref[...])
    o_ref[...] = x1 + gtf_ref[0] * _rms(fx, gqf_ref[...])


def _out_ffn(x2, att2, ml2, mod3, g_post_mix, g_pre_ffn, g_post_ffn, woa, wom, wfi, wfo, *, tiles_per_batch):
    t, d = x2.shape
    tm = FFN_TM
    hidden = wfo.shape[0]
    resident = lambda a: pl.BlockSpec(a.shape, lambda i: (0,) * a.ndim, pipeline_mode=pl.Buffered(1))
    mod = lambda k: pl.BlockSpec((1, 1, d), lambda i: (i // tiles_per_batch, 0, k))
    row = pl.BlockSpec((1, d), lambda i: (0, 0))
    return pl.pallas_call(
        functools.partial(_out_ffn_kernel, hidden=hidden),
        grid=(t // tm,),
        in_specs=[pl.BlockSpec((tm, d), lambda i: (i, 0)),
                  pl.BlockSpec((tm, ATT_WIDTH), lambda i: (i, 0)),
                  pl.BlockSpec((tm, ML_WIDTH), lambda i: (i, 0)),
                  mod(2), mod(3), mod(4), mod(5), row, row, row,
                  resident(woa), resident(wom), resident(wfi), resident(wfo)],
        out_specs=pl.BlockSpec((tm, d), lambda i: (i, 0)),
        out_shape=jax.ShapeDtypeStruct((t, d), F32),
        compiler_params=pltpu.CompilerParams(dimension_semantics=("arbitrary",),
                                             vmem_limit_bytes=VMEM_LIMIT),
        name="out_ffn",
    )(x2, att2, ml2, mod3, mod3, mod3, mod3, g_post_mix, g_pre_ffn, g_post_ffn, woa, wom, wfi, wfo)


def _rope_tables(n_tokens):
    pos = jnp.arange(n_tokens)
    row = (pos // GRID_W).astype(F32)
    col = (pos % GRID_W).astype(F32)
    half = ATT_HEAD_DIM // 4
    inv_freq = jnp.power(ROPE_BASE, -jnp.arange(half, dtype=F32) / half)
    ang_r = row[:, None] * inv_freq
    ang_c = col[:, None] * inv_freq
    z = jnp.zeros_like(ang_r)
    reps = LANES // ATT_HEAD_DIM
    cos = jnp.tile(jnp.concatenate([jnp.cos(ang_r)] * 2 + [jnp.cos(ang_c)] * 2, axis=1), (1, reps))
    sin_lo = jnp.tile(jnp.concatenate([-jnp.sin(ang_r), z, -jnp.sin(ang_c), z], axis=1), (1, reps))
    sin_hi = jnp.tile(jnp.concatenate([z, jnp.sin(ang_r), z, jnp.sin(ang_c)], axis=1), (1, reps))
    return cos, sin_lo, sin_hi


def _permute_heads(w, axis):
    shape = w.shape
    split = shape[:axis] + (ATT_HEADS, ATT_HEAD_DIM) + shape[axis + 1:]
    return jnp.take(w.reshape(split), jnp.array(_HEAD_PERM), axis=axis).reshape(shape)


def kernel(x, c, ctx, c_ctx, w_ada, b_ada, g_pre_mix, w_in, w_conv_qk, b_gates, attn_sink,
           g_mlstm_out, w_out, g_post_mix, g_pre_ffn, w_ffn_in, w_ffn_out, g_post_ffn):
    bsz, s, d = x.shape
    l = ctx.shape[1]
    assert w_ada.shape[0] == 1, "single-layer block"
    assert bsz < MOD_ROWS and s % INPROJ_TM == 0 and l % ML_CHUNK == 0 and s % FFN_TM == 0

    cc = jnp.zeros((MOD_ROWS, d), F32).at[:bsz].set(c).at[bsz].set(c_ctx)
    mod3 = _ada(cc, w_ada[0], b_ada).reshape(MOD_ROWS, 1, 6 * d)

    w = w_in[0]
    o_q, o_k, o_v = 0, ATT_WIDTH, ATT_WIDTH + ATT_KV_WIDTH
    o_mq = o_v + ATT_KV_WIDTH
    o_mv = o_mq + 2 * ML_QK_WIDTH
    o_mo = o_mv + ML_WIDTH
    o_mg = o_mo + ML_WIDTH
    w_q = _permute_heads(w[:, o_q:o_k], 1) * (ATT_HEAD_DIM ** -0.5)
    w_g = jnp.pad(w[:, o_mg:], ((0, 0), (0, LANES - ML_GATES)))
    shared = [w[:, o_k:o_v], w_g, w[:, o_mq:o_mv], w[:, o_mv:o_mo]]
    w_lat = jnp.concatenate([w_q] + shared + [w[:, o_mo:o_mg]], axis=1).astype(BF16)
    w_ctx = jnp.concatenate(shared, axis=1).astype(BF16)
    wvt = w[:, o_v:o_mq].T.astype(BF16)
    wgt = w[:, o_mg:].T.astype(BF16)
    bg_row = jnp.pad(b_gates, ((0, 0), (0, LANES - ML_GATES)))
    bg_col = b_gates.reshape(ML_GATES, 1)
    wc = w_conv_qk[0]

    q, k, vt, qkx, vx, ox, gcx, grx = _inproj(
        x, mod3, lambda b: b, g_pre_mix, w_lat, wvt, wgt, bg_row, bg_col, wc, _rope_tables(s),
        tm=INPROJ_TM, latent=True)
    kc, vct, qkc, vc, gcc, grc = _inproj(
        ctx, mod3, lambda b: bsz, g_pre_mix, w_ctx, wvt, wgt, bg_row, bg_col, wc, None,
        tm=l, latent=False)

    sink_row = jnp.repeat(attn_sink[0][jnp.array(_HEAD_PERM)], ATT_BLOCK)[None, :]
    att = _attention(q, k, vt, kc, vct, sink_row)
    ml = _mlstm(qkx, vx, ox, gcx, grx, qkc, vc, gcc, grc, g_mlstm_out)

    wo = w_out[0]
    woa = _permute_heads(wo[:ATT_WIDTH], 0).astype(BF16)
    wom = wo[ATT_WIDTH:].astype(BF16)
    out = _out_ffn(x.reshape(bsz * s, d), att.reshape(bsz * s, ATT_WIDTH), ml.reshape(bsz * s, ML_WIDTH),
                   mod3, g_post_mix, g_pre_ffn, g_post_ffn, woa, wom,
                   w_ffn_in[0].astype(BF16), w_ffn_out[0].astype(BF16), tiles_per_batch=s // FFN_TM)
    return out.reshape(bsz, s, d)
```

```python
import functools

import jax
import jax.numpy as jnp
from jax import lax
from jax.experimental import pallas as pl
from jax.experimental.pallas import tpu as pltpu

F32 = jnp.float32
BF16 = jnp.bfloat16

EPS = 1e-6
GRID_W = 64
ROPE_BASE = 10000.0

ATT_HEADS = 8
ATT_KV_HEADS = 2
ATT_GROUP = ATT_HEADS // ATT_KV_HEADS
ATT_HEAD_DIM = 64
ATT_BLOCK = 128
ATT_WIDTH = ATT_HEADS * ATT_HEAD_DIM
ATT_KV_WIDTH = ATT_KV_HEADS * ATT_HEAD_DIM

ML_HEADS = 4
ML_V_DIM = 128
ML_QK_DIM = 64
ML_WIDTH = ML_HEADS * ML_V_DIM
ML_QK_WIDTH = ML_HEADS * ML_QK_DIM
ML_GATES = 4 * ML_HEADS
ML_CHUNK = 128

LANES = 128
SUBLANES = 8
VMEM_LIMIT = 56 * 1024 * 1024

INPROJ_TM = 512
FFN_TM = 256
ADA_TN = 1536
MOD_ROWS = 16

_HEAD_PERM = tuple(h * ATT_GROUP + g for g in range(ATT_GROUP) for h in range(ATT_KV_HEADS))


def _silu(v):
    return v * jax.nn.sigmoid(v)


def _log_sigmoid(v):
    return jnp.minimum(v, 0.0) - jnp.log1p(jnp.exp(-jnp.abs(v)))


def _rms(v, g):
    return v * lax.rsqrt(jnp.mean(v * v, axis=-1, keepdims=True) + EPS) * g


def _dot(a, b):
    return jnp.dot(a, b, preferred_element_type=F32)


def _dot_nt(a, b):
    return lax.dot_general(a, b, (((1,), (1,)), ((), ())), preferred_element_type=F32)


def _ada_kernel(cc_ref, w_ref, b_ref, o_ref):
    a = _silu(cc_ref[...])
    o_ref[...] = _dot(a.astype(BF16), w_ref[...].astype(BF16)) + b_ref[...]


def _ada(cc, w, b):
    d, n = w.shape
    return pl.pallas_call(
        _ada_kernel,
        grid=(n // ADA_TN,),
        in_specs=[pl.BlockSpec((MOD_ROWS, d), lambda j: (0, 0)),
                  pl.BlockSpec((d, ADA_TN), lambda j: (0, j)),
                  pl.BlockSpec((1, ADA_TN), lambda j: (0, j))],
        out_specs=pl.BlockSpec((MOD_ROWS, ADA_TN), lambda j: (0, j)),
        out_shape=jax.ShapeDtypeStruct((MOD_ROWS, n), F32),
        compiler_params=pltpu.CompilerParams(dimension_semantics=("arbitrary",),
                                             vmem_limit_bytes=VMEM_LIMIT),
        name="ada",
    )(cc, w, b)


def _rope(v, cos, sin_lo, sin_hi):
    return (v * cos + pltpu.roll(v, LANES - 16, axis=1) * sin_lo
            + pltpu.roll(v, 16, axis=1) * sin_hi)


def _inproj_kernel(*refs, tm, n_tiles, latent):
    if latent:
        (x_ref, xp_ref, xn_ref, sh_ref, sc_ref, g_ref, w_ref, wvt_ref, wgt_ref, bgr_ref, bgc_ref,
         wc_ref, cos_ref, sl_ref, shi_ref,
         q_ref, k_ref, vt_ref, qk_ref, mvt_ref, o_ref, gcol_ref, grow_ref) = refs
    else:
        (x_ref, xp_ref, xn_ref, sh_ref, sc_ref, g_ref, w_ref, wvt_ref, wgt_ref, bgr_ref, bgc_ref,
         wc_ref,
         k_ref, vt_ref, qk_ref, mvt_ref, gcol_ref, grow_ref) = refs
    i = pl.program_id(0)

    xt = jnp.concatenate([x_ref[0], xp_ref[0], xn_ref[0]], axis=0)
    ms = jnp.mean(xt * xt, axis=-1, keepdims=True)
    scale = g_ref[...] * (1.0 + sc_ref[0])
    hb = (xt * lax.rsqrt(ms + EPS) * scale + sh_ref[0]).astype(BF16)
    r = _dot(hb, w_ref[...])
    hm = hb[:tm]

    c = 0
    if latent:
        cos, sl, shi = cos_ref[...], sl_ref[...], shi_ref[...]
        for g in range(ATT_WIDTH // LANES):
            q_ref[0, :, g * LANES:(g + 1) * LANES] = _rope(
                r[:tm, c + g * LANES:c + (g + 1) * LANES], cos, sl, shi).astype(BF16)
        c += ATT_WIDTH
        k_ref[0] = _rope(r[:tm, c:c + LANES], cos, sl, shi).astype(BF16)
    else:
        k_ref[0] = r[:tm, c:c + LANES].astype(BF16)
    c += ATT_KV_WIDTH

    gcol_ref[0] = r[:tm, c:c + LANES] + bgr_ref[...]
    c += LANES
    g_t = _dot_nt(wgt_ref[...], hm) + bgc_ref[...]
    for j in range(tm // ML_CHUNK):
        grow_ref[0, j] = g_t[:, j * ML_CHUNK:(j + 1) * ML_CHUNK]

    v_t = _dot_nt(wvt_ref[...], hm)
    vt_ref[0] = v_t[:ATT_KV_WIDTH].astype(BF16)
    for j in range(tm // ML_CHUNK):
        mvt_ref[0, j] = v_t[ATT_KV_WIDTH:, j * ML_CHUNK:(j + 1) * ML_CHUNK].astype(BF16)

    y = r[:, c:c + 2 * ML_QK_WIDTH]
    c += 2 * ML_QK_WIDTH
    ym = y[:tm]
    keep_prev = jnp.where(i == 0, 0.0, 1.0)
    keep_next = jnp.where(i == n_tiles - 1, 0.0, 1.0)
    row = lax.broadcasted_iota(jnp.int32, (tm, 1), 0)
    prev = jnp.where(row == 0, y[tm + SUBLANES - 1:tm + SUBLANES] * keep_prev,
                     pltpu.roll(ym, 1, axis=0))
    nxt = jnp.where(row == tm - 1, y[tm + SUBLANES:tm + SUBLANES + 1] * keep_next,
                    pltpu.roll(ym, tm - 1, axis=0))
    wc = wc_ref[...]
    act = _silu(prev * wc[0:1] + ym * wc[1:2] + nxt * wc[2:3])
    qk_ref[0, :, :ML_QK_WIDTH] = (act[:, :ML_QK_WIDTH] * (ML_QK_DIM ** -0.5)).astype(BF16)
    qk_ref[0, :, ML_QK_WIDTH:] = act[:, ML_QK_WIDTH:].astype(BF16)

    if latent:
        o_ref[0] = jax.nn.sigmoid(r[:tm, c:c + ML_WIDTH]).astype(BF16)


def _inproj(x, mod3, mod_row, g_pre, w_main, wvt, wgt, bg_row, bg_col, wc, rope_tabs, *, tm, latent):
    bsz, t, d = x.shape
    n_tiles = t // tm
    hb = tm // SUBLANES
    n_hblk = t // SUBLANES
    n = w_main.shape[1]

    def const(shape):
        return pl.BlockSpec(shape, lambda i, b: (0,) * len(shape))

    in_specs = [
        pl.BlockSpec((1, tm, d), lambda i, b: (b, i, 0)),
        pl.BlockSpec((1, SUBLANES, d), lambda i, b: (b, jnp.maximum(i * hb - 1, 0), 0)),
        pl.BlockSpec((1, SUBLANES, d), lambda i, b: (b, jnp.minimum((i + 1) * hb, n_hblk - 1), 0)),
        pl.BlockSpec((1, 1, d), lambda i, b: (mod_row(b), 0, 0)),
        pl.BlockSpec((1, 1, d), lambda i, b: (mod_row(b), 0, 1)),
        const((1, d)), const((d, n)), const((ATT_KV_WIDTH + ML_WIDTH, d)), const((ML_GATES, d)),
        const((1, LANES)), const((ML_GATES, 1)), const((3, 2 * ML_QK_WIDTH)),
    ]
    args = [x, x, x, mod3, mod3, g_pre, w_main, wvt, wgt, bg_row, bg_col, wc]
    tok = lambda width, dt: (pl.BlockSpec((1, tm, width), lambda i, b: (b, i, 0)),
                             jax.ShapeDtypeStruct((bsz, t, width), dt))
    chunked = lambda rows, dt: (pl.BlockSpec((1, tm // ML_CHUNK, rows, ML_CHUNK), lambda i, b: (b, i, 0, 0)),
                                jax.ShapeDtypeStruct((bsz, t // ML_CHUNK, rows, ML_CHUNK), dt))
    outs = []
    if latent:
        in_specs += [pl.BlockSpec((tm, LANES), lambda i, b: (i, 0))] * 3
        args += list(rope_tabs)
        outs.append(tok(ATT_WIDTH, BF16))
    outs.append(tok(ATT_KV_WIDTH, BF16))
    outs.append((pl.BlockSpec((1, LANES, tm), lambda i, b: (b, 0, i)),
                 jax.ShapeDtypeStruct((bsz, ATT_KV_WIDTH, t), BF16)))
    outs.append(tok(2 * ML_QK_WIDTH, BF16))
    outs.append(chunked(ML_WIDTH, BF16))
    if latent:
        outs.append(tok(ML_WIDTH, BF16))
    outs.append(tok(LANES, F32))
    outs.append(chunked(ML_GATES, F32))
    return pl.pallas_call(
        functools.partial(_inproj_kernel, tm=tm, n_tiles=n_tiles, latent=latent),
        grid=(n_tiles, bsz),
        in_specs=in_specs,
        out_specs=[o[0] for o in outs],
        out_shape=[o[1] for o in outs],
        compiler_params=pltpu.CompilerParams(dimension_semantics=("arbitrary", "arbitrary"),
                                             vmem_limit_bytes=VMEM_LIMIT),
        name="inproj_latent" if latent else "inproj_context",
    )(*args)


def _attn_kernel(q_ref, kp_ref, kc_ref, kn_ref, kx_ref, vp_ref, vc_ref, vn_ref, vx_ref, sink_ref,
                 o_ref, *, nb):
    n = pl.program_id(1)
    blk = ATT_BLOCK
    q = q_ref[0]
    lane = lax.broadcasted_iota(jnp.int32, (blk, LANES), 1)
    zero = jnp.zeros((blk, LANES), BF16)
    parts = []
    for g in range(ATT_GROUP):
        slab = q[:, g * LANES:(g + 1) * LANES]
        parts.append(jnp.where(lane < ATT_HEAD_DIM, slab, zero))
        parts.append(jnp.where(lane >= ATT_HEAD_DIM, slab, zero))
    qs = jnp.concatenate(parts, axis=0)
    ncol = ATT_HEADS * blk

    key = lax.broadcasted_iota(jnp.int32, (blk, ncol), 0)
    qry = lax.broadcasted_iota(jnp.int32, (blk, ncol), 1) % blk
    s_p = jnp.where((key >= qry) & (n > 0), _dot_nt(kp_ref[0], qs), -jnp.inf)
    s_c = _dot_nt(kc_ref[0], qs)
    s_n = jnp.where((key <= qry) & (n < nb - 1), _dot_nt(kn_ref[0], qs), -jnp.inf)
    s_x = _dot_nt(kx_ref[0], qs)
    sink = sink_ref[...]

    def cmax(s):
        return jnp.max(s, axis=0, keepdims=True)

    m = jnp.maximum(jnp.maximum(jnp.maximum(cmax(s_p), cmax(s_c)), jnp.maximum(cmax(s_n), cmax(s_x))),
                    sink)
    ps = [jnp.exp(s - m) for s in (s_p, s_c, s_n, s_x)]
    den = sum(jnp.sum(p, axis=0, keepdims=True) for p in ps) + jnp.exp(sink - m)
    p = jnp.concatenate([p.astype(BF16) for p in ps], axis=0)
    vt = jnp.concatenate([vp_ref[0], vc_ref[0], vn_ref[0], vx_ref[0]], axis=1)
    ot = _dot(vt, p) * (1.0 / den)

    dim = lax.broadcasted_iota(jnp.int32, (LANES, blk), 0)
    for g in range(ATT_GROUP):
        a = ot[:, (2 * g) * blk:(2 * g + 1) * blk]
        b = ot[:, (2 * g + 1) * blk:(2 * g + 2) * blk]
        o_ref[0, :, g * LANES:(g + 1) * LANES] = jnp.where(dim < ATT_HEAD_DIM, a, b).T.astype(BF16)


def _attention(q, k, vt, kx, vxt, sink_row):
    bsz, s, _ = q.shape
    l = kx.shape[1]
    nb = s // ATT_BLOCK
    blk = ATT_BLOCK
    prev = lambda n: jnp.maximum(n - 1, 0)
    nxt = lambda n: jnp.minimum(n + 1, nb - 1)
    kspec = lambda f: pl.BlockSpec((1, blk, LANES), lambda b, n: (b, f(n), 0))
    vspec = lambda f: pl.BlockSpec((1, LANES, blk), lambda b, n: (b, 0, f(n)))
    same = lambda n: n
    return pl.pallas_call(
        functools.partial(_attn_kernel, nb=nb),
        grid=(bsz, nb),
        in_specs=[pl.BlockSpec((1, blk, ATT_WIDTH), lambda b, n: (b, n, 0)),
                  kspec(prev), kspec(same), kspec(nxt),
                  pl.BlockSpec((1, l, LANES), lambda b, n: (b, 0, 0)),
                  vspec(prev), vspec(same), vspec(nxt),
                  pl.BlockSpec((1, LANES, l), lambda b, n: (b, 0, 0)),
                  pl.BlockSpec((1, ATT_HEADS * blk), lambda b, n: (0, 0))],
        out_specs=pl.BlockSpec((1, blk, ATT_WIDTH), lambda b, n: (b, n, 0)),
        out_shape=jax.ShapeDtypeStruct((bsz, s, ATT_WIDTH), BF16),
        compiler_params=pltpu.CompilerParams(dimension_semantics=("arbitrary", "arbitrary"),
                                             vmem_limit_bytes=VMEM_LIMIT),
        name="attention",
    )(q, k, k, k, kx, vt, vt, vt, vxt, sink_row)


ML_STATE_ROWS = ML_V_DIM + SUBLANES


def _mlstm_kernel(qkx_ref, vtx_ref, ox_ref, gcx_ref, grx_ref, qkc_ref, vtc_ref, grc_ref,
                  gain_ref, out_ref, hf_ref, hb_ref, st_ref, m_ref, *, ncx, ncc):
    lc = ML_CHUNK
    qkw = ML_QK_WIDTH
    sr = ML_STATE_ROWS
    st_ref[...] = jnp.zeros(st_ref.shape, F32)
    m_ref[...] = jnp.zeros(m_ref.shape, F32)

    rr = lax.broadcasted_iota(jnp.int32, (lc, lc), 0)
    cc = lax.broadcasted_iota(jnp.int32, (lc, lc), 1)
    tril = rr >= cc
    triu = rr <= cc
    tril_b = jnp.where(tril, 1.0, 0.0).astype(BF16)
    triu_b = jnp.where(triu, 1.0, 0.0).astype(BF16)
    head_of_lane = lax.broadcasted_iota(jnp.int32, (lc, qkw), 1) // ML_QK_DIM
    own_block = jnp.concatenate(
        [lax.broadcasted_iota(jnp.int32, (sr, qkw), 1) // ML_QK_DIM == h for h in range(ML_HEADS)], axis=0)
    gate_lane = lax.broadcasted_iota(jnp.int32, (ML_GATES, lc), 1)
    pad_rows = jnp.zeros((SUBLANES - 1, lc), F32)

    def split(v):
        hi = v.astype(BF16)
        return hi, (v - hi.astype(F32)).astype(BF16)

    def step(dirn, qk, vt, gcol, grow, h_ref, cj):
        fwd = dirn == 0
        last = lc - 1 if fwd else 0
        gates = slice(2 * ML_HEADS * dirn, 2 * ML_HEADS * (dirn + 1))
        hi, lo = split(_log_sigmoid(grow))
        br = _dot(jnp.concatenate([hi, lo], axis=0), triu_b if fwd else tril_b)
        b_row = pltpu.roll(br[:ML_GATES] + br[ML_GATES:], ML_GATES - ML_HEADS, axis=0)
        u_row = grow - b_row
        m_in = m_ref[...]
        run = u_row
        k = 1
        while k < lc:
            if fwd:
                shifted = jnp.where(gate_lane >= k, pltpu.roll(run, k, axis=1), -jnp.inf)
            else:
                shifted = jnp.where(gate_lane < lc - k, pltpu.roll(run, lc - k, axis=1), -jnp.inf)
            run = jnp.maximum(run, shifted)
            k *= 2
        m_run = jnp.maximum(run, m_in)
        m_t = b_row + m_run
        m_out = jnp.broadcast_to(m_t[:, last:last + 1], (ML_GATES, lc))
        b_tot = jnp.broadcast_to(b_row[:, last:last + 1], (ML_GATES, lc))
        decay = jnp.exp(b_tot + m_in - m_out)
        w_key = jnp.exp(b_tot + u_row - m_out)
        w_int = jnp.exp(m_in - m_run)
        floor = jnp.exp(-m_t)

        q4 = qk[:, :qkw]
        k4 = qk[:, qkw:]
        st = st_ref[dirn]
        if h_ref is not None:
            hi, lo = split(_log_sigmoid(gcol))
            bc = _dot(tril_b if fwd else triu_b, jnp.concatenate([hi, lo], axis=1))
            b_col = pltpu.roll(bc[:, :LANES] + bc[:, LANES:], LANES - ML_HEADS, axis=1)
            u_col = gcol - b_col
            zero = jnp.zeros_like(q4)
            qs = jnp.concatenate([jnp.where(head_of_lane == h, q4, zero) for h in range(ML_HEADS)], axis=0)
            qk_t = _dot_nt(k4, qs)
            inter = _dot_nt(st.astype(BF16), q4)
            valid = triu if fwd else tril
            for h in range(ML_HEADS):
                c = 2 * ML_HEADS * dirn + h
                e = jnp.exp(jnp.where(valid, u_col[:, c:c + 1] - m_run[c:c + 1], -jnp.inf))
                s_t = qk_t[:, h * lc:(h + 1) * lc] * e
                num = _dot(vt[h * ML_V_DIM:(h + 1) * ML_V_DIM], s_t.astype(BF16))
                num = num + w_int[c:c + 1] * inter[h * sr:h * sr + ML_V_DIM]
                den = (jnp.sum(s_t, axis=0, keepdims=True)
                       + w_int[c:c + 1] * inter[h * sr + ML_V_DIM:h * sr + ML_V_DIM + 1])
                h_ref[cj, h * ML_V_DIM:(h + 1) * ML_V_DIM, :] = num / jnp.maximum(jnp.abs(den), floor[c:c + 1])

        pieces, decays = [], []
        for h in range(ML_HEADS):
            c = 2 * ML_HEADS * dirn + h
            pieces += [vt[h * ML_V_DIM:(h + 1) * ML_V_DIM].astype(F32) * w_key[c:c + 1], w_key[c:c + 1], pad_rows]
            decays.append(jnp.broadcast_to(jnp.concatenate([decay[c:c + 1]] * (qkw // lc), axis=1), (sr, qkw)))
        upd = _dot(jnp.concatenate(pieces, axis=0).astype(BF16), k4)
        st_ref[dirn] = jnp.concatenate(decays, axis=0) * st + jnp.where(own_block, upd, 0.0)
        m_ref[gates, :] = m_out[gates]

    for j in range(ncc):
        for dirn, cj in ((0, j), (1, ncc - 1 - j)):
            rows = slice(cj * lc, (cj + 1) * lc)
            step(dirn, qkc_ref[0, rows, :], vtc_ref[0, cj], None, grc_ref[0, cj], None, cj)

    def body(j, carry):
        for dirn, cj, h_ref in ((0, j, hf_ref), (1, ncx - 1 - j, hb_ref)):
            rows = pl.ds(pl.multiple_of(cj * lc, lc), lc)
            step(dirn, qkx_ref[0, rows, :], vtx_ref[0, cj], gcx_ref[0, rows, :], grx_ref[0, cj], h_ref, cj)
        return carry

    lax.fori_loop(0, ncx, body, 0)

    gain = gain_ref[...]

    def finish(j, carry):
        rows = pl.ds(pl.multiple_of(j * lc, lc), lc)
        for h in range(ML_HEADS):
            cols = slice(h * ML_V_DIM, (h + 1) * ML_V_DIM)
            hs = (hf_ref[j, cols, :] + hb_ref[j, cols, :]).T
            out_ref[0, rows, cols] = (_rms(hs, gain[:, cols]) * ox_ref[0, rows, cols].astype(F32)).astype(BF16)
        return carry

    lax.fori_loop(0, ncx, finish, 0)


def _mlstm(qkx, vtx, ox, gcx, grx, qkc, vtc, grc, gain):
    bsz, s, _ = qkx.shape
    ncx, ncc = s // ML_CHUNK, qkc.shape[1] // ML_CHUNK
    per_b = lambda a: pl.BlockSpec((1,) + a.shape[1:], lambda b: (b,) + (0,) * (a.ndim - 1))
    ins = [qkx, vtx, ox, gcx, grx, qkc, vtc, grc]
    return pl.pallas_call(
        functools.partial(_mlstm_kernel, ncx=ncx, ncc=ncc),
        grid=(bsz,),
        in_specs=[per_b(a) for a in ins] + [pl.BlockSpec((1, ML_WIDTH), lambda b: (0, 0))],
        out_specs=pl.BlockSpec((1, s, ML_WIDTH), lambda b: (b, 0, 0)),
        out_shape=jax.ShapeDtypeStruct((bsz, s, ML_WIDTH), BF16),
        scratch_shapes=[pltpu.VMEM((ncx, ML_WIDTH, ML_CHUNK), F32), pltpu.VMEM((ncx, ML_WIDTH, ML_CHUNK), F32),
                        pltpu.VMEM((2, ML_HEADS * ML_STATE_ROWS, ML_QK_WIDTH), F32),
                        pltpu.VMEM((ML_GATES, ML_CHUNK), F32)],
        compiler_params=pltpu.CompilerParams(dimension_semantics=("arbitrary",),
                                             vmem_limit_bytes=VMEM_LIMIT),
        name="mlstm",
    )(*ins, gain)


def _out_ffn_kernel(x_ref, att_ref, ml_ref, gtm_ref, shf_ref, scf_ref, gtf_ref,
                    gpm_ref, gpf_ref, gqf_ref, woa_ref, wom_ref, wfi_ref, wfo_ref, o_ref, *, hidden):
    mix = _dot(att_ref[...], woa_ref[...]) + _dot(ml_ref[...], wom_ref[...])
    x1 = x_ref[...] + gtm_ref[0] * _rms(mix, gpm_ref[...])
    h = (_rms(x1, gpf_ref[...]) * (1.0 + scf_ref[0]) + shf_ref[0]).astype(BF16)
    gu = _dot(h, wfi_ref[...])
    act = (_silu(gu[:, :hidden]) * gu[:, hidden:]).astype(BF16)
    fx = _dot(act, wfo_ref[...])
    o_ref[...] = x1 + gtf_ref[0] * _rms(fx, gqf_ref[...])


def _out_ffn(x2, att2, ml2, mod3, g_post_mix, g_pre_ffn, g_post_ffn, woa, wom, wfi, wfo, *, tiles_per_batch):
    t, d = x2.shape
    tm = FFN_TM
    hidden = wfo.shape[0]
    resident = lambda a: pl.BlockSpec(a.shape, lambda i: (0,) * a.ndim, pipeline_mode=pl.Buffered(1))
    mod = lambda k: pl.BlockSpec((1, 1, d), lambda i: (i // tiles_per_batch, 0, k))
    row = pl.BlockSpec((1, d), lambda i: (0, 0))
    return pl.pallas_call(
        functools.partial(_out_ffn_kernel, hidden=hidden),
        grid=(t // tm,),
        in_specs=[pl.BlockSpec((tm, d), lambda i: (i, 0)),
                  pl.BlockSpec((tm, ATT_WIDTH), lambda i: (i, 0)),
                  pl.BlockSpec((tm, ML_WIDTH), lambda i: (i, 0)),
                  mod(2), mod(3), mod(4), mod(5), row, row, row,
                  resident(woa), resident(wom), resident(wfi), resident(wfo)],
        out_specs=pl.BlockSpec((tm, d), lambda i: (i, 0)),
        out_shape=jax.ShapeDtypeStruct((t, d), F32),
        compiler_params=pltpu.CompilerParams(dimension_semantics=("arbitrary",),
                                             vmem_limit_bytes=VMEM_LIMIT),
        name="out_ffn",
    )(x2, att2, ml2, mod3, mod3, mod3, mod3, g_post_mix, g_pre_ffn, g_post_ffn, woa, wom, wfi, wfo)


def _rope_tables(n_tokens):
    pos = jnp.arange(n_tokens)
    row = (pos // GRID_W).astype(F32)
    col = (pos % GRID_W).astype(F32)
    half = ATT_HEAD_DIM // 4
    inv_freq = jnp.power(ROPE_BASE, -jnp.arange(half, dtype=F32) / half)
    ang_r = row[:, None] * inv_freq
    ang_c = col[:, None] * inv_freq
    z = jnp.zeros_like(ang_r)
    reps = LANES // ATT_HEAD_DIM
    cos = jnp.tile(jnp.concatenate([jnp.cos(ang_r)] * 2 + [jnp.cos(ang_c)] * 2, axis=1), (1, reps))
    sin_lo = jnp.tile(jnp.concatenate([-jnp.sin(ang_r), z, -jnp.sin(ang_c), z], axis=1), (1, reps))
    sin_hi = jnp.tile(jnp.concatenate([z, jnp.sin(ang_r), z, jnp.sin(ang_c)], axis=1), (1, reps))
    return cos, sin_lo, sin_hi


def _permute_heads(w, axis):
    shape = w.shape
    split = shape[:axis] + (ATT_HEADS, ATT_HEAD_DIM) + shape[axis + 1:]
    return jnp.take(w.reshape(split), jnp.array(_HEAD_PERM), axis=axis).reshape(shape)


def kernel(x, c, ctx, c_ctx, w_ada, b_ada, g_pre_mix, w_in, w_conv_qk, b_gates, attn_sink,
           g_mlstm_out, w_out, g_post_mix, g_pre_ffn, w_ffn_in, w_ffn_out, g_post_ffn):
    bsz, s, d = x.shape
    l = ctx.shape[1]
    assert w_ada.shape[0] == 1, "single-layer block"
    assert bsz < MOD_ROWS and s % INPROJ_TM == 0 and l % ML_CHUNK == 0 and s % FFN_TM == 0

    cc = jnp.zeros((MOD_ROWS, d), F32).at[:bsz].set(c).at[bsz].set(c_ctx)
    mod3 = _ada(cc, w_ada[0], b_ada).reshape(MOD_ROWS, 1, 6 * d)

    w = w_in[0]
    o_q, o_k, o_v = 0, ATT_WIDTH, ATT_WIDTH + ATT_KV_WIDTH
    o_mq = o_v + ATT_KV_WIDTH
    o_mv = o_mq + 2 * ML_QK_WIDTH
    o_mo = o_mv + ML_WIDTH
    o_mg = o_mo + ML_WIDTH
    w_q = _permute_heads(w[:, o_q:o_k], 1) * (ATT_HEAD_DIM ** -0.5)
    w_g = jnp.pad(w[:, o_mg:], ((0, 0), (0, LANES - ML_GATES)))
    shared = [w[:, o_k:o_v], w_g, w[:, o_mq:o_mv]]
    w_lat = jnp.concatenate([w_q] + shared + [w[:, o_mo:o_mg]], axis=1).astype(BF16)
    w_ctx = jnp.concatenate(shared, axis=1).astype(BF16)
    wvt = jnp.concatenate([w[:, o_v:o_mq], w[:, o_mv:o_mo]], axis=1).T.astype(BF16)
    wgt = w[:, o_mg:].T.astype(BF16)
    bg_row = jnp.pad(b_gates, ((0, 0), (0, LANES - ML_GATES)))
    bg_col = b_gates.reshape(ML_GATES, 1)
    wc = w_conv_qk[0]

    q, k, vt, qkx, vtx, ox, gcx, grx = _inproj(
        x, mod3, lambda b: b, g_pre_mix, w_lat, wvt, wgt, bg_row, bg_col, wc, _rope_tables(s),
        tm=INPROJ_TM, latent=True)
    kc, vct, qkc, vtc, _, grc = _inproj(
        ctx, mod3, lambda b: bsz, g_pre_mix, w_ctx, wvt, wgt, bg_row, bg_col, wc, None,
        tm=l, latent=False)

    sink_row = jnp.repeat(attn_sink[0][jnp.array(_HEAD_PERM)], ATT_BLOCK)[None, :]
    att = _attention(q, k, vt, kc, vct, sink_row)
    ml = _mlstm(qkx, vtx, ox, gcx, grx, qkc, vtc, grc, g_mlstm_out)

    wo = w_out[0]
    woa = _permute_heads(wo[:ATT_WIDTH], 0).astype(BF16)
    wom = wo[ATT_WIDTH:].astype(BF16)
    out = _out_ffn(x.reshape(bsz * s, d), att.reshape(bsz * s, ATT_WIDTH), ml.reshape(bsz * s, ML_WIDTH),
                   mod3, g_post_mix, g_pre_ffn, g_post_ffn, woa, wom,
                   w_ffn_in[0].astype(BF16), w_ffn_out[0].astype(BF16), tiles_per_batch=s // FFN_TM)
    return out.reshape(bsz, s, d)
```

```python
import functools

import jax
import jax.numpy as jnp
from jax import lax
from jax.experimental import pallas as pl
from jax.experimental.pallas import tpu as pltpu

F32 = jnp.float32
BF16 = jnp.bfloat16

EPS = 1e-6
GRID_W = 64
ROPE_BASE = 10000.0

ATT_HEADS = 8
ATT_KV_HEADS = 2
ATT_GROUP = ATT_HEADS // ATT_KV_HEADS
ATT_HEAD_DIM = 64
ATT_BLOCK = 128
ATT_WIDTH = ATT_HEADS * ATT_HEAD_DIM
ATT_KV_WIDTH = ATT_KV_HEADS * ATT_HEAD_DIM

ML_HEADS = 4
ML_V_DIM = 128
ML_QK_DIM = 64
ML_WIDTH = ML_HEADS * ML_V_DIM
ML_QK_WIDTH = ML_HEADS * ML_QK_DIM
ML_GATES = 4 * ML_HEADS
ML_CHUNK = 128

LANES = 128
SUBLANES = 8
VMEM_LIMIT = 56 * 1024 * 1024

INPROJ_TM = 512
FFN_TM = 256
ADA_TN = 1536
MOD_ROWS = 16

_HEAD_PERM = tuple(h * ATT_GROUP + g for g in range(ATT_GROUP) for h in range(ATT_KV_HEADS))


def _silu(v):
    return v * jax.nn.sigmoid(v)


def _log_sigmoid(v):
    return jnp.minimum(v, 0.0) - jnp.log1p(jnp.exp(-jnp.abs(v)))


def _rms(v, g):
    return v * lax.rsqrt(jnp.mean(v * v, axis=-1, keepdims=True) + EPS) * g


def _dot(a, b):
    return jnp.dot(a, b, preferred_element_type=F32)


def _dot_nt(a, b):
    return lax.dot_general(a, b, (((1,), (1,)), ((), ())), preferred_element_type=F32)


def _ada_kernel(cc_ref, w_ref, b_ref, o_ref):
    a = _silu(cc_ref[...])
    o_ref[...] = _dot(a.astype(BF16), w_ref[...].astype(BF16)) + b_ref[...]


def _ada(cc, w, b):
    d, n = w.shape
    return pl.pallas_call(
        _ada_kernel,
        grid=(n // ADA_TN,),
        in_specs=[pl.BlockSpec((MOD_ROWS, d), lambda j: (0, 0)),
                  pl.BlockSpec((d, ADA_TN), lambda j: (0, j)),
                  pl.BlockSpec((1, ADA_TN), lambda j: (0, j))],
        out_specs=pl.BlockSpec((MOD_ROWS, ADA_TN), lambda j: (0, j)),
        out_shape=jax.ShapeDtypeStruct((MOD_ROWS, n), F32),
        compiler_params=pltpu.CompilerParams(dimension_semantics=("arbitrary",),
                                             vmem_limit_bytes=VMEM_LIMIT),
        name="ada",
    )(cc, w, b)


def _rope(v, cos, sin_lo, sin_hi):
    return (v * cos + pltpu.roll(v, LANES - 16, axis=1) * sin_lo
            + pltpu.roll(v, 16, axis=1) * sin_hi)


def _inproj_kernel(*refs, tm, n_tiles, latent):
    if latent:
        (x_ref, xp_ref, xn_ref, sh_ref, sc_ref, g_ref, w_ref, wvt_ref, wgt_ref, bgr_ref, bgc_ref,
         wc_ref, cos_ref, sl_ref, shi_ref,
         q_ref, k_ref, vt_ref, qk_ref, mvt_ref, o_ref, gcol_ref, grow_ref) = refs
    else:
        (x_ref, xp_ref, xn_ref, sh_ref, sc_ref, g_ref, w_ref, wvt_ref, wgt_ref, bgr_ref, bgc_ref,
         wc_ref,
         k_ref, vt_ref, qk_ref, mvt_ref, gcol_ref, grow_ref) = refs
    i = pl.program_id(0)

    xt = jnp.concatenate([x_ref[0], xp_ref[0], xn_ref[0]], axis=0)
    ms = jnp.mean(xt * xt, axis=-1, keepdims=True)
    scale = g_ref[...] * (1.0 + sc_ref[0])
    hb = (xt * lax.rsqrt(ms + EPS) * scale + sh_ref[0]).astype(BF16)
    r = _dot(hb, w_ref[...])
    hm = hb[:tm]

    c = 0
    if latent:
        cos, sl, shi = cos_ref[...], sl_ref[...], shi_ref[...]
        for g in range(ATT_WIDTH // LANES):
            q_ref[0, :, g * LANES:(g + 1) * LANES] = _rope(
                r[:tm, c + g * LANES:c + (g + 1) * LANES], cos, sl, shi).astype(BF16)
        c += ATT_WIDTH
        k_ref[0] = _rope(r[:tm, c:c + LANES], cos, sl, shi).astype(BF16)
    else:
        k_ref[0] = r[:tm, c:c + LANES].astype(BF16)
    c += ATT_KV_WIDTH

    gcol_ref[0] = r[:tm, c:c + LANES] + bgr_ref[...]
    c += LANES
    g_t = _dot_nt(wgt_ref[...], hm) + bgc_ref[...]
    for j in range(tm // ML_CHUNK):
        grow_ref[0, j] = g_t[:, j * ML_CHUNK:(j + 1) * ML_CHUNK]

    v_t = _dot_nt(wvt_ref[...], hm)
    vt_ref[0] = v_t[:ATT_KV_WIDTH].astype(BF16)
    for j in range(tm // ML_CHUNK):
        mvt_ref[0, j] = v_t[ATT_KV_WIDTH:, j * ML_CHUNK:(j + 1) * ML_CHUNK].astype(BF16)

    y = r[:, c:c + 2 * ML_QK_WIDTH]
    c += 2 * ML_QK_WIDTH
    ym = y[:tm]
    keep_prev = jnp.where(i == 0, 0.0, 1.0)
    keep_next = jnp.where(i == n_tiles - 1, 0.0, 1.0)
    row = lax.broadcasted_iota(jnp.int32, (tm, 1), 0)
    prev = jnp.where(row == 0, y[tm + SUBLANES - 1:tm + SUBLANES] * keep_prev,
                     pltpu.roll(ym, 1, axis=0))
    nxt = jnp.where(row == tm - 1, y[tm + SUBLANES:tm + SUBLANES + 1] * keep_next,
                    pltpu.roll(ym, tm - 1, axis=0))
    wc = wc_ref[...]
    act = _silu(prev * wc[0:1] + ym * wc[1:2] + nxt * wc[2:3])
    qk_ref[0, :, :ML_QK_WIDTH] = (act[:, :ML_QK_WIDTH] * (ML_QK_DIM ** -0.5)).astype(BF16)
    qk_ref[0, :, ML_QK_WIDTH:] = act[:, ML_QK_WIDTH:].astype(BF16)

    if latent:
        o_ref[0] = jax.nn.sigmoid(r[:tm, c:c + ML_WIDTH]).astype(BF16)


def _inproj(x, mod3, mod_row, g_pre, w_main, wvt, wgt, bg_row, bg_col, wc, rope_tabs, *, tm, latent):
    bsz, t, d = x.shape
    n_tiles = t // tm
    hb = tm // SUBLANES
    n_hblk = t // SUBLANES
    n = w_main.shape[1]

    def const(shape):
        return pl.BlockSpec(shape, lambda i, b: (0,) * len(shape))

    in_specs = [
        pl.BlockSpec((1, tm, d), lambda i, b: (b, i, 0)),
        pl.BlockSpec((1, SUBLANES, d), lambda i, b: (b, jnp.maximum(i * hb - 1, 0), 0)),
        pl.BlockSpec((1, SUBLANES, d), lambda i, b: (b, jnp.minimum((i + 1) * hb, n_hblk - 1), 0)),
        pl.BlockSpec((1, 1, d), lambda i, b: (mod_row(b), 0, 0)),
        pl.BlockSpec((1, 1, d), lambda i, b: (mod_row(b), 0, 1)),
        const((1, d)), const((d, n)), const((ATT_KV_WIDTH + ML_WIDTH, d)), const((ML_GATES, d)),
        const((1, LANES)), const((ML_GATES, 1)), const((3, 2 * ML_QK_WIDTH)),
    ]
    args = [x, x, x, mod3, mod3, g_pre, w_main, wvt, wgt, bg_row, bg_col, wc]
    tok = lambda width, dt: (pl.BlockSpec((1, tm, width), lambda i, b: (b, i, 0)),
                             jax.ShapeDtypeStruct((bsz, t, width), dt))
    chunked = lambda rows, dt: (pl.BlockSpec((1, tm // ML_CHUNK, rows, ML_CHUNK), lambda i, b: (b, i, 0, 0)),
                                jax.ShapeDtypeStruct((bsz, t // ML_CHUNK, rows, ML_CHUNK), dt))
    outs = []
    if latent:
        in_specs += [pl.BlockSpec((tm, LANES), lambda i, b: (i, 0))] * 3
        args += list(rope_tabs)
        outs.append(tok(ATT_WIDTH, BF16))
    outs.append(tok(ATT_KV_WIDTH, BF16))
    outs.append((pl.BlockSpec((1, LANES, tm), lambda i, b: (b, 0, i)),
                 jax.ShapeDtypeStruct((bsz, ATT_KV_WIDTH, t), BF16)))
    outs.append(tok(2 * ML_QK_WIDTH, BF16))
    outs.append(chunked(ML_WIDTH, BF16))
    if latent:
        outs.append(tok(ML_WIDTH, BF16))
    outs.append(tok(LANES, F32))
    outs.append(chunked(ML_GATES, F32))
    return pl.pallas_call(
        functools.partial(_inproj_kernel, tm=tm, n_tiles=n_tiles, latent=latent),
        grid=(n_tiles, bsz),
        in_specs=in_specs,
        out_specs=[o[0] for o in outs],
        out_shape=[o[1] for o in outs],
        compiler_params=pltpu.CompilerParams(dimension_semantics=("arbitrary", "arbitrary"),
                                             vmem_limit_bytes=VMEM_LIMIT),
        name="inproj_latent" if latent else "inproj_context",
    )(*args)


def _attn_kernel(q_ref, kp_ref, kc_ref, kn_ref, kx_ref, vp_ref, vc_ref, vn_ref, vx_ref, sink_ref,
                 o_ref, *, nb):
    n = pl.program_id(1)
    blk = ATT_BLOCK
    q = q_ref[0]
    lane = lax.broadcasted_iota(jnp.int32, (blk, LANES), 1)
    zero = jnp.zeros((blk, LANES), BF16)
    parts = []
    for g in range(ATT_GROUP):
        slab = q[:, g * LANES:(g + 1) * LANES]
        parts.append(jnp.where(lane < ATT_HEAD_DIM, slab, zero))
        parts.append(jnp.where(lane >= ATT_HEAD_DIM, slab, zero))
    qs = jnp.concatenate(parts, axis=0)
    ncol = ATT_HEADS * blk

    key = lax.broadcasted_iota(jnp.int32, (blk, ncol), 0)
    qry = lax.broadcasted_iota(jnp.int32, (blk, ncol), 1) % blk
    s_p = jnp.where((key >= qry) & (n > 0), _dot_nt(kp_ref[0], qs), -jnp.inf)
    s_c = _dot_nt(kc_ref[0], qs)
    s_n = jnp.where((key <= qry) & (n < nb - 1), _dot_nt(kn_ref[0], qs), -jnp.inf)
    s_x = _dot_nt(kx_ref[0], qs)
    sink = sink_ref[...]

    def cmax(s):
        return jnp.max(s, axis=0, keepdims=True)

    m = jnp.maximum(jnp.maximum(jnp.maximum(cmax(s_p), cmax(s_c)), jnp.maximum(cmax(s_n), cmax(s_x))),
                    sink)
    ps = [jnp.exp(s - m) for s in (s_p, s_c, s_n, s_x)]
    den = sum(jnp.sum(p, axis=0, keepdims=True) for p in ps) + jnp.exp(sink - m)
    p = jnp.concatenate([p.astype(BF16) for p in ps], axis=0)
    vt = jnp.concatenate([vp_ref[0], vc_ref[0], vn_ref[0], vx_ref[0]], axis=1)
    ot = _dot(vt, p) * (1.0 / den)

    dim = lax.broadcasted_iota(jnp.int32, (LANES, blk), 0)
    for g in range(ATT_GROUP):
        a = ot[:, (2 * g) * blk:(2 * g + 1) * blk]
        b = ot[:, (2 * g + 1) * blk:(2 * g + 2) * blk]
        o_ref[0, :, g * LANES:(g + 1) * LANES] = jnp.where(dim < ATT_HEAD_DIM, a, b).T.astype(BF16)


def _attention(q, k, vt, kx, vxt, sink_row):
    bsz, s, _ = q.shape
    l = kx.shape[1]
    nb = s // ATT_BLOCK
    blk = ATT_BLOCK
    prev = lambda n: jnp.maximum(n - 1, 0)
    nxt = lambda n: jnp.minimum(n + 1, nb - 1)
    kspec = lambda f: pl.BlockSpec((1, blk, LANES), lambda b, n: (b, f(n), 0))
    vspec = lambda f: pl.BlockSpec((1, LANES, blk), lambda b, n: (b, 0, f(n)))
    same = lambda n: n
    return pl.pallas_call(
        functools.partial(_attn_kernel, nb=nb),
        grid=(bsz, nb),
        in_specs=[pl.BlockSpec((1, blk, ATT_WIDTH), lambda b, n: (b, n, 0)),
                  kspec(prev), kspec(same), kspec(nxt),
                  pl.BlockSpec((1, l, LANES), lambda b, n: (b, 0, 0)),
                  vspec(prev), vspec(same), vspec(nxt),
                  pl.BlockSpec((1, LANES, l), lambda b, n: (b, 0, 0)),
                  pl.BlockSpec((1, ATT_HEADS * blk), lambda b, n: (0, 0))],
        out_specs=pl.BlockSpec((1, blk, ATT_WIDTH), lambda b, n: (b, n, 0)),
        out_shape=jax.ShapeDtypeStruct((bsz, s, ATT_WIDTH), BF16),
        compiler_params=pltpu.CompilerParams(dimension_semantics=("arbitrary", "arbitrary"),
                                             vmem_limit_bytes=VMEM_LIMIT),
        name="attention",
    )(q, k, k, k, kx, vt, vt, vt, vxt, sink_row)


ML_STATE_ROWS = ML_V_DIM + SUBLANES


def _mlstm_kernel(qkx_ref, vtx_ref, ox_ref, gcx_ref, grx_ref, qkc_ref, vtc_ref, grc_ref,
                  gain_ref, out_ref, sin_ref, st_ref, ucol_ref, mrun_ref, wint_ref, floor_ref,
                  wkey_ref, decay_ref, *, ncx, ncc):
    lc = ML_CHUNK
    qkw = ML_QK_WIDTH
    sr = ML_STATE_ROWS
    ng = ML_GATES
    hg = ML_GATES // 2
    st_ref[...] = jnp.zeros(st_ref.shape, F32)

    rr = lax.broadcasted_iota(jnp.int32, (lc, lc), 0)
    cc = lax.broadcasted_iota(jnp.int32, (lc, lc), 1)
    tril = rr >= cc
    triu = rr <= cc
    tril_b = jnp.where(tril, 1.0, 0.0).astype(BF16)
    triu_b = jnp.where(triu, 1.0, 0.0).astype(BF16)
    head_of_lane = lax.broadcasted_iota(jnp.int32, (lc, qkw), 1) // ML_QK_DIM
    own_block = jnp.concatenate(
        [lax.broadcasted_iota(jnp.int32, (sr, qkw), 1) // ML_QK_DIM == h for h in range(ML_HEADS)], axis=0)
    pad_rows = jnp.zeros((SUBLANES - 1, lc), F32)

    def split(v):
        hi = v.astype(BF16)
        return hi, (v - hi.astype(F32)).astype(BF16)

    def gate_rows(g):
        n16 = g.shape[0]
        fwd_row = (lax.broadcasted_iota(jnp.int32, g.shape, 0) & (ng - 1)) < hg
        lane = lax.broadcasted_iota(jnp.int32, g.shape, 1)
        hi, lo = split(_log_sigmoid(g))
        cat = jnp.concatenate([hi, lo], axis=0)
        bu = _dot(cat, triu_b)
        bl = _dot(cat, tril_b)
        b = jnp.where(fwd_row, bu[:n16] + bu[n16:], bl[:n16] + bl[n16:])
        b = pltpu.roll(b, n16 - ML_HEADS, axis=0)
        u = g - b
        run_f = run_b = u
        k = 1
        while k < lc:
            run_f = jnp.maximum(run_f, jnp.where(lane >= k, pltpu.roll(run_f, k, axis=1), -jnp.inf))
            run_b = jnp.maximum(run_b, jnp.where(lane < lc - k, pltpu.roll(run_b, lc - k, axis=1), -jnp.inf))
            k *= 2
        run = jnp.where(fwd_row, run_f, run_b)

        def at_end(a):
            return jnp.where(fwd_row, jnp.broadcast_to(a[:, lc - 1:lc], a.shape),
                             jnp.broadcast_to(a[:, 0:1], a.shape))

        return u, b, run, at_end(b), at_end(run)

    def derived(u, b, run, b_tot, run_end, m_in):
        m_run = jnp.maximum(run, m_in)
        m_out = b_tot + jnp.maximum(run_end, m_in)
        return (m_run, jnp.exp(m_in - m_run), jnp.exp(-(b + m_run)),
                jnp.exp(b_tot + u - m_out), jnp.exp(b_tot + m_in - m_out))

    gc_rows = gate_rows(grc_ref[0].reshape(ncc * ng, lc))
    gx_rows = gate_rows(grx_ref[0].reshape(ncx * ng, lc))

    def scan_m(rows, n, m_f, m_b):
        _, _, _, b_tot, run_end = rows
        part = lambda a, c, d: a[c * ng + d * hg:c * ng + (d + 1) * hg]
        ins_f, ins_b = [], [None] * n
        for c in range(n):
            ins_f.append(m_f)
            m_f = part(b_tot, c, 0) + jnp.maximum(part(run_end, c, 0), m_f)
        for c in reversed(range(n)):
            ins_b[c] = m_b
            m_b = part(b_tot, c, 1) + jnp.maximum(part(run_end, c, 1), m_b)
        return jnp.concatenate([x for c in range(n) for x in (ins_f[c], ins_b[c])], axis=0), m_f, m_b

    m0 = jnp.zeros((hg, lc), F32)
    m_in_c, m_f, m_b = scan_m(gc_rows, ncc, m0, m0)
    m_in_x, _, _ = scan_m(gx_rows, ncx, m_f, m_b)
    _, _, _, wkey_c, decay_c = derived(*gc_rows, m_in_c)
    for ref, val in zip((mrun_ref, wint_ref, floor_ref, wkey_ref, decay_ref), derived(*gx_rows, m_in_x)):
        ref[...] = val

    fwd_col = (lax.broadcasted_iota(jnp.int32, (lc, LANES), 1) & (ng - 1)) < hg

    def token_major_u(j):
        rows = pl.ds(pl.multiple_of(j * lc, lc), lc)
        gcol = gcx_ref[0, rows, :]
        hi, lo = split(_log_sigmoid(gcol))
        cat = jnp.concatenate([hi, lo], axis=1)
        bl = _dot(tril_b, cat)
        bu = _dot(triu_b, cat)
        b = jnp.where(fwd_col, bl[:, :LANES] + bl[:, LANES:], bu[:, :LANES] + bu[:, LANES:])
        ucol_ref[rows, :] = gcol - pltpu.roll(b, LANES - ML_HEADS, axis=1)

    def advance(dirn, k4, vt, w_key, decay):
        st = st_ref[dirn]
        pieces, decays = [], []
        for h in range(ML_HEADS):
            c = hg * dirn + h
            pieces += [vt[h * ML_V_DIM:(h + 1) * ML_V_DIM].astype(F32) * w_key[c:c + 1], w_key[c:c + 1], pad_rows]
            decays.append(jnp.broadcast_to(jnp.concatenate([decay[c:c + 1]] * (qkw // lc), axis=1), (sr, qkw)))
        upd = _dot(jnp.concatenate(pieces, axis=0).astype(BF16), k4)
        st_ref[dirn] = jnp.concatenate(decays, axis=0) * st + jnp.where(own_block, upd, 0.0)

    for j in range(ncc):
        for dirn, cj in ((0, j), (1, ncc - 1 - j)):
            grows = slice(cj * ng, (cj + 1) * ng)
            advance(dirn, qkc_ref[0, cj * lc:(cj + 1) * lc, qkw:], vtc_ref[0, cj], wkey_c[grows], decay_c[grows])

    def scan_body(j, carry):
        for dirn, cj in ((0, j), (1, ncx - 1 - j)):
            rows = pl.ds(pl.multiple_of(cj * lc, lc), lc)
            grows = pl.ds(pl.multiple_of(cj * ng, ng), ng)
            sin_ref[dirn, cj] = st_ref[dirn].astype(BF16)
            advance(dirn, qkx_ref[0, rows, qkw:], vtx_ref[0, cj], wkey_ref[grows, :], decay_ref[grows, :])
        token_major_u(j)
        return carry

    lax.fori_loop(0, ncx, scan_body, 0, unroll=2)

    gain = gain_ref[...]

    def out_body(j, carry):
        rows = pl.ds(pl.multiple_of(j * lc, lc), lc)
        grows = pl.ds(pl.multiple_of(j * ng, ng), ng)
        q4 = qkx_ref[0, rows, :qkw]
        k4 = qkx_ref[0, rows, qkw:]
        vt = vtx_ref[0, j]
        u_col = ucol_ref[rows, :]
        m_run, w_int, floor = mrun_ref[grows, :], wint_ref[grows, :], floor_ref[grows, :]
        zero = jnp.zeros_like(q4)
        qs = jnp.concatenate([jnp.where(head_of_lane == h, q4, zero) for h in range(ML_HEADS)], axis=0)
        qk_t = _dot_nt(k4, qs)
        hsum = [None] * ML_HEADS
        for dirn in range(2):
            inter = _dot_nt(sin_ref[dirn, j], q4)
            valid = triu if dirn == 0 else tril
            for h in range(ML_HEADS):
                c = hg * dirn + h
                e = jnp.exp(jnp.where(valid, u_col[:, c:c + 1] - m_run[c:c + 1], -jnp.inf))
                s_t = qk_t[:, h * lc:(h + 1) * lc] * e
                num = _dot(vt[h * ML_V_DIM:(h + 1) * ML_V_DIM], s_t.astype(BF16))
                num = num + w_int[c:c + 1] * inter[h * sr:h * sr + ML_V_DIM]
                den = (jnp.sum(s_t, axis=0, keepdims=True)
                       + w_int[c:c + 1] * inter[h * sr + ML_V_DIM:h * sr + ML_V_DIM + 1])
                hv = num / jnp.maximum(jnp.abs(den), floor[c:c + 1])
                hsum[h] = hv if dirn == 0 else hsum[h] + hv
        for h in range(ML_HEADS):
            cols = slice(h * ML_V_DIM, (h + 1) * ML_V_DIM)
            out_ref[0, rows, cols] = (_rms(hsum[h].T, gain[:, cols])
                                      * ox_ref[0, rows, cols].astype(F32)).astype(BF16)
        return carry

    lax.fori_loop(0, ncx, out_body, 0, unroll=4)


def _mlstm(qkx, vtx, ox, gcx, grx, qkc, vtc, grc, gain):
    bsz, s, _ = qkx.shape
    ncx, ncc = s // ML_CHUNK, qkc.shape[1] // ML_CHUNK
    per_b = lambda a: pl.BlockSpec((1,) + a.shape[1:], lambda b: (b,) + (0,) * (a.ndim - 1))
    ins = [qkx, vtx, ox, gcx, grx, qkc, vtc, grc]
    return pl.pallas_call(
        functools.partial(_mlstm_kernel, ncx=ncx, ncc=ncc),
        grid=(bsz,),
        in_specs=[per_b(a) for a in ins] + [pl.BlockSpec((1, ML_WIDTH), lambda b: (0, 0))],
        out_specs=pl.BlockSpec((1, s, ML_WIDTH), lambda b: (b, 0, 0)),
        out_shape=jax.ShapeDtypeStruct((bsz, s, ML_WIDTH), BF16),
        scratch_shapes=[pltpu.VMEM((2, ncx, ML_HEADS * ML_STATE_ROWS, ML_QK_WIDTH), BF16),
                        pltpu.VMEM((2, ML_HEADS * ML_STATE_ROWS, ML_QK_WIDTH), F32),
                        pltpu.VMEM((s, LANES), F32)]
                       + [pltpu.VMEM((ncx * ML_GATES, ML_CHUNK), F32)] * 5,
        compiler_params=pltpu.CompilerParams(dimension_semantics=("arbitrary",),
                                             vmem_limit_bytes=VMEM_LIMIT),
        name="mlstm",
    )(*ins, gain)


def _out_ffn_kernel(x_ref, att_ref, ml_ref, gtm_ref, shf_ref, scf_ref, gtf_ref,
                    gpm_ref, gpf_ref, gqf_ref, woa_ref, wom_ref, wfi_ref, wfo_ref, o_ref, *, hidden):
    mix = _dot(att_ref[...], woa_ref[...]) + _dot(ml_ref[...], wom_ref[...])
    x1 = x_ref[...] + gtm_ref[0] * _rms(mix, gpm_ref[...])
    h = (_rms(x1, gpf_ref[...]) * (1.0 + scf_ref[0]) + shf_ref[0]).astype(BF16)
    gu = _dot(h, wfi_ref[...])
    act = (_silu(gu[:, :hidden]) * gu[:, hidden:]).astype(BF16)
    fx = _dot(act, wfo_ref[...])
    o_ref[...] = x1 + gtf_ref[0] * _rms(fx, gqf_ref[...])


def _out_ffn(x2, att2, ml2, mod3, g_post_mix, g_pre_ffn, g_post_ffn, woa, wom, wfi, wfo, *, tiles_per_batch):
    t, d = x2.shape
    tm = FFN_TM
    hidden = wfo.shape[0]
    resident = lambda a: pl.BlockSpec(a.shape, lambda i: (0,) * a.ndim, pipeline_mode=pl.Buffered(1))
    mod = lambda k: pl.BlockSpec((1, 1, d), lambda i: (i // tiles_per_batch, 0, k))
    row = pl.BlockSpec((1, d), lambda i: (0, 0))
    return pl.pallas_call(
        functools.partial(_out_ffn_kernel, hidden=hidden),
        grid=(t // tm,),
        in_specs=[pl.BlockSpec((tm, d), lambda i: (i, 0)),
                  pl.BlockSpec((tm, ATT_WIDTH), lambda i: (i, 0)),
                  pl.BlockSpec((tm, ML_WIDTH), lambda i: (i, 0)),
                  mod(2), mod(3), mod(4), mod(5), row, row, row,
                  resident(woa), resident(wom), resident(wfi), resident(wfo)],
        out_specs=pl.BlockSpec((tm, d), lambda i: (i, 0)),
        out_shape=jax.ShapeDtypeStruct((t, d), F32),
        compiler_params=pltpu.CompilerParams(dimension_semantics=("arbitrary",),
                                             vmem_limit_bytes=VMEM_LIMIT),
        name="out_ffn",
    )(x2, att2, ml2, mod3, mod3, mod3, mod3, g_post_mix, g_pre_ffn, g_post_ffn, woa, wom, wfi, wfo)


def _rope_tables(n_tokens):
    pos = jnp.arange(n_tokens)
    row = (pos // GRID_W).astype(F32)
    col = (pos % GRID_W).astype(F32)
    half = ATT_HEAD_DIM // 4
    inv_freq = jnp.power(ROPE_BASE, -jnp.arange(half, dtype=F32) / half)
    ang_r = row[:, None] * inv_freq
    ang_c = col[:, None] * inv_freq
    z = jnp.zeros_like(ang_r)
    reps = LANES // ATT_HEAD_DIM
    cos = jnp.tile(jnp.concatenate([jnp.cos(ang_r)] * 2 + [jnp.cos(ang_c)] * 2, axis=1), (1, reps))
    sin_lo = jnp.tile(jnp.concatenate([-jnp.sin(ang_r), z, -jnp.sin(ang_c), z], axis=1), (1, reps))
    sin_hi = jnp.tile(jnp.concatenate([z, jnp.sin(ang_r), z, jnp.sin(ang_c)], axis=1), (1, reps))
    return cos, sin_lo, sin_hi


def _permute_heads(w, axis):
    shape = w.shape
    split = shape[:axis] + (ATT_HEADS, ATT_HEAD_DIM) + shape[axis + 1:]
    return jnp.take(w.reshape(split), jnp.array(_HEAD_PERM), axis=axis).reshape(shape)


def kernel(x, c, ctx, c_ctx, w_ada, b_ada, g_pre_mix, w_in, w_conv_qk, b_gates, attn_sink,
           g_mlstm_out, w_out, g_post_mix, g_pre_ffn, w_ffn_in, w_ffn_out, g_post_ffn):
    bsz, s, d = x.shape
    l = ctx.shape[1]
    assert w_ada.shape[0] == 1, "single-layer block"
    assert bsz < MOD_ROWS and s % INPROJ_TM == 0 and l % ML_CHUNK == 0 and s % FFN_TM == 0

    cc = jnp.zeros((MOD_ROWS, d), F32).at[:bsz].set(c).at[bsz].set(c_ctx)
    mod3 = _ada(cc, w_ada[0], b_ada).reshape(MOD_ROWS, 1, 6 * d)

    w = w_in[0]
    o_q, o_k, o_v = 0, ATT_WIDTH, ATT_WIDTH + ATT_KV_WIDTH
    o_mq = o_v + ATT_KV_WIDTH
    o_mv = o_mq + 2 * ML_QK_WIDTH
    o_mo = o_mv + ML_WIDTH
    o_mg = o_mo + ML_WIDTH
    w_q = _permute_heads(w[:, o_q:o_k], 1) * (ATT_HEAD_DIM ** -0.5)
    w_g = jnp.pad(w[:, o_mg:], ((0, 0), (0, LANES - ML_GATES)))
    shared = [w[:, o_k:o_v], w_g, w[:, o_mq:o_mv]]
    w_lat = jnp.concatenate([w_q] + shared + [w[:, o_mo:o_mg]], axis=1).astype(BF16)
    w_ctx = jnp.concatenate(shared, axis=1).astype(BF16)
    wvt = jnp.concatenate([w[:, o_v:o_mq], w[:, o_mv:o_mo]], axis=1).T.astype(BF16)
    wgt = w[:, o_mg:].T.astype(BF16)
    bg_row = jnp.pad(b_gates, ((0, 0), (0, LANES - ML_GATES)))
    bg_col = b_gates.reshape(ML_GATES, 1)
    wc = w_conv_qk[0]

    q, k, vt, qkx, vtx, ox, gcx, grx = _inproj(
        x, mod3, lambda b: b, g_pre_mix, w_lat, wvt, wgt, bg_row, bg_col, wc, _rope_tables(s),
        tm=INPROJ_TM, latent=True)
    kc, vct, qkc, vtc, _, grc = _inproj(
        ctx, mod3, lambda b: bsz, g_pre_mix, w_ctx, wvt, wgt, bg_row, bg_col, wc, None,
        tm=l, latent=False)

    sink_row = jnp.repeat(attn_sink[0][jnp.array(_HEAD_PERM)], ATT_BLOCK)[None, :]
    att = _attention(q, k, vt, kc, vct, sink_row)
    ml = _mlstm(qkx, vtx, ox, gcx, grx, qkc, vtc, grc, g_mlstm_out)

    wo = w_out[0]
    woa = _permute_heads(wo[:ATT_WIDTH], 0).astype(BF16)
    wom = wo[ATT_WIDTH:].astype(BF16)
    out = _out_ffn(x.reshape(bsz * s, d), att.reshape(bsz * s, ATT_WIDTH), ml.reshape(bsz * s, ML_WIDTH),
                   mod3, g_post_mix, g_pre_ffn, g_post_ffn, woa, wom,
                   w_ffn_in[0].astype(BF16), w_ffn_out[0].astype(BF16), tiles_per_batch=s // FFN_TM)
    return out.reshape(bsz, s, d)
```

```python
import functools

import jax
import jax.numpy as jnp
from jax import lax
from jax.experimental import pallas as pl
from jax.experimental.pallas import tpu as pltpu

F32 = jnp.float32
BF16 = jnp.bfloat16

EPS = 1e-6
GRID_W = 64
ROPE_BASE = 10000.0
LOG2_E = 1.4426950408889634

ATT_HEADS = 8
ATT_KV_HEADS = 2
ATT_GROUP = ATT_HEADS // ATT_KV_HEADS
ATT_HEAD_DIM = 64
ATT_BLOCK = 128
ATT_WIDTH = ATT_HEADS * ATT_HEAD_DIM
ATT_KV_WIDTH = ATT_KV_HEADS * ATT_HEAD_DIM
ATT_ONES_ROWS = 16

ML_HEADS = 4
ML_V_DIM = 128
ML_QK_DIM = 64
ML_WIDTH = ML_HEADS * ML_V_DIM
ML_QK_WIDTH = ML_HEADS * ML_QK_DIM
ML_GATES = 4 * ML_HEADS
ML_CHUNK = 128

LANES = 128
SUBLANES = 8
VMEM_LIMIT = 56 * 1024 * 1024

INPROJ_TM = 512
FFN_TM = 512
FFN_ROWS = 256
ADA_TN = 1536
MOD_ROWS = 16

_HEAD_PERM = tuple(h * ATT_GROUP + g for g in range(ATT_GROUP) for h in range(ATT_KV_HEADS))


def _silu(v):
    return v * jax.nn.sigmoid(v)


def _log_sigmoid(v):
    return jnp.minimum(v, 0.0) - jnp.log1p(jnp.exp(-jnp.abs(v)))


def _rms(v, g):
    return v * lax.rsqrt(jnp.mean(v * v, axis=-1, keepdims=True) + EPS) * g


def _dot(a, b):
    return jnp.dot(a, b, preferred_element_type=F32)


def _dot_nt(a, b):
    return lax.dot_general(a, b, (((1,), (1,)), ((), ())), preferred_element_type=F32)


def _ada_kernel(cc_ref, w_ref, b_ref, o_ref):
    a = _silu(cc_ref[...])
    o_ref[...] = _dot(a.astype(BF16), w_ref[...].astype(BF16)) + b_ref[...]


def _ada(cc, w, b):
    d, n = w.shape
    return pl.pallas_call(
        _ada_kernel,
        grid=(n // ADA_TN,),
        in_specs=[pl.BlockSpec((MOD_ROWS, d), lambda j: (0, 0)),
                  pl.BlockSpec((d, ADA_TN), lambda j: (0, j)),
                  pl.BlockSpec((1, ADA_TN), lambda j: (0, j))],
        out_specs=pl.BlockSpec((MOD_ROWS, ADA_TN), lambda j: (0, j)),
        out_shape=jax.ShapeDtypeStruct((MOD_ROWS, n), F32),
        compiler_params=pltpu.CompilerParams(dimension_semantics=("arbitrary",),
                                             vmem_limit_bytes=VMEM_LIMIT),
        name="ada",
    )(cc, w, b)


def _rope(v, cos, sin_lo, sin_hi):
    return (v * cos + pltpu.roll(v, LANES - 16, axis=1) * sin_lo
            + pltpu.roll(v, 16, axis=1) * sin_hi)


def _inproj_kernel(*refs, tm, n_tiles, latent):
    if latent:
        (x_ref, xp_ref, xn_ref, sh_ref, sc_ref, g_ref, w_ref, wvt_ref, wgt_ref, bgr_ref, bgc_ref,
         wc_ref, cos_ref, sl_ref, shi_ref,
         q_ref, k_ref, vt_ref, qk_ref, mvt_ref, o_ref, gcol_ref, grow_ref) = refs
    else:
        (x_ref, xp_ref, xn_ref, sh_ref, sc_ref, g_ref, w_ref, wvt_ref, wgt_ref, bgr_ref, bgc_ref,
         wc_ref,
         k_ref, vt_ref, qk_ref, mvt_ref, gcol_ref, grow_ref) = refs
    i = pl.program_id(0)

    xt = jnp.concatenate([x_ref[0], xp_ref[0], xn_ref[0]], axis=0)
    ms = jnp.mean(xt * xt, axis=-1, keepdims=True)
    scale = g_ref[...] * (1.0 + sc_ref[0])
    hb = (xt * lax.rsqrt(ms + EPS) * scale + sh_ref[0]).astype(BF16)
    hm = hb[:tm]

    def project(lo, width, lhs):
        return _dot(lhs, w_ref[:, lo:lo + width])

    c = 0
    if latent:
        r_q = project(c, ATT_WIDTH, hm)
        c += ATT_WIDTH
    r_kg = project(c, ATT_KV_WIDTH + LANES, hm)
    c += ATT_KV_WIDTH + LANES
    if latent:
        cos, sl, shi = cos_ref[...], sl_ref[...], shi_ref[...]
        for g in range(ATT_WIDTH // LANES):
            q_ref[0, :, g * LANES:(g + 1) * LANES] = _rope(
                r_q[:, g * LANES:(g + 1) * LANES], cos, sl, shi).astype(BF16)

    y = project(c, 2 * ML_QK_WIDTH, hb)
    c += 2 * ML_QK_WIDTH
    if latent:
        k_ref[0] = _rope(r_kg[:, :LANES], cos, sl, shi).astype(BF16)
    else:
        k_ref[0] = r_kg[:, :LANES].astype(BF16)
    gcol_ref[0] = r_kg[:, LANES:] + bgr_ref[...]

    v_t = _dot_nt(wvt_ref[...], hm)

    ym = y[:tm]
    keep_prev = jnp.where(i == 0, 0.0, 1.0)
    keep_next = jnp.where(i == n_tiles - 1, 0.0, 1.0)
    row = lax.broadcasted_iota(jnp.int32, (tm, 1), 0)
    prev = jnp.where(row == 0, y[tm + SUBLANES - 1:tm + SUBLANES] * keep_prev,
                     pltpu.roll(ym, 1, axis=0))
    nxt = jnp.where(row == tm - 1, y[tm + SUBLANES:tm + SUBLANES + 1] * keep_next,
                    pltpu.roll(ym, tm - 1, axis=0))
    wc = wc_ref[...]
    act = _silu(prev * wc[0:1] + ym * wc[1:2] + nxt * wc[2:3])
    qk_ref[0, :, :ML_QK_WIDTH] = (act[:, :ML_QK_WIDTH] * (ML_QK_DIM ** -0.5)).astype(BF16)
    qk_ref[0, :, ML_QK_WIDTH:] = act[:, ML_QK_WIDTH:].astype(BF16)

    g_t = _dot_nt(wgt_ref[...], hm) + bgc_ref[...]
    if latent:
        r_o = project(c, ML_WIDTH, hm)
    vt_ref[0] = v_t[:ATT_KV_WIDTH].astype(BF16)
    for j in range(tm // ML_CHUNK):
        mvt_ref[0, j] = v_t[ATT_KV_WIDTH:, j * ML_CHUNK:(j + 1) * ML_CHUNK].astype(BF16)
        grow_ref[0, j] = g_t[:, j * ML_CHUNK:(j + 1) * ML_CHUNK]
    if latent:
        o_ref[0] = jax.nn.sigmoid(r_o).astype(BF16)


def _inproj(x, mod3, mod_row, g_pre, w_main, wvt, wgt, bg_row, bg_col, wc, rope_tabs, *, tm, latent):
    bsz, t, d = x.shape
    n_tiles = t // tm
    hb = tm // SUBLANES
    n_hblk = t // SUBLANES
    n = w_main.shape[1]

    def const(shape):
        return pl.BlockSpec(shape, lambda i, b: (0,) * len(shape))

    in_specs = [
        pl.BlockSpec((1, tm, d), lambda i, b: (b, i, 0)),
        pl.BlockSpec((1, SUBLANES, d), lambda i, b: (b, jnp.maximum(i * hb - 1, 0), 0)),
        pl.BlockSpec((1, SUBLANES, d), lambda i, b: (b, jnp.minimum((i + 1) * hb, n_hblk - 1), 0)),
        pl.BlockSpec((1, 1, d), lambda i, b: (mod_row(b), 0, 0)),
        pl.BlockSpec((1, 1, d), lambda i, b: (mod_row(b), 0, 1)),
        const((1, d)), const((d, n)), const((ATT_KV_WIDTH + ML_WIDTH, d)), const((ML_GATES, d)),
        const((1, LANES)), const((ML_GATES, 1)), const((3, 2 * ML_QK_WIDTH)),
    ]
    args = [x, x, x, mod3, mod3, g_pre, w_main, wvt, wgt, bg_row, bg_col, wc]
    tok = lambda width, dt: (pl.BlockSpec((1, tm, width), lambda i, b: (b, i, 0)),
                             jax.ShapeDtypeStruct((bsz, t, width), dt))
    chunked = lambda rows, dt: (pl.BlockSpec((1, tm // ML_CHUNK, rows, ML_CHUNK), lambda i, b: (b, i, 0, 0)),
                                jax.ShapeDtypeStruct((bsz, t // ML_CHUNK, rows, ML_CHUNK), dt))
    outs = []
    if latent:
        in_specs += [pl.BlockSpec((tm, LANES), lambda i, b: (i, 0))] * 3
        args += list(rope_tabs)
        outs.append(tok(ATT_WIDTH, BF16))
    outs.append(tok(ATT_KV_WIDTH, BF16))
    outs.append((pl.BlockSpec((1, LANES, tm), lambda i, b: (b, 0, i)),
                 jax.ShapeDtypeStruct((bsz, ATT_KV_WIDTH, t), BF16)))
    outs.append(tok(2 * ML_QK_WIDTH, BF16))
    outs.append(chunked(ML_WIDTH, BF16))
    if latent:
        outs.append(tok(ML_WIDTH, BF16))
    outs.append(tok(LANES, F32))
    outs.append(chunked(ML_GATES, F32))
    return pl.pallas_call(
        functools.partial(_inproj_kernel, tm=tm, n_tiles=n_tiles, latent=latent),
        grid=(n_tiles, bsz),
        in_specs=in_specs,
        out_specs=[o[0] for o in outs],
        out_shape=[o[1] for o in outs],
        compiler_params=pltpu.CompilerParams(dimension_semantics=("arbitrary", "arbitrary"),
                                             vmem_limit_bytes=VMEM_LIMIT),
        name="inproj_latent" if latent else "inproj_context",
    )(*args)


def _attn_kernel(q_ref, kp_ref, kc_ref, kn_ref, kx_ref, vp_ref, vc_ref, vn_ref, vx_ref, sink_ref,
                 o_ref, *, n_tiles):
    i = pl.program_id(1)
    blk = ATT_BLOCK
    lane = lax.broadcasted_iota(jnp.int32, (blk, LANES), 1)
    zero = jnp.zeros((blk, LANES), BF16)
    half_groups = ATT_GROUP // 2
    n_slots = 2 * half_groups
    sink = sink_ref[:, :n_slots * blk], sink_ref[:, n_slots * blk:]

    def stack_heads(q, half):
        parts = []
        for g in range(half * half_groups, (half + 1) * half_groups):
            slab = q[:, g * LANES:(g + 1) * LANES]
            parts.append(jnp.where(lane < ATT_HEAD_DIM, slab, zero))
            parts.append(jnp.where(lane >= ATT_HEAD_DIM, slab, zero))
        return jnp.concatenate(parts, axis=0)

    key = lax.broadcasted_iota(jnp.int32, (blk, blk), 0)
    qry = lax.broadcasted_iota(jnp.int32, (blk, blk), 1)
    ninf = jnp.full((blk, blk), -jnp.inf, F32)
    bias_prev = jnp.where(key >= qry, 0.0, ninf)
    bias_next = jnp.where(key <= qry, 0.0, ninf)
    slots = lambda b: jnp.concatenate([b] * n_slots, axis=1)

    def scores(qs, k_prev, k_cur, k_next):
        return [_dot_nt(k_prev, qs), _dot_nt(k_cur, qs), _dot_nt(k_next, qs), _dot_nt(kx_ref[0], qs)]

    def softmax(s, b_prev, b_next, sink_h):
        s = jnp.concatenate([s[0] + slots(b_prev), s[1], s[2] + slots(b_next), s[3]], axis=0)
        m = jnp.maximum(sink_h, jnp.max(s, axis=0, keepdims=True))
        return jnp.exp2(s - m).astype(BF16), jnp.exp2(sink_h - m)

    n_keys = 3 * blk + kx_ref.shape[1]
    ones_rows = jnp.ones((ATT_ONES_ROWS, n_keys), BF16)

    def weighted_values(p, v_prev, v_cur, v_next):
        vt = jnp.concatenate([v_prev, v_cur, v_next, vx_ref[0]], axis=1)
        return _dot(jnp.concatenate([vt, ones_rows], axis=0), p)

    dim = lax.broadcasted_iota(jnp.int32, (LANES, blk), 0)

    def emit(rows, half, ot, p_sink):
        ot = ot[:LANES] * (1.0 / (ot[LANES:LANES + 1] + p_sink))
        for j in range(half_groups):
            g = half * half_groups + j
            a = ot[:, (2 * j) * blk:(2 * j + 1) * blk]
            b = ot[:, (2 * j + 1) * blk:(2 * j + 2) * blk]
            o_ref[0, rows, g * LANES:(g + 1) * LANES] = jnp.where(dim < ATT_HEAD_DIM, a, b).T.astype(BF16)

    lo, hi = slice(0, blk), slice(blk, 2 * blk)
    k_lo, k_hi = kc_ref[0, lo, :], kc_ref[0, hi, :]
    v_lo, v_hi = vc_ref[0, :, lo], vc_ref[0, :, hi]
    rows_of = (lo, hi)
    keys = ((kp_ref[0], k_lo, k_hi), (k_lo, k_hi, kn_ref[0]))
    vals = ((vp_ref[0], v_lo, v_hi), (v_lo, v_hi, vn_ref[0]))
    bias = ((jnp.where(i == 0, ninf, bias_prev), bias_next),
            (bias_prev, jnp.where(i == n_tiles - 1, ninf, bias_next)))
    units = [(b, half) for b in range(2) for half in range(2)]

    def stage_scores(u):
        b, half = u
        return scores(stack_heads(q_ref[0, rows_of[b], :], half), *keys[b])

    def stage_softmax(u, s):
        b, half = u
        return softmax(s, *bias[b], sink[half])

    def stage_values(u, p):
        return weighted_values(p, *vals[u[0]])

    def stage_emit(u, ot, p_sink):
        emit(rows_of[u[0]], u[1], ot, p_sink)

    n_units = len(units)
    s, p, ot = {}, {}, {}
    s[0] = stage_scores(units[0])
    for t in range(n_units):
        if t + 1 < n_units:
            s[t + 1] = stage_scores(units[t + 1])
        p[t] = stage_softmax(units[t], s.pop(t))
        if t >= 1:
            ot[t - 1] = stage_values(units[t - 1], p[t - 1][0])
        if t >= 2:
            stage_emit(units[t - 2], ot.pop(t - 2), p.pop(t - 2)[1])
    ot[n_units - 1] = stage_values(units[n_units - 1], p[n_units - 1][0])
    for t in (n_units - 2, n_units - 1):
        stage_emit(units[t], ot.pop(t), p.pop(t)[1])


def _attention(q, k, vt, kx, vxt, sink_row):
    bsz, s, _ = q.shape
    l = kx.shape[1]
    blk = ATT_BLOCK
    n_tiles = s // (2 * blk)
    nb = s // blk
    prev = lambda i: jnp.maximum(2 * i - 1, 0)
    nxt = lambda i: jnp.minimum(2 * i + 2, nb - 1)
    return pl.pallas_call(
        functools.partial(_attn_kernel, n_tiles=n_tiles),
        grid=(bsz, n_tiles),
        in_specs=[pl.BlockSpec((1, 2 * blk, ATT_WIDTH), lambda b, i: (b, i, 0)),
                  pl.BlockSpec((1, blk, LANES), lambda b, i: (b, prev(i), 0)),
                  pl.BlockSpec((1, 2 * blk, LANES), lambda b, i: (b, i, 0)),
                  pl.BlockSpec((1, blk, LANES), lambda b, i: (b, nxt(i), 0)),
                  pl.BlockSpec((1, l, LANES), lambda b, i: (b, 0, 0)),
                  pl.BlockSpec((1, LANES, blk), lambda b, i: (b, 0, prev(i))),
                  pl.BlockSpec((1, LANES, 2 * blk), lambda b, i: (b, 0, i)),
                  pl.BlockSpec((1, LANES, blk), lambda b, i: (b, 0, nxt(i))),
                  pl.BlockSpec((1, LANES, l), lambda b, i: (b, 0, 0)),
                  pl.BlockSpec((1, ATT_HEADS * blk), lambda b, i: (0, 0))],
        out_specs=pl.BlockSpec((1, 2 * blk, ATT_WIDTH), lambda b, i: (b, i, 0)),
        out_shape=jax.ShapeDtypeStruct((bsz, s, ATT_WIDTH), BF16),
        compiler_params=pltpu.CompilerParams(dimension_semantics=("arbitrary", "arbitrary"),
                                             vmem_limit_bytes=VMEM_LIMIT),
        name="attention",
    )(q, k, k, k, kx, vt, vt, vt, vxt, sink_row)


ML_STATE_ROWS = ML_V_DIM + SUBLANES


def _mlstm_kernel(qkx_ref, vtx_ref, ox_ref, gcx_ref, grx_ref, qkc_ref, vtc_ref, grc_ref,
                  gain_ref, out_ref, sin_ref, st_ref, ucol_ref, mrun_ref, wint_ref, floor_ref,
                  wkey_ref, decay_ref, *, ncx, ncc):
    lc = ML_CHUNK
    qkw = ML_QK_WIDTH
    sr = ML_STATE_ROWS
    ng = ML_GATES
    hg = ML_GATES // 2
    st_ref[...] = jnp.zeros(st_ref.shape, F32)

    rr = lax.broadcasted_iota(jnp.int32, (lc, lc), 0)
    cc = lax.broadcasted_iota(jnp.int32, (lc, lc), 1)
    tril = rr >= cc
    triu = rr <= cc
    tril_b = jnp.where(tril, 1.0, 0.0).astype(BF16)
    triu_b = jnp.where(triu, 1.0, 0.0).astype(BF16)
    head_of_lane = lax.broadcasted_iota(jnp.int32, (lc, qkw), 1) // ML_QK_DIM
    own_block = jnp.concatenate(
        [lax.broadcasted_iota(jnp.int32, (sr, qkw), 1) // ML_QK_DIM == h for h in range(ML_HEADS)], axis=0)
    pad_rows = jnp.zeros((SUBLANES - 1, lc), F32)

    def split(v):
        hi = v.astype(BF16)
        return hi, (v - hi.astype(F32)).astype(BF16)

    def gate_rows(g):
        n16 = g.shape[0]
        fwd_row = (lax.broadcasted_iota(jnp.int32, g.shape, 0) & (ng - 1)) < hg
        lane = lax.broadcasted_iota(jnp.int32, g.shape, 1)
        hi, lo = split(_log_sigmoid(g))
        cat = jnp.concatenate([hi, lo], axis=0)
        bu = _dot(cat, triu_b)
        bl = _dot(cat, tril_b)
        b = jnp.where(fwd_row, bu[:n16] + bu[n16:], bl[:n16] + bl[n16:])
        b = pltpu.roll(b, n16 - ML_HEADS, axis=0)
        u = g - b
        run_f = run_b = u
        k = 1
        while k < lc:
            run_f = jnp.maximum(run_f, jnp.where(lane >= k, pltpu.roll(run_f, k, axis=1), -jnp.inf))
            run_b = jnp.maximum(run_b, jnp.where(lane < lc - k, pltpu.roll(run_b, lc - k, axis=1), -jnp.inf))
            k *= 2
        run = jnp.where(fwd_row, run_f, run_b)

        def at_end(a):
            return jnp.where(fwd_row, jnp.broadcast_to(a[:, lc - 1:lc], a.shape),
                             jnp.broadcast_to(a[:, 0:1], a.shape))

        return u, b, run, at_end(b), at_end(run)

    def derived(u, b, run, b_tot, run_end, m_in):
        m_run = jnp.maximum(run, m_in)
        m_out = b_tot + jnp.maximum(run_end, m_in)
        return (m_run, jnp.exp(m_in - m_run), jnp.exp(-(b + m_run)),
                jnp.exp(b_tot + u - m_out), jnp.exp(b_tot + m_in - m_out))

    gc_rows = gate_rows(grc_ref[0].reshape(ncc * ng, lc))
    gx_rows = gate_rows(grx_ref[0].reshape(ncx * ng, lc))

    def scan_m(rows, n, m_f, m_b):
        _, _, _, b_tot, run_end = rows
        part = lambda a, c, d: a[c * ng + d * hg:c * ng + (d + 1) * hg]
        ins_f, ins_b = [], [None] * n
        for c in range(n):
            ins_f.append(m_f)
            m_f = part(b_tot, c, 0) + jnp.maximum(part(run_end, c, 0), m_f)
        for c in reversed(range(n)):
            ins_b[c] = m_b
            m_b = part(b_tot, c, 1) + jnp.maximum(part(run_end, c, 1), m_b)
        return jnp.concatenate([x for c in range(n) for x in (ins_f[c], ins_b[c])], axis=0), m_f, m_b

    m0 = jnp.zeros((hg, lc), F32)
    m_in_c, m_f, m_b = scan_m(gc_rows, ncc, m0, m0)
    m_in_x, _, _ = scan_m(gx_rows, ncx, m_f, m_b)
    _, _, _, wkey_c, decay_c = derived(*gc_rows, m_in_c)
    for ref, val in zip((mrun_ref, wint_ref, floor_ref, wkey_ref, decay_ref), derived(*gx_rows, m_in_x)):
        ref[...] = val

    fwd_col = (lax.broadcasted_iota(jnp.int32, (lc, LANES), 1) & (ng - 1)) < hg

    def token_major_u(j):
        rows = pl.ds(pl.multiple_of(j * lc, lc), lc)
        gcol = gcx_ref[0, rows, :]
        hi, lo = split(_log_sigmoid(gcol))
        cat = jnp.concatenate([hi, lo], axis=1)
        bl = _dot(tril_b, cat)
        bu = _dot(triu_b, cat)
        b = jnp.where(fwd_col, bl[:, :LANES] + bl[:, LANES:], bu[:, :LANES] + bu[:, LANES:])
        ucol_ref[rows, :] = gcol - pltpu.roll(b, LANES - ML_HEADS, axis=1)

    def advance(dirn, k4, vt, w_key, decay):
        st = st_ref[dirn]
        pieces, decays = [], []
        for h in range(ML_HEADS):
            c = hg * dirn + h
            pieces += [vt[h * ML_V_DIM:(h + 1) * ML_V_DIM].astype(F32) * w_key[c:c + 1], w_key[c:c + 1], pad_rows]
            decays.append(jnp.broadcast_to(jnp.concatenate([decay[c:c + 1]] * (qkw // lc), axis=1), (sr, qkw)))
        upd = _dot(jnp.concatenate(pieces, axis=0).astype(BF16), k4)
        st_ref[dirn] = jnp.concatenate(decays, axis=0) * st + jnp.where(own_block, upd, 0.0)

    for j in range(ncc):
        for dirn, cj in ((0, j), (1, ncc - 1 - j)):
            grows = slice(cj * ng, (cj + 1) * ng)
            advance(dirn, qkc_ref[0, cj * lc:(cj + 1) * lc, qkw:], vtc_ref[0, cj], wkey_c[grows], decay_c[grows])

    def scan_body(j, carry):
        for dirn, cj in ((0, j), (1, ncx - 1 - j)):
            rows = pl.ds(pl.multiple_of(cj * lc, lc), lc)
            grows = pl.ds(pl.multiple_of(cj * ng, ng), ng)
            sin_ref[dirn, cj] = st_ref[dirn].astype(BF16)
            advance(dirn, qkx_ref[0, rows, qkw:], vtx_ref[0, cj], wkey_ref[grows, :], decay_ref[grows, :])
        token_major_u(j)
        return carry

    lax.fori_loop(0, ncx, scan_body, 0, unroll=2)

    gain = gain_ref[...]

    def out_body(j, carry):
        rows = pl.ds(pl.multiple_of(j * lc, lc), lc)
        grows = pl.ds(pl.multiple_of(j * ng, ng), ng)
        q4 = qkx_ref[0, rows, :qkw]
        k4 = qkx_ref[0, rows, qkw:]
        vt = vtx_ref[0, j]
        u_col = ucol_ref[rows, :]
        m_run, w_int, floor = mrun_ref[grows, :], wint_ref[grows, :], floor_ref[grows, :]
        zero = jnp.zeros_like(q4)
        qs = jnp.concatenate([jnp.where(head_of_lane == h, q4, zero) for h in range(ML_HEADS)], axis=0)
        qk_t = _dot_nt(k4, qs)
        hsum = [None] * ML_HEADS
        for dirn in range(2):
            inter = _dot_nt(sin_ref[dirn, j], q4)
            valid = triu if dirn == 0 else tril
            for h in range(ML_HEADS):
                c = hg * dirn + h
                e = jnp.exp(jnp.where(valid, u_col[:, c:c + 1] - m_run[c:c + 1], -jnp.inf))
                s_t = qk_t[:, h * lc:(h + 1) * lc] * e
                num = _dot(vt[h * ML_V_DIM:(h + 1) * ML_V_DIM], s_t.astype(BF16))
                num = num + w_int[c:c + 1] * inter[h * sr:h * sr + ML_V_DIM]
                den = (jnp.sum(s_t, axis=0, keepdims=True)
                       + w_int[c:c + 1] * inter[h * sr + ML_V_DIM:h * sr + ML_V_DIM + 1])
                hv = num / jnp.maximum(jnp.abs(den), floor[c:c + 1])
                hsum[h] = hv if dirn == 0 else hsum[h] + hv
        for h in range(ML_HEADS):
            cols = slice(h * ML_V_DIM, (h + 1) * ML_V_DIM)
            out_ref[0, rows, cols] = (_rms(hsum[h].T, gain[:, cols])
                                      * ox_ref[0, rows, cols].astype(F32)).astype(BF16)
        return carry

    lax.fori_loop(0, ncx, out_body, 0, unroll=4)


def _mlstm(qkx, vtx, ox, gcx, grx, qkc, vtc, grc, gain):
    bsz, s, _ = qkx.shape
    ncx, ncc = s // ML_CHUNK, qkc.shape[1] // ML_CHUNK
    per_b = lambda a: pl.BlockSpec((1,) + a.shape[1:], lambda b: (b,) + (0,) * (a.ndim - 1))
    ins = [qkx, vtx, ox, gcx, grx, qkc, vtc, grc]
    return pl.pallas_call(
        functools.partial(_mlstm_kernel, ncx=ncx, ncc=ncc),
        grid=(bsz,),
        in_specs=[per_b(a) for a in ins] + [pl.BlockSpec((1, ML_WIDTH), lambda b: (0, 0))],
        out_specs=pl.BlockSpec((1, s, ML_WIDTH), lambda b: (b, 0, 0)),
        out_shape=jax.ShapeDtypeStruct((bsz, s, ML_WIDTH), BF16),
        scratch_shapes=[pltpu.VMEM((2, ncx, ML_HEADS * ML_STATE_ROWS, ML_QK_WIDTH), BF16),
                        pltpu.VMEM((2, ML_HEADS * ML_STATE_ROWS, ML_QK_WIDTH), F32),
                        pltpu.VMEM((s, LANES), F32)]
                       + [pltpu.VMEM((ncx * ML_GATES, ML_CHUNK), F32)] * 5,
        compiler_params=pltpu.CompilerParams(dimension_semantics=("arbitrary",),
                                             vmem_limit_bytes=VMEM_LIMIT),
        name="mlstm",
    )(*ins, gain)


def _out_ffn_kernel(x_ref, att_ref, ml_ref, gtm_ref, shf_ref, scf_ref, gtf_ref,
                    gpm_ref, gpf_ref, gqf_ref, woa_ref, wom_ref, wfi_ref, wfo_ref, o_ref, *, hidden):
    rows = [slice(r * FFN_ROWS, (r + 1) * FFN_ROWS) for r in range(x_ref.shape[0] // FFN_ROWS)]

    def mix_stage(r):
        return _dot(att_ref[r, :], woa_ref[...]) + _dot(ml_ref[r, :], wom_ref[...])

    def norm_stage(r, mix):
        x1 = x_ref[r, :] + gtm_ref[0] * _rms(mix, gpm_ref[...])
        return x1, (_rms(x1, gpf_ref[...]) * (1.0 + scf_ref[0]) + shf_ref[0]).astype(BF16)

    def act_stage(gu):
        return (_silu(gu[:, :hidden]) * gu[:, hidden:]).astype(BF16)

    def out_stage(r, x1, fx):
        o_ref[r, :] = x1 + gtf_ref[0] * _rms(fx, gqf_ref[...])

    a, b = rows
    mix_a = mix_stage(a)
    mix_b = mix_stage(b)
    x1_a, h_a = norm_stage(a, mix_a)
    gu_a = _dot(h_a, wfi_ref[...])
    x1_b, h_b = norm_stage(b, mix_b)
    gu_b = _dot(h_b, wfi_ref[...])
    act_a = act_stage(gu_a)
    fx_a = _dot(act_a, wfo_ref[...])
    act_b = act_stage(gu_b)
    fx_b = _dot(act_b, wfo_ref[...])
    out_stage(a, x1_a, fx_a)
    out_stage(b, x1_b, fx_b)


def _out_ffn(x2, att2, ml2, mod3, g_post_mix, g_pre_ffn, g_post_ffn, woa, wom, wfi, wfo, *, tiles_per_batch):
    t, d = x2.shape
    tm = FFN_TM
    hidden = wfo.shape[0]
    resident = lambda a: pl.BlockSpec(a.shape, lambda i: (0,) * a.ndim, pipeline_mode=pl.Buffered(1))
    mod = lambda k: pl.BlockSpec((1, 1, d), lambda i: (i // tiles_per_batch, 0, k))
    row = pl.BlockSpec((1, d), lambda i: (0, 0))
    return pl.pallas_call(
        functools.partial(_out_ffn_kernel, hidden=hidden),
        grid=(t // tm,),
        in_specs=[pl.BlockSpec((tm, d), lambda i: (i, 0)),
                  pl.BlockSpec((tm, ATT_WIDTH), lambda i: (i, 0)),
                  pl.BlockSpec((tm, ML_WIDTH), lambda i: (i, 0)),
                  mod(2), mod(3), mod(4), mod(5), row, row, row,
                  resident(woa), resident(wom), resident(wfi), resident(wfo)],
        out_specs=pl.BlockSpec((tm, d), lambda i: (i, 0)),
        out_shape=jax.ShapeDtypeStruct((t, d), F32),
        compiler_params=pltpu.CompilerParams(dimension_semantics=("arbitrary",),
                                             vmem_limit_bytes=VMEM_LIMIT),
        name="out_ffn",
    )(x2, att2, ml2, mod3, mod3, mod3, mod3, g_post_mix, g_pre_ffn, g_post_ffn, woa, wom, wfi, wfo)


def _rope_tables(n_tokens):
    pos = jnp.arange(n_tokens)
    row = (pos // GRID_W).astype(F32)
    col = (pos % GRID_W).astype(F32)
    half = ATT_HEAD_DIM // 4
    inv_freq = jnp.power(ROPE_BASE, -jnp.arange(half, dtype=F32) / half)
    ang_r = row[:, None] * inv_freq
    ang_c = col[:, None] * inv_freq
    z = jnp.zeros_like(ang_r)
    reps = LANES // ATT_HEAD_DIM
    cos = jnp.tile(jnp.concatenate([jnp.cos(ang_r)] * 2 + [jnp.cos(ang_c)] * 2, axis=1), (1, reps))
    sin_lo = jnp.tile(jnp.concatenate([-jnp.sin(ang_r), z, -jnp.sin(ang_c), z], axis=1), (1, reps))
    sin_hi = jnp.tile(jnp.concatenate([z, jnp.sin(ang_r), z, jnp.sin(ang_c)], axis=1), (1, reps))
    return cos, sin_lo, sin_hi


def _permute_heads(w, axis):
    shape = w.shape
    split = shape[:axis] + (ATT_HEADS, ATT_HEAD_DIM) + shape[axis + 1:]
    return jnp.take(w.reshape(split), jnp.array(_HEAD_PERM), axis=axis).reshape(shape)


def kernel(x, c, ctx, c_ctx, w_ada, b_ada, g_pre_mix, w_in, w_conv_qk, b_gates, attn_sink,
           g_mlstm_out, w_out, g_post_mix, g_pre_ffn, w_ffn_in, w_ffn_out, g_post_ffn):
    bsz, s, d = x.shape
    l = ctx.shape[1]
    assert w_ada.shape[0] == 1, "single-layer block"
    assert bsz < MOD_ROWS and s % INPROJ_TM == 0 and l % ML_CHUNK == 0 and s % FFN_TM == 0

    cc = jnp.zeros((MOD_ROWS, d), F32).at[:bsz].set(c).at[bsz].set(c_ctx)
    mod3 = _ada(cc, w_ada[0], b_ada).reshape(MOD_ROWS, 1, 6 * d)

    w = w_in[0]
    o_q, o_k, o_v = 0, ATT_WIDTH, ATT_WIDTH + ATT_KV_WIDTH
    o_mq = o_v + ATT_KV_WIDTH
    o_mv = o_mq + 2 * ML_QK_WIDTH
    o_mo = o_mv + ML_WIDTH
    o_mg = o_mo + ML_WIDTH
    w_q = _permute_heads(w[:, o_q:o_k], 1) * (ATT_HEAD_DIM ** -0.5 * LOG2_E)
    w_g = jnp.pad(w[:, o_mg:], ((0, 0), (0, LANES - ML_GATES)))
    shared = [w[:, o_k:o_v], w_g, w[:, o_mq:o_mv]]
    w_lat = jnp.concatenate([w_q] + shared + [w[:, o_mo:o_mg]], axis=1).astype(BF16)
    w_ctx = jnp.concatenate(shared, axis=1).astype(BF16)
    wvt = jnp.concatenate([w[:, o_v:o_mq], w[:, o_mv:o_mo]], axis=1).T.astype(BF16)
    wgt = w[:, o_mg:].T.astype(BF16)
    bg_row = jnp.pad(b_gates, ((0, 0), (0, LANES - ML_GATES)))
    bg_col = b_gates.reshape(ML_GATES, 1)
    wc = w_conv_qk[0]

    q, k, vt, qkx, vtx, ox, gcx, grx = _inproj(
        x, mod3, lambda b: b, g_pre_mix, w_lat, wvt, wgt, bg_row, bg_col, wc, _rope_tables(s),
        tm=INPROJ_TM, latent=True)
    kc, vct, qkc, vtc, _, grc = _inproj(
        ctx, mod3, lambda b: bsz, g_pre_mix, w_ctx, wvt, wgt, bg_row, bg_col, wc, None,
        tm=l, latent=False)

    sink_row = jnp.repeat(attn_sink[0][jnp.array(_HEAD_PERM)] * LOG2_E, ATT_BLOCK)[None, :]
    att = _attention(q, k, vt, kc, vct, sink_row)
    ml = _mlstm(qkx, vtx, ox, gcx, grx, qkc, vtc, grc, g_mlstm_out)

    wo = w_out[0]
    woa = _permute_heads(wo[:ATT_WIDTH], 0).astype(BF16)
    wom = wo[ATT_WIDTH:].astype(BF16)
    out = _out_ffn(x.reshape(bsz * s, d), att.reshape(bsz * s, ATT_WIDTH), ml.reshape(bsz * s, ML_WIDTH),
                   mod3, g_post_mix, g_pre_ffn, g_post_ffn, woa, wom,
                   w_ffn_in[0].astype(BF16), w_ffn_out[0].astype(BF16), tiles_per_batch=s // FFN_TM)
    return out.reshape(bsz, s, d)
```

```python
import functools

import jax
import jax.numpy as jnp
from jax import lax
from jax.experimental import pallas as pl
from jax.experimental.pallas import tpu as pltpu

F32 = jnp.float32
BF16 = jnp.bfloat16

EPS = 1e-6
GRID_W = 64
ROPE_BASE = 10000.0
LOG2_E = 1.4426950408889634

ATT_HEADS = 8
ATT_KV_HEADS = 2
ATT_GROUP = ATT_HEADS // ATT_KV_HEADS
ATT_HEAD_DIM = 64
ATT_BLOCK = 128
ATT_WIDTH = ATT_HEADS * ATT_HEAD_DIM
ATT_KV_WIDTH = ATT_KV_HEADS * ATT_HEAD_DIM
ATT_BLOCKS_PER_STEP = 8
ATT_ONES_ROWS = 16

ML_HEADS = 4
ML_V_DIM = 128
ML_QK_DIM = 64
ML_WIDTH = ML_HEADS * ML_V_DIM
ML_QK_WIDTH = ML_HEADS * ML_QK_DIM
ML_GATES = 4 * ML_HEADS
ML_CHUNK = 128

LANES = 128
SUBLANES = 8
VMEM_LIMIT = 56 * 1024 * 1024

INPROJ_TM = 1024
INPROJ_ROWS = 512
FFN_TM = 512
FFN_ROWS = 256
ADA_TN = 1536
MOD_ROWS = 16

_HEAD_PERM = tuple(h * ATT_GROUP + g for g in range(ATT_GROUP) for h in range(ATT_KV_HEADS))


def _silu(v):
    return v * jax.nn.sigmoid(v)


def _log_sigmoid(v):
    return jnp.minimum(v, 0.0) - jnp.log1p(jnp.exp(-jnp.abs(v)))


def _rms(v, g):
    return v * lax.rsqrt(jnp.mean(v * v, axis=-1, keepdims=True) + EPS) * g


def _dot(a, b):
    return jnp.dot(a, b, preferred_element_type=F32)


def _dot_nt(a, b):
    return lax.dot_general(a, b, (((1,), (1,)), ((), ())), preferred_element_type=F32)


def _ada_kernel(cc_ref, w_ref, b_ref, o_ref):
    a = _silu(cc_ref[...])
    o_ref[...] = _dot(a.astype(BF16), w_ref[...].astype(BF16)) + b_ref[...]


def _ada(cc, w, b):
    d, n = w.shape
    return pl.pallas_call(
        _ada_kernel,
        grid=(n // ADA_TN,),
        in_specs=[pl.BlockSpec((MOD_ROWS, d), lambda j: (0, 0)),
                  pl.BlockSpec((d, ADA_TN), lambda j: (0, j)),
                  pl.BlockSpec((1, ADA_TN), lambda j: (0, j))],
        out_specs=pl.BlockSpec((MOD_ROWS, ADA_TN), lambda j: (0, j)),
        out_shape=jax.ShapeDtypeStruct((MOD_ROWS, n), F32),
        compiler_params=pltpu.CompilerParams(dimension_semantics=("arbitrary",),
                                             vmem_limit_bytes=VMEM_LIMIT),
        name="ada",
    )(cc, w, b)


def _rope(v, cos, sin_lo, sin_hi):
    return (v * cos + pltpu.roll(v, LANES - 16, axis=1) * sin_lo
            + pltpu.roll(v, 16, axis=1) * sin_hi)


def _inproj_kernel(*refs, tm, gm, n_tiles, latent):
    if latent:
        (x_ref, xp_ref, xn_ref, sh_ref, sc_ref, g_ref, w_ref, wvt_ref, bgr_ref, bgc_ref,
         wc_ref, cos_ref, sl_ref, shi_ref,
         q_ref, k_ref, vt_ref, qk_ref, mvt_ref, o_ref, gcol_ref, grow_ref) = refs
    else:
        (x_ref, xp_ref, xn_ref, sh_ref, sc_ref, g_ref, w_ref, wvt_ref, bgr_ref, bgc_ref,
         wc_ref,
         k_ref, vt_ref, qk_ref, mvt_ref, gcol_ref, grow_ref) = refs
    i = pl.program_id(0)
    n_groups = tm // gm
    scale = g_ref[...] * (1.0 + sc_ref[0])
    shift = sh_ref[0]
    wc = wc_ref[...]
    row = lax.broadcasted_iota(jnp.int32, (gm, 1), 0)

    def project(lo, width, lhs):
        return _dot(lhs, w_ref[:, lo:lo + width])

    def group(r):
        rows = slice(r * gm, (r + 1) * gm)
        before = xp_ref[0] if r == 0 else x_ref[0, r * gm - SUBLANES:r * gm, :]
        after = xn_ref[0] if r == n_groups - 1 else x_ref[0, (r + 1) * gm:(r + 1) * gm + SUBLANES, :]
        keep_prev = jnp.where(i == 0, 0.0, 1.0) if r == 0 else 1.0
        keep_next = jnp.where(i == n_tiles - 1, 0.0, 1.0) if r == n_groups - 1 else 1.0
        xt = jnp.concatenate([x_ref[0, rows, :], before, after], axis=0)
        ms = jnp.mean(xt * xt, axis=-1, keepdims=True)
        hb = (xt * lax.rsqrt(ms + EPS) * scale + shift).astype(BF16)
        hm = hb[:gm]
        yield

        c = 0
        if latent:
            r_q = project(c, ATT_WIDTH, hm)
            c += ATT_WIDTH
        r_kg = project(c, ATT_KV_WIDTH + LANES, hm)
        c += ATT_KV_WIDTH + LANES
        yield

        if latent:
            cos, sl, shi = cos_ref[rows, :], sl_ref[rows, :], shi_ref[rows, :]
            for g in range(ATT_WIDTH // LANES):
                q_ref[0, rows, g * LANES:(g + 1) * LANES] = _rope(
                    r_q[:, g * LANES:(g + 1) * LANES], cos, sl, shi).astype(BF16)
        y = project(c, 2 * ML_QK_WIDTH, hb)
        c += 2 * ML_QK_WIDTH
        yield

        if latent:
            k_ref[0, rows, :] = _rope(r_kg[:, :LANES], cos, sl, shi).astype(BF16)
        else:
            k_ref[0, rows, :] = r_kg[:, :LANES].astype(BF16)
        gcol_ref[0, rows, :] = r_kg[:, LANES:] + bgr_ref[...]
        v_t = _dot_nt(wvt_ref[...], hm)
        yield

        ym = y[:gm]
        prev = jnp.where(row == 0, y[gm + SUBLANES - 1:gm + SUBLANES] * keep_prev,
                         pltpu.roll(ym, 1, axis=0))
        nxt = jnp.where(row == gm - 1, y[gm + SUBLANES:gm + SUBLANES + 1] * keep_next,
                        pltpu.roll(ym, gm - 1, axis=0))
        act = _silu(prev * wc[0:1] + ym * wc[1:2] + nxt * wc[2:3])
        qk_ref[0, rows, :ML_QK_WIDTH] = (act[:, :ML_QK_WIDTH] * (ML_QK_DIM ** -0.5)).astype(BF16)
        qk_ref[0, rows, ML_QK_WIDTH:] = act[:, ML_QK_WIDTH:].astype(BF16)
        if latent:
            r_o = project(c, ML_WIDTH, hm)
        yield

        g_t = v_t[ATT_KV_WIDTH + ML_WIDTH:] + bgc_ref[...]
        vt_ref[0, :, rows] = v_t[:ATT_KV_WIDTH].astype(BF16)
        for j in range(gm // ML_CHUNK):
            cols = slice(j * ML_CHUNK, (j + 1) * ML_CHUNK)
            mvt_ref[0, r * (gm // ML_CHUNK) + j] = v_t[ATT_KV_WIDTH:ATT_KV_WIDTH + ML_WIDTH, cols].astype(BF16)
            grow_ref[0, r * (gm // ML_CHUNK) + j] = g_t[:, cols]
        if latent:
            o_ref[0, rows, :] = jax.nn.sigmoid(r_o).astype(BF16)

    done = object()
    waiting = [group(r) for r in range(n_groups)]
    active = []
    while waiting or active:
        if waiting:
            active.append(waiting.pop(0))
        active = [g for g in active if next(g, done) is not done]


def _inproj(x, mod3, mod_row, g_pre, w_main, wvt, bg_row, bg_col, wc, rope_tabs, *, tm, gm, latent):
    bsz, t, d = x.shape
    n_tiles = t // tm
    hb = tm // SUBLANES
    n_hblk = t // SUBLANES
    n = w_main.shape[1]

    def const(shape):
        return pl.BlockSpec(shape, lambda i, b: (0,) * len(shape))

    in_specs = [
        pl.BlockSpec((1, tm, d), lambda i, b: (b, i, 0)),
        pl.BlockSpec((1, SUBLANES, d), lambda i, b: (b, jnp.maximum(i * hb - 1, 0), 0)),
        pl.BlockSpec((1, SUBLANES, d), lambda i, b: (b, jnp.minimum((i + 1) * hb, n_hblk - 1), 0)),
        pl.BlockSpec((1, 1, d), lambda i, b: (mod_row(b), 0, 0)),
        pl.BlockSpec((1, 1, d), lambda i, b: (mod_row(b), 0, 1)),
        const((1, d)), const((d, n)), const((ATT_KV_WIDTH + ML_WIDTH + ML_GATES, d)),
        const((1, LANES)), const((ML_GATES, 1)), const((3, 2 * ML_QK_WIDTH)),
    ]
    args = [x, x, x, mod3, mod3, g_pre, w_main, wvt, bg_row, bg_col, wc]
    tok = lambda width, dt: (pl.BlockSpec((1, tm, width), lambda i, b: (b, i, 0)),
                             jax.ShapeDtypeStruct((bsz, t, width), dt))
    chunked = lambda rows, dt: (pl.BlockSpec((1, tm // ML_CHUNK, rows, ML_CHUNK), lambda i, b: (b, i, 0, 0)),
                                jax.ShapeDtypeStruct((bsz, t // ML_CHUNK, rows, ML_CHUNK), dt))
    outs = []
    if latent:
        in_specs += [pl.BlockSpec((tm, LANES), lambda i, b: (i, 0))] * 3
        args += list(rope_tabs)
        outs.append(tok(ATT_WIDTH, BF16))
    outs.append(tok(ATT_KV_WIDTH, BF16))
    outs.append((pl.BlockSpec((1, LANES, tm), lambda i, b: (b, 0, i)),
                 jax.ShapeDtypeStruct((bsz, ATT_KV_WIDTH, t), BF16)))
    outs.append(tok(2 * ML_QK_WIDTH, BF16))
    outs.append(chunked(ML_WIDTH, BF16))
    if latent:
        outs.append(tok(ML_WIDTH, BF16))
    outs.append(tok(LANES, F32))
    outs.append(chunked(ML_GATES, F32))
    return pl.pallas_call(
        functools.partial(_inproj_kernel, tm=tm, gm=gm, n_tiles=n_tiles, latent=latent),
        grid=(n_tiles, bsz),
        in_specs=in_specs,
        out_specs=[o[0] for o in outs],
        out_shape=[o[1] for o in outs],
        compiler_params=pltpu.CompilerParams(dimension_semantics=("arbitrary", "arbitrary"),
                                             vmem_limit_bytes=VMEM_LIMIT),
        name="inproj_latent" if latent else "inproj_context",
    )(*args)


def _attn_kernel(q_ref, kp_ref, kc_ref, kn_ref, kx_ref, vp_ref, vc_ref, vn_ref, vx_ref, sink_ref,
                 o_ref, *, n_tiles):
    i = pl.program_id(1)
    blk = ATT_BLOCK
    lane = lax.broadcasted_iota(jnp.int32, (blk, LANES), 1)
    zero = jnp.zeros((blk, LANES), BF16)
    half_groups = ATT_GROUP // 2
    n_slots = 2 * half_groups
    sink = sink_ref[:, :n_slots * blk], sink_ref[:, n_slots * blk:]

    def stack_heads(q, half):
        parts = []
        for g in range(half * half_groups, (half + 1) * half_groups):
            slab = q[:, g * LANES:(g + 1) * LANES]
            parts.append(jnp.where(lane < ATT_HEAD_DIM, slab, zero))
            parts.append(jnp.where(lane >= ATT_HEAD_DIM, slab, zero))
        return jnp.concatenate(parts, axis=0)

    key = lax.broadcasted_iota(jnp.int32, (blk, blk), 0)
    qry = lax.broadcasted_iota(jnp.int32, (blk, blk), 1)
    ninf = jnp.full((blk, blk), -jnp.inf, F32)
    bias_prev = jnp.where(key >= qry, 0.0, ninf)
    bias_next = jnp.where(key <= qry, 0.0, ninf)
    slots = lambda b: jnp.concatenate([b] * n_slots, axis=1)

    def scores(qs, k_prev, k_cur, k_next):
        return [_dot_nt(k_prev, qs), _dot_nt(k_cur, qs), _dot_nt(k_next, qs), _dot_nt(kx_ref[0], qs)]

    def softmax(s, b_prev, b_next, sink_h):
        s = jnp.concatenate([s[0] + slots(b_prev), s[1], s[2] + slots(b_next), s[3]], axis=0)
        m = jnp.maximum(sink_h, jnp.max(s, axis=0, keepdims=True))
        return jnp.exp2(s - m).astype(BF16), jnp.exp2(sink_h - m)

    n_keys = 3 * blk + kx_ref.shape[1]
    ones_rows = jnp.ones((ATT_ONES_ROWS, n_keys), BF16)

    def weighted_values(p, v_prev, v_cur, v_next):
        vt = jnp.concatenate([v_prev, v_cur, v_next, vx_ref[0]], axis=1)
        return _dot(jnp.concatenate([vt, ones_rows], axis=0), p)

    dim = lax.broadcasted_iota(jnp.int32, (LANES, blk), 0)

    def emit(rows, half, ot, p_sink):
        ot = ot[:LANES] * (1.0 / (ot[LANES:LANES + 1] + p_sink))
        for j in range(half_groups):
            g = half * half_groups + j
            a = ot[:, (2 * j) * blk:(2 * j + 1) * blk]
            b = ot[:, (2 * j + 1) * blk:(2 * j + 2) * blk]
            o_ref[0, rows, g * LANES:(g + 1) * LANES] = jnp.where(dim < ATT_HEAD_DIM, a, b).T.astype(BF16)

    nblk = q_ref.shape[1] // blk
    rows_of = [slice(b * blk, (b + 1) * blk) for b in range(nblk)]
    k_blocks = [kp_ref[0]] + [kc_ref[0, r, :] for r in rows_of] + [kn_ref[0]]
    v_blocks = [vp_ref[0]] + [vc_ref[0, :, r] for r in rows_of] + [vn_ref[0]]
    keys = [tuple(k_blocks[b:b + 3]) for b in range(nblk)]
    vals = [tuple(v_blocks[b:b + 3]) for b in range(nblk)]
    bias = [(jnp.where(i == 0, ninf, bias_prev) if b == 0 else bias_prev,
             jnp.where(i == n_tiles - 1, ninf, bias_next) if b == nblk - 1 else bias_next)
            for b in range(nblk)]
    units = [(b, half) for b in range(nblk) for half in range(2)]

    def stage_scores(u):
        b, half = u
        return scores(stack_heads(q_ref[0, rows_of[b], :], half), *keys[b])

    def stage_softmax(u, s):
        b, half = u
        return softmax(s, *bias[b], sink[half])

    def stage_values(u, p):
        return weighted_values(p, *vals[u[0]])

    def stage_emit(u, ot, p_sink):
        emit(rows_of[u[0]], u[1], ot, p_sink)

    n_units = len(units)
    s, p, ot = {}, {}, {}
    s[0] = stage_scores(units[0])
    for t in range(n_units):
        if t + 1 < n_units:
            s[t + 1] = stage_scores(units[t + 1])
        p[t] = stage_softmax(units[t], s.pop(t))
        if t >= 1:
            ot[t - 1] = stage_values(units[t - 1], p[t - 1][0])
        if t >= 2:
            stage_emit(units[t - 2], ot.pop(t - 2), p.pop(t - 2)[1])
    ot[n_units - 1] = stage_values(units[n_units - 1], p[n_units - 1][0])
    for t in (n_units - 2, n_units - 1):
        stage_emit(units[t], ot.pop(t), p.pop(t)[1])


def _attention(q, k, vt, kx, vxt, sink_row):
    bsz, s, _ = q.shape
    l = kx.shape[1]
    blk = ATT_BLOCK
    per = ATT_BLOCKS_PER_STEP
    n_tiles = s // (per * blk)
    nb = s // blk
    prev = lambda i: jnp.maximum(per * i - 1, 0)
    nxt = lambda i: jnp.minimum(per * (i + 1), nb - 1)
    return pl.pallas_call(
        functools.partial(_attn_kernel, n_tiles=n_tiles),
        grid=(bsz, n_tiles),
        in_specs=[pl.BlockSpec((1, per * blk, ATT_WIDTH), lambda b, i: (b, i, 0)),
                  pl.BlockSpec((1, blk, LANES), lambda b, i: (b, prev(i), 0)),
                  pl.BlockSpec((1, per * blk, LANES), lambda b, i: (b, i, 0)),
                  pl.BlockSpec((1, blk, LANES), lambda b, i: (b, nxt(i), 0)),
                  pl.BlockSpec((1, l, LANES), lambda b, i: (b, 0, 0)),
                  pl.BlockSpec((1, LANES, blk), lambda b, i: (b, 0, prev(i))),
                  pl.BlockSpec((1, LANES, per * blk), lambda b, i: (b, 0, i)),
                  pl.BlockSpec((1, LANES, blk), lambda b, i: (b, 0, nxt(i))),
                  pl.BlockSpec((1, LANES, l), lambda b, i: (b, 0, 0)),
                  pl.BlockSpec((1, ATT_HEADS * blk), lambda b, i: (0, 0))],
        out_specs=pl.BlockSpec((1, per * blk, ATT_WIDTH), lambda b, i: (b, i, 0)),
        out_shape=jax.ShapeDtypeStruct((bsz, s, ATT_WIDTH), BF16),
        compiler_params=pltpu.CompilerParams(dimension_semantics=("arbitrary", "arbitrary"),
                                             vmem_limit_bytes=VMEM_LIMIT),
        name="attention",
    )(q, k, k, k, kx, vt, vt, vt, vxt, sink_row)


ML_STATE_ROWS = ML_V_DIM + SUBLANES


def _mlstm_kernel(qkx_ref, vtx_ref, ox_ref, gcx_ref, grx_ref, qkc_ref, vtc_ref, grc_ref,
                  gain_ref, out_ref, sin_ref, st_ref, ucol_ref, mrun_ref, wint_ref, floor_ref,
                  wkey_ref, decay_ref, *, ncx, ncc):
    lc = ML_CHUNK
    qkw = ML_QK_WIDTH
    sr = ML_STATE_ROWS
    ng = ML_GATES
    hg = ML_GATES // 2
    st_ref[...] = jnp.zeros(st_ref.shape, F32)

    rr = lax.broadcasted_iota(jnp.int32, (lc, lc), 0)
    cc = lax.broadcasted_iota(jnp.int32, (lc, lc), 1)
    tril = rr >= cc
    triu = rr <= cc
    tril_b = jnp.where(tril, 1.0, 0.0).astype(BF16)
    triu_b = jnp.where(triu, 1.0, 0.0).astype(BF16)
    head_of_lane = lax.broadcasted_iota(jnp.int32, (lc, qkw), 1) // ML_QK_DIM
    own_block = jnp.concatenate(
        [lax.broadcasted_iota(jnp.int32, (sr, qkw), 1) // ML_QK_DIM == h for h in range(ML_HEADS)], axis=0)
    pad_rows = jnp.zeros((SUBLANES - 1, lc), F32)

    def split(v):
        hi = v.astype(BF16)
        return hi, (v - hi.astype(F32)).astype(BF16)

    def gate_rows(g):
        n16 = g.shape[0]
        fwd_row = (lax.broadcasted_iota(jnp.int32, g.shape, 0) & (ng - 1)) < hg
        lane = lax.broadcasted_iota(jnp.int32, g.shape, 1)
        hi, lo = split(_log_sigmoid(g))
        cat = jnp.concatenate([hi, lo], axis=0)
        bu = _dot(cat, triu_b)
        bl = _dot(cat, tril_b)
        b = jnp.where(fwd_row, bu[:n16] + bu[n16:], bl[:n16] + bl[n16:])
        b = pltpu.roll(b, n16 - ML_HEADS, axis=0)
        u = g - b
        run_f = run_b = u
        k = 1
        while k < lc:
            run_f = jnp.maximum(run_f, jnp.where(lane >= k, pltpu.roll(run_f, k, axis=1), -jnp.inf))
            run_b = jnp.maximum(run_b, jnp.where(lane < lc - k, pltpu.roll(run_b, lc - k, axis=1), -jnp.inf))
            k *= 2
        run = jnp.where(fwd_row, run_f, run_b)

        def at_end(a):
            return jnp.where(fwd_row, jnp.broadcast_to(a[:, lc - 1:lc], a.shape),
                             jnp.broadcast_to(a[:, 0:1], a.shape))

        return u, b, run, at_end(b), at_end(run)

    def derived(u, b, run, b_tot, run_end, m_in):
        m_run = jnp.maximum(run, m_in)
        m_out = b_tot + jnp.maximum(run_end, m_in)
        return (m_run * LOG2_E, jnp.exp(m_in - m_run), jnp.exp(-(b + m_run)),
                jnp.exp(b_tot + u - m_out), jnp.exp(b_tot + m_in - m_out))

    gc_rows = gate_rows(grc_ref[0].reshape(ncc * ng, lc))
    gx_rows = gate_rows(grx_ref[0].reshape(ncx * ng, lc))

    def scan_m(rows, n, m_f, m_b):
        _, _, _, b_tot, run_end = rows
        part = lambda a, c, d: a[c * ng + d * hg:c * ng + (d + 1) * hg]
        ins_f, ins_b = [], [None] * n
        for c in range(n):
            ins_f.append(m_f)
            m_f = part(b_tot, c, 0) + jnp.maximum(part(run_end, c, 0), m_f)
        for c in reversed(range(n)):
            ins_b[c] = m_b
            m_b = part(b_tot, c, 1) + jnp.maximum(part(run_end, c, 1), m_b)
        return jnp.concatenate([x for c in range(n) for x in (ins_f[c], ins_b[c])], axis=0), m_f, m_b

    m0 = jnp.zeros((hg, lc), F32)
    m_in_c, m_f, m_b = scan_m(gc_rows, ncc, m0, m0)
    m_in_x, _, _ = scan_m(gx_rows, ncx, m_f, m_b)
    _, _, _, wkey_c, decay_c = derived(*gc_rows, m_in_c)
    for ref, val in zip((mrun_ref, wint_ref, floor_ref, wkey_ref, decay_ref), derived(*gx_rows, m_in_x)):
        ref[...] = val

    fwd_col = (lax.broadcasted_iota(jnp.int32, (lc, LANES), 1) & (ng - 1)) < hg

    def token_major_u(j):
        rows = pl.ds(pl.multiple_of(j * lc, lc), lc)
        gcol = gcx_ref[0, rows, :]
        hi, lo = split(_log_sigmoid(gcol))
        cat = jnp.concatenate([hi, lo], axis=1)
        bl = _dot(tril_b, cat)
        bu = _dot(triu_b, cat)
        b = jnp.where(fwd_col, bl[:, :LANES] + bl[:, LANES:], bu[:, :LANES] + bu[:, LANES:])
        ucol_ref[rows, :] = (gcol - pltpu.roll(b, LANES - ML_HEADS, axis=1)) * LOG2_E

    def advance(dirn, k4, vt, w_key, decay):
        st = st_ref[dirn]
        pieces, decays = [], []
        for h in range(ML_HEADS):
            c = hg * dirn + h
            pieces += [vt[h * ML_V_DIM:(h + 1) * ML_V_DIM].astype(F32) * w_key[c:c + 1], w_key[c:c + 1], pad_rows]
            decays.append(jnp.broadcast_to(jnp.concatenate([decay[c:c + 1]] * (qkw // lc), axis=1), (sr, qkw)))
        upd = _dot(jnp.concatenate(pieces, axis=0).astype(BF16), k4)
        st_ref[dirn] = jnp.concatenate(decays, axis=0) * st + jnp.where(own_block, upd, 0.0)

    for j in range(ncc):
        for dirn, cj in ((0, j), (1, ncc - 1 - j)):
            grows = slice(cj * ng, (cj + 1) * ng)
            advance(dirn, qkc_ref[0, cj * lc:(cj + 1) * lc, qkw:], vtc_ref[0, cj], wkey_c[grows], decay_c[grows])

    def scan_body(j, carry):
        for dirn, cj in ((0, j), (1, ncx - 1 - j)):
            rows = pl.ds(pl.multiple_of(cj * lc, lc), lc)
            grows = pl.ds(pl.multiple_of(cj * ng, ng), ng)
            sin_ref[dirn, cj] = st_ref[dirn].astype(BF16)
            advance(dirn, qkx_ref[0, rows, qkw:], vtx_ref[0, cj], wkey_ref[grows, :], decay_ref[grows, :])
        token_major_u(j)
        return carry

    lax.fori_loop(0, ncx, scan_body, 0, unroll=2)

    gain = gain_ref[...]

    def out_body(j, carry):
        rows = pl.ds(pl.multiple_of(j * lc, lc), lc)
        grows = pl.ds(pl.multiple_of(j * ng, ng), ng)
        q4 = qkx_ref[0, rows, :qkw]
        k4 = qkx_ref[0, rows, qkw:]
        vt = vtx_ref[0, j]
        u_col = ucol_ref[rows, :]
        m_run, w_int, floor = mrun_ref[grows, :], wint_ref[grows, :], floor_ref[grows, :]
        zero = jnp.zeros_like(q4)
        qs = jnp.concatenate([jnp.where(head_of_lane == h, q4, zero) for h in range(ML_HEADS)], axis=0)
        qk_t = _dot_nt(k4, qs)
        hsum = [None] * ML_HEADS
        for dirn in range(2):
            inter = _dot_nt(sin_ref[dirn, j], q4)
            valid = triu if dirn == 0 else tril
            for h in range(ML_HEADS):
                c = hg * dirn + h
                e = jnp.exp2(jnp.where(valid, u_col[:, c:c + 1] - m_run[c:c + 1], -jnp.inf))
                s_t = qk_t[:, h * lc:(h + 1) * lc] * e
                num = _dot(vt[h * ML_V_DIM:(h + 1) * ML_V_DIM], s_t.astype(BF16))
                num = num + w_int[c:c + 1] * inter[h * sr:h * sr + ML_V_DIM]
                den = (jnp.sum(s_t, axis=0, keepdims=True)
                       + w_int[c:c + 1] * inter[h * sr + ML_V_DIM:h * sr + ML_V_DIM + 1])
                hv = num * (1.0 / jnp.maximum(jnp.abs(den), floor[c:c + 1]))
                hsum[h] = hv if dirn == 0 else hsum[h] + hv
        for h in range(ML_HEADS):
            cols = slice(h * ML_V_DIM, (h + 1) * ML_V_DIM)
            hs = hsum[h]
            hn = hs * lax.rsqrt(jnp.mean(hs * hs, axis=0, keepdims=True) + EPS)
            out_ref[0, rows, cols] = (hn.T * gain[:, cols] * ox_ref[0, rows, cols].astype(F32)).astype(BF16)
        return carry

    lax.fori_loop(0, ncx, out_body, 0, unroll=4)


def _mlstm(qkx, vtx, ox, gcx, grx, qkc, vtc, grc, gain):
    bsz, s, _ = qkx.shape
    ncx, ncc = s // ML_CHUNK, qkc.shape[1] // ML_CHUNK
    per_b = lambda a: pl.BlockSpec((1,) + a.shape[1:], lambda b: (b,) + (0,) * (a.ndim - 1))
    ins = [qkx, vtx, ox, gcx, grx, qkc, vtc, grc]
    return pl.pallas_call(
        functools.partial(_mlstm_kernel, ncx=ncx, ncc=ncc),
        grid=(bsz,),
        in_specs=[per_b(a) for a in ins] + [pl.BlockSpec((1, ML_WIDTH), lambda b: (0, 0))],
        out_specs=pl.BlockSpec((1, s, ML_WIDTH), lambda b: (b, 0, 0)),
        out_shape=jax.ShapeDtypeStruct((bsz, s, ML_WIDTH), BF16),
        scratch_shapes=[pltpu.VMEM((2, ncx, ML_HEADS * ML_STATE_ROWS, ML_QK_WIDTH), BF16),
                        pltpu.VMEM((2, ML_HEADS * ML_STATE_ROWS, ML_QK_WIDTH), F32),
                        pltpu.VMEM((s, LANES), F32)]
                       + [pltpu.VMEM((ncx * ML_GATES, ML_CHUNK), F32)] * 5,
        compiler_params=pltpu.CompilerParams(dimension_semantics=("arbitrary",),
                                             vmem_limit_bytes=VMEM_LIMIT),
        name="mlstm",
    )(*ins, gain)


def _out_ffn_kernel(x_ref, att_ref, ml_ref, gtm_ref, shf_ref, scf_ref, gtf_ref,
                    gpm_ref, gpf_ref, gqf_ref, woa_ref, wom_ref, wfi_ref, wfo_ref, o_ref, *, hidden):
    rows = [slice(r * FFN_ROWS, (r + 1) * FFN_ROWS) for r in range(x_ref.shape[0] // FFN_ROWS)]

    def mix_stage(r):
        return _dot(att_ref[r, :], woa_ref[...]) + _dot(ml_ref[r, :], wom_ref[...])

    def norm_stage(r, mix):
        x1 = x_ref[r, :] + gtm_ref[0] * _rms(mix, gpm_ref[...])
        return x1, (_rms(x1, gpf_ref[...]) * (1.0 + scf_ref[0]) + shf_ref[0]).astype(BF16)

    def act_stage(gu):
        return (_silu(gu[:, :hidden]) * gu[:, hidden:]).astype(BF16)

    def out_stage(r, x1, fx):
        o_ref[r, :] = x1 + gtf_ref[0] * _rms(fx, gqf_ref[...])

    a, b = rows
    mix_a = mix_stage(a)
    mix_b = mix_stage(b)
    x1_a, h_a = norm_stage(a, mix_a)
    gu_a = _dot(h_a, wfi_ref[...])
    x1_b, h_b = norm_stage(b, mix_b)
    gu_b = _dot(h_b, wfi_ref[...])
    act_a = act_stage(gu_a)
    fx_a = _dot(act_a, wfo_ref[...])
    act_b = act_stage(gu_b)
    fx_b = _dot(act_b, wfo_ref[...])
    out_stage(a, x1_a, fx_a)
    out_stage(b, x1_b, fx_b)


def _out_ffn(x2, att2, ml2, mod3, g_post_mix, g_pre_ffn, g_post_ffn, woa, wom, wfi, wfo, *, tiles_per_batch):
    t, d = x2.shape
    tm = FFN_TM
    hidden = wfo.shape[0]
    resident = lambda a: pl.BlockSpec(a.shape, lambda i: (0,) * a.ndim, pipeline_mode=pl.Buffered(1))
    mod = lambda k: pl.BlockSpec((1, 1, d), lambda i: (i // tiles_per_batch, 0, k))
    row = pl.BlockSpec((1, d), lambda i: (0, 0))
    return pl.pallas_call(
        functools.partial(_out_ffn_kernel, hidden=hidden),
        grid=(t // tm,),
        in_specs=[pl.BlockSpec((tm, d), lambda i: (i, 0)),
                  pl.BlockSpec((tm, ATT_WIDTH), lambda i: (i, 0)),
                  pl.BlockSpec((tm, ML_WIDTH), lambda i: (i, 0)),
                  mod(2), mod(3), mod(4), mod(5), row, row, row,
                  resident(woa), resident(wom), resident(wfi), resident(wfo)],
        out_specs=pl.BlockSpec((tm, d), lambda i: (i, 0)),
        out_shape=jax.ShapeDtypeStruct((t, d), F32),
        compiler_params=pltpu.CompilerParams(dimension_semantics=("arbitrary",),
                                             vmem_limit_bytes=VMEM_LIMIT),
        name="out_ffn",
    )(x2, att2, ml2, mod3, mod3, mod3, mod3, g_post_mix, g_pre_ffn, g_post_ffn, woa, wom, wfi, wfo)


def _rope_tables(n_tokens):
    pos = jnp.arange(n_tokens)
    row = (pos // GRID_W).astype(F32)
    col = (pos % GRID_W).astype(F32)
    half = ATT_HEAD_DIM // 4
    inv_freq = jnp.power(ROPE_BASE, -jnp.arange(half, dtype=F32) / half)
    ang_r = row[:, None] * inv_freq
    ang_c = col[:, None] * inv_freq
    z = jnp.zeros_like(ang_r)
    reps = LANES // ATT_HEAD_DIM
    cos = jnp.tile(jnp.concatenate([jnp.cos(ang_r)] * 2 + [jnp.cos(ang_c)] * 2, axis=1), (1, reps))
    sin_lo = jnp.tile(jnp.concatenate([-jnp.sin(ang_r), z, -jnp.sin(ang_c), z], axis=1), (1, reps))
    sin_hi = jnp.tile(jnp.concatenate([z, jnp.sin(ang_r), z, jnp.sin(ang_c)], axis=1), (1, reps))
    return cos, sin_lo, sin_hi


def _permute_heads(w, axis):
    shape = w.shape
    split = shape[:axis] + (ATT_HEADS, ATT_HEAD_DIM) + shape[axis + 1:]
    return jnp.take(w.reshape(split), jnp.array(_HEAD_PERM), axis=axis).reshape(shape)


def kernel(x, c, ctx, c_ctx, w_ada, b_ada, g_pre_mix, w_in, w_conv_qk, b_gates, attn_sink,
           g_mlstm_out, w_out, g_post_mix, g_pre_ffn, w_ffn_in, w_ffn_out, g_post_ffn):
    bsz, s, d = x.shape
    l = ctx.shape[1]
    assert w_ada.shape[0] == 1, "single-layer block"
    assert bsz < MOD_ROWS and s % INPROJ_TM == 0 and l % ML_CHUNK == 0 and s % FFN_TM == 0

    cc = jnp.zeros((MOD_ROWS, d), F32).at[:bsz].set(c).at[bsz].set(c_ctx)
    mod3 = _ada(cc, w_ada[0], b_ada).reshape(MOD_ROWS, 1, 6 * d)

    w = w_in[0]
    o_q, o_k, o_v = 0, ATT_WIDTH, ATT_WIDTH + ATT_KV_WIDTH
    o_mq = o_v + ATT_KV_WIDTH
    o_mv = o_mq + 2 * ML_QK_WIDTH
    o_mo = o_mv + ML_WIDTH
    o_mg = o_mo + ML_WIDTH
    w_q = _permute_heads(w[:, o_q:o_k], 1) * (ATT_HEAD_DIM ** -0.5 * LOG2_E)
    w_g = jnp.pad(w[:, o_mg:], ((0, 0), (0, LANES - ML_GATES)))
    shared = [w[:, o_k:o_v], w_g, w[:, o_mq:o_mv]]
    w_lat = jnp.concatenate([w_q] + shared + [w[:, o_mo:o_mg]], axis=1).astype(BF16)
    w_ctx = jnp.concatenate(shared, axis=1).astype(BF16)
    wvt = jnp.concatenate([w[:, o_v:o_mq], w[:, o_mv:o_mo], w[:, o_mg:]], axis=1).T.astype(BF16)
    bg_row = jnp.pad(b_gates, ((0, 0), (0, LANES - ML_GATES)))
    bg_col = b_gates.reshape(ML_GATES, 1)
    wc = w_conv_qk[0]

    q, k, vt, qkx, vtx, ox, gcx, grx = _inproj(
        x, mod3, lambda b: b, g_pre_mix, w_lat, wvt, bg_row, bg_col, wc, _rope_tables(s),
        tm=INPROJ_TM, gm=INPROJ_ROWS, latent=True)
    kc, vct, qkc, vtc, _, grc = _inproj(
        ctx, mod3, lambda b: bsz, g_pre_mix, w_ctx, wvt, bg_row, bg_col, wc, None,
        tm=l, gm=l, latent=False)

    sink_row = jnp.repeat(attn_sink[0][jnp.array(_HEAD_PERM)] * LOG2_E, ATT_BLOCK)[None, :]
    att = _attention(q, k, vt, kc, vct, sink_row)
    ml = _mlstm(qkx, vtx, ox, gcx, grx, qkc, vtc, grc, g_mlstm_out)

    wo = w_out[0]
    woa = _permute_heads(wo[:ATT_WIDTH], 0).astype(BF16)
    wom = wo[ATT_WIDTH:].astype(BF16)
    out = _out_ffn(x.reshape(bsz * s, d), att.reshape(bsz * s, ATT_WIDTH), ml.reshape(bsz * s, ML_WIDTH),
                   mod3, g_post_mix, g_pre_ffn, g_post_ffn, woa, wom,
                   w_ffn_in[0].astype(BF16), w_ffn_out[0].astype(BF16), tiles_per_batch=s // FFN_TM)
    return out.reshape(bsz, s, d)
```

```python
import functools

import jax
import jax.numpy as jnp
from jax import lax
from jax.experimental import pallas as pl
from jax.experimental.pallas import tpu as pltpu

F32 = jnp.float32
BF16 = jnp.bfloat16

EPS = 1e-6
GRID_W = 64
ROPE_BASE = 10000.0
LOG2_E = 1.4426950408889634

ATT_HEADS = 8
ATT_KV_HEADS = 2
ATT_GROUP = ATT_HEADS // ATT_KV_HEADS
ATT_HEAD_DIM = 64
ATT_BLOCK = 128
ATT_WIDTH = ATT_HEADS * ATT_HEAD_DIM
ATT_KV_WIDTH = ATT_KV_HEADS * ATT_HEAD_DIM
ATT_BLOCKS_PER_STEP = 16
ATT_ONES_ROWS = 16

ML_HEADS = 4
ML_V_DIM = 128
ML_QK_DIM = 64
ML_WIDTH = ML_HEADS * ML_V_DIM
ML_QK_WIDTH = ML_HEADS * ML_QK_DIM
ML_GATES = 4 * ML_HEADS
ML_CHUNK = 128

LANES = 128
SUBLANES = 8
VMEM_LIMIT = 56 * 1024 * 1024

INPROJ_TM = 1024
INPROJ_ROWS = 512
FFN_TM = 512
FFN_ROWS = 256
ADA_TN = 1536
MOD_ROWS = 16

_HEAD_PERM = tuple(h * ATT_GROUP + g for g in range(ATT_GROUP) for h in range(ATT_KV_HEADS))


def _silu(v):
    return v * jax.nn.sigmoid(v)


def _log_sigmoid(v):
    return jnp.minimum(v, 0.0) - jnp.log1p(jnp.exp(-jnp.abs(v)))


def _rms(v, g):
    return v * lax.rsqrt(jnp.mean(v * v, axis=-1, keepdims=True) + EPS) * g


def _dot(a, b):
    return jnp.dot(a, b, preferred_element_type=F32)


def _dot_nt(a, b):
    return lax.dot_general(a, b, (((1,), (1,)), ((), ())), preferred_element_type=F32)


def _ada_kernel(cc_ref, w_ref, b_ref, o_ref):
    a = _silu(cc_ref[...])
    o_ref[...] = _dot(a.astype(BF16), w_ref[...].astype(BF16)) + b_ref[...]


def _ada(cc, w, b):
    d, n = w.shape
    return pl.pallas_call(
        _ada_kernel,
        grid=(n // ADA_TN,),
        in_specs=[pl.BlockSpec((MOD_ROWS, d), lambda j: (0, 0)),
                  pl.BlockSpec((d, ADA_TN), lambda j: (0, j)),
                  pl.BlockSpec((1, ADA_TN), lambda j: (0, j))],
        out_specs=pl.BlockSpec((MOD_ROWS, ADA_TN), lambda j: (0, j)),
        out_shape=jax.ShapeDtypeStruct((MOD_ROWS, n), F32),
        compiler_params=pltpu.CompilerParams(dimension_semantics=("arbitrary",),
                                             vmem_limit_bytes=VMEM_LIMIT),
        name="ada",
    )(cc, w, b)


def _rope(v, cos, sin_lo, sin_hi):
    return (v * cos + pltpu.roll(v, LANES - 16, axis=1) * sin_lo
            + pltpu.roll(v, 16, axis=1) * sin_hi)


def _inproj_kernel(*refs, tm, gm, n_tiles, latent):
    if latent:
        (x_ref, xp_ref, xn_ref, sh_ref, sc_ref, g_ref, w_ref, wvt_ref, bgr_ref, bgc_ref,
         wc_ref, cos_ref, sl_ref, shi_ref,
         q_ref, k_ref, vt_ref, qk_ref, mvt_ref, o_ref, gcol_ref, grow_ref) = refs
    else:
        (x_ref, xp_ref, xn_ref, sh_ref, sc_ref, g_ref, w_ref, wvt_ref, bgr_ref, bgc_ref,
         wc_ref,
         k_ref, vt_ref, qk_ref, mvt_ref, gcol_ref, grow_ref) = refs
    i = pl.program_id(0)
    n_groups = tm // gm
    scale = g_ref[...] * (1.0 + sc_ref[0])
    shift = sh_ref[0]
    wc = wc_ref[...]
    row = lax.broadcasted_iota(jnp.int32, (gm, 1), 0)

    def project(lo, width, lhs):
        return _dot(lhs, w_ref[:, lo:lo + width])

    def group(r):
        rows = slice(r * gm, (r + 1) * gm)
        before = xp_ref[0] if r == 0 else x_ref[0, r * gm - SUBLANES:r * gm, :]
        after = xn_ref[0] if r == n_groups - 1 else x_ref[0, (r + 1) * gm:(r + 1) * gm + SUBLANES, :]
        keep_prev = jnp.where(i == 0, 0.0, 1.0) if r == 0 else 1.0
        keep_next = jnp.where(i == n_tiles - 1, 0.0, 1.0) if r == n_groups - 1 else 1.0
        xt = jnp.concatenate([x_ref[0, rows, :], before, after], axis=0)
        ms = jnp.mean(xt * xt, axis=-1, keepdims=True)
        hb = (xt * lax.rsqrt(ms + EPS) * scale + shift).astype(BF16)
        hm = hb[:gm]
        yield

        c = 0
        if latent:
            r_q = project(c, ATT_WIDTH, hm)
            c += ATT_WIDTH
        r_kg = project(c, ATT_KV_WIDTH + LANES, hm)
        c += ATT_KV_WIDTH + LANES
        yield

        if latent:
            cos, sl, shi = cos_ref[rows, :], sl_ref[rows, :], shi_ref[rows, :]
            for g in range(ATT_WIDTH // LANES):
                q_ref[0, rows, g * LANES:(g + 1) * LANES] = _rope(
                    r_q[:, g * LANES:(g + 1) * LANES], cos, sl, shi).astype(BF16)
        y = project(c, 2 * ML_QK_WIDTH, hb)
        c += 2 * ML_QK_WIDTH
        yield

        if latent:
            k_ref[0, rows, :] = _rope(r_kg[:, :LANES], cos, sl, shi).astype(BF16)
        else:
            k_ref[0, rows, :] = r_kg[:, :LANES].astype(BF16)
        gcol_ref[0, rows, :] = r_kg[:, LANES:] + bgr_ref[...]
        v_t = _dot_nt(wvt_ref[...], hm)
        yield

        ym = y[:gm]
        prev = jnp.where(row == 0, y[gm + SUBLANES - 1:gm + SUBLANES] * keep_prev,
                         pltpu.roll(ym, 1, axis=0))
        nxt = jnp.where(row == gm - 1, y[gm + SUBLANES:gm + SUBLANES + 1] * keep_next,
                        pltpu.roll(ym, gm - 1, axis=0))
        act = _silu(prev * wc[0:1] + ym * wc[1:2] + nxt * wc[2:3])
        qk_ref[0, rows, :ML_QK_WIDTH] = (act[:, :ML_QK_WIDTH] * (ML_QK_DIM ** -0.5)).astype(BF16)
        qk_ref[0, rows, ML_QK_WIDTH:] = act[:, ML_QK_WIDTH:].astype(BF16)
        if latent:
            r_o = project(c, ML_WIDTH, hm)
        yield

        g_t = v_t[ATT_KV_WIDTH + ML_WIDTH:] + bgc_ref[...]
        vt_ref[0, :, rows] = v_t[:ATT_KV_WIDTH].astype(BF16)
        for j in range(gm // ML_CHUNK):
            cols = slice(j * ML_CHUNK, (j + 1) * ML_CHUNK)
            mvt_ref[0, r * (gm // ML_CHUNK) + j] = v_t[ATT_KV_WIDTH:ATT_KV_WIDTH + ML_WIDTH, cols].astype(BF16)
            grow_ref[0, r * (gm // ML_CHUNK) + j] = g_t[:, cols]
        if latent:
            o_ref[0, rows, :] = jax.nn.sigmoid(r_o).astype(BF16)

    done = object()
    waiting = [group(r) for r in range(n_groups)]
    active = []
    while waiting or active:
        if waiting:
            active.append(waiting.pop(0))
        active = [g for g in active if next(g, done) is not done]


def _inproj(x, mod3, mod_row, g_pre, w_main, wvt, bg_row, bg_col, wc, rope_tabs, *, tm, gm, latent):
    bsz, t, d = x.shape
    n_tiles = t // tm
    hb = tm // SUBLANES
    n_hblk = t // SUBLANES
    n = w_main.shape[1]

    def const(shape):
        return pl.BlockSpec(shape, lambda i, b: (0,) * len(shape))

    in_specs = [
        pl.BlockSpec((1, tm, d), lambda i, b: (b, i, 0)),
        pl.BlockSpec((1, SUBLANES, d), lambda i, b: (b, jnp.maximum(i * hb - 1, 0), 0)),
        pl.BlockSpec((1, SUBLANES, d), lambda i, b: (b, jnp.minimum((i + 1) * hb, n_hblk - 1), 0)),
        pl.BlockSpec((1, 1, d), lambda i, b: (mod_row(b), 0, 0)),
        pl.BlockSpec((1, 1, d), lambda i, b: (mod_row(b), 0, 1)),
        const((1, d)), const((d, n)), const((ATT_KV_WIDTH + ML_WIDTH + ML_GATES, d)),
        const((1, LANES)), const((ML_GATES, 1)), const((3, 2 * ML_QK_WIDTH)),
    ]
    args = [x, x, x, mod3, mod3, g_pre, w_main, wvt, bg_row, bg_col, wc]
    tok = lambda width, dt: (pl.BlockSpec((1, tm, width), lambda i, b: (b, i, 0)),
                             jax.ShapeDtypeStruct((bsz, t, width), dt))
    chunked = lambda rows, dt: (pl.BlockSpec((1, tm // ML_CHUNK, rows, ML_CHUNK), lambda i, b: (b, i, 0, 0)),
                                jax.ShapeDtypeStruct((bsz, t // ML_CHUNK, rows, ML_CHUNK), dt))
    outs = []
    if latent:
        in_specs += [pl.BlockSpec((tm, LANES), lambda i, b: (i, 0))] * 3
        args += list(rope_tabs)
        outs.append(tok(ATT_WIDTH, BF16))
    outs.append(tok(ATT_KV_WIDTH, BF16))
    outs.append((pl.BlockSpec((1, LANES, tm), lambda i, b: (b, 0, i)),
                 jax.ShapeDtypeStruct((bsz, ATT_KV_WIDTH, t), BF16)))
    outs.append(tok(2 * ML_QK_WIDTH, BF16))
    outs.append(chunked(ML_WIDTH, BF16))
    if latent:
        outs.append(tok(ML_WIDTH, BF16))
    outs.append(tok(LANES, F32))
    outs.append(chunked(ML_GATES, F32))
    return pl.pallas_call(
        functools.partial(_inproj_kernel, tm=tm, gm=gm, n_tiles=n_tiles, latent=latent),
        grid=(n_tiles, bsz),
        in_specs=in_specs,
        out_specs=[o[0] for o in outs],
        out_shape=[o[1] for o in outs],
        compiler_params=pltpu.CompilerParams(dimension_semantics=("arbitrary", "arbitrary"),
                                             vmem_limit_bytes=VMEM_LIMIT),
        name="inproj_latent" if latent else "inproj_context",
    )(*args)


def _attn_kernel(q_ref, kp_ref, kc_ref, kn_ref, kx_ref, vp_ref, vc_ref, vn_ref, vx_ref, sink_ref,
                 o_ref, *, n_tiles):
    i = pl.program_id(1)
    blk = ATT_BLOCK
    lane = lax.broadcasted_iota(jnp.int32, (blk, LANES), 1)
    zero = jnp.zeros((blk, LANES), BF16)
    half_groups = ATT_GROUP // 2
    n_slots = 2 * half_groups
    sink = sink_ref[:, :n_slots * blk], sink_ref[:, n_slots * blk:]

    def stack_heads(q, half):
        parts = []
        for g in range(half * half_groups, (half + 1) * half_groups):
            slab = q[:, g * LANES:(g + 1) * LANES]
            parts.append(jnp.where(lane < ATT_HEAD_DIM, slab, zero))
            parts.append(jnp.where(lane >= ATT_HEAD_DIM, slab, zero))
        return jnp.concatenate(parts, axis=0)

    key = lax.broadcasted_iota(jnp.int32, (blk, blk), 0)
    qry = lax.broadcasted_iota(jnp.int32, (blk, blk), 1)
    ninf = jnp.full((blk, blk), -jnp.inf, F32)
    bias_prev = jnp.where(key >= qry, 0.0, ninf)
    bias_next = jnp.where(key <= qry, 0.0, ninf)
    slots = lambda b: jnp.concatenate([b] * n_slots, axis=1)

    def scores(qs, k_prev, k_cur, k_next):
        return [_dot_nt(k_prev, qs), _dot_nt(k_cur, qs), _dot_nt(k_next, qs), _dot_nt(kx_ref[0], qs)]

    def softmax(s, b_prev, b_next, sink_h):
        s = jnp.concatenate([s[0] + slots(b_prev), s[1], s[2] + slots(b_next), s[3]], axis=0)
        m = jnp.maximum(sink_h, jnp.max(s, axis=0, keepdims=True))
        return jnp.exp2(s - m).astype(BF16), jnp.exp2(sink_h - m)

    n_keys = 3 * blk + kx_ref.shape[1]
    ones_rows = jnp.ones((ATT_ONES_ROWS, n_keys), BF16)

    def weighted_values(p, v_prev, v_cur, v_next):
        vt = jnp.concatenate([v_prev, v_cur, v_next, vx_ref[0]], axis=1)
        return _dot(jnp.concatenate([vt, ones_rows], axis=0), p)

    dim = lax.broadcasted_iota(jnp.int32, (LANES, blk), 0)

    def emit(rows, half, ot, p_sink):
        ot = ot[:LANES] * (1.0 / (ot[LANES:LANES + 1] + p_sink))
        for j in range(half_groups):
            g = half * half_groups + j
            a = ot[:, (2 * j) * blk:(2 * j + 1) * blk]
            b = ot[:, (2 * j + 1) * blk:(2 * j + 2) * blk]
            o_ref[0, rows, g * LANES:(g + 1) * LANES] = jnp.where(dim < ATT_HEAD_DIM, a, b).T.astype(BF16)

    nblk = q_ref.shape[1] // blk
    rows_of = [slice(b * blk, (b + 1) * blk) for b in range(nblk)]
    k_blocks = [kp_ref[0]] + [kc_ref[0, r, :] for r in rows_of] + [kn_ref[0]]
    v_blocks = [vp_ref[0]] + [vc_ref[0, :, r] for r in rows_of] + [vn_ref[0]]
    keys = [tuple(k_blocks[b:b + 3]) for b in range(nblk)]
    vals = [tuple(v_blocks[b:b + 3]) for b in range(nblk)]
    bias = [(jnp.where(i == 0, ninf, bias_prev) if b == 0 else bias_prev,
             jnp.where(i == n_tiles - 1, ninf, bias_next) if b == nblk - 1 else bias_next)
            for b in range(nblk)]
    units = [(b, half) for b in range(nblk) for half in range(2)]

    def stage_scores(u):
        b, half = u
        return scores(stack_heads(q_ref[0, rows_of[b], :], half), *keys[b])

    def stage_softmax(u, s):
        b, half = u
        return softmax(s, *bias[b], sink[half])

    def stage_values(u, p):
        return weighted_values(p, *vals[u[0]])

    def stage_emit(u, ot, p_sink):
        emit(rows_of[u[0]], u[1], ot, p_sink)

    n_units = len(units)
    s, p, ot = {}, {}, {}
    s[0] = stage_scores(units[0])
    for t in range(n_units):
        if t + 1 < n_units:
            s[t + 1] = stage_scores(units[t + 1])
        p[t] = stage_softmax(units[t], s.pop(t))
        if t >= 1:
            ot[t - 1] = stage_values(units[t - 1], p[t - 1][0])
        if t >= 2:
            stage_emit(units[t - 2], ot.pop(t - 2), p.pop(t - 2)[1])
    ot[n_units - 1] = stage_values(units[n_units - 1], p[n_units - 1][0])
    for t in (n_units - 2, n_units - 1):
        stage_emit(units[t], ot.pop(t), p.pop(t)[1])


def _attention(q, k, vt, kx, vxt, sink_row):
    bsz, s, _ = q.shape
    l = kx.shape[1]
    blk = ATT_BLOCK
    per = ATT_BLOCKS_PER_STEP
    n_tiles = s // (per * blk)
    nb = s // blk
    prev = lambda i: jnp.maximum(per * i - 1, 0)
    nxt = lambda i: jnp.minimum(per * (i + 1), nb - 1)
    return pl.pallas_call(
        functools.partial(_attn_kernel, n_tiles=n_tiles),
        grid=(bsz, n_tiles),
        in_specs=[pl.BlockSpec((1, per * blk, ATT_WIDTH), lambda b, i: (b, i, 0)),
                  pl.BlockSpec((1, blk, LANES), lambda b, i: (b, prev(i), 0)),
                  pl.BlockSpec((1, per * blk, LANES), lambda b, i: (b, i, 0)),
                  pl.BlockSpec((1, blk, LANES), lambda b, i: (b, nxt(i), 0)),
                  pl.BlockSpec((1, l, LANES), lambda b, i: (b, 0, 0)),
                  pl.BlockSpec((1, LANES, blk), lambda b, i: (b, 0, prev(i))),
                  pl.BlockSpec((1, LANES, per * blk), lambda b, i: (b, 0, i)),
                  pl.BlockSpec((1, LANES, blk), lambda b, i: (b, 0, nxt(i))),
                  pl.BlockSpec((1, LANES, l), lambda b, i: (b, 0, 0)),
                  pl.BlockSpec((1, ATT_HEADS * blk), lambda b, i: (0, 0))],
        out_specs=pl.BlockSpec((1, per * blk, ATT_WIDTH), lambda b, i: (b, i, 0)),
        out_shape=jax.ShapeDtypeStruct((bsz, s, ATT_WIDTH), BF16),
        compiler_params=pltpu.CompilerParams(dimension_semantics=("arbitrary", "arbitrary"),
                                             vmem_limit_bytes=VMEM_LIMIT),
        name="attention",
    )(q, k, k, k, kx, vt, vt, vt, vxt, sink_row)


ML_STATE_ROWS = ML_V_DIM + SUBLANES


def _mlstm_kernel(qkx_ref, vtx_ref, ox_ref, gcx_ref, grx_ref, qkc_ref, vtc_ref, grc_ref,
                  gain_ref, out_ref, sin_ref, st_ref, ucol_ref, mrun_ref, wint_ref, floor_ref,
                  wkey_ref, decay_ref, *, ncx, ncc):
    lc = ML_CHUNK
    qkw = ML_QK_WIDTH
    sr = ML_STATE_ROWS
    ng = ML_GATES
    hg = ML_GATES // 2
    st_ref[...] = jnp.zeros(st_ref.shape, F32)

    rr = lax.broadcasted_iota(jnp.int32, (lc, lc), 0)
    cc = lax.broadcasted_iota(jnp.int32, (lc, lc), 1)
    tril = rr >= cc
    triu = rr <= cc
    tril_b = jnp.where(tril, 1.0, 0.0).astype(BF16)
    triu_b = jnp.where(triu, 1.0, 0.0).astype(BF16)
    head_of_lane = lax.broadcasted_iota(jnp.int32, (lc, qkw), 1) // ML_QK_DIM
    own_block = jnp.concatenate(
        [lax.broadcasted_iota(jnp.int32, (sr, qkw), 1) // ML_QK_DIM == h for h in range(ML_HEADS)], axis=0)

    pad_rows = jnp.zeros((SUBLANES - 1, lc), F32)

    def split(v):
        hi = v.astype(BF16)
        return hi, (v - hi.astype(F32)).astype(BF16)

    def gate_rows(g):
        n16 = g.shape[0]
        fwd_row = (lax.broadcasted_iota(jnp.int32, g.shape, 0) & (ng - 1)) < hg
        lane = lax.broadcasted_iota(jnp.int32, g.shape, 1)
        hi, lo = split(_log_sigmoid(g))
        cat = jnp.concatenate([hi, lo], axis=0)
        bu = _dot(cat, triu_b)
        bl = _dot(cat, tril_b)
        b = jnp.where(fwd_row, bu[:n16] + bu[n16:], bl[:n16] + bl[n16:])
        b = pltpu.roll(b, n16 - ML_HEADS, axis=0)
        u = g - b
        run_f = run_b = u
        k = 1
        while k < lc:
            run_f = jnp.maximum(run_f, jnp.where(lane >= k, pltpu.roll(run_f, k, axis=1), -jnp.inf))
            run_b = jnp.maximum(run_b, jnp.where(lane < lc - k, pltpu.roll(run_b, lc - k, axis=1), -jnp.inf))
            k *= 2
        run = jnp.where(fwd_row, run_f, run_b)

        def at_end(a):
            return jnp.where(fwd_row, jnp.broadcast_to(a[:, lc - 1:lc], a.shape),
                             jnp.broadcast_to(a[:, 0:1], a.shape))

        return u, b, run, at_end(b), at_end(run)

    def derived(u, b, run, b_tot, run_end, m_in):
        m_run = jnp.maximum(run, m_in)
        m_out = b_tot + jnp.maximum(run_end, m_in)
        return (m_run * LOG2_E, jnp.exp(m_in - m_run), jnp.exp(-(b + m_run)),
                jnp.exp(b_tot + u - m_out), jnp.exp(b_tot + m_in - m_out))

    gc_rows = gate_rows(grc_ref[0].reshape(ncc * ng, lc))
    gx_rows = gate_rows(grx_ref[0].reshape(ncx * ng, lc))

    def scan_m(rows, n, m_f, m_b):
        _, _, _, b_tot, run_end = rows
        part = lambda a, c, d: a[c * ng + d * hg:c * ng + (d + 1) * hg]
        ins_f, ins_b = [], [None] * n
        for c in range(n):
            ins_f.append(m_f)
            m_f = part(b_tot, c, 0) + jnp.maximum(part(run_end, c, 0), m_f)
        for c in reversed(range(n)):
            ins_b[c] = m_b
            m_b = part(b_tot, c, 1) + jnp.maximum(part(run_end, c, 1), m_b)
        return jnp.concatenate([x for c in range(n) for x in (ins_f[c], ins_b[c])], axis=0), m_f, m_b

    m0 = jnp.zeros((hg, lc), F32)
    m_in_c, m_f, m_b = scan_m(gc_rows, ncc, m0, m0)
    m_in_x, _, _ = scan_m(gx_rows, ncx, m_f, m_b)
    _, _, _, wkey_c, decay_c = derived(*gc_rows, m_in_c)
    for ref, val in zip((mrun_ref, wint_ref, floor_ref, wkey_ref, decay_ref), derived(*gx_rows, m_in_x)):
        ref[...] = val

    fwd_col = (lax.broadcasted_iota(jnp.int32, (lc, LANES), 1) & (ng - 1)) < hg

    def token_major_u(j):
        rows = pl.ds(pl.multiple_of(j * lc, lc), lc)
        gcol = gcx_ref[0, rows, :]
        hi, lo = split(_log_sigmoid(gcol))
        cat = jnp.concatenate([hi, lo], axis=1)
        bl = _dot(tril_b, cat)
        bu = _dot(triu_b, cat)
        b = jnp.where(fwd_col, bl[:, :LANES] + bl[:, LANES:], bu[:, :LANES] + bu[:, LANES:])
        ucol_ref[rows, :] = (gcol - pltpu.roll(b, LANES - ML_HEADS, axis=1)) * LOG2_E

    def advance(dirn, k4, vt, w_key, decay):
        st = st_ref[dirn]
        pieces, decays = [], []
        for h in range(ML_HEADS):
            c = hg * dirn + h
            pieces += [vt[h * ML_V_DIM:(h + 1) * ML_V_DIM].astype(F32) * w_key[c:c + 1], w_key[c:c + 1], pad_rows]
            decays.append(jnp.broadcast_to(jnp.concatenate([decay[c:c + 1]] * (qkw // lc), axis=1), (sr, qkw)))
        upd = _dot(jnp.concatenate(pieces, axis=0).astype(BF16), k4)
        st_ref[dirn] = jnp.concatenate(decays, axis=0) * st + jnp.where(own_block, upd, 0.0)

    for j in range(ncc):
        for dirn, cj in ((0, j), (1, ncc - 1 - j)):
            grows = slice(cj * ng, (cj + 1) * ng)
            advance(dirn, qkc_ref[0, cj * lc:(cj + 1) * lc, qkw:], vtc_ref[0, cj], wkey_c[grows], decay_c[grows])

    def scan_body(j, carry):
        for dirn, cj in ((0, j), (1, ncx - 1 - j)):
            rows = pl.ds(pl.multiple_of(cj * lc, lc), lc)
            grows = pl.ds(pl.multiple_of(cj * ng, ng), ng)
            sin_ref[dirn, cj] = st_ref[dirn].astype(BF16)
            advance(dirn, qkx_ref[0, rows, qkw:], vtx_ref[0, cj], wkey_ref[grows, :], decay_ref[grows, :])
        token_major_u(j)
        return carry

    lax.fori_loop(0, ncx, scan_body, 0, unroll=8)

    gain = gain_ref[...]

    def out_body(j, carry):
        rows = pl.ds(pl.multiple_of(j * lc, lc), lc)
        grows = pl.ds(pl.multiple_of(j * ng, ng), ng)
        q4 = qkx_ref[0, rows, :qkw]
        k4 = qkx_ref[0, rows, qkw:]
        vt = vtx_ref[0, j]
        u_col = ucol_ref[rows, :]
        m_run, w_int, floor = mrun_ref[grows, :], wint_ref[grows, :], floor_ref[grows, :]
        zero = jnp.zeros_like(q4)
        qs = jnp.concatenate([jnp.where(head_of_lane == h, q4, zero) for h in range(ML_HEADS)], axis=0)
        qk_t = _dot_nt(k4, qs)
        hsum = [None] * ML_HEADS
        for dirn in range(2):
            inter = _dot_nt(sin_ref[dirn, j], q4)
            valid = triu if dirn == 0 else tril
            for h in range(ML_HEADS):
                c = hg * dirn + h
                e = jnp.exp2(jnp.where(valid, u_col[:, c:c + 1] - m_run[c:c + 1], -jnp.inf))
                s_t = qk_t[:, h * lc:(h + 1) * lc] * e
                num = _dot(vt[h * ML_V_DIM:(h + 1) * ML_V_DIM], s_t.astype(BF16))
                num = num + w_int[c:c + 1] * inter[h * sr:h * sr + ML_V_DIM]
                den = (jnp.sum(s_t, axis=0, keepdims=True)
                       + w_int[c:c + 1] * inter[h * sr + ML_V_DIM:h * sr + ML_V_DIM + 1])
                hv = num * (1.0 / jnp.maximum(jnp.abs(den), floor[c:c + 1]))
                hsum[h] = hv if dirn == 0 else hsum[h] + hv
        for h in range(ML_HEADS):
            cols = slice(h * ML_V_DIM, (h + 1) * ML_V_DIM)
            hs = hsum[h]
            hn = hs * lax.rsqrt(jnp.mean(hs * hs, axis=0, keepdims=True) + EPS)
            out_ref[0, rows, cols] = (hn.T * gain[:, cols] * ox_ref[0, rows, cols].astype(F32)).astype(BF16)
        return carry

    lax.fori_loop(0, ncx, out_body, 0, unroll=8)


def _mlstm(qkx, vtx, ox, gcx, grx, qkc, vtc, grc, gain):
    bsz, s, _ = qkx.shape
    ncx, ncc = s // ML_CHUNK, qkc.shape[1] // ML_CHUNK
    per_b = lambda a: pl.BlockSpec((1,) + a.shape[1:], lambda b: (b,) + (0,) * (a.ndim - 1))
    ins = [qkx, vtx, ox, gcx, grx, qkc, vtc, grc]
    return pl.pallas_call(
        functools.partial(_mlstm_kernel, ncx=ncx, ncc=ncc),
        grid=(bsz,),
        in_specs=[per_b(a) for a in ins] + [pl.BlockSpec((1, ML_WIDTH), lambda b: (0, 0))],
        out_specs=pl.BlockSpec((1, s, ML_WIDTH), lambda b: (b, 0, 0)),
        out_shape=jax.ShapeDtypeStruct((bsz, s, ML_WIDTH), BF16),
        scratch_shapes=[pltpu.VMEM((2, ncx, ML_HEADS * ML_STATE_ROWS, ML_QK_WIDTH), BF16),
                        pltpu.VMEM((2, ML_HEADS * ML_STATE_ROWS, ML_QK_WIDTH), F32),
                        pltpu.VMEM((s, LANES), F32)]
                       + [pltpu.VMEM((ncx * ML_GATES, ML_CHUNK), F32)] * 5,
        compiler_params=pltpu.CompilerParams(dimension_semantics=("arbitrary",),
                                             vmem_limit_bytes=VMEM_LIMIT),
        name="mlstm",
    )(*ins, gain)


def _out_ffn_kernel(x_ref, att_ref, ml_ref, gtm_ref, shf_ref, scf_ref, gtf_ref,
                    gpm_ref, gpf_ref, gqf_ref, woa_ref, wom_ref, wfi_ref, wfo_ref, o_ref, *, hidden):
    rows = [slice(r * FFN_ROWS, (r + 1) * FFN_ROWS) for r in range(x_ref.shape[0] // FFN_ROWS)]

    def mix_stage(r):
        return _dot(att_ref[r, :], woa_ref[...]) + _dot(ml_ref[r, :], wom_ref[...])

    def norm_stage(r, mix):
        x1 = x_ref[r, :] + gtm_ref[0] * _rms(mix, gpm_ref[...])
        return x1, (_rms(x1, gpf_ref[...]) * (1.0 + scf_ref[0]) + shf_ref[0]).astype(BF16)

    def act_stage(gu):
        return (_silu(gu[:, :hidden]) * gu[:, hidden:]).astype(BF16)

    def out_stage(r, x1, fx):
        o_ref[r, :] = x1 + gtf_ref[0] * _rms(fx, gqf_ref[...])

    a, b = rows
    mix_a = mix_stage(a)
    mix_b = mix_stage(b)
    x1_a, h_a = norm_stage(a, mix_a)
    gu_a = _dot(h_a, wfi_ref[...])
    x1_b, h_b = norm_stage(b, mix_b)
    gu_b = _dot(h_b, wfi_ref[...])
    act_a = act_stage(gu_a)
    fx_a = _dot(act_a, wfo_ref[...])
    act_b = act_stage(gu_b)
    fx_b = _dot(act_b, wfo_ref[...])
    out_stage(a, x1_a, fx_a)
    out_stage(b, x1_b, fx_b)


def _out_ffn(x2, att2, ml2, mod3, g_post_mix, g_pre_ffn, g_post_ffn, woa, wom, wfi, wfo, *, tiles_per_batch):
    t, d = x2.shape
    tm = FFN_TM
    hidden = wfo.shape[0]
    resident = lambda a: pl.BlockSpec(a.shape, lambda i: (0,) * a.ndim, pipeline_mode=pl.Buffered(1))
    mod = lambda k: pl.BlockSpec((1, 1, d), lambda i: (i // tiles_per_batch, 0, k))
    row = pl.BlockSpec((1, d), lambda i: (0, 0))
    return pl.pallas_call(
        functools.partial(_out_ffn_kernel, hidden=hidden),
        grid=(t // tm,),
        in_specs=[pl.BlockSpec((tm, d), lambda i: (i, 0)),
                  pl.BlockSpec((tm, ATT_WIDTH), lambda i: (i, 0)),
                  pl.BlockSpec((tm, ML_WIDTH), lambda i: (i, 0)),
                  mod(2), mod(3), mod(4), mod(5), row, row, row,
                  resident(woa), resident(wom), resident(wfi), resident(wfo)],
        out_specs=pl.BlockSpec((tm, d), lambda i: (i, 0)),
        out_shape=jax.ShapeDtypeStruct((t, d), F32),
        compiler_params=pltpu.CompilerParams(dimension_semantics=("arbitrary",),
                                             vmem_limit_bytes=VMEM_LIMIT),
        name="out_ffn",
    )(x2, att2, ml2, mod3, mod3, mod3, mod3, g_post_mix, g_pre_ffn, g_post_ffn, woa, wom, wfi, wfo)


def _rope_tables(n_tokens):
    pos = jnp.arange(n_tokens)
    row = (pos // GRID_W).astype(F32)
    col = (pos % GRID_W).astype(F32)
    half = ATT_HEAD_DIM // 4
    inv_freq = jnp.power(ROPE_BASE, -jnp.arange(half, dtype=F32) / half)
    ang_r = row[:, None] * inv_freq
    ang_c = col[:, None] * inv_freq
    z = jnp.zeros_like(ang_r)
    reps = LANES // ATT_HEAD_DIM
    cos = jnp.tile(jnp.concatenate([jnp.cos(ang_r)] * 2 + [jnp.cos(ang_c)] * 2, axis=1), (1, reps))
    sin_lo = jnp.tile(jnp.concatenate([-jnp.sin(ang_r), z, -jnp.sin(ang_c), z], axis=1), (1, reps))
    sin_hi = jnp.tile(jnp.concatenate([z, jnp.sin(ang_r), z, jnp.sin(ang_c)], axis=1), (1, reps))
    return cos, sin_lo, sin_hi


def _permute_heads(w, axis):
    shape = w.shape
    split = shape[:axis] + (ATT_HEADS, ATT_HEAD_DIM) + shape[axis + 1:]
    return jnp.take(w.reshape(split), jnp.array(_HEAD_PERM), axis=axis).reshape(shape)


def kernel(x, c, ctx, c_ctx, w_ada, b_ada, g_pre_mix, w_in, w_conv_qk, b_gates, attn_sink,
           g_mlstm_out, w_out, g_post_mix, g_pre_ffn, w_ffn_in, w_ffn_out, g_post_ffn):
    bsz, s, d = x.shape
    l = ctx.shape[1]
    assert w_ada.shape[0] == 1, "single-layer block"
    assert bsz < MOD_ROWS and s % INPROJ_TM == 0 and l % ML_CHUNK == 0 and s % FFN_TM == 0

    cc = jnp.zeros((MOD_ROWS, d), F32).at[:bsz].set(c).at[bsz].set(c_ctx)
    mod3 = _ada(cc, w_ada[0], b_ada).reshape(MOD_ROWS, 1, 6 * d)

    w = w_in[0]
    o_q, o_k, o_v = 0, ATT_WIDTH, ATT_WIDTH + ATT_KV_WIDTH
    o_mq = o_v + ATT_KV_WIDTH
    o_mv = o_mq + 2 * ML_QK_WIDTH
    o_mo = o_mv + ML_WIDTH
    o_mg = o_mo + ML_WIDTH
    w_q = _permute_heads(w[:, o_q:o_k], 1) * (ATT_HEAD_DIM ** -0.5 * LOG2_E)
    w_g = jnp.pad(w[:, o_mg:], ((0, 0), (0, LANES - ML_GATES)))
    shared = [w[:, o_k:o_v], w_g, w[:, o_mq:o_mv]]
    w_lat = jnp.concatenate([w_q] + shared + [w[:, o_mo:o_mg]], axis=1).astype(BF16)
    w_ctx = jnp.concatenate(shared, axis=1).astype(BF16)
    wvt = jnp.concatenate([w[:, o_v:o_mq], w[:, o_mv:o_mo], w[:, o_mg:]], axis=1).T.astype(BF16)
    bg_row = jnp.pad(b_gates, ((0, 0), (0, LANES - ML_GATES)))
    bg_col = b_gates.reshape(ML_GATES, 1)
    wc = w_conv_qk[0]

    q, k, vt, qkx, vtx, ox, gcx, grx = _inproj(
        x, mod3, lambda b: b, g_pre_mix, w_lat, wvt, bg_row, bg_col, wc, _rope_tables(s),
        tm=INPROJ_TM, gm=INPROJ_ROWS, latent=True)
    kc, vct, qkc, vtc, _, grc = _inproj(
        ctx, mod3, lambda b: bsz, g_pre_mix, w_ctx, wvt, bg_row, bg_col, wc, None,
        tm=l, gm=l, latent=False)

    sink_row = jnp.repeat(attn_sink[0][jnp.array(_HEAD_PERM)] * LOG2_E, ATT_BLOCK)[None, :]
    att = _attention(q, k, vt, kc, vct, sink_row)
    ml = _mlstm(qkx, vtx, ox, gcx, grx, qkc, vtc, grc, g_mlstm_out)

    wo = w_out[0]
    woa = _permute_heads(wo[:ATT_WIDTH], 0).astype(BF16)
    wom = wo[ATT_WIDTH:].astype(BF16)
    out = _out_ffn(x.reshape(bsz * s, d), att.reshape(bsz * s, ATT_WIDTH), ml.reshape(bsz * s, ML_WIDTH),
                   mod3, g_post_mix, g_pre_ffn, g_post_ffn, woa, wom,
                   w_ffn_in[0].astype(BF16), w_ffn_out[0].astype(BF16), tiles_per_batch=s // FFN_TM)
    return out.reshape(bsz, s, d)
```

```python
import functools

import jax
import jax.numpy as jnp
from jax import lax
from jax.experimental import pallas as pl
from jax.experimental.pallas import tpu as pltpu

F32 = jnp.float32
BF16 = jnp.bfloat16

EPS = 1e-6
GRID_W = 64
ROPE_BASE = 10000.0
LOG2_E = 1.4426950408889634

ATT_HEADS = 8
ATT_KV_HEADS = 2
ATT_GROUP = ATT_HEADS // ATT_KV_HEADS
ATT_HEAD_DIM = 64
ATT_BLOCK = 128
ATT_WIDTH = ATT_HEADS * ATT_HEAD_DIM
ATT_KV_WIDTH = ATT_KV_HEADS * ATT_HEAD_DIM
ATT_BLOCKS_PER_STEP = 16
ATT_ONES_ROWS = 16

ML_HEADS = 4
ML_V_DIM = 128
ML_QK_DIM = 64
ML_WIDTH = ML_HEADS * ML_V_DIM
ML_QK_WIDTH = ML_HEADS * ML_QK_DIM
ML_GATES = 4 * ML_HEADS
ML_CHUNK = 128

LANES = 128
SUBLANES = 8
VMEM_LIMIT = 56 * 1024 * 1024

INPROJ_TM = 1024
INPROJ_ROWS = 512
FFN_TM = 512
FFN_ROWS = 256
FFN_STAGE_ROWS_WIDE = 64
FFN_STAGE_ROWS_NARROW = 352
ADA_TN = 1536
MOD_ROWS = 16

_HEAD_PERM = tuple(h * ATT_GROUP + g for g in range(ATT_GROUP) for h in range(ATT_KV_HEADS))


def _silu(v):
    return v * jax.nn.sigmoid(v)


def _log_sigmoid(v):
    return jnp.minimum(v, 0.0) - jnp.log1p(jnp.exp(-jnp.abs(v)))


def _rms(v, g):
    return v * lax.rsqrt(jnp.mean(v * v, axis=-1, keepdims=True) + EPS) * g


def _dot(a, b):
    return jnp.dot(a, b, preferred_element_type=F32)


def _dot_nt(a, b):
    return lax.dot_general(a, b, (((1,), (1,)), ((), ())), preferred_element_type=F32)


def _ada_kernel(cc_ref, w_ref, b_ref, o_ref):
    a = _silu(cc_ref[...])
    o_ref[...] = _dot(a.astype(BF16), w_ref[...].astype(BF16)) + b_ref[...]


def _ada(cc, w, b):
    d, n = w.shape
    return pl.pallas_call(
        _ada_kernel,
        grid=(n // ADA_TN,),
        in_specs=[pl.BlockSpec((MOD_ROWS, d), lambda j: (0, 0)),
                  pl.BlockSpec((d, ADA_TN), lambda j: (0, j)),
                  pl.BlockSpec((1, ADA_TN), lambda j: (0, j))],
        out_specs=pl.BlockSpec((MOD_ROWS, ADA_TN), lambda j: (0, j)),
        out_shape=jax.ShapeDtypeStruct((MOD_ROWS, n), F32),
        compiler_params=pltpu.CompilerParams(dimension_semantics=("arbitrary",),
                                             vmem_limit_bytes=VMEM_LIMIT),
        name="ada",
    )(cc, w, b)


def _rope(v, cos, sin_lo, sin_hi):
    return (v * cos + pltpu.roll(v, LANES - 16, axis=1) * sin_lo
            + pltpu.roll(v, 16, axis=1) * sin_hi)


def _inproj_kernel(*refs, tm, gm, n_tiles, latent):
    if latent:
        (x_ref, xp_ref, xn_ref, sh_ref, sc_ref, g_ref, w_ref, wvt_ref, bgr_ref, bgc_ref,
         wc_ref, cos_ref, sl_ref, shi_ref,
         q_ref, k_ref, vt_ref, qk_ref, mvt_ref, o_ref, gcol_ref, grow_ref) = refs
    else:
        (x_ref, xp_ref, xn_ref, sh_ref, sc_ref, g_ref, w_ref, wvt_ref, bgr_ref, bgc_ref,
         wc_ref,
         k_ref, vt_ref, qk_ref, mvt_ref, gcol_ref, grow_ref) = refs
    i = pl.program_id(0)
    n_groups = tm // gm
    scale = g_ref[...] * (1.0 + sc_ref[0])
    shift = sh_ref[0]
    wc = wc_ref[...]
    row = lax.broadcasted_iota(jnp.int32, (gm, 1), 0)

    def project(lo, width, lhs):
        return _dot(lhs, w_ref[:, lo:lo + width])

    def group(r):
        rows = slice(r * gm, (r + 1) * gm)
        before = xp_ref[0] if r == 0 else x_ref[0, r * gm - SUBLANES:r * gm, :]
        after = xn_ref[0] if r == n_groups - 1 else x_ref[0, (r + 1) * gm:(r + 1) * gm + SUBLANES, :]
        keep_prev = jnp.where(i == 0, 0.0, 1.0) if r == 0 else 1.0
        keep_next = jnp.where(i == n_tiles - 1, 0.0, 1.0) if r == n_groups - 1 else 1.0
        xt = jnp.concatenate([x_ref[0, rows, :], before, after], axis=0)
        ms = jnp.mean(xt * xt, axis=-1, keepdims=True)
        hb = (xt * lax.rsqrt(ms + EPS) * scale + shift).astype(BF16)
        hm = hb[:gm]
        yield

        c = 0
        if latent:
            r_q = project(c, ATT_WIDTH, hm)
            c += ATT_WIDTH
        r_kg = project(c, ATT_KV_WIDTH + LANES, hm)
        c += ATT_KV_WIDTH + LANES
        yield

        if latent:
            cos, sl, shi = cos_ref[rows, :], sl_ref[rows, :], shi_ref[rows, :]
            for g in range(ATT_WIDTH // LANES):
                q_ref[0, rows, g * LANES:(g + 1) * LANES] = _rope(
                    r_q[:, g * LANES:(g + 1) * LANES], cos, sl, shi).astype(BF16)
        y = project(c, 2 * ML_QK_WIDTH, hb)
        c += 2 * ML_QK_WIDTH
        yield

        if latent:
            k_ref[0, rows, :] = _rope(r_kg[:, :LANES], cos, sl, shi).astype(BF16)
        else:
            k_ref[0, rows, :] = r_kg[:, :LANES].astype(BF16)
        gcol_ref[0, rows, :] = r_kg[:, LANES:] + bgr_ref[...]
        v_t = _dot_nt(wvt_ref[...], hm)
        yield

        ym = y[:gm]
        prev = jnp.where(row == 0, y[gm + SUBLANES - 1:gm + SUBLANES] * keep_prev,
                         pltpu.roll(ym, 1, axis=0))
        nxt = jnp.where(row == gm - 1, y[gm + SUBLANES:gm + SUBLANES + 1] * keep_next,
                        pltpu.roll(ym, gm - 1, axis=0))
        act = _silu(prev * wc[0:1] + ym * wc[1:2] + nxt * wc[2:3])
        qk_ref[0, rows, :ML_QK_WIDTH] = (act[:, :ML_QK_WIDTH] * (ML_QK_DIM ** -0.5)).astype(BF16)
        qk_ref[0, rows, ML_QK_WIDTH:] = act[:, ML_QK_WIDTH:].astype(BF16)
        if latent:
            r_o = project(c, ML_WIDTH, hm)
        yield

        g_t = v_t[ATT_KV_WIDTH + ML_WIDTH:] + bgc_ref[...]
        vt_ref[0, :, rows] = v_t[:ATT_KV_WIDTH].astype(BF16)
        for j in range(gm // ML_CHUNK):
            cols = slice(j * ML_CHUNK, (j + 1) * ML_CHUNK)
            mvt_ref[0, r * (gm // ML_CHUNK) + j] = v_t[ATT_KV_WIDTH:ATT_KV_WIDTH + ML_WIDTH, cols].astype(BF16)
            grow_ref[0, r * (gm // ML_CHUNK) + j] = g_t[:, cols]
        if latent:
            o_ref[0, rows, :] = jax.nn.sigmoid(r_o).astype(BF16)

    done = object()
    waiting = [group(r) for r in range(n_groups)]
    active = []
    while waiting or active:
        if waiting:
            active.append(waiting.pop(0))
        active = [g for g in active if next(g, done) is not done]


def _inproj(x, mod3, mod_row, g_pre, w_main, wvt, bg_row, bg_col, wc, rope_tabs, *, tm, gm, latent):
    bsz, t, d = x.shape
    n_tiles = t // tm
    hb = tm // SUBLANES
    n_hblk = t // SUBLANES
    n = w_main.shape[1]

    def const(shape):
        return pl.BlockSpec(shape, lambda i, b: (0,) * len(shape))

    in_specs = [
        pl.BlockSpec((1, tm, d), lambda i, b: (b, i, 0)),
        pl.BlockSpec((1, SUBLANES, d), lambda i, b: (b, jnp.maximum(i * hb - 1, 0), 0)),
        pl.BlockSpec((1, SUBLANES, d), lambda i, b: (b, jnp.minimum((i + 1) * hb, n_hblk - 1), 0)),
        pl.BlockSpec((1, 1, d), lambda i, b: (mod_row(b), 0, 0)),
        pl.BlockSpec((1, 1, d), lambda i, b: (mod_row(b), 0, 1)),
        const((1, d)), const((d, n)), const((ATT_KV_WIDTH + ML_WIDTH + ML_GATES, d)),
        const((1, LANES)), const((ML_GATES, 1)), const((3, 2 * ML_QK_WIDTH)),
    ]
    args = [x, x, x, mod3, mod3, g_pre, w_main, wvt, bg_row, bg_col, wc]
    tok = lambda width, dt: (pl.BlockSpec((1, tm, width), lambda i, b: (b, i, 0)),
                             jax.ShapeDtypeStruct((bsz, t, width), dt))
    chunked = lambda rows, dt: (pl.BlockSpec((1, tm // ML_CHUNK, rows, ML_CHUNK), lambda i, b: (b, i, 0, 0)),
                                jax.ShapeDtypeStruct((bsz, t // ML_CHUNK, rows, ML_CHUNK), dt))
    outs = []
    if latent:
        in_specs += [pl.BlockSpec((tm, LANES), lambda i, b: (i, 0))] * 3
        args += list(rope_tabs)
        outs.append(tok(ATT_WIDTH, BF16))
    outs.append(tok(ATT_KV_WIDTH, BF16))
    outs.append((pl.BlockSpec((1, LANES, tm), lambda i, b: (b, 0, i)),
                 jax.ShapeDtypeStruct((bsz, ATT_KV_WIDTH, t), BF16)))
    outs.append(tok(2 * ML_QK_WIDTH, BF16))
    outs.append(chunked(ML_WIDTH, BF16))
    if latent:
        outs.append(tok(ML_WIDTH, BF16))
    outs.append(tok(LANES, F32))
    outs.append(chunked(ML_GATES, F32))
    return pl.pallas_call(
        functools.partial(_inproj_kernel, tm=tm, gm=gm, n_tiles=n_tiles, latent=latent),
        grid=(n_tiles, bsz),
        in_specs=in_specs,
        out_specs=[o[0] for o in outs],
        out_shape=[o[1] for o in outs],
        compiler_params=pltpu.CompilerParams(dimension_semantics=("arbitrary", "arbitrary"),
                                             vmem_limit_bytes=VMEM_LIMIT),
        name="inproj_latent" if latent else "inproj_context",
    )(*args)


def _attn_kernel(q_ref, kp_ref, kc_ref, kn_ref, kx_ref, vp_ref, vc_ref, vn_ref, vx_ref, sink_ref,
                 o_ref, *, n_tiles):
    i = pl.program_id(1)
    blk = ATT_BLOCK
    lane = lax.broadcasted_iota(jnp.int32, (blk, LANES), 1)
    zero = jnp.zeros((blk, LANES), BF16)
    half_groups = ATT_GROUP // 2
    n_slots = 2 * half_groups
    sink = sink_ref[:, :n_slots * blk], sink_ref[:, n_slots * blk:]

    def stack_heads(q, half):
        parts = []
        for g in range(half * half_groups, (half + 1) * half_groups):
            slab = q[:, g * LANES:(g + 1) * LANES]
            parts.append(jnp.where(lane < ATT_HEAD_DIM, slab, zero))
            parts.append(jnp.where(lane >= ATT_HEAD_DIM, slab, zero))
        return jnp.concatenate(parts, axis=0)

    key = lax.broadcasted_iota(jnp.int32, (blk, blk), 0)
    qry = lax.broadcasted_iota(jnp.int32, (blk, blk), 1)
    ninf = jnp.full((blk, blk), -jnp.inf, F32)
    bias_prev = jnp.where(key >= qry, 0.0, ninf)
    bias_next = jnp.where(key <= qry, 0.0, ninf)
    slots = lambda b: jnp.concatenate([b] * n_slots, axis=1)

    def scores(qs, k_prev, k_cur, k_next):
        return [_dot_nt(k_prev, qs), _dot_nt(k_cur, qs), _dot_nt(k_next, qs), _dot_nt(kx_ref[0], qs)]

    def softmax(s, b_prev, b_next, sink_h):
        s = jnp.concatenate([s[0] + slots(b_prev), s[1], s[2] + slots(b_next), s[3]], axis=0)
        m = jnp.maximum(sink_h, jnp.max(s, axis=0, keepdims=True))
        return jnp.exp2(s - m).astype(BF16), jnp.exp2(sink_h - m)

    n_keys = 3 * blk + kx_ref.shape[1]
    ones_rows = jnp.ones((ATT_ONES_ROWS, n_keys), BF16)

    def weighted_values(p, v_prev, v_cur, v_next):
        vt = jnp.concatenate([v_prev, v_cur, v_next, vx_ref[0]], axis=1)
        return _dot(jnp.concatenate([vt, ones_rows], axis=0), p)

    dim = lax.broadcasted_iota(jnp.int32, (LANES, blk), 0)

    def emit(rows, half, ot, p_sink):
        ot = ot[:LANES] * (1.0 / (ot[LANES:LANES + 1] + p_sink))
        for j in range(half_groups):
            g = half * half_groups + j
            a = ot[:, (2 * j) * blk:(2 * j + 1) * blk]
            b = ot[:, (2 * j + 1) * blk:(2 * j + 2) * blk]
            o_ref[0, rows, g * LANES:(g + 1) * LANES] = jnp.where(dim < ATT_HEAD_DIM, a, b).T.astype(BF16)

    nblk = q_ref.shape[1] // blk
    rows_of = [slice(b * blk, (b + 1) * blk) for b in range(nblk)]
    k_blocks = [kp_ref[0]] + [kc_ref[0, r, :] for r in rows_of] + [kn_ref[0]]
    v_blocks = [vp_ref[0]] + [vc_ref[0, :, r] for r in rows_of] + [vn_ref[0]]
    keys = [tuple(k_blocks[b:b + 3]) for b in range(nblk)]
    vals = [tuple(v_blocks[b:b + 3]) for b in range(nblk)]
    bias = [(jnp.where(i == 0, ninf, bias_prev) if b == 0 else bias_prev,
             jnp.where(i == n_tiles - 1, ninf, bias_next) if b == nblk - 1 else bias_next)
            for b in range(nblk)]
    units = [(b, half) for b in range(nblk) for half in range(2)]

    def stage_scores(u):
        b, half = u
        return scores(stack_heads(q_ref[0, rows_of[b], :], half), *keys[b])

    def stage_softmax(u, s):
        b, half = u
        return softmax(s, *bias[b], sink[half])

    def stage_values(u, p):
        return weighted_values(p, *vals[u[0]])

    def stage_emit(u, ot, p_sink):
        emit(rows_of[u[0]], u[1], ot, p_sink)

    n_units = len(units)
    s, p, ot = {}, {}, {}
    s[0] = stage_scores(units[0])
    for t in range(n_units):
        if t + 1 < n_units:
            s[t + 1] = stage_scores(units[t + 1])
        p[t] = stage_softmax(units[t], s.pop(t))
        if t >= 1:
            ot[t - 1] = stage_values(units[t - 1], p[t - 1][0])
        if t >= 2:
            stage_emit(units[t - 2], ot.pop(t - 2), p.pop(t - 2)[1])
    ot[n_units - 1] = stage_values(units[n_units - 1], p[n_units - 1][0])
    for t in (n_units - 2, n_units - 1):
        stage_emit(units[t], ot.pop(t), p.pop(t)[1])


def _attention(q, k, vt, kx, vxt, sink_row):
    bsz, s, _ = q.shape
    l = kx.shape[1]
    blk = ATT_BLOCK
    per = ATT_BLOCKS_PER_STEP
    n_tiles = s // (per * blk)
    nb = s // blk
    prev = lambda i: jnp.maximum(per * i - 1, 0)
    nxt = lambda i: jnp.minimum(per * (i + 1), nb - 1)
    return pl.pallas_call(
        functools.partial(_attn_kernel, n_tiles=n_tiles),
        grid=(bsz, n_tiles),
        in_specs=[pl.BlockSpec((1, per * blk, ATT_WIDTH), lambda b, i: (b, i, 0)),
                  pl.BlockSpec((1, blk, LANES), lambda b, i: (b, prev(i), 0)),
                  pl.BlockSpec((1, per * blk, LANES), lambda b, i: (b, i, 0)),
                  pl.BlockSpec((1, blk, LANES), lambda b, i: (b, nxt(i), 0)),
                  pl.BlockSpec((1, l, LANES), lambda b, i: (b, 0, 0)),
                  pl.BlockSpec((1, LANES, blk), lambda b, i: (b, 0, prev(i))),
                  pl.BlockSpec((1, LANES, per * blk), lambda b, i: (b, 0, i)),
                  pl.BlockSpec((1, LANES, blk), lambda b, i: (b, 0, nxt(i))),
                  pl.BlockSpec((1, LANES, l), lambda b, i: (b, 0, 0)),
                  pl.BlockSpec((1, ATT_HEADS * blk), lambda b, i: (0, 0))],
        out_specs=pl.BlockSpec((1, per * blk, ATT_WIDTH), lambda b, i: (b, i, 0)),
        out_shape=jax.ShapeDtypeStruct((bsz, s, ATT_WIDTH), BF16),
        compiler_params=pltpu.CompilerParams(dimension_semantics=("arbitrary", "arbitrary"),
                                             vmem_limit_bytes=VMEM_LIMIT),
        name="attention",
    )(q, k, k, k, kx, vt, vt, vt, vxt, sink_row)


ML_STATE_ROWS = ML_V_DIM + SUBLANES


def _mlstm_kernel(qkx_ref, vtx_ref, ox_ref, gcx_ref, grx_ref, qkc_ref, vtc_ref, grc_ref,
                  gain_ref, out_ref, sin_ref, st_ref, ucol_ref, mrun_ref, wint_ref, floor_ref,
                  wkey_ref, decay_ref, *, ncx, ncc):
    lc = ML_CHUNK
    qkw = ML_QK_WIDTH
    sr = ML_STATE_ROWS
    ng = ML_GATES
    hg = ML_GATES // 2
    st_ref[...] = jnp.zeros(st_ref.shape, F32)

    rr = lax.broadcasted_iota(jnp.int32, (lc, lc), 0)
    cc = lax.broadcasted_iota(jnp.int32, (lc, lc), 1)
    tril = rr >= cc
    triu = rr <= cc
    tril_b = jnp.where(tril, 1.0, 0.0).astype(BF16)
    triu_b = jnp.where(triu, 1.0, 0.0).astype(BF16)
    head_of_lane = lax.broadcasted_iota(jnp.int32, (lc, qkw), 1) // ML_QK_DIM
    own_block = jnp.concatenate(
        [lax.broadcasted_iota(jnp.int32, (sr, qkw), 1) // ML_QK_DIM == h for h in range(ML_HEADS)], axis=0)

    pad_rows = jnp.zeros((SUBLANES - 1, lc), F32)

    def split(v):
        hi = v.astype(BF16)
        return hi, (v - hi.astype(F32)).astype(BF16)

    def gate_rows(g):
        n16 = g.shape[0]
        fwd_row = (lax.broadcasted_iota(jnp.int32, g.shape, 0) & (ng - 1)) < hg
        lane = lax.broadcasted_iota(jnp.int32, g.shape, 1)
        hi, lo = split(_log_sigmoid(g))
        cat = jnp.concatenate([hi, lo], axis=0)
        bu = _dot(cat, triu_b)
        bl = _dot(cat, tril_b)
        b = jnp.where(fwd_row, bu[:n16] + bu[n16:], bl[:n16] + bl[n16:])
        b = pltpu.roll(b, n16 - ML_HEADS, axis=0)
        u = g - b
        run_f = run_b = u
        k = 1
        while k < lc:
            run_f = jnp.maximum(run_f, jnp.where(lane >= k, pltpu.roll(run_f, k, axis=1), -jnp.inf))
            run_b = jnp.maximum(run_b, jnp.where(lane < lc - k, pltpu.roll(run_b, lc - k, axis=1), -jnp.inf))
            k *= 2
        run = jnp.where(fwd_row, run_f, run_b)

        def at_end(a):
            return jnp.where(fwd_row, jnp.broadcast_to(a[:, lc - 1:lc], a.shape),
                             jnp.broadcast_to(a[:, 0:1], a.shape))

        return u, b, run, at_end(b), at_end(run)

    def derived(u, b, run, b_tot, run_end, m_in):
        m_run = jnp.maximum(run, m_in)
        m_out = b_tot + jnp.maximum(run_end, m_in)
        return (m_run * LOG2_E, jnp.exp(m_in - m_run), jnp.exp(-(b + m_run)),
                jnp.exp(b_tot + u - m_out), jnp.exp(b_tot + m_in - m_out))

    gc_rows = gate_rows(grc_ref[0].reshape(ncc * ng, lc))
    gx_rows = gate_rows(grx_ref[0].reshape(ncx * ng, lc))

    def scan_m(rows, n, m_f, m_b):
        _, _, _, b_tot, run_end = rows
        part = lambda a, c, d: a[c * ng + d * hg:c * ng + (d + 1) * hg]
        ins_f, ins_b = [], [None] * n
        for c in range(n):
            ins_f.append(m_f)
            m_f = part(b_tot, c, 0) + jnp.maximum(part(run_end, c, 0), m_f)
        for c in reversed(range(n)):
            ins_b[c] = m_b
            m_b = part(b_tot, c, 1) + jnp.maximum(part(run_end, c, 1), m_b)
        return jnp.concatenate([x for c in range(n) for x in (ins_f[c], ins_b[c])], axis=0), m_f, m_b

    m0 = jnp.zeros((hg, lc), F32)
    m_in_c, m_f, m_b = scan_m(gc_rows, ncc, m0, m0)
    m_in_x, _, _ = scan_m(gx_rows, ncx, m_f, m_b)
    _, _, _, wkey_c, decay_c = derived(*gc_rows, m_in_c)
    for ref, val in zip((mrun_ref, wint_ref, floor_ref, wkey_ref, decay_ref), derived(*gx_rows, m_in_x)):
        ref[...] = val

    fwd_col = (lax.broadcasted_iota(jnp.int32, (lc, LANES), 1) & (ng - 1)) < hg

    def token_major_u(j):
        rows = pl.ds(pl.multiple_of(j * lc, lc), lc)
        gcol = gcx_ref[0, rows, :]
        hi, lo = split(_log_sigmoid(gcol))
        cat = jnp.concatenate([hi, lo], axis=1)
        bl = _dot(tril_b, cat)
        bu = _dot(triu_b, cat)
        b = jnp.where(fwd_col, bl[:, :LANES] + bl[:, LANES:], bu[:, :LANES] + bu[:, LANES:])
        ucol_ref[rows, :] = (gcol - pltpu.roll(b, LANES - ML_HEADS, axis=1)) * LOG2_E

    def advance(dirn, k4, vt, w_key, decay):
        st = st_ref[dirn]
        pieces, decays = [], []
        for h in range(ML_HEADS):
            c = hg * dirn + h
            pieces += [vt[h * ML_V_DIM:(h + 1) * ML_V_DIM].astype(F32) * w_key[c:c + 1], w_key[c:c + 1], pad_rows]
            decays.append(jnp.broadcast_to(jnp.concatenate([decay[c:c + 1]] * (qkw // lc), axis=1), (sr, qkw)))
        upd = _dot(jnp.concatenate(pieces, axis=0).astype(BF16), k4)
        st_ref[dirn] = jnp.concatenate(decays, axis=0) * st + jnp.where(own_block, upd, 0.0)

    for j in range(ncc):
        for dirn, cj in ((0, j), (1, ncc - 1 - j)):
            grows = slice(cj * ng, (cj + 1) * ng)
            advance(dirn, qkc_ref[0, cj * lc:(cj + 1) * lc, qkw:], vtc_ref[0, cj], wkey_c[grows], decay_c[grows])

    def scan_body(j, carry):
        for dirn, cj in ((0, j), (1, ncx - 1 - j)):
            rows = pl.ds(pl.multiple_of(cj * lc, lc), lc)
            grows = pl.ds(pl.multiple_of(cj * ng, ng), ng)
            sin_ref[dirn, cj] = st_ref[dirn].astype(BF16)
            advance(dirn, qkx_ref[0, rows, qkw:], vtx_ref[0, cj], wkey_ref[grows, :], decay_ref[grows, :])
        token_major_u(j)
        return carry

    lax.fori_loop(0, ncx, scan_body, 0, unroll=8)

    gain = gain_ref[...]

    def out_body(j, carry):
        rows = pl.ds(pl.multiple_of(j * lc, lc), lc)
        grows = pl.ds(pl.multiple_of(j * ng, ng), ng)
        q4 = qkx_ref[0, rows, :qkw]
        k4 = qkx_ref[0, rows, qkw:]
        vt = vtx_ref[0, j]
        u_col = ucol_ref[rows, :]
        m_run, w_int, floor = mrun_ref[grows, :], wint_ref[grows, :], floor_ref[grows, :]
        zero = jnp.zeros_like(q4)
        qs = jnp.concatenate([jnp.where(head_of_lane == h, q4, zero) for h in range(ML_HEADS)], axis=0)
        qk_t = _dot_nt(k4, qs)
        hsum = [None] * ML_HEADS
        for dirn in range(2):
            inter = _dot_nt(sin_ref[dirn, j], q4)
            valid = triu if dirn == 0 else tril
            for h in range(ML_HEADS):
                c = hg * dirn + h
                e = jnp.exp2(jnp.where(valid, u_col[:, c:c + 1] - m_run[c:c + 1], -jnp.inf))
                s_t = qk_t[:, h * lc:(h + 1) * lc] * e
                num = _dot(vt[h * ML_V_DIM:(h + 1) * ML_V_DIM], s_t.astype(BF16))
                num = num + w_int[c:c + 1] * inter[h * sr:h * sr + ML_V_DIM]
                den = (jnp.sum(s_t, axis=0, keepdims=True)
                       + w_int[c:c + 1] * inter[h * sr + ML_V_DIM:h * sr + ML_V_DIM + 1])
                hv = num * (1.0 / jnp.maximum(jnp.abs(den), floor[c:c + 1]))
                hsum[h] = hv if dirn == 0 else hsum[h] + hv
        for h in range(ML_HEADS):
            cols = slice(h * ML_V_DIM, (h + 1) * ML_V_DIM)
            hs = hsum[h]
            hn = hs * lax.rsqrt(jnp.mean(hs * hs, axis=0, keepdims=True) + EPS)
            out_ref[0, rows, cols] = (hn.T * gain[:, cols] * ox_ref[0, rows, cols].astype(F32)).astype(BF16)
        return carry

    lax.fori_loop(0, ncx, out_body, 0, unroll=8)


def _mlstm(qkx, vtx, ox, gcx, grx, qkc, vtc, grc, gain):
    bsz, s, _ = qkx.shape
    ncx, ncc = s // ML_CHUNK, qkc.shape[1] // ML_CHUNK
    per_b = lambda a: pl.BlockSpec((1,) + a.shape[1:], lambda b: (b,) + (0,) * (a.ndim - 1))
    ins = [qkx, vtx, ox, gcx, grx, qkc, vtc, grc]
    return pl.pallas_call(
        functools.partial(_mlstm_kernel, ncx=ncx, ncc=ncc),
        grid=(bsz,),
        in_specs=[per_b(a) for a in ins] + [pl.BlockSpec((1, ML_WIDTH), lambda b: (0, 0))],
        out_specs=pl.BlockSpec((1, s, ML_WIDTH), lambda b: (b, 0, 0)),
        out_shape=jax.ShapeDtypeStruct((bsz, s, ML_WIDTH), BF16),
        scratch_shapes=[pltpu.VMEM((2, ncx, ML_HEADS * ML_STATE_ROWS, ML_QK_WIDTH), BF16),
                        pltpu.VMEM((2, ML_HEADS * ML_STATE_ROWS, ML_QK_WIDTH), F32),
                        pltpu.VMEM((s, LANES), F32)]
                       + [pltpu.VMEM((ncx * ML_GATES, ML_CHUNK), F32)] * 5,
        compiler_params=pltpu.CompilerParams(dimension_semantics=("arbitrary",),
                                             vmem_limit_bytes=VMEM_LIMIT),
        name="mlstm",
    )(*ins, gain)


def _out_ffn_kernel(x_ref, att_ref, ml_ref, gtm_ref, shf_ref, scf_ref, gtf_ref,
                    gpm_ref, gpf_ref, gqf_ref, wo_hbm, wfi_hbm, wfo_hbm, o_ref,
                    woa_ref, wom_ref, wfi_ref, wfo_ref, wide_ref, narrow_ref, sem_ref, *, hidden):
    def stream(jobs, stage_ref, sems):
        def copy(k):
            src, _, n = jobs[k]
            return pltpu.make_async_copy(src, stage_ref.at[k % 2, pl.ds(0, n)], sems.at[k % 2])
        copy(0).start()
        for k, (_, dst, n) in enumerate(jobs):
            if k + 1 < len(jobs):
                copy(k + 1).start()
            copy(k).wait()
            dst[...] = stage_ref[k % 2, :n, :].astype(BF16)

    @pl.when(pl.program_id(0) == 0)
    def _():
        hd = ATT_HEAD_DIM
        nrows = narrow_ref.shape[1]
        narrow = [(wo_hbm.at[pl.ds(j * hd, hd)], woa_ref.at[pl.ds(p * hd, hd)], hd)
                  for p, j in enumerate(_HEAD_PERM)]
        for lo in range(0, ML_WIDTH, nrows):
            n = min(nrows, ML_WIDTH - lo)
            narrow.append((wo_hbm.at[pl.ds(ATT_WIDTH + lo, n)], wom_ref.at[pl.ds(lo, n)], n))
        for lo in range(0, hidden, nrows):
            n = min(nrows, hidden - lo)
            narrow.append((wfo_hbm.at[pl.ds(lo, n)], wfo_ref.at[pl.ds(lo, n)], n))
        wrows = wide_ref.shape[1]
        wide = [(wfi_hbm.at[pl.ds(lo, wrows)], wfi_ref.at[pl.ds(lo, wrows)], wrows)
                for lo in range(0, wfi_ref.shape[0], wrows)]
        stream(wide, wide_ref, sem_ref.at[0])
        stream(narrow, narrow_ref, sem_ref.at[1])

    rows = [slice(r * FFN_ROWS, (r + 1) * FFN_ROWS) for r in range(x_ref.shape[0] // FFN_ROWS)]

    def mix_stage(r):
        return _dot(att_ref[r, :], woa_ref[...]) + _dot(ml_ref[r, :], wom_ref[...])

    def norm_stage(r, mix):
        x1 = x_ref[r, :] + gtm_ref[0] * _rms(mix, gpm_ref[...])
        return x1, (_rms(x1, gpf_ref[...]) * (1.0 + scf_ref[0]) + shf_ref[0]).astype(BF16)

    def act_stage(gu):
        return (_silu(gu[:, :hidden]) * gu[:, hidden:]).astype(BF16)

    def out_stage(r, x1, fx):
        o_ref[r, :] = x1 + gtf_ref[0] * _rms(fx, gqf_ref[...])

    a, b = rows
    mix_a = mix_stage(a)
    mix_b = mix_stage(b)
    x1_a, h_a = norm_stage(a, mix_a)
    gu_a = _dot(h_a, wfi_ref[...])
    x1_b, h_b = norm_stage(b, mix_b)
    gu_b = _dot(h_b, wfi_ref[...])
    act_a = act_stage(gu_a)
    fx_a = _dot(act_a, wfo_ref[...])
    act_b = act_stage(gu_b)
    fx_b = _dot(act_b, wfo_ref[...])
    out_stage(a, x1_a, fx_a)
    out_stage(b, x1_b, fx_b)


def _out_ffn(x2, att2, ml2, mod3, g_post_mix, g_pre_ffn, g_post_ffn, w_out, w_ffn_in, w_ffn_out, *, tiles_per_batch):
    t, d = x2.shape
    tm = FFN_TM
    hidden = w_ffn_out.shape[0]
    hbm = pl.BlockSpec(memory_space=pl.ANY)
    mod = lambda k: pl.BlockSpec((1, 1, d), lambda i: (i // tiles_per_batch, 0, k))
    row = pl.BlockSpec((1, d), lambda i: (0, 0))
    return pl.pallas_call(
        functools.partial(_out_ffn_kernel, hidden=hidden),
        grid=(t // tm,),
        in_specs=[pl.BlockSpec((tm, d), lambda i: (i, 0)),
                  pl.BlockSpec((tm, ATT_WIDTH), lambda i: (i, 0)),
                  pl.BlockSpec((tm, ML_WIDTH), lambda i: (i, 0)),
                  mod(2), mod(3), mod(4), mod(5), row, row, row, hbm, hbm, hbm],
        out_specs=pl.BlockSpec((tm, d), lambda i: (i, 0)),
        out_shape=jax.ShapeDtypeStruct((t, d), F32),
        scratch_shapes=[pltpu.VMEM((ATT_WIDTH, d), BF16), pltpu.VMEM((ML_WIDTH, d), BF16),
                        pltpu.VMEM(w_ffn_in.shape, BF16), pltpu.VMEM(w_ffn_out.shape, BF16),
                        pltpu.VMEM((2, FFN_STAGE_ROWS_WIDE, w_ffn_in.shape[1]), F32),
                        pltpu.VMEM((2, FFN_STAGE_ROWS_NARROW, d), F32),
                        pltpu.SemaphoreType.DMA((2, 2))],
        compiler_params=pltpu.CompilerParams(dimension_semantics=("arbitrary",),
                                             vmem_limit_bytes=VMEM_LIMIT),
        name="out_ffn",
    )(x2, att2, ml2, mod3, mod3, mod3, mod3, g_post_mix, g_pre_ffn, g_post_ffn, w_out, w_ffn_in, w_ffn_out)


def _rope_tables(n_tokens):
    pos = jnp.arange(n_tokens)
    row = (pos // GRID_W).astype(F32)
    col = (pos % GRID_W).astype(F32)
    half = ATT_HEAD_DIM // 4
    inv_freq = jnp.power(ROPE_BASE, -jnp.arange(half, dtype=F32) / half)
    ang_r = row[:, None] * inv_freq
    ang_c = col[:, None] * inv_freq
    z = jnp.zeros_like(ang_r)
    reps = LANES // ATT_HEAD_DIM
    cos = jnp.tile(jnp.concatenate([jnp.cos(ang_r)] * 2 + [jnp.cos(ang_c)] * 2, axis=1), (1, reps))
    sin_lo = jnp.tile(jnp.concatenate([-jnp.sin(ang_r), z, -jnp.sin(ang_c), z], axis=1), (1, reps))
    sin_hi = jnp.tile(jnp.concatenate([z, jnp.sin(ang_r), z, jnp.sin(ang_c)], axis=1), (1, reps))
    return cos, sin_lo, sin_hi


def _permute_heads(w, axis):
    shape = w.shape
    split = shape[:axis] + (ATT_HEADS, ATT_HEAD_DIM) + shape[axis + 1:]
    return jnp.take(w.reshape(split), jnp.array(_HEAD_PERM), axis=axis).reshape(shape)


def kernel(x, c, ctx, c_ctx, w_ada, b_ada, g_pre_mix, w_in, w_conv_qk, b_gates, attn_sink,
           g_mlstm_out, w_out, g_post_mix, g_pre_ffn, w_ffn_in, w_ffn_out, g_post_ffn):
    bsz, s, d = x.shape
    l = ctx.shape[1]
    assert w_ada.shape[0] == 1, "single-layer block"
    assert bsz < MOD_ROWS and s % INPROJ_TM == 0 and l % ML_CHUNK == 0 and s % FFN_TM == 0

    cc = jnp.zeros((MOD_ROWS, d), F32).at[:bsz].set(c).at[bsz].set(c_ctx)
    mod3 = _ada(cc, w_ada[0], b_ada).reshape(MOD_ROWS, 1, 6 * d)

    w = w_in[0]
    o_q, o_k, o_v = 0, ATT_WIDTH, ATT_WIDTH + ATT_KV_WIDTH
    o_mq = o_v + ATT_KV_WIDTH
    o_mv = o_mq + 2 * ML_QK_WIDTH
    o_mo = o_mv + ML_WIDTH
    o_mg = o_mo + ML_WIDTH
    w_q = _permute_heads(w[:, o_q:o_k], 1) * (ATT_HEAD_DIM ** -0.5 * LOG2_E)
    w_g = jnp.pad(w[:, o_mg:], ((0, 0), (0, LANES - ML_GATES)))
    shared = [w[:, o_k:o_v], w_g, w[:, o_mq:o_mv]]
    w_lat = jnp.concatenate([w_q] + shared + [w[:, o_mo:o_mg]], axis=1).astype(BF16)
    w_ctx = jnp.concatenate(shared, axis=1).astype(BF16)
    wvt = jnp.concatenate([w[:, o_v:o_mq], w[:, o_mv:o_mo], w[:, o_mg:]], axis=1).T.astype(BF16)
    bg_row = jnp.pad(b_gates, ((0, 0), (0, LANES - ML_GATES)))
    bg_col = b_gates.reshape(ML_GATES, 1)
    wc = w_conv_qk[0]

    q, k, vt, qkx, vtx, ox, gcx, grx = _inproj(
        x, mod3, lambda b: b, g_pre_mix, w_lat, wvt, bg_row, bg_col, wc, _rope_tables(s),
        tm=INPROJ_TM, gm=INPROJ_ROWS, latent=True)
    kc, vct, qkc, vtc, _, grc = _inproj(
        ctx, mod3, lambda b: bsz, g_pre_mix, w_ctx, wvt, bg_row, bg_col, wc, None,
        tm=l, gm=l, latent=False)

    sink_row = jnp.repeat(attn_sink[0][jnp.array(_HEAD_PERM)] * LOG2_E, ATT_BLOCK)[None, :]
    att = _attention(q, k, vt, kc, vct, sink_row)
    ml = _mlstm(qkx, vtx, ox, gcx, grx, qkc, vtc, grc, g_mlstm_out)

    out = _out_ffn(x.reshape(bsz * s, d), att.reshape(bsz * s, ATT_WIDTH), ml.reshape(bsz * s, ML_WIDTH),
                   mod3, g_post_mix, g_pre_ffn, g_post_ffn, w_out[0], w_ffn_in[0], w_ffn_out[0],
                   tiles_per_batch=s // FFN_TM)
    return out.reshape(bsz, s, d)
```

```python
import functools

import jax
import jax.numpy as jnp
from jax import lax
from jax.experimental import pallas as pl
from jax.experimental.pallas import tpu as pltpu

F32 = jnp.float32
BF16 = jnp.bfloat16

EPS = 1e-6
GRID_W = 64
ROPE_BASE = 10000.0
LOG2_E = 1.4426950408889634

ATT_HEADS = 8
ATT_KV_HEADS = 2
ATT_GROUP = ATT_HEADS // ATT_KV_HEADS
ATT_HEAD_DIM = 64
ATT_BLOCK = 128
ATT_WIDTH = ATT_HEADS * ATT_HEAD_DIM
ATT_KV_WIDTH = ATT_KV_HEADS * ATT_HEAD_DIM
ATT_BLOCKS_PER_STEP = 16
ATT_ONES_ROWS = 16

ML_HEADS = 4
ML_V_DIM = 128
ML_QK_DIM = 64
ML_WIDTH = ML_HEADS * ML_V_DIM
ML_QK_WIDTH = ML_HEADS * ML_QK_DIM
ML_GATES = 4 * ML_HEADS
ML_CHUNK = 128

LANES = 128
SUBLANES = 8
VMEM_LIMIT = 56 * 1024 * 1024

INPROJ_TM = 1024
INPROJ_ROWS = 512
FFN_TM = 512
FFN_ROWS = 256
FFN_STAGE_SLOTS = 4
FFN_STAGE_ROWS_WIDE = 32
FFN_STAGE_ROWS_NARROW = 176
ADA_TN = 1536
MOD_ROWS = 16

_HEAD_PERM = tuple(h * ATT_GROUP + g for g in range(ATT_GROUP) for h in range(ATT_KV_HEADS))


def _silu(v):
    return v * jax.nn.sigmoid(v)


def _log_sigmoid(v):
    return jnp.minimum(v, 0.0) - jnp.log1p(jnp.exp(-jnp.abs(v)))


def _rms(v, g):
    return v * lax.rsqrt(jnp.mean(v * v, axis=-1, keepdims=True) + EPS) * g


def _dot(a, b):
    return jnp.dot(a, b, preferred_element_type=F32)


def _dot_nt(a, b):
    return lax.dot_general(a, b, (((1,), (1,)), ((), ())), preferred_element_type=F32)


def _ada_kernel(cc_ref, w_ref, b_ref, o_ref):
    a = _silu(cc_ref[...])
    o_ref[...] = _dot(a.astype(BF16), w_ref[...].astype(BF16)) + b_ref[...]


def _ada(cc, w, b):
    d, n = w.shape
    return pl.pallas_call(
        _ada_kernel,
        grid=(n // ADA_TN,),
        in_specs=[pl.BlockSpec((MOD_ROWS, d), lambda j: (0, 0)),
                  pl.BlockSpec((d, ADA_TN), lambda j: (0, j)),
                  pl.BlockSpec((1, ADA_TN), lambda j: (0, j))],
        out_specs=pl.BlockSpec((MOD_ROWS, ADA_TN), lambda j: (0, j)),
        out_shape=jax.ShapeDtypeStruct((MOD_ROWS, n), F32),
        compiler_params=pltpu.CompilerParams(dimension_semantics=("arbitrary",),
                                             vmem_limit_bytes=VMEM_LIMIT),
        name="ada",
    )(cc, w, b)


def _rope(v, cos, sin_lo, sin_hi):
    return (v * cos + pltpu.roll(v, LANES - 16, axis=1) * sin_lo
            + pltpu.roll(v, 16, axis=1) * sin_hi)


def _inproj_kernel(*refs, tm, gm, n_tiles, latent):
    if latent:
        (x_ref, xp_ref, xn_ref, sh_ref, sc_ref, g_ref, w_ref, wvt_ref, bgr_ref, bgc_ref,
         wc_ref, cos_ref, sl_ref, shi_ref,
         q_ref, k_ref, vt_ref, qk_ref, mvt_ref, o_ref, gcol_ref, grow_ref) = refs
    else:
        (x_ref, xp_ref, xn_ref, sh_ref, sc_ref, g_ref, w_ref, wvt_ref, bgr_ref, bgc_ref,
         wc_ref,
         k_ref, vt_ref, qk_ref, mvt_ref, gcol_ref, grow_ref) = refs
    i = pl.program_id(0)
    n_groups = tm // gm
    scale = g_ref[...] * (1.0 + sc_ref[0])
    shift = sh_ref[0]
    wc = wc_ref[...]
    row = lax.broadcasted_iota(jnp.int32, (gm, 1), 0)

    def project(lo, width, lhs):
        return _dot(lhs, w_ref[:, lo:lo + width])

    def group(r):
        rows = slice(r * gm, (r + 1) * gm)
        before = xp_ref[0] if r == 0 else x_ref[0, r * gm - SUBLANES:r * gm, :]
        after = xn_ref[0] if r == n_groups - 1 else x_ref[0, (r + 1) * gm:(r + 1) * gm + SUBLANES, :]
        keep_prev = jnp.where(i == 0, 0.0, 1.0) if r == 0 else 1.0
        keep_next = jnp.where(i == n_tiles - 1, 0.0, 1.0) if r == n_groups - 1 else 1.0
        xt = jnp.concatenate([x_ref[0, rows, :], before, after], axis=0)
        ms = jnp.mean(xt * xt, axis=-1, keepdims=True)
        hb = (xt * lax.rsqrt(ms + EPS) * scale + shift).astype(BF16)
        hm = hb[:gm]
        yield

        c = 0
        if latent:
            r_q = project(c, ATT_WIDTH, hm)
            c += ATT_WIDTH
        r_kg = project(c, ATT_KV_WIDTH + LANES, hm)
        c += ATT_KV_WIDTH + LANES
        yield

        if latent:
            cos, sl, shi = cos_ref[rows, :], sl_ref[rows, :], shi_ref[rows, :]
            for g in range(ATT_WIDTH // LANES):
                q_ref[0, rows, g * LANES:(g + 1) * LANES] = _rope(
                    r_q[:, g * LANES:(g + 1) * LANES], cos, sl, shi).astype(BF16)
        y = project(c, 2 * ML_QK_WIDTH, hb)
        c += 2 * ML_QK_WIDTH
        yield

        if latent:
            k_ref[0, rows, :] = _rope(r_kg[:, :LANES], cos, sl, shi).astype(BF16)
        else:
            k_ref[0, rows, :] = r_kg[:, :LANES].astype(BF16)
        gcol_ref[0, rows, :] = r_kg[:, LANES:] + bgr_ref[...]
        v_t = _dot_nt(wvt_ref[...], hm)
        yield

        ym = y[:gm]
        prev = jnp.where(row == 0, y[gm + SUBLANES - 1:gm + SUBLANES] * keep_prev,
                         pltpu.roll(ym, 1, axis=0))
        nxt = jnp.where(row == gm - 1, y[gm + SUBLANES:gm + SUBLANES + 1] * keep_next,
                        pltpu.roll(ym, gm - 1, axis=0))
        act = _silu(prev * wc[0:1] + ym * wc[1:2] + nxt * wc[2:3])
        qk_ref[0, rows, :ML_QK_WIDTH] = (act[:, :ML_QK_WIDTH] * (ML_QK_DIM ** -0.5)).astype(BF16)
        qk_ref[0, rows, ML_QK_WIDTH:] = act[:, ML_QK_WIDTH:].astype(BF16)
        if latent:
            r_o = project(c, ML_WIDTH, hm)
        yield

        g_t = v_t[ATT_KV_WIDTH + ML_WIDTH:] + bgc_ref[...]
        vt_ref[0, :, rows] = v_t[:ATT_KV_WIDTH].astype(BF16)
        for j in range(gm // ML_CHUNK):
            cols = slice(j * ML_CHUNK, (j + 1) * ML_CHUNK)
            mvt_ref[0, r * (gm // ML_CHUNK) + j] = v_t[ATT_KV_WIDTH:ATT_KV_WIDTH + ML_WIDTH, cols].astype(BF16)
            grow_ref[0, r * (gm // ML_CHUNK) + j] = g_t[:, cols]
        if latent:
            o_ref[0, rows, :] = jax.nn.sigmoid(r_o).astype(BF16)

    done = object()
    waiting = [group(r) for r in range(n_groups)]
    active = []
    while waiting or active:
        if waiting:
            active.append(waiting.pop(0))
        active = [g for g in active if next(g, done) is not done]


def _inproj(x, mod3, mod_row, g_pre, w_main, wvt, bg_row, bg_col, wc, rope_tabs, *, tm, gm, latent):
    bsz, t, d = x.shape
    n_tiles = t // tm
    hb = tm // SUBLANES
    n_hblk = t // SUBLANES
    n = w_main.shape[1]

    def const(shape):
        return pl.BlockSpec(shape, lambda i, b: (0,) * len(shape))

    in_specs = [
        pl.BlockSpec((1, tm, d), lambda i, b: (b, i, 0)),
        pl.BlockSpec((1, SUBLANES, d), lambda i, b: (b, jnp.maximum(i * hb - 1, 0), 0)),
        pl.BlockSpec((1, SUBLANES, d), lambda i, b: (b, jnp.minimum((i + 1) * hb, n_hblk - 1), 0)),
        pl.BlockSpec((1, 1, d), lambda i, b: (mod_row(b), 0, 0)),
        pl.BlockSpec((1, 1, d), lambda i, b: (mod_row(b), 0, 1)),
        const((1, d)), const((d, n)), const((ATT_KV_WIDTH + ML_WIDTH + ML_GATES, d)),
        const((1, LANES)), const((ML_GATES, 1)), const((3, 2 * ML_QK_WIDTH)),
    ]
    args = [x, x, x, mod3, mod3, g_pre, w_main, wvt, bg_row, bg_col, wc]
    tok = lambda width, dt: (pl.BlockSpec((1, tm, width), lambda i, b: (b, i, 0)),
                             jax.ShapeDtypeStruct((bsz, t, width), dt))
    chunked = lambda rows, dt: (pl.BlockSpec((1, tm // ML_CHUNK, rows, ML_CHUNK), lambda i, b: (b, i, 0, 0)),
                                jax.ShapeDtypeStruct((bsz, t // ML_CHUNK, rows, ML_CHUNK), dt))
    outs = []
    if latent:
        in_specs += [pl.BlockSpec((tm, LANES), lambda i, b: (i, 0))] * 3
        args += list(rope_tabs)
        outs.append(tok(ATT_WIDTH, BF16))
    outs.append(tok(ATT_KV_WIDTH, BF16))
    outs.append((pl.BlockSpec((1, LANES, tm), lambda i, b: (b, 0, i)),
                 jax.ShapeDtypeStruct((bsz, ATT_KV_WIDTH, t), BF16)))
    outs.append(tok(2 * ML_QK_WIDTH, BF16))
    outs.append(chunked(ML_WIDTH, BF16))
    if latent:
        outs.append(tok(ML_WIDTH, BF16))
    outs.append(tok(LANES, F32))
    outs.append(chunked(ML_GATES, F32))
    return pl.pallas_call(
        functools.partial(_inproj_kernel, tm=tm, gm=gm, n_tiles=n_tiles, latent=latent),
        grid=(n_tiles, bsz),
        in_specs=in_specs,
        out_specs=[o[0] for o in outs],
        out_shape=[o[1] for o in outs],
        compiler_params=pltpu.CompilerParams(dimension_semantics=("arbitrary", "arbitrary"),
                                             vmem_limit_bytes=VMEM_LIMIT),
        name="inproj_latent" if latent else "inproj_context",
    )(*args)


def _attn_kernel(q_ref, kp_ref, kc_ref, kn_ref, kx_ref, vp_ref, vc_ref, vn_ref, vx_ref, sink_ref,
                 o_ref, *, n_tiles):
    i = pl.program_id(1)
    blk = ATT_BLOCK
    lane = lax.broadcasted_iota(jnp.int32, (blk, LANES), 1)
    zero = jnp.zeros((blk, LANES), BF16)
    half_groups = ATT_GROUP // 2
    n_slots = 2 * half_groups
    sink = sink_ref[:, :n_slots * blk], sink_ref[:, n_slots * blk:]

    def stack_heads(q, half):
        parts = []
        for g in range(half * half_groups, (half + 1) * half_groups):
            slab = q[:, g * LANES:(g + 1) * LANES]
            parts.append(jnp.where(lane < ATT_HEAD_DIM, slab, zero))
            parts.append(jnp.where(lane >= ATT_HEAD_DIM, slab, zero))
        return jnp.concatenate(parts, axis=0)

    key = lax.broadcasted_iota(jnp.int32, (blk, blk), 0)
    qry = lax.broadcasted_iota(jnp.int32, (blk, blk), 1)
    ninf = jnp.full((blk, blk), -jnp.inf, F32)
    bias_prev = jnp.where(key >= qry, 0.0, ninf)
    bias_next = jnp.where(key <= qry, 0.0, ninf)
    slots = lambda b: jnp.concatenate([b] * n_slots, axis=1)

    def scores(qs, k_prev, k_cur, k_next):
        return [_dot_nt(k_prev, qs), _dot_nt(k_cur, qs), _dot_nt(k_next, qs), _dot_nt(kx_ref[0], qs)]

    def softmax(s, b_prev, b_next, sink_h):
        s = jnp.concatenate([s[0] + slots(b_prev), s[1], s[2] + slots(b_next), s[3]], axis=0)
        m = jnp.maximum(sink_h, jnp.max(s, axis=0, keepdims=True))
        return jnp.exp2(s - m).astype(BF16), jnp.exp2(sink_h - m)

    n_keys = 3 * blk + kx_ref.shape[1]
    ones_rows = jnp.ones((ATT_ONES_ROWS, n_keys), BF16)

    def weighted_values(p, v_prev, v_cur, v_next):
        vt = jnp.concatenate([v_prev, v_cur, v_next, vx_ref[0]], axis=1)
        return _dot(jnp.concatenate([vt, ones_rows], axis=0), p)

    dim = lax.broadcasted_iota(jnp.int32, (LANES, blk), 0)

    def emit(rows, half, ot, p_sink):
        ot = ot[:LANES] * (1.0 / (ot[LANES:LANES + 1] + p_sink))
        for j in range(half_groups):
            g = half * half_groups + j
            a = ot[:, (2 * j) * blk:(2 * j + 1) * blk]
            b = ot[:, (2 * j + 1) * blk:(2 * j + 2) * blk]
            o_ref[0, rows, g * LANES:(g + 1) * LANES] = jnp.where(dim < ATT_HEAD_DIM, a, b).T.astype(BF16)

    nblk = q_ref.shape[1] // blk
    rows_of = [slice(b * blk, (b + 1) * blk) for b in range(nblk)]
    k_blocks = [kp_ref[0]] + [kc_ref[0, r, :] for r in rows_of] + [kn_ref[0]]
    v_blocks = [vp_ref[0]] + [vc_ref[0, :, r] for r in rows_of] + [vn_ref[0]]
    keys = [tuple(k_blocks[b:b + 3]) for b in range(nblk)]
    vals = [tuple(v_blocks[b:b + 3]) for b in range(nblk)]
    bias = [(jnp.where(i == 0, ninf, bias_prev) if b == 0 else bias_prev,
             jnp.where(i == n_tiles - 1, ninf, bias_next) if b == nblk - 1 else bias_next)
            for b in range(nblk)]
    units = [(b, half) for b in range(nblk) for half in range(2)]

    def stage_scores(u):
        b, half = u
        return scores(stack_heads(q_ref[0, rows_of[b], :], half), *keys[b])

    def stage_softmax(u, s):
        b, half = u
        return softmax(s, *bias[b], sink[half])

    def stage_values(u, p):
        return weighted_values(p, *vals[u[0]])

    def stage_emit(u, ot, p_sink):
        emit(rows_of[u[0]], u[1], ot, p_sink)

    n_units = len(units)
    s, p, ot = {}, {}, {}
    s[0] = stage_scores(units[0])
    for t in range(n_units):
        if t + 1 < n_units:
            s[t + 1] = stage_scores(units[t + 1])
        p[t] = stage_softmax(units[t], s.pop(t))
        if t >= 1:
            ot[t - 1] = stage_values(units[t - 1], p[t - 1][0])
        if t >= 2:
            stage_emit(units[t - 2], ot.pop(t - 2), p.pop(t - 2)[1])
    ot[n_units - 1] = stage_values(units[n_units - 1], p[n_units - 1][0])
    for t in (n_units - 2, n_units - 1):
        stage_emit(units[t], ot.pop(t), p.pop(t)[1])


def _attention(q, k, vt, kx, vxt, sink_row):
    bsz, s, _ = q.shape
    l = kx.shape[1]
    blk = ATT_BLOCK
    per = ATT_BLOCKS_PER_STEP
    n_tiles = s // (per * blk)
    nb = s // blk
    prev = lambda i: jnp.maximum(per * i - 1, 0)
    nxt = lambda i: jnp.minimum(per * (i + 1), nb - 1)
    return pl.pallas_call(
        functools.partial(_attn_kernel, n_tiles=n_tiles),
        grid=(bsz, n_tiles),
        in_specs=[pl.BlockSpec((1, per * blk, ATT_WIDTH), lambda b, i: (b, i, 0)),
                  pl.BlockSpec((1, blk, LANES), lambda b, i: (b, prev(i), 0)),
                  pl.BlockSpec((1, per * blk, LANES), lambda b, i: (b, i, 0)),
                  pl.BlockSpec((1, blk, LANES), lambda b, i: (b, nxt(i), 0)),
                  pl.BlockSpec((1, l, LANES), lambda b, i: (b, 0, 0)),
                  pl.BlockSpec((1, LANES, blk), lambda b, i: (b, 0, prev(i))),
                  pl.BlockSpec((1, LANES, per * blk), lambda b, i: (b, 0, i)),
                  pl.BlockSpec((1, LANES, blk), lambda b, i: (b, 0, nxt(i))),
                  pl.BlockSpec((1, LANES, l), lambda b, i: (b, 0, 0)),
                  pl.BlockSpec((1, ATT_HEADS * blk), lambda b, i: (0, 0))],
        out_specs=pl.BlockSpec((1, per * blk, ATT_WIDTH), lambda b, i: (b, i, 0)),
        out_shape=jax.ShapeDtypeStruct((bsz, s, ATT_WIDTH), BF16),
        compiler_params=pltpu.CompilerParams(dimension_semantics=("arbitrary", "arbitrary"),
                                             vmem_limit_bytes=VMEM_LIMIT),
        name="attention",
    )(q, k, k, k, kx, vt, vt, vt, vxt, sink_row)


ML_STATE_ROWS = ML_V_DIM + SUBLANES


def _mlstm_kernel(qkx_ref, vtx_ref, ox_ref, gcx_ref, grx_ref, qkc_ref, vtc_ref, grc_ref,
                  gain_ref, out_ref, sin_ref, st_ref, ucol_ref, mrun_ref, wint_ref, floor_ref,
                  wkey_ref, decay_ref, *, ncx, ncc):
    lc = ML_CHUNK
    qkw = ML_QK_WIDTH
    sr = ML_STATE_ROWS
    ng = ML_GATES
    hg = ML_GATES // 2
    st_ref[...] = jnp.zeros(st_ref.shape, F32)

    rr = lax.broadcasted_iota(jnp.int32, (lc, lc), 0)
    cc = lax.broadcasted_iota(jnp.int32, (lc, lc), 1)
    tril = rr >= cc
    triu = rr <= cc
    tril_b = jnp.where(tril, 1.0, 0.0).astype(BF16)
    triu_b = jnp.where(triu, 1.0, 0.0).astype(BF16)
    head_of_lane = lax.broadcasted_iota(jnp.int32, (lc, qkw), 1) // ML_QK_DIM
    own_block = jnp.concatenate(
        [lax.broadcasted_iota(jnp.int32, (sr, qkw), 1) // ML_QK_DIM == h for h in range(ML_HEADS)], axis=0)

    pad_rows = jnp.zeros((SUBLANES - 1, lc), F32)

    def split(v):
        hi = v.astype(BF16)
        return hi, (v - hi.astype(F32)).astype(BF16)

    def gate_rows(g):
        n16 = g.shape[0]
        fwd_row = (lax.broadcasted_iota(jnp.int32, g.shape, 0) & (ng - 1)) < hg
        lane = lax.broadcasted_iota(jnp.int32, g.shape, 1)
        hi, lo = split(_log_sigmoid(g))
        cat = jnp.concatenate([hi, lo], axis=0)
        bu = _dot(cat, triu_b)
        bl = _dot(cat, tril_b)
        b = jnp.where(fwd_row, bu[:n16] + bu[n16:], bl[:n16] + bl[n16:])
        b = pltpu.roll(b, n16 - ML_HEADS, axis=0)
        u = g - b
        run_f = run_b = u
        k = 1
        while k < lc:
            run_f = jnp.maximum(run_f, jnp.where(lane >= k, pltpu.roll(run_f, k, axis=1), -jnp.inf))
            run_b = jnp.maximum(run_b, jnp.where(lane < lc - k, pltpu.roll(run_b, lc - k, axis=1), -jnp.inf))
            k *= 2
        run = jnp.where(fwd_row, run_f, run_b)

        def at_end(a):
            return jnp.where(fwd_row, jnp.broadcast_to(a[:, lc - 1:lc], a.shape),
                             jnp.broadcast_to(a[:, 0:1], a.shape))

        return u, b, run, at_end(b), at_end(run)

    def derived(u, b, run, b_tot, run_end, m_in):
        m_run = jnp.maximum(run, m_in)
        m_out = b_tot + jnp.maximum(run_end, m_in)
        return (m_run * LOG2_E, jnp.exp(m_in - m_run), jnp.exp(-(b + m_run)),
                jnp.exp(b_tot + u - m_out), jnp.exp(b_tot + m_in - m_out))

    gc_rows = gate_rows(grc_ref[0].reshape(ncc * ng, lc))
    gx_rows = gate_rows(grx_ref[0].reshape(ncx * ng, lc))

    def scan_m(rows, n, m_f, m_b):
        _, _, _, b_tot, run_end = rows
        part = lambda a, c, d: a[c * ng + d * hg:c * ng + (d + 1) * hg]
        ins_f, ins_b = [], [None] * n
        for c in range(n):
            ins_f.append(m_f)
            m_f = part(b_tot, c, 0) + jnp.maximum(part(run_end, c, 0), m_f)
        for c in reversed(range(n)):
            ins_b[c] = m_b
            m_b = part(b_tot, c, 1) + jnp.maximum(part(run_end, c, 1), m_b)
        return jnp.concatenate([x for c in range(n) for x in (ins_f[c], ins_b[c])], axis=0), m_f, m_b

    m0 = jnp.zeros((hg, lc), F32)
    m_in_c, m_f, m_b = scan_m(gc_rows, ncc, m0, m0)
    m_in_x, _, _ = scan_m(gx_rows, ncx, m_f, m_b)
    _, _, _, wkey_c, decay_c = derived(*gc_rows, m_in_c)
    for ref, val in zip((mrun_ref, wint_ref, floor_ref, wkey_ref, decay_ref), derived(*gx_rows, m_in_x)):
        ref[...] = val

    fwd_col = (lax.broadcasted_iota(jnp.int32, (lc, LANES), 1) & (ng - 1)) < hg

    def token_major_u(j):
        rows = pl.ds(pl.multiple_of(j * lc, lc), lc)
        gcol = gcx_ref[0, rows, :]
        hi, lo = split(_log_sigmoid(gcol))
        cat = jnp.concatenate([hi, lo], axis=1)
        bl = _dot(tril_b, cat)
        bu = _dot(triu_b, cat)
        b = jnp.where(fwd_col, bl[:, :LANES] + bl[:, LANES:], bu[:, :LANES] + bu[:, LANES:])
        ucol_ref[rows, :] = (gcol - pltpu.roll(b, LANES - ML_HEADS, axis=1)) * LOG2_E

    def advance(dirn, k4, vt, w_key, decay):
        st = st_ref[dirn]
        pieces, decays = [], []
        for h in range(ML_HEADS):
            c = hg * dirn + h
            pieces += [vt[h * ML_V_DIM:(h + 1) * ML_V_DIM].astype(F32) * w_key[c:c + 1], w_key[c:c + 1], pad_rows]
            decays.append(jnp.broadcast_to(jnp.concatenate([decay[c:c + 1]] * (qkw // lc), axis=1), (sr, qkw)))
        upd = _dot(jnp.concatenate(pieces, axis=0).astype(BF16), k4)
        st_ref[dirn] = jnp.concatenate(decays, axis=0) * st + jnp.where(own_block, upd, 0.0)

    for j in range(ncc):
        for dirn, cj in ((0, j), (1, ncc - 1 - j)):
            grows = slice(cj * ng, (cj + 1) * ng)
            advance(dirn, qkc_ref[0, cj * lc:(cj + 1) * lc, qkw:], vtc_ref[0, cj], wkey_c[grows], decay_c[grows])

    def scan_body(j, carry):
        for dirn, cj in ((0, j), (1, ncx - 1 - j)):
            rows = pl.ds(pl.multiple_of(cj * lc, lc), lc)
            grows = pl.ds(pl.multiple_of(cj * ng, ng), ng)
            sin_ref[dirn, cj] = st_ref[dirn].astype(BF16)
            advance(dirn, qkx_ref[0, rows, qkw:], vtx_ref[0, cj], wkey_ref[grows, :], decay_ref[grows, :])
        token_major_u(j)
        return carry

    lax.fori_loop(0, ncx, scan_body, 0, unroll=8)

    gain = gain_ref[...]

    def out_body(j, carry):
        rows = pl.ds(pl.multiple_of(j * lc, lc), lc)
        grows = pl.ds(pl.multiple_of(j * ng, ng), ng)
        q4 = qkx_ref[0, rows, :qkw]
        k4 = qkx_ref[0, rows, qkw:]
        vt = vtx_ref[0, j]
        u_col = ucol_ref[rows, :]
        m_run, w_int, floor = mrun_ref[grows, :], wint_ref[grows, :], floor_ref[grows, :]
        zero = jnp.zeros_like(q4)
        qs = jnp.concatenate([jnp.where(head_of_lane == h, q4, zero) for h in range(ML_HEADS)], axis=0)
        qk_t = _dot_nt(k4, qs)
        hsum = [None] * ML_HEADS
        for dirn in range(2):
            inter = _dot_nt(sin_ref[dirn, j], q4)
            valid = triu if dirn == 0 else tril
            for h in range(ML_HEADS):
                c = hg * dirn + h
                e = jnp.exp2(jnp.where(valid, u_col[:, c:c + 1] - m_run[c:c + 1], -jnp.inf))
                s_t = qk_t[:, h * lc:(h + 1) * lc] * e
                num = _dot(vt[h * ML_V_DIM:(h + 1) * ML_V_DIM], s_t.astype(BF16))
                num = num + w_int[c:c + 1] * inter[h * sr:h * sr + ML_V_DIM]
                den = (jnp.sum(s_t, axis=0, keepdims=True)
                       + w_int[c:c + 1] * inter[h * sr + ML_V_DIM:h * sr + ML_V_DIM + 1])
                hv = num * (1.0 / jnp.maximum(jnp.abs(den), floor[c:c + 1]))
                hsum[h] = hv if dirn == 0 else hsum[h] + hv
        for h in range(ML_HEADS):
            cols = slice(h * ML_V_DIM, (h + 1) * ML_V_DIM)
            hs = hsum[h]
            hn = hs * lax.rsqrt(jnp.mean(hs * hs, axis=0, keepdims=True) + EPS)
            out_ref[0, rows, cols] = (hn.T * gain[:, cols] * ox_ref[0, rows, cols].astype(F32)).astype(BF16)
        return carry

    lax.fori_loop(0, ncx, out_body, 0, unroll=8)


def _mlstm(qkx, vtx, ox, gcx, grx, qkc, vtc, grc, gain):
    bsz, s, _ = qkx.shape
    ncx, ncc = s // ML_CHUNK, qkc.shape[1] // ML_CHUNK
    per_b = lambda a: pl.BlockSpec((1,) + a.shape[1:], lambda b: (b,) + (0,) * (a.ndim - 1))
    ins = [qkx, vtx, ox, gcx, grx, qkc, vtc, grc]
    return pl.pallas_call(
        functools.partial(_mlstm_kernel, ncx=ncx, ncc=ncc),
        grid=(bsz,),
        in_specs=[per_b(a) for a in ins] + [pl.BlockSpec((1, ML_WIDTH), lambda b: (0, 0))],
        out_specs=pl.BlockSpec((1, s, ML_WIDTH), lambda b: (b, 0, 0)),
        out_shape=jax.ShapeDtypeStruct((bsz, s, ML_WIDTH), BF16),
        scratch_shapes=[pltpu.VMEM((2, ncx, ML_HEADS * ML_STATE_ROWS, ML_QK_WIDTH), BF16),
                        pltpu.VMEM((2, ML_HEADS * ML_STATE_ROWS, ML_QK_WIDTH), F32),
                        pltpu.VMEM((s, LANES), F32)]
                       + [pltpu.VMEM((ncx * ML_GATES, ML_CHUNK), F32)] * 5,
        compiler_params=pltpu.CompilerParams(dimension_semantics=("arbitrary",),
                                             vmem_limit_bytes=VMEM_LIMIT),
        name="mlstm",
    )(*ins, gain)


def _out_ffn_kernel(x_ref, att_ref, ml_ref, gtm_ref, shf_ref, scf_ref, gtf_ref,
                    gpm_ref, gpf_ref, gqf_ref, wo_hbm, wfi_hbm, wfo_hbm, o_ref,
                    woa_ref, wom_ref, wfi_ref, wfo_ref, wide_ref, narrow_ref, sem_ref, *, hidden):
    def stream(jobs, stage_ref, sems):
        slots = stage_ref.shape[0]

        def copy(k):
            src, _, n = jobs[k]
            return pltpu.make_async_copy(src, stage_ref.at[k % slots, pl.ds(0, n)], sems.at[k % slots])
        for k in range(min(slots - 1, len(jobs))):
            copy(k).start()
        for k, (_, dst, n) in enumerate(jobs):
            if k + slots - 1 < len(jobs):
                copy(k + slots - 1).start()
            copy(k).wait()
            dst[...] = stage_ref[k % slots, :n, :].astype(BF16)

    @pl.when(pl.program_id(0) == 0)
    def _():
        hd = ATT_HEAD_DIM
        nrows = narrow_ref.shape[1]
        narrow = [(wo_hbm.at[pl.ds(j * hd, hd)], woa_ref.at[pl.ds(p * hd, hd)], hd)
                  for p, j in enumerate(_HEAD_PERM)]
        for lo in range(0, ML_WIDTH, nrows):
            n = min(nrows, ML_WIDTH - lo)
            narrow.append((wo_hbm.at[pl.ds(ATT_WIDTH + lo, n)], wom_ref.at[pl.ds(lo, n)], n))
        for lo in range(0, hidden, nrows):
            n = min(nrows, hidden - lo)
            narrow.append((wfo_hbm.at[pl.ds(lo, n)], wfo_ref.at[pl.ds(lo, n)], n))
        wrows = wide_ref.shape[1]
        wide = [(wfi_hbm.at[pl.ds(lo, wrows)], wfi_ref.at[pl.ds(lo, wrows)], wrows)
                for lo in range(0, wfi_ref.shape[0], wrows)]
        stream(wide, wide_ref, sem_ref.at[0])
        stream(narrow, narrow_ref, sem_ref.at[1])

    rows = [slice(r * FFN_ROWS, (r + 1) * FFN_ROWS) for r in range(x_ref.shape[0] // FFN_ROWS)]

    def mix_stage(r):
        return _dot(att_ref[r, :], woa_ref[...]) + _dot(ml_ref[r, :], wom_ref[...])

    def norm_stage(r, mix):
        x1 = x_ref[r, :] + gtm_ref[0] * _rms(mix, gpm_ref[...])
        return x1, (_rms(x1, gpf_ref[...]) * (1.0 + scf_ref[0]) + shf_ref[0]).astype(BF16)

    def act_stage(gu):
        return (_silu(gu[:, :hidden]) * gu[:, hidden:]).astype(BF16)

    def out_stage(r, x1, fx):
        o_ref[r, :] = x1 + gtf_ref[0] * _rms(fx, gqf_ref[...])

    a, b = rows
    mix_a = mix_stage(a)
    mix_b = mix_stage(b)
    x1_a, h_a = norm_stage(a, mix_a)
    gu_a = _dot(h_a, wfi_ref[...])
    x1_b, h_b = norm_stage(b, mix_b)
    gu_b = _dot(h_b, wfi_ref[...])
    act_a = act_stage(gu_a)
    fx_a = _dot(act_a, wfo_ref[...])
    act_b = act_stage(gu_b)
    fx_b = _dot(act_b, wfo_ref[...])
    out_stage(a, x1_a, fx_a)
    out_stage(b, x1_b, fx_b)


def _out_ffn(x2, att2, ml2, mod3, g_post_mix, g_pre_ffn, g_post_ffn, w_out, w_ffn_in, w_ffn_out, *, tiles_per_batch):
    t, d = x2.shape
    tm = FFN_TM
    hidden = w_ffn_out.shape[0]
    hbm = pl.BlockSpec(memory_space=pl.ANY)
    mod = lambda k: pl.BlockSpec((1, 1, d), lambda i: (i // tiles_per_batch, 0, k))
    row = pl.BlockSpec((1, d), lambda i: (0, 0))
    return pl.pallas_call(
        functools.partial(_out_ffn_kernel, hidden=hidden),
        grid=(t // tm,),
        in_specs=[pl.BlockSpec((tm, d), lambda i: (i, 0)),
                  pl.BlockSpec((tm, ATT_WIDTH), lambda i: (i, 0)),
                  pl.BlockSpec((tm, ML_WIDTH), lambda i: (i, 0)),
                  mod(2), mod(3), mod(4), mod(5), row, row, row, hbm, hbm, hbm],
        out_specs=pl.BlockSpec((tm, d), lambda i: (i, 0)),
        out_shape=jax.ShapeDtypeStruct((t, d), F32),
        scratch_shapes=[pltpu.VMEM((ATT_WIDTH, d), BF16), pltpu.VMEM((ML_WIDTH, d), BF16),
                        pltpu.VMEM(w_ffn_in.shape, BF16), pltpu.VMEM(w_ffn_out.shape, BF16),
                        pltpu.VMEM((FFN_STAGE_SLOTS, FFN_STAGE_ROWS_WIDE, w_ffn_in.shape[1]), F32),
                        pltpu.VMEM((FFN_STAGE_SLOTS, FFN_STAGE_ROWS_NARROW, d), F32),
                        pltpu.SemaphoreType.DMA((2, FFN_STAGE_SLOTS))],
        compiler_params=pltpu.CompilerParams(dimension_semantics=("arbitrary",),
                                             vmem_limit_bytes=VMEM_LIMIT),
        name="out_ffn",
    )(x2, att2, ml2, mod3, mod3, mod3, mod3, g_post_mix, g_pre_ffn, g_post_ffn, w_out, w_ffn_in, w_ffn_out)


def _rope_tables(n_tokens):
    pos = jnp.arange(n_tokens)
    row = (pos // GRID_W).astype(F32)
    col = (pos % GRID_W).astype(F32)
    half = ATT_HEAD_DIM // 4
    inv_freq = jnp.power(ROPE_BASE, -jnp.arange(half, dtype=F32) / half)
    ang_r = row[:, None] * inv_freq
    ang_c = col[:, None] * inv_freq
    z = jnp.zeros_like(ang_r)
    reps = LANES // ATT_HEAD_DIM
    cos = jnp.tile(jnp.concatenate([jnp.cos(ang_r)] * 2 + [jnp.cos(ang_c)] * 2, axis=1), (1, reps))
    sin_lo = jnp.tile(jnp.concatenate([-jnp.sin(ang_r), z, -jnp.sin(ang_c), z], axis=1), (1, reps))
    sin_hi = jnp.tile(jnp.concatenate([z, jnp.sin(ang_r), z, jnp.sin(ang_c)], axis=1), (1, reps))
    return cos, sin_lo, sin_hi


def _permute_heads(w, axis):
    shape = w.shape
    split = shape[:axis] + (ATT_HEADS, ATT_HEAD_DIM) + shape[axis + 1:]
    return jnp.take(w.reshape(split), jnp.array(_HEAD_PERM), axis=axis).reshape(shape)


def kernel(x, c, ctx, c_ctx, w_ada, b_ada, g_pre_mix, w_in, w_conv_qk, b_gates, attn_sink,
           g_mlstm_out, w_out, g_post_mix, g_pre_ffn, w_ffn_in, w_ffn_out, g_post_ffn):
    bsz, s, d = x.shape
    l = ctx.shape[1]
    assert w_ada.shape[0] == 1, "single-layer block"
    assert bsz < MOD_ROWS and s % INPROJ_TM == 0 and l % ML_CHUNK == 0 and s % FFN_TM == 0

    cc = jnp.zeros((MOD_ROWS, d), F32).at[:bsz].set(c).at[bsz].set(c_ctx)
    mod3 = _ada(cc, w_ada[0], b_ada).reshape(MOD_ROWS, 1, 6 * d)

    w = w_in[0]
    o_q, o_k, o_v = 0, ATT_WIDTH, ATT_WIDTH + ATT_KV_WIDTH
    o_mq = o_v + ATT_KV_WIDTH
    o_mv = o_mq + 2 * ML_QK_WIDTH
    o_mo = o_mv + ML_WIDTH
    o_mg = o_mo + ML_WIDTH
    w_q = _permute_heads(w[:, o_q:o_k], 1) * (ATT_HEAD_DIM ** -0.5 * LOG2_E)
    w_g = jnp.pad(w[:, o_mg:], ((0, 0), (0, LANES - ML_GATES)))
    shared = [w[:, o_k:o_v], w_g, w[:, o_mq:o_mv]]
    w_lat = jnp.concatenate([w_q] + shared + [w[:, o_mo:o_mg]], axis=1).astype(BF16)
    w_ctx = jnp.concatenate(shared, axis=1).astype(BF16)
    wvt = jnp.concatenate([w[:, o_v:o_mq], w[:, o_mv:o_mo], w[:, o_mg:]], axis=1).T.astype(BF16)
    bg_row = jnp.pad(b_gates, ((0, 0), (0, LANES - ML_GATES)))
    bg_col = b_gates.reshape(ML_GATES, 1)
    wc = w_conv_qk[0]

    q, k, vt, qkx, vtx, ox, gcx, grx = _inproj(
        x, mod3, lambda b: b, g_pre_mix, w_lat, wvt, bg_row, bg_col, wc, _rope_tables(s),
        tm=INPROJ_TM, gm=INPROJ_ROWS, latent=True)
    kc, vct, qkc, vtc, _, grc = _inproj(
        ctx, mod3, lambda b: bsz, g_pre_mix, w_ctx, wvt, bg_row, bg_col, wc, None,
        tm=l, gm=l, latent=False)

    sink_row = jnp.repeat(attn_sink[0][jnp.array(_HEAD_PERM)] * LOG2_E, ATT_BLOCK)[None, :]
    att = _attention(q, k, vt, kc, vct, sink_row)
    ml = _mlstm(qkx, vtx, ox, gcx, grx, qkc, vtc, grc, g_mlstm_out)

    out = _out_ffn(x.reshape(bsz * s, d), att.reshape(bsz * s, ATT_WIDTH), ml.reshape(bsz * s, ML_WIDTH),
                   mod3, g_post_mix, g_pre_ffn, g_post_ffn, w_out[0], w_ffn_in[0], w_ffn_out[0],
                   tiles_per_batch=s // FFN_TM)
    return out.reshape(bsz, s, d)
```

```python
import functools

import jax
import jax.numpy as jnp
from jax import lax
from jax.experimental import pallas as pl
from jax.experimental.pallas import tpu as pltpu

F32 = jnp.float32
BF16 = jnp.bfloat16

EPS = 1e-6
GRID_W = 64
ROPE_BASE = 10000.0
LOG2_E = 1.4426950408889634

ATT_HEADS = 8
ATT_KV_HEADS = 2
ATT_GROUP = ATT_HEADS // ATT_KV_HEADS
ATT_HEAD_DIM = 64
ATT_BLOCK = 128
ATT_WIDTH = ATT_HEADS * ATT_HEAD_DIM
ATT_KV_WIDTH = ATT_KV_HEADS * ATT_HEAD_DIM
ATT_BLOCKS_PER_STEP = 16
ATT_ONES_ROWS = 16

ML_HEADS = 4
ML_V_DIM = 128
ML_QK_DIM = 64
ML_WIDTH = ML_HEADS * ML_V_DIM
ML_QK_WIDTH = ML_HEADS * ML_QK_DIM
ML_GATES = 4 * ML_HEADS
ML_CHUNK = 128

LANES = 128
SUBLANES = 8
VMEM_LIMIT = 56 * 1024 * 1024

INPROJ_TM = 1024
INPROJ_ROWS = 512
FFN_TM = 512
FFN_ROWS = 256
ADA_TN = 1536
MOD_ROWS = 16

_HEAD_PERM = tuple(h * ATT_GROUP + g for g in range(ATT_GROUP) for h in range(ATT_KV_HEADS))


def _silu(v):
    return v * jax.nn.sigmoid(v)


def _log_sigmoid(v):
    return jnp.minimum(v, 0.0) - jnp.log1p(jnp.exp(-jnp.abs(v)))


def _rms(v, g):
    return v * lax.rsqrt(jnp.mean(v * v, axis=-1, keepdims=True) + EPS) * g


def _dot(a, b):
    return jnp.dot(a, b, preferred_element_type=F32)


def _dot_nt(a, b):
    return lax.dot_general(a, b, (((1,), (1,)), ((), ())), preferred_element_type=F32)


def _ada_kernel(c_ref, cctx_ref, w_ref, b_ref, o_ref):
    pad = jnp.zeros((MOD_ROWS - c_ref.shape[0] - 1, c_ref.shape[1]), F32)
    a = _silu(jnp.concatenate([c_ref[...], cctx_ref[...], pad], axis=0))
    o_ref[...] = _dot(a.astype(BF16), w_ref[...].astype(BF16)) + b_ref[...]


def _ada(c, c_ctx, w, b):
    d, n = w.shape
    return pl.pallas_call(
        _ada_kernel,
        grid=(n // ADA_TN,),
        in_specs=[pl.BlockSpec(c.shape, lambda j: (0, 0)),
                  pl.BlockSpec((1, d), lambda j: (0, 0)),
                  pl.BlockSpec((d, ADA_TN), lambda j: (0, j)),
                  pl.BlockSpec((1, ADA_TN), lambda j: (0, j))],
        out_specs=pl.BlockSpec((MOD_ROWS, ADA_TN), lambda j: (0, j)),
        out_shape=jax.ShapeDtypeStruct((MOD_ROWS, n), F32),
        compiler_params=pltpu.CompilerParams(dimension_semantics=("arbitrary",),
                                             vmem_limit_bytes=VMEM_LIMIT),
        name="ada",
    )(c, c_ctx.reshape(1, d), w, b)


def _rope(v, cos, sin_lo, sin_hi):
    return (v * cos + pltpu.roll(v, LANES - 16, axis=1) * sin_lo
            + pltpu.roll(v, 16, axis=1) * sin_hi)


def _inproj_kernel(*refs, tm, gm, n_tiles, latent, mod_row):
    if latent:
        (x_ref, xp_ref, xn_ref, sh_ref, sc_ref, g_ref, w_ref, wvt_ref, bgr_ref, bgc_ref,
         wc_ref, cos_ref, sl_ref, shi_ref,
         q_ref, k_ref, vt_ref, qk_ref, mvt_ref, o_ref, gcol_ref, grow_ref) = refs
    else:
        (x_ref, xp_ref, xn_ref, sh_ref, sc_ref, g_ref, w_ref, wvt_ref, bgr_ref, bgc_ref,
         wc_ref,
         k_ref, vt_ref, qk_ref, mvt_ref, gcol_ref, grow_ref) = refs
    i = pl.program_id(0)
    n_groups = tm // gm
    r_mod = (pl.program_id(1) if mod_row is None else mod_row) % SUBLANES
    scale = g_ref[...] * (1.0 + sc_ref[pl.ds(r_mod, 1), :])
    shift = sh_ref[pl.ds(r_mod, 1), :]
    wc = wc_ref[...]
    row = lax.broadcasted_iota(jnp.int32, (gm, 1), 0)

    def project(lo, width, lhs):
        return _dot(lhs, w_ref[:, lo:lo + width])

    def group(r):
        rows = slice(r * gm, (r + 1) * gm)
        before = xp_ref[0] if r == 0 else x_ref[0, r * gm - SUBLANES:r * gm, :]
        after = xn_ref[0] if r == n_groups - 1 else x_ref[0, (r + 1) * gm:(r + 1) * gm + SUBLANES, :]
        keep_prev = jnp.where(i == 0, 0.0, 1.0) if r == 0 else 1.0
        keep_next = jnp.where(i == n_tiles - 1, 0.0, 1.0) if r == n_groups - 1 else 1.0
        xt = jnp.concatenate([x_ref[0, rows, :], before, after], axis=0)
        ms = jnp.mean(xt * xt, axis=-1, keepdims=True)
        hb = (xt * lax.rsqrt(ms + EPS) * scale + shift).astype(BF16)
        hm = hb[:gm]
        yield

        c = ATT_WIDTH
        if latent:
            r_q = project(0, ATT_WIDTH, hm)
        r_kg = project(c, ATT_KV_WIDTH + LANES, hm)
        c += ATT_KV_WIDTH + LANES
        yield

        if latent:
            cos, sl, shi = cos_ref[rows, :], sl_ref[rows, :], shi_ref[rows, :]
            for g in range(ATT_WIDTH // LANES):
                q_ref[0, rows, g * LANES:(g + 1) * LANES] = _rope(
                    r_q[:, g * LANES:(g + 1) * LANES], cos, sl, shi).astype(BF16)
        y = project(c, 2 * ML_QK_WIDTH, hb)
        c += 2 * ML_QK_WIDTH
        yield

        if latent:
            k_ref[0, rows, :] = _rope(r_kg[:, :LANES], cos, sl, shi).astype(BF16)
        else:
            k_ref[0, rows, :] = r_kg[:, :LANES].astype(BF16)
        gcol_ref[0, rows, :] = r_kg[:, LANES:] + bgr_ref[...]
        v_t = _dot_nt(wvt_ref[...], hm)
        yield

        ym = y[:gm]
        prev = jnp.where(row == 0, y[gm + SUBLANES - 1:gm + SUBLANES] * keep_prev,
                         pltpu.roll(ym, 1, axis=0))
        nxt = jnp.where(row == gm - 1, y[gm + SUBLANES:gm + SUBLANES + 1] * keep_next,
                        pltpu.roll(ym, gm - 1, axis=0))
        act = _silu(prev * wc[0:1] + ym * wc[1:2] + nxt * wc[2:3])
        qk_ref[0, rows, :ML_QK_WIDTH] = (act[:, :ML_QK_WIDTH] * (ML_QK_DIM ** -0.5)).astype(BF16)
        qk_ref[0, rows, ML_QK_WIDTH:] = act[:, ML_QK_WIDTH:].astype(BF16)
        if latent:
            r_o = project(c, ML_WIDTH, hm)
        yield

        g_t = v_t[ATT_KV_WIDTH + ML_WIDTH:] + bgc_ref[...]
        vt_ref[0, :, rows] = v_t[:ATT_KV_WIDTH].astype(BF16)
        for j in range(gm // ML_CHUNK):
            cols = slice(j * ML_CHUNK, (j + 1) * ML_CHUNK)
            mvt_ref[0, r * (gm // ML_CHUNK) + j] = v_t[ATT_KV_WIDTH:ATT_KV_WIDTH + ML_WIDTH, cols].astype(BF16)
            grow_ref[0, r * (gm // ML_CHUNK) + j] = g_t[:, cols]
        if latent:
            o_ref[0, rows, :] = jax.nn.sigmoid(r_o).astype(BF16)

    done = object()
    waiting = [group(r) for r in range(n_groups)]
    active = []
    while waiting or active:
        if waiting:
            active.append(waiting.pop(0))
        active = [g for g in active if next(g, done) is not done]


def _inproj(x, mod, mod_row, g_pre, w_main, wvt, bg_row, bg_col, wc, rope_tabs, *, tm, gm, latent):
    bsz, t, d = x.shape
    n_tiles = t // tm
    hb = tm // SUBLANES
    n_hblk = t // SUBLANES
    n = w_main.shape[1]
    mod_blk = (lambda b: b // SUBLANES) if mod_row is None else (lambda b: mod_row // SUBLANES)

    def const(shape):
        return pl.BlockSpec(shape, lambda i, b: (0,) * len(shape))

    in_specs = [
        pl.BlockSpec((1, tm, d), lambda i, b: (b, i, 0)),
        pl.BlockSpec((1, SUBLANES, d), lambda i, b: (b, jnp.maximum(i * hb - 1, 0), 0)),
        pl.BlockSpec((1, SUBLANES, d), lambda i, b: (b, jnp.minimum((i + 1) * hb, n_hblk - 1), 0)),
        pl.BlockSpec((SUBLANES, d), lambda i, b: (mod_blk(b), 0)),
        pl.BlockSpec((SUBLANES, d), lambda i, b: (mod_blk(b), 1)),
        const((1, d)), const((d, n)), const((ATT_KV_WIDTH + ML_WIDTH + ML_GATES, d)),
        const((1, LANES)), const((ML_GATES, 1)), const((3, 2 * ML_QK_WIDTH)),
    ]
    args = [x, x, x, mod, mod, g_pre, w_main, wvt, bg_row, bg_col, wc]
    tok = lambda width, dt: (pl.BlockSpec((1, tm, width), lambda i, b: (b, i, 0)),
                             jax.ShapeDtypeStruct((bsz, t, width), dt))
    chunked = lambda rows, dt: (pl.BlockSpec((1, tm // ML_CHUNK, rows, ML_CHUNK), lambda i, b: (b, i, 0, 0)),
                                jax.ShapeDtypeStruct((bsz, t // ML_CHUNK, rows, ML_CHUNK), dt))
    outs = []
    if latent:
        in_specs += [pl.BlockSpec((tm, LANES), lambda i, b: (i, 0))] * 3
        args += list(rope_tabs)
        outs.append(tok(ATT_WIDTH, BF16))
    outs.append(tok(ATT_KV_WIDTH, BF16))
    outs.append((pl.BlockSpec((1, LANES, tm), lambda i, b: (b, 0, i)),
                 jax.ShapeDtypeStruct((bsz, ATT_KV_WIDTH, t), BF16)))
    outs.append(tok(2 * ML_QK_WIDTH, BF16))
    outs.append(chunked(ML_WIDTH, BF16))
    if latent:
        outs.append(tok(ML_WIDTH, BF16))
    outs.append(tok(LANES, F32))
    outs.append(chunked(ML_GATES, F32))
    return pl.pallas_call(
        functools.partial(_inproj_kernel, tm=tm, gm=gm, n_tiles=n_tiles, latent=latent, mod_row=mod_row),
        grid=(n_tiles, bsz),
        in_specs=in_specs,
        out_specs=[o[0] for o in outs],
        out_shape=[o[1] for o in outs],
        compiler_params=pltpu.CompilerParams(dimension_semantics=("arbitrary", "arbitrary"),
                                             vmem_limit_bytes=VMEM_LIMIT),
        name="inproj_latent" if latent else "inproj_context",
    )(*args)


def _attn_kernel(q_ref, kp_ref, kc_ref, kn_ref, kx_ref, vp_ref, vc_ref, vn_ref, vx_ref, sink_ref,
                 o_ref, *, n_tiles):
    i = pl.program_id(1)
    blk = ATT_BLOCK
    lane = lax.broadcasted_iota(jnp.int32, (blk, LANES), 1)
    zero = jnp.zeros((blk, LANES), BF16)
    half_groups = ATT_GROUP // 2
    n_slots = 2 * half_groups
    sink = sink_ref[:, :n_slots * blk], sink_ref[:, n_slots * blk:]

    def stack_heads(q, half):
        parts = []
        for g in range(half * half_groups, (half + 1) * half_groups):
            slab = q[:, g * LANES:(g + 1) * LANES]
            parts.append(jnp.where(lane < ATT_HEAD_DIM, slab, zero))
            parts.append(jnp.where(lane >= ATT_HEAD_DIM, slab, zero))
        return jnp.concatenate(parts, axis=0)

    key = lax.broadcasted_iota(jnp.int32, (blk, blk), 0)
    qry = lax.broadcasted_iota(jnp.int32, (blk, blk), 1)
    ninf = jnp.full((blk, blk), -jnp.inf, F32)
    bias_prev = jnp.where(key >= qry, 0.0, ninf)
    bias_next = jnp.where(key <= qry, 0.0, ninf)
    slots = lambda b: jnp.concatenate([b] * n_slots, axis=1)

    def scores(qs, k_prev, k_cur, k_next):
        return [_dot_nt(k_prev, qs), _dot_nt(k_cur, qs), _dot_nt(k_next, qs), _dot_nt(kx_ref[0], qs)]

    def softmax(s, b_prev, b_next, sink_h):
        s = jnp.concatenate([s[0] + slots(b_prev), s[1], s[2] + slots(b_next), s[3]], axis=0)
        m = jnp.maximum(sink_h, jnp.max(s, axis=0, keepdims=True))
        return jnp.exp2(s - m).astype(BF16), jnp.exp2(sink_h - m)

    n_keys = 3 * blk + kx_ref.shape[1]
    ones_rows = jnp.ones((ATT_ONES_ROWS, n_keys), BF16)

    def weighted_values(p, v_prev, v_cur, v_next):
        vt = jnp.concatenate([v_prev, v_cur, v_next, vx_ref[0]], axis=1)
        return _dot(jnp.concatenate([vt, ones_rows], axis=0), p)

    dim = lax.broadcasted_iota(jnp.int32, (LANES, blk), 0)

    def emit(rows, half, ot, p_sink):
        ot = ot[:LANES] * (1.0 / (ot[LANES:LANES + 1] + p_sink))
        for j in range(half_groups):
            g = half * half_groups + j
            a = ot[:, (2 * j) * blk:(2 * j + 1) * blk]
            b = ot[:, (2 * j + 1) * blk:(2 * j + 2) * blk]
            o_ref[0, rows, g * LANES:(g + 1) * LANES] = jnp.where(dim < ATT_HEAD_DIM, a, b).T.astype(BF16)

    nblk = q_ref.shape[1] // blk
    rows_of = [slice(b * blk, (b + 1) * blk) for b in range(nblk)]
    k_blocks = [kp_ref[0]] + [kc_ref[0, r, :] for r in rows_of] + [kn_ref[0]]
    v_blocks = [vp_ref[0]] + [vc_ref[0, :, r] for r in rows_of] + [vn_ref[0]]
    keys = [tuple(k_blocks[b:b + 3]) for b in range(nblk)]
    vals = [tuple(v_blocks[b:b + 3]) for b in range(nblk)]
    bias = [(jnp.where(i == 0, ninf, bias_prev) if b == 0 else bias_prev,
             jnp.where(i == n_tiles - 1, ninf, bias_next) if b == nblk - 1 else bias_next)
            for b in range(nblk)]
    units = [(b, half) for b in range(nblk) for half in range(2)]

    def stage_scores(u):
        b, half = u
        return scores(stack_heads(q_ref[0, rows_of[b], :], half), *keys[b])

    def stage_softmax(u, s):
        b, half = u
        return softmax(s, *bias[b], sink[half])

    def stage_values(u, p):
        return weighted_values(p, *vals[u[0]])

    def stage_emit(u, ot, p_sink):
        emit(rows_of[u[0]], u[1], ot, p_sink)

    n_units = len(units)
    s, p, ot = {}, {}, {}
    s[0] = stage_scores(units[0])
    for t in range(n_units):
        if t + 1 < n_units:
            s[t + 1] = stage_scores(units[t + 1])
        p[t] = stage_softmax(units[t], s.pop(t))
        if t >= 1:
            ot[t - 1] = stage_values(units[t - 1], p[t - 1][0])
        if t >= 2:
            stage_emit(units[t - 2], ot.pop(t - 2), p.pop(t - 2)[1])
    ot[n_units - 1] = stage_values(units[n_units - 1], p[n_units - 1][0])
    for t in (n_units - 2, n_units - 1):
        stage_emit(units[t], ot.pop(t), p.pop(t)[1])


def _attention(q, k, vt, kx, vxt, sink_row):
    bsz, s, _ = q.shape
    l = kx.shape[1]
    blk = ATT_BLOCK
    per = ATT_BLOCKS_PER_STEP
    n_tiles = s // (per * blk)
    nb = s // blk
    prev = lambda i: jnp.maximum(per * i - 1, 0)
    nxt = lambda i: jnp.minimum(per * (i + 1), nb - 1)
    return pl.pallas_call(
        functools.partial(_attn_kernel, n_tiles=n_tiles),
        grid=(bsz, n_tiles),
        in_specs=[pl.BlockSpec((1, per * blk, ATT_WIDTH), lambda b, i: (b, i, 0)),
                  pl.BlockSpec((1, blk, LANES), lambda b, i: (b, prev(i), 0)),
                  pl.BlockSpec((1, per * blk, LANES), lambda b, i: (b, i, 0)),
                  pl.BlockSpec((1, blk, LANES), lambda b, i: (b, nxt(i), 0)),
                  pl.BlockSpec((1, l, LANES), lambda b, i: (b, 0, 0)),
                  pl.BlockSpec((1, LANES, blk), lambda b, i: (b, 0, prev(i))),
                  pl.BlockSpec((1, LANES, per * blk), lambda b, i: (b, 0, i)),
                  pl.BlockSpec((1, LANES, blk), lambda b, i: (b, 0, nxt(i))),
                  pl.BlockSpec((1, LANES, l), lambda b, i: (b, 0, 0)),
                  pl.BlockSpec((1, ATT_HEADS * blk), lambda b, i: (0, 0))],
        out_specs=pl.BlockSpec((1, per * blk, ATT_WIDTH), lambda b, i: (b, i, 0)),
        out_shape=jax.ShapeDtypeStruct((bsz, s, ATT_WIDTH), BF16),
        compiler_params=pltpu.CompilerParams(dimension_semantics=("arbitrary", "arbitrary"),
                                             vmem_limit_bytes=VMEM_LIMIT),
        name="attention",
    )(q, k, k, k, kx, vt, vt, vt, vxt, sink_row)


ML_STATE_ROWS = ML_V_DIM + SUBLANES


def _mlstm_kernel(qkx_ref, vtx_ref, ox_ref, gcx_ref, grx_ref, qkc_ref, vtc_ref, grc_ref,
                  gain_ref, out_ref, sin_ref, st_ref, ucol_ref, mrun_ref, wint_ref, floor_ref,
                  wkey_ref, decay_ref, *, ncx, ncc):
    lc = ML_CHUNK
    qkw = ML_QK_WIDTH
    sr = ML_STATE_ROWS
    ng = ML_GATES
    hg = ML_GATES // 2
    st_ref[...] = jnp.zeros(st_ref.shape, F32)

    rr = lax.broadcasted_iota(jnp.int32, (lc, lc), 0)
    cc = lax.broadcasted_iota(jnp.int32, (lc, lc), 1)
    tril = rr >= cc
    triu = rr <= cc
    tril_b = jnp.where(tril, 1.0, 0.0).astype(BF16)
    triu_b = jnp.where(triu, 1.0, 0.0).astype(BF16)
    head_of_lane = lax.broadcasted_iota(jnp.int32, (lc, qkw), 1) // ML_QK_DIM
    own_block = jnp.concatenate(
        [lax.broadcasted_iota(jnp.int32, (sr, qkw), 1) // ML_QK_DIM == h for h in range(ML_HEADS)], axis=0)

    pad_rows = jnp.zeros((SUBLANES - 1, lc), F32)

    def split(v):
        hi = v.astype(BF16)
        return hi, (v - hi.astype(F32)).astype(BF16)

    def gate_rows(g):
        n16 = g.shape[0]
        fwd_row = (lax.broadcasted_iota(jnp.int32, g.shape, 0) & (ng - 1)) < hg
        lane = lax.broadcasted_iota(jnp.int32, g.shape, 1)
        hi, lo = split(_log_sigmoid(g))
        cat = jnp.concatenate([hi, lo], axis=0)
        bu = _dot(cat, triu_b)
        bl = _dot(cat, tril_b)
        b = jnp.where(fwd_row, bu[:n16] + bu[n16:], bl[:n16] + bl[n16:])
        b = pltpu.roll(b, n16 - ML_HEADS, axis=0)
        u = g - b
        run_f = run_b = u
        k = 1
        while k < lc:
            run_f = jnp.maximum(run_f, jnp.where(lane >= k, pltpu.roll(run_f, k, axis=1), -jnp.inf))
            run_b = jnp.maximum(run_b, jnp.where(lane < lc - k, pltpu.roll(run_b, lc - k, axis=1), -jnp.inf))
            k *= 2
        run = jnp.where(fwd_row, run_f, run_b)

        def at_end(a):
            return jnp.where(fwd_row, jnp.broadcast_to(a[:, lc - 1:lc], a.shape),
                             jnp.broadcast_to(a[:, 0:1], a.shape))

        return u, b, run, at_end(b), at_end(run)

    def derived(u, b, run, b_tot, run_end, m_in):
        m_run = jnp.maximum(run, m_in)
        m_out = b_tot + jnp.maximum(run_end, m_in)
        return (m_run * LOG2_E, jnp.exp(m_in - m_run), jnp.exp(-(b + m_run)),
                jnp.exp(b_tot + u - m_out), jnp.exp(b_tot + m_in - m_out))

    gc_rows = gate_rows(grc_ref[0].reshape(ncc * ng, lc))
    gx_rows = gate_rows(grx_ref[0].reshape(ncx * ng, lc))

    def scan_m(rows, n, m_f, m_b):
        _, _, _, b_tot, run_end = rows
        part = lambda a, c, d: a[c * ng + d * hg:c * ng + (d + 1) * hg]
        ins_f, ins_b = [], [None] * n
        for c in range(n):
            ins_f.append(m_f)
            m_f = part(b_tot, c, 0) + jnp.maximum(part(run_end, c, 0), m_f)
        for c in reversed(range(n)):
            ins_b[c] = m_b
            m_b = part(b_tot, c, 1) + jnp.maximum(part(run_end, c, 1), m_b)
        return jnp.concatenate([x for c in range(n) for x in (ins_f[c], ins_b[c])], axis=0), m_f, m_b

    m0 = jnp.zeros((hg, lc), F32)
    m_in_c, m_f, m_b = scan_m(gc_rows, ncc, m0, m0)
    m_in_x, _, _ = scan_m(gx_rows, ncx, m_f, m_b)
    _, _, _, wkey_c, decay_c = derived(*gc_rows, m_in_c)
    for ref, val in zip((mrun_ref, wint_ref, floor_ref, wkey_ref, decay_ref), derived(*gx_rows, m_in_x)):
        ref[...] = val

    fwd_col = (lax.broadcasted_iota(jnp.int32, (lc, LANES), 1) & (ng - 1)) < hg

    def token_major_u(j):
        rows = pl.ds(pl.multiple_of(j * lc, lc), lc)
        gcol = gcx_ref[0, rows, :]
        hi, lo = split(_log_sigmoid(gcol))
        cat = jnp.concatenate([hi, lo], axis=1)
        bl = _dot(tril_b, cat)
        bu = _dot(triu_b, cat)
        b = jnp.where(fwd_col, bl[:, :LANES] + bl[:, LANES:], bu[:, :LANES] + bu[:, LANES:])
        ucol_ref[rows, :] = (gcol - pltpu.roll(b, LANES - ML_HEADS, axis=1)) * LOG2_E

    def advance(dirn, k4, vt, w_key, decay):
        st = st_ref[dirn]
        pieces, decays = [], []
        for h in range(ML_HEADS):
            c = hg * dirn + h
            pieces += [vt[h * ML_V_DIM:(h + 1) * ML_V_DIM].astype(F32) * w_key[c:c + 1], w_key[c:c + 1], pad_rows]
            decays.append(jnp.broadcast_to(jnp.concatenate([decay[c:c + 1]] * (qkw // lc), axis=1), (sr, qkw)))
        upd = _dot(jnp.concatenate(pieces, axis=0).astype(BF16), k4)
        st_ref[dirn] = jnp.concatenate(decays, axis=0) * st + jnp.where(own_block, upd, 0.0)

    for j in range(ncc):
        for dirn, cj in ((0, j), (1, ncc - 1 - j)):
            grows = slice(cj * ng, (cj + 1) * ng)
            advance(dirn, qkc_ref[0, cj * lc:(cj + 1) * lc, qkw:], vtc_ref[0, cj], wkey_c[grows], decay_c[grows])

    def scan_body(j, carry):
        for dirn, cj in ((0, j), (1, ncx - 1 - j)):
            rows = pl.ds(pl.multiple_of(cj * lc, lc), lc)
            grows = pl.ds(pl.multiple_of(cj * ng, ng), ng)
            sin_ref[dirn, cj] = st_ref[dirn].astype(BF16)
            advance(dirn, qkx_ref[0, rows, qkw:], vtx_ref[0, cj], wkey_ref[grows, :], decay_ref[grows, :])
        token_major_u(j)
        return carry

    lax.fori_loop(0, ncx, scan_body, 0, unroll=8)

    gain = gain_ref[...]

    def out_body(j, carry):
        rows = pl.ds(pl.multiple_of(j * lc, lc), lc)
        grows = pl.ds(pl.multiple_of(j * ng, ng), ng)
        q4 = qkx_ref[0, rows, :qkw]
        k4 = qkx_ref[0, rows, qkw:]
        vt = vtx_ref[0, j]
        u_col = ucol_ref[rows, :]
        m_run, w_int, floor = mrun_ref[grows, :], wint_ref[grows, :], floor_ref[grows, :]
        zero = jnp.zeros_like(q4)
        qs = jnp.concatenate([jnp.where(head_of_lane == h, q4, zero) for h in range(ML_HEADS)], axis=0)
        qk_t = _dot_nt(k4, qs)
        hsum = [None] * ML_HEADS
        for dirn in range(2):
            inter = _dot_nt(sin_ref[dirn, j], q4)
            valid = triu if dirn == 0 else tril
            for h in range(ML_HEADS):
                c = hg * dirn + h
                e = jnp.exp2(jnp.where(valid, u_col[:, c:c + 1] - m_run[c:c + 1], -jnp.inf))
                s_t = qk_t[:, h * lc:(h + 1) * lc] * e
                num = _dot(vt[h * ML_V_DIM:(h + 1) * ML_V_DIM], s_t.astype(BF16))
                num = num + w_int[c:c + 1] * inter[h * sr:h * sr + ML_V_DIM]
                den = (jnp.sum(s_t, axis=0, keepdims=True)
                       + w_int[c:c + 1] * inter[h * sr + ML_V_DIM:h * sr + ML_V_DIM + 1])
                hv = num * (1.0 / jnp.maximum(jnp.abs(den), floor[c:c + 1]))
                hsum[h] = hv if dirn == 0 else hsum[h] + hv
        for h in range(ML_HEADS):
            cols = slice(h * ML_V_DIM, (h + 1) * ML_V_DIM)
            hs = hsum[h]
            hn = hs * lax.rsqrt(jnp.mean(hs * hs, axis=0, keepdims=True) + EPS)
            out_ref[0, rows, cols] = (hn.T * gain[:, cols] * ox_ref[0, rows, cols].astype(F32)).astype(BF16)
        return carry

    lax.fori_loop(0, ncx, out_body, 0, unroll=8)


def _mlstm(qkx, vtx, ox, gcx, grx, qkc, vtc, grc, gain):
    bsz, s, _ = qkx.shape
    ncx, ncc = s // ML_CHUNK, qkc.shape[1] // ML_CHUNK
    per_b = lambda a: pl.BlockSpec((1,) + a.shape[1:], lambda b: (b,) + (0,) * (a.ndim - 1))
    ins = [qkx, vtx, ox, gcx, grx, qkc, vtc, grc]
    return pl.pallas_call(
        functools.partial(_mlstm_kernel, ncx=ncx, ncc=ncc),
        grid=(bsz,),
        in_specs=[per_b(a) for a in ins] + [pl.BlockSpec((1, ML_WIDTH), lambda b: (0, 0))],
        out_specs=pl.BlockSpec((1, s, ML_WIDTH), lambda b: (b, 0, 0)),
        out_shape=jax.ShapeDtypeStruct((bsz, s, ML_WIDTH), BF16),
        scratch_shapes=[pltpu.VMEM((2, ncx, ML_HEADS * ML_STATE_ROWS, ML_QK_WIDTH), BF16),
                        pltpu.VMEM((2, ML_HEADS * ML_STATE_ROWS, ML_QK_WIDTH), F32),
                        pltpu.VMEM((s, LANES), F32)]
                       + [pltpu.VMEM((ncx * ML_GATES, ML_CHUNK), F32)] * 5,
        compiler_params=pltpu.CompilerParams(dimension_semantics=("arbitrary",),
                                             vmem_limit_bytes=VMEM_LIMIT),
        name="mlstm",
    )(*ins, gain)


def _out_ffn_kernel(x_ref, att_ref, ml_ref, gtm_ref, shf_ref, scf_ref, gtf_ref,
                    gpm_ref, gpf_ref, gqf_ref, woa_ref, wom_ref, wfi_ref, wfo_ref, o_ref, *, hidden,
                    tiles_per_batch):
    rows = [slice(r * FFN_ROWS, (r + 1) * FFN_ROWS) for r in range(x_ref.shape[0] // FFN_ROWS)]
    r_mod = pl.ds((pl.program_id(0) // tiles_per_batch) % SUBLANES, 1)
    gtm, shf, scf, gtf = gtm_ref[r_mod, :], shf_ref[r_mod, :], scf_ref[r_mod, :], gtf_ref[r_mod, :]

    def mix_stage(r):
        return _dot(att_ref[r, :], woa_ref[...]) + _dot(ml_ref[r, :], wom_ref[...])

    def norm_stage(r, mix):
        x1 = x_ref[r, :] + gtm * _rms(mix, gpm_ref[...])
        return x1, (_rms(x1, gpf_ref[...]) * (1.0 + scf) + shf).astype(BF16)

    def act_stage(gu):
        return (_silu(gu[:, :hidden]) * gu[:, hidden:]).astype(BF16)

    def out_stage(r, x1, fx):
        o_ref[r, :] = x1 + gtf * _rms(fx, gqf_ref[...])

    a, b = rows
    mix_a = mix_stage(a)
    mix_b = mix_stage(b)
    x1_a, h_a = norm_stage(a, mix_a)
    gu_a = _dot(h_a, wfi_ref[...])
    x1_b, h_b = norm_stage(b, mix_b)
    gu_b = _dot(h_b, wfi_ref[...])
    act_a = act_stage(gu_a)
    fx_a = _dot(act_a, wfo_ref[...])
    act_b = act_stage(gu_b)
    fx_b = _dot(act_b, wfo_ref[...])
    out_stage(a, x1_a, fx_a)
    out_stage(b, x1_b, fx_b)


def _out_ffn(x2, att2, ml2, mod, g_post_mix, g_pre_ffn, g_post_ffn, woa, wom, wfi, wfo, *, tiles_per_batch):
    t, d = x2.shape
    tm = FFN_TM
    hidden = wfo.shape[0]
    resident = lambda a: pl.BlockSpec(a.shape, lambda i: (0,) * a.ndim, pipeline_mode=pl.Buffered(1))
    mod_spec = lambda k: pl.BlockSpec((SUBLANES, d), lambda i: (i // tiles_per_batch // SUBLANES, k))
    row = pl.BlockSpec((1, d), lambda i: (0, 0))
    return pl.pallas_call(
        functools.partial(_out_ffn_kernel, hidden=hidden, tiles_per_batch=tiles_per_batch),
        grid=(t // tm,),
        in_specs=[pl.BlockSpec((tm, d), lambda i: (i, 0)),
                  pl.BlockSpec((tm, ATT_WIDTH), lambda i: (i, 0)),
                  pl.BlockSpec((tm, ML_WIDTH), lambda i: (i, 0)),
                  mod_spec(2), mod_spec(3), mod_spec(4), mod_spec(5), row, row, row,
                  resident(woa), resident(wom), resident(wfi), resident(wfo)],
        out_specs=pl.BlockSpec((tm, d), lambda i: (i, 0)),
        out_shape=jax.ShapeDtypeStruct((t, d), F32),
        compiler_params=pltpu.CompilerParams(dimension_semantics=("arbitrary",),
                                             vmem_limit_bytes=VMEM_LIMIT),
        name="out_ffn",
    )(x2, att2, ml2, mod, mod, mod, mod, g_post_mix, g_pre_ffn, g_post_ffn, woa, wom, wfi, wfo)


def _rope_tables(n_tokens):
    pos = jnp.arange(n_tokens)
    row = (pos // GRID_W).astype(F32)
    col = (pos % GRID_W).astype(F32)
    half = ATT_HEAD_DIM // 4
    inv_freq = jnp.power(ROPE_BASE, -jnp.arange(half, dtype=F32) / half)
    ang_r = row[:, None] * inv_freq
    ang_c = col[:, None] * inv_freq
    z = jnp.zeros_like(ang_r)
    reps = LANES // ATT_HEAD_DIM
    cos = jnp.tile(jnp.concatenate([jnp.cos(ang_r)] * 2 + [jnp.cos(ang_c)] * 2, axis=1), (1, reps))
    sin_lo = jnp.tile(jnp.concatenate([-jnp.sin(ang_r), z, -jnp.sin(ang_c), z], axis=1), (1, reps))
    sin_hi = jnp.tile(jnp.concatenate([z, jnp.sin(ang_r), z, jnp.sin(ang_c)], axis=1), (1, reps))
    return cos, sin_lo, sin_hi


def _permute_heads(w, axis):
    shape = w.shape
    grouped = shape[:axis] + (ATT_KV_HEADS, ATT_GROUP, ATT_HEAD_DIM) + shape[axis + 1:]
    return jnp.swapaxes(w.reshape(grouped), axis, axis + 1).reshape(shape)


def kernel(x, c, ctx, c_ctx, w_ada, b_ada, g_pre_mix, w_in, w_conv_qk, b_gates, attn_sink,
           g_mlstm_out, w_out, g_post_mix, g_pre_ffn, w_ffn_in, w_ffn_out, g_post_ffn):
    bsz, s, d = x.shape
    l = ctx.shape[1]
    assert w_ada.shape[0] == 1, "single-layer block"
    assert bsz < MOD_ROWS and s % INPROJ_TM == 0 and l % ML_CHUNK == 0 and s % FFN_TM == 0

    mod = _ada(c, c_ctx, w_ada[0], b_ada)

    w = w_in[0]
    o_q, o_k, o_v = 0, ATT_WIDTH, ATT_WIDTH + ATT_KV_WIDTH
    o_mq = o_v + ATT_KV_WIDTH
    o_mv = o_mq + 2 * ML_QK_WIDTH
    o_mo = o_mv + ML_WIDTH
    o_mg = o_mo + ML_WIDTH
    w_q = _permute_heads(w[:, o_q:o_k], 1) * (ATT_HEAD_DIM ** -0.5 * LOG2_E)
    w_g = jnp.pad(w[:, o_mg:], ((0, 0), (0, LANES - ML_GATES)))
    shared = [w[:, o_k:o_v], w_g, w[:, o_mq:o_mv]]
    w_lat = jnp.concatenate([w_q] + shared + [w[:, o_mo:o_mg]], axis=1).astype(BF16)
    wvt = jnp.concatenate([w[:, o_v:o_mq], w[:, o_mv:o_mo], w[:, o_mg:]], axis=1).T.astype(BF16)
    bg_row = jnp.pad(b_gates, ((0, 0), (0, LANES - ML_GATES)))
    bg_col = b_gates.reshape(ML_GATES, 1)
    wc = w_conv_qk[0]

    q, k, vt, qkx, vtx, ox, gcx, grx = _inproj(
        x, mod, None, g_pre_mix, w_lat, wvt, bg_row, bg_col, wc, _rope_tables(s),
        tm=INPROJ_TM, gm=INPROJ_ROWS, latent=True)
    kc, vct, qkc, vtc, _, grc = _inproj(
        ctx, mod, bsz, g_pre_mix, w_lat, wvt, bg_row, bg_col, wc, None,
        tm=l, gm=l, latent=False)

    sink_row = jnp.repeat(attn_sink[0][jnp.array(_HEAD_PERM)] * LOG2_E, ATT_BLOCK)[None, :]
    att = _attention(q, k, vt, kc, vct, sink_row)
    ml = _mlstm(qkx, vtx, ox, gcx, grx, qkc, vtc, grc, g_mlstm_out)

    wo = w_out[0]
    woa = _permute_heads(wo[:ATT_WIDTH], 0).astype(BF16)
    wom = wo[ATT_WIDTH:].astype(BF16)
    out = _out_ffn(x.reshape(bsz * s, d), att.reshape(bsz * s, ATT_WIDTH), ml.reshape(bsz * s, ML_WIDTH),
                   mod, g_post_mix, g_pre_ffn, g_post_ffn, woa, wom,
                   w_ffn_in[0].astype(BF16), w_ffn_out[0].astype(BF16), tiles_per_batch=s // FFN_TM)
    return out.reshape(bsz, s, d)
```

```python
import functools

import jax
import jax.numpy as jnp
from jax import lax
from jax.experimental import pallas as pl
from jax.experimental.pallas import tpu as pltpu

F32 = jnp.float32
BF16 = jnp.bfloat16

EPS = 1e-6
GRID_W = 64
ROPE_BASE = 10000.0
LOG2_E = 1.4426950408889634

ATT_HEADS = 8
ATT_KV_HEADS = 2
ATT_GROUP = ATT_HEADS // ATT_KV_HEADS
ATT_HEAD_DIM = 64
ATT_BLOCK = 128
ATT_WIDTH = ATT_HEADS * ATT_HEAD_DIM
ATT_KV_WIDTH = ATT_KV_HEADS * ATT_HEAD_DIM
ATT_UNIT_GROUPS = 2
ATT_ONES_ROWS = 16

ML_HEADS = 4
ML_V_DIM = 128
ML_QK_DIM = 64
ML_WIDTH = ML_HEADS * ML_V_DIM
ML_QK_WIDTH = ML_HEADS * ML_QK_DIM
ML_GATES = 4 * ML_HEADS
ML_CHUNK = 128

LANES = 128
SUBLANES = 8
VMEM_LIMIT = 56 * 1024 * 1024

INPROJ_TM = 1024
INPROJ_ROWS = 512
FFN_TM = 512
FFN_ROWS = 256
ADA_TN = 1536
MOD_ROWS = 16

_HEAD_PERM = tuple(h * ATT_GROUP + g for g in range(ATT_GROUP) for h in range(ATT_KV_HEADS))


def _silu(v):
    return v * jax.nn.sigmoid(v)


def _log_sigmoid(v):
    return jnp.minimum(v, 0.0) - jnp.log1p(jnp.exp(-jnp.abs(v)))


def _rms(v, g):
    return v * lax.rsqrt(jnp.mean(v * v, axis=-1, keepdims=True) + EPS) * g


def _dot(a, b):
    return jnp.dot(a, b, preferred_element_type=F32)


def _dot_nt(a, b):
    return lax.dot_general(a, b, (((1,), (1,)), ((), ())), preferred_element_type=F32)


def _ada_kernel(c_ref, cctx_ref, w_ref, b_ref, o_ref):
    pad = jnp.zeros((MOD_ROWS - c_ref.shape[0] - 1, c_ref.shape[1]), F32)
    a = _silu(jnp.concatenate([c_ref[...], cctx_ref[...], pad], axis=0))
    o_ref[...] = _dot(a.astype(BF16), w_ref[...].astype(BF16)) + b_ref[...]


def _ada(c, c_ctx, w, b):
    d, n = w.shape
    return pl.pallas_call(
        _ada_kernel,
        grid=(n // ADA_TN,),
        in_specs=[pl.BlockSpec(c.shape, lambda j: (0, 0)),
                  pl.BlockSpec((1, d), lambda j: (0, 0)),
                  pl.BlockSpec((d, ADA_TN), lambda j: (0, j)),
                  pl.BlockSpec((1, ADA_TN), lambda j: (0, j))],
        out_specs=pl.BlockSpec((MOD_ROWS, ADA_TN), lambda j: (0, j)),
        out_shape=jax.ShapeDtypeStruct((MOD_ROWS, n), F32),
        compiler_params=pltpu.CompilerParams(dimension_semantics=("arbitrary",),
                                             vmem_limit_bytes=VMEM_LIMIT),
        name="ada",
    )(c, c_ctx.reshape(1, d), w, b)


def _rope(v, cos, sin_lo, sin_hi):
    return (v * cos + pltpu.roll(v, LANES - 16, axis=1) * sin_lo
            + pltpu.roll(v, 16, axis=1) * sin_hi)


def _inproj_kernel(*refs, tm, gm, n_tiles, latent, mod_row):
    if latent:
        (x_ref, xp_ref, xn_ref, sh_ref, sc_ref, g_ref, w_ref, wvt_ref, bgr_ref, bgc_ref,
         wc_ref, cos_ref, sl_ref, shi_ref,
         q_ref, k_ref, vt_ref, qk_ref, mvt_ref, o_ref, gcol_ref, grow_ref) = refs
    else:
        (x_ref, xp_ref, xn_ref, sh_ref, sc_ref, g_ref, w_ref, wvt_ref, bgr_ref, bgc_ref,
         wc_ref,
         k_ref, vt_ref, qk_ref, mvt_ref, gcol_ref, grow_ref) = refs
    i = pl.program_id(0)
    n_groups = tm // gm
    r_mod = (pl.program_id(1) if mod_row is None else mod_row) % SUBLANES
    scale = g_ref[...] * (1.0 + sc_ref[pl.ds(r_mod, 1), :])
    shift = sh_ref[pl.ds(r_mod, 1), :]
    wc = wc_ref[...]
    row = lax.broadcasted_iota(jnp.int32, (gm, 1), 0)

    def project(lo, width, lhs):
        return _dot(lhs, w_ref[:, lo:lo + width])

    def group(r):
        rows = slice(r * gm, (r + 1) * gm)
        before = xp_ref[0] if r == 0 else x_ref[0, r * gm - SUBLANES:r * gm, :]
        after = xn_ref[0] if r == n_groups - 1 else x_ref[0, (r + 1) * gm:(r + 1) * gm + SUBLANES, :]
        keep_prev = jnp.where(i == 0, 0.0, 1.0) if r == 0 else 1.0
        keep_next = jnp.where(i == n_tiles - 1, 0.0, 1.0) if r == n_groups - 1 else 1.0
        xt = jnp.concatenate([x_ref[0, rows, :], before, after], axis=0)
        ms = jnp.mean(xt * xt, axis=-1, keepdims=True)
        hb = (xt * lax.rsqrt(ms + EPS) * scale + shift).astype(BF16)
        hm = hb[:gm]
        yield

        c = ATT_WIDTH
        if latent:
            r_q = project(0, ATT_WIDTH, hm)
        r_kg = project(c, ATT_KV_WIDTH + LANES, hm)
        c += ATT_KV_WIDTH + LANES
        yield

        if latent:
            cos, sl, shi = cos_ref[rows, :], sl_ref[rows, :], shi_ref[rows, :]
            for g in range(ATT_WIDTH // LANES):
                q_ref[0, rows, g * LANES:(g + 1) * LANES] = _rope(
                    r_q[:, g * LANES:(g + 1) * LANES], cos, sl, shi).astype(BF16)
        y = project(c, 2 * ML_QK_WIDTH, hb)
        c += 2 * ML_QK_WIDTH
        yield

        if latent:
            k_ref[0, rows, :] = _rope(r_kg[:, :LANES], cos, sl, shi).astype(BF16)
        else:
            k_ref[0, rows, :] = r_kg[:, :LANES].astype(BF16)
        gcol_ref[0, rows, :] = r_kg[:, LANES:] + bgr_ref[...]
        v_t = _dot_nt(wvt_ref[...], hm)
        yield

        ym = y[:gm]
        prev = jnp.where(row == 0, y[gm + SUBLANES - 1:gm + SUBLANES] * keep_prev,
                         pltpu.roll(ym, 1, axis=0))
        nxt = jnp.where(row == gm - 1, y[gm + SUBLANES:gm + SUBLANES + 1] * keep_next,
                        pltpu.roll(ym, gm - 1, axis=0))
        act = _silu(prev * wc[0:1] + ym * wc[1:2] + nxt * wc[2:3])
        qk_ref[0, rows, :ML_QK_WIDTH] = (act[:, :ML_QK_WIDTH] * (ML_QK_DIM ** -0.5)).astype(BF16)
        qk_ref[0, rows, ML_QK_WIDTH:] = act[:, ML_QK_WIDTH:].astype(BF16)
        if latent:
            r_o = project(c, ML_WIDTH, hm)
        yield

        g_t = v_t[ATT_KV_WIDTH + ML_WIDTH:] + bgc_ref[...]
        vt_ref[0, :, rows] = v_t[:ATT_KV_WIDTH].astype(BF16)
        for j in range(gm // ML_CHUNK):
            cols = slice(j * ML_CHUNK, (j + 1) * ML_CHUNK)
            mvt_ref[0, r * (gm // ML_CHUNK) + j] = v_t[ATT_KV_WIDTH:ATT_KV_WIDTH + ML_WIDTH, cols].astype(BF16)
            grow_ref[0, r * (gm // ML_CHUNK) + j] = g_t[:, cols]
        if latent:
            o_ref[0, rows, :] = jax.nn.sigmoid(r_o).astype(BF16)

    done = object()
    waiting = [group(r) for r in range(n_groups)]
    active = []
    while waiting or active:
        if waiting:
            active.append(waiting.pop(0))
        active = [g for g in active if next(g, done) is not done]


def _inproj(x, mod, mod_row, g_pre, w_main, wvt, bg_row, bg_col, wc, rope_tabs, *, tm, gm, latent):
    bsz, t, d = x.shape
    n_tiles = t // tm
    hb = tm // SUBLANES
    n_hblk = t // SUBLANES
    n = w_main.shape[1]
    mod_blk = (lambda b: b // SUBLANES) if mod_row is None else (lambda b: mod_row // SUBLANES)

    def const(shape):
        return pl.BlockSpec(shape, lambda i, b: (0,) * len(shape))

    in_specs = [
        pl.BlockSpec((1, tm, d), lambda i, b: (b, i, 0)),
        pl.BlockSpec((1, SUBLANES, d), lambda i, b: (b, jnp.maximum(i * hb - 1, 0), 0)),
        pl.BlockSpec((1, SUBLANES, d), lambda i, b: (b, jnp.minimum((i + 1) * hb, n_hblk - 1), 0)),
        pl.BlockSpec((SUBLANES, d), lambda i, b: (mod_blk(b), 0)),
        pl.BlockSpec((SUBLANES, d), lambda i, b: (mod_blk(b), 1)),
        const((1, d)), const((d, n)), const((ATT_KV_WIDTH + ML_WIDTH + ML_GATES, d)),
        const((1, LANES)), const((ML_GATES, 1)), const((3, 2 * ML_QK_WIDTH)),
    ]
    args = [x, x, x, mod, mod, g_pre, w_main, wvt, bg_row, bg_col, wc]
    tok = lambda width, dt: (pl.BlockSpec((1, tm, width), lambda i, b: (b, i, 0)),
                             jax.ShapeDtypeStruct((bsz, t, width), dt))
    chunked = lambda rows, dt: (pl.BlockSpec((1, tm // ML_CHUNK, rows, ML_CHUNK), lambda i, b: (b, i, 0, 0)),
                                jax.ShapeDtypeStruct((bsz, t // ML_CHUNK, rows, ML_CHUNK), dt))
    outs = []
    if latent:
        in_specs += [pl.BlockSpec((tm, LANES), lambda i, b: (i, 0))] * 3
        args += list(rope_tabs)
        outs.append(tok(ATT_WIDTH, BF16))
    outs.append(tok(ATT_KV_WIDTH, BF16))
    outs.append((pl.BlockSpec((1, LANES, tm), lambda i, b: (b, 0, i)),
                 jax.ShapeDtypeStruct((bsz, ATT_KV_WIDTH, t), BF16)))
    outs.append(tok(2 * ML_QK_WIDTH, BF16))
    outs.append(chunked(ML_WIDTH, BF16))
    if latent:
        outs.append(tok(ML_WIDTH, BF16))
    outs.append(tok(LANES, F32))
    outs.append(chunked(ML_GATES, F32))
    return pl.pallas_call(
        functools.partial(_inproj_kernel, tm=tm, gm=gm, n_tiles=n_tiles, latent=latent, mod_row=mod_row),
        grid=(n_tiles, bsz),
        in_specs=in_specs,
        out_specs=[o[0] for o in outs],
        out_shape=[o[1] for o in outs],
        compiler_params=pltpu.CompilerParams(dimension_semantics=("arbitrary", "arbitrary"),
                                             vmem_limit_bytes=VMEM_LIMIT),
        name="inproj_latent" if latent else "inproj_context",
    )(*args)


def _attn_kernel(q_ref, k_ref, kx_ref, vt_ref, vx_ref, sink_ref, o_ref):
    blk = ATT_BLOCK
    lane = lax.broadcasted_iota(jnp.int32, (blk, LANES), 1)
    zero = jnp.zeros((blk, LANES), BF16)
    half_groups = ATT_UNIT_GROUPS
    n_slots = ATT_KV_HEADS * half_groups
    n_parts = ATT_GROUP // half_groups
    sink = [sink_ref[:, part * n_slots * blk:(part + 1) * n_slots * blk] for part in range(n_parts)]

    def stack_heads(q, half):
        parts = []
        for g in range(half * half_groups, (half + 1) * half_groups):
            slab = q[:, g * LANES:(g + 1) * LANES]
            parts.append(jnp.where(lane < ATT_HEAD_DIM, slab, zero))
            parts.append(jnp.where(lane >= ATT_HEAD_DIM, slab, zero))
        return jnp.concatenate(parts, axis=0)

    key = lax.broadcasted_iota(jnp.int32, (blk, blk), 0)
    qry = lax.broadcasted_iota(jnp.int32, (blk, blk), 1)
    ninf = jnp.full((blk, blk), -jnp.inf, F32)
    bias_prev = jnp.where(key >= qry, 0.0, ninf)
    bias_next = jnp.where(key <= qry, 0.0, ninf)
    slots = lambda b: jnp.concatenate([b] * n_slots, axis=1)

    def scores(qs, k_prev, k_cur, k_next):
        return [_dot_nt(k_prev, qs), _dot_nt(k_cur, qs), _dot_nt(k_next, qs), _dot_nt(kx_ref[0], qs)]

    def softmax(s, b_prev, b_next, sink_h):
        s = jnp.concatenate([s[0] + slots(b_prev), s[1], s[2] + slots(b_next), s[3]], axis=0)
        m = jnp.maximum(sink_h, jnp.max(s, axis=0, keepdims=True))
        return jnp.exp2(s - m).astype(BF16), jnp.exp2(sink_h - m)

    n_keys = 3 * blk + kx_ref.shape[1]
    ones_rows = jnp.ones((ATT_ONES_ROWS, n_keys), BF16)

    def weighted_values(p, v_prev, v_cur, v_next):
        vt = jnp.concatenate([v_prev, v_cur, v_next, vx_ref[0]], axis=1)
        return _dot(jnp.concatenate([vt, ones_rows], axis=0), p)

    dim = lax.broadcasted_iota(jnp.int32, (LANES, blk), 0)

    def emit(rows, half, ot, p_sink):
        ot = ot[:LANES] * (1.0 / (ot[LANES:LANES + 1] + p_sink))
        for j in range(half_groups):
            g = half * half_groups + j
            a = ot[:, (2 * j) * blk:(2 * j + 1) * blk]
            b = ot[:, (2 * j + 1) * blk:(2 * j + 2) * blk]
            o_ref[0, rows, g * LANES:(g + 1) * LANES] = jnp.where(dim < ATT_HEAD_DIM, a, b).T.astype(BF16)

    nblk = q_ref.shape[1] // blk
    rows_of = [slice(b * blk, (b + 1) * blk) for b in range(nblk)]
    k_blocks = [k_ref[0, r, :] for r in rows_of]
    v_blocks = [vt_ref[0, :, r] for r in rows_of]
    near = lambda blocks, b: (blocks[max(b - 1, 0)], blocks[b], blocks[min(b + 1, nblk - 1)])
    bias = [(ninf if b == 0 else bias_prev, ninf if b == nblk - 1 else bias_next) for b in range(nblk)]
    units = [(b, half) for b in range(nblk) for half in range(n_parts)]

    def stage_scores(u):
        b, half = u
        return scores(stack_heads(q_ref[0, rows_of[b], :], half), *near(k_blocks, b))

    def stage_softmax(u, s):
        b, half = u
        return softmax(s, *bias[b], sink[half])

    def stage_values(u, p):
        return weighted_values(p, *near(v_blocks, u[0]))

    def stage_emit(u, ot, p_sink):
        emit(rows_of[u[0]], u[1], ot, p_sink)

    n_units = len(units)
    s, p, ot = {}, {}, {}
    s[0] = stage_scores(units[0])
    for t in range(n_units):
        if t + 1 < n_units:
            s[t + 1] = stage_scores(units[t + 1])
        p[t] = stage_softmax(units[t], s.pop(t))
        if t >= 1:
            ot[t - 1] = stage_values(units[t - 1], p[t - 1][0])
        if t >= 2:
            stage_emit(units[t - 2], ot.pop(t - 2), p.pop(t - 2)[1])
    ot[n_units - 1] = stage_values(units[n_units - 1], p[n_units - 1][0])
    for t in (n_units - 2, n_units - 1):
        stage_emit(units[t], ot.pop(t), p.pop(t)[1])


def _attention(q, k, vt, kx, vxt, sink_row):
    bsz, s, _ = q.shape
    per_b = lambda a: pl.BlockSpec((1,) + a.shape[1:], lambda b: (b, 0, 0))
    return pl.pallas_call(
        _attn_kernel,
        grid=(bsz,),
        in_specs=[per_b(q), per_b(k), per_b(kx), per_b(vt), per_b(vxt),
                  pl.BlockSpec(sink_row.shape, lambda b: (0, 0))],
        out_specs=pl.BlockSpec((1, s, ATT_WIDTH), lambda b: (b, 0, 0)),
        out_shape=jax.ShapeDtypeStruct((bsz, s, ATT_WIDTH), BF16),
        compiler_params=pltpu.CompilerParams(dimension_semantics=("arbitrary",),
                                             vmem_limit_bytes=VMEM_LIMIT),
        name="attention",
    )(q, k, kx, vt, vxt, sink_row)


ML_STATE_ROWS = ML_V_DIM + SUBLANES


def _mlstm_kernel(qkx_ref, vtx_ref, ox_ref, gcx_ref, grx_ref, qkc_ref, vtc_ref, grc_ref,
                  gain_ref, out_ref, sin_ref, st_ref, ucol_ref, mrun_ref, wint_ref, floor_ref,
                  wkey_ref, decay_ref, *, ncx, ncc):
    lc = ML_CHUNK
    qkw = ML_QK_WIDTH
    sr = ML_STATE_ROWS
    ng = ML_GATES
    hg = ML_GATES // 2
    st_ref[...] = jnp.zeros(st_ref.shape, F32)

    rr = lax.broadcasted_iota(jnp.int32, (lc, lc), 0)
    cc = lax.broadcasted_iota(jnp.int32, (lc, lc), 1)
    tril = rr >= cc
    triu = rr <= cc
    tril_b = jnp.where(tril, 1.0, 0.0).astype(BF16)
    triu_b = jnp.where(triu, 1.0, 0.0).astype(BF16)
    head_of_lane = lax.broadcasted_iota(jnp.int32, (lc, qkw), 1) // ML_QK_DIM
    own_block = jnp.concatenate(
        [lax.broadcasted_iota(jnp.int32, (sr, qkw), 1) // ML_QK_DIM == h for h in range(ML_HEADS)], axis=0)

    pad_rows = jnp.zeros((SUBLANES - 1, lc), F32)

    def split(v):
        hi = v.astype(BF16)
        return hi, (v - hi.astype(F32)).astype(BF16)

    def gate_rows(g):
        n16 = g.shape[0]
        fwd_row = (lax.broadcasted_iota(jnp.int32, g.shape, 0) & (ng - 1)) < hg
        lane = lax.broadcasted_iota(jnp.int32, g.shape, 1)
        hi, lo = split(_log_sigmoid(g))
        cat = jnp.concatenate([hi, lo], axis=0)
        bu = _dot(cat, triu_b)
        bl = _dot(cat, tril_b)
        b = jnp.where(fwd_row, bu[:n16] + bu[n16:], bl[:n16] + bl[n16:])
        b = pltpu.roll(b, n16 - ML_HEADS, axis=0)
        u = g - b
        run_f = run_b = u
        k = 1
        while k < lc:
            run_f = jnp.maximum(run_f, jnp.where(lane >= k, pltpu.roll(run_f, k, axis=1), -jnp.inf))
            run_b = jnp.maximum(run_b, jnp.where(lane < lc - k, pltpu.roll(run_b, lc - k, axis=1), -jnp.inf))
            k *= 2
        run = jnp.where(fwd_row, run_f, run_b)

        def at_end(a):
            return jnp.where(fwd_row, jnp.broadcast_to(a[:, lc - 1:lc], a.shape),
                             jnp.broadcast_to(a[:, 0:1], a.shape))

        return u, b, run, at_end(b), at_end(run)

    def derived(u, b, run, b_tot, run_end, m_in):
        m_run = jnp.maximum(run, m_in)
        m_out = b_tot + jnp.maximum(run_end, m_in)
        return (m_run * LOG2_E, jnp.exp(m_in - m_run), jnp.exp(-(b + m_run)),
                jnp.exp(b_tot + u - m_out), jnp.exp(b_tot + m_in - m_out))

    gc_rows = gate_rows(grc_ref[0].reshape(ncc * ng, lc))
    gx_rows = gate_rows(grx_ref[0].reshape(ncx * ng, lc))

    def scan_m(rows, n, m_f, m_b):
        _, _, _, b_tot, run_end = rows
        part = lambda a, c, d: a[c * ng + d * hg:c * ng + (d + 1) * hg]
        ins_f, ins_b = [], [None] * n
        for c in range(n):
            ins_f.append(m_f)
            m_f = part(b_tot, c, 0) + jnp.maximum(part(run_end, c, 0), m_f)
        for c in reversed(range(n)):
            ins_b[c] = m_b
            m_b = part(b_tot, c, 1) + jnp.maximum(part(run_end, c, 1), m_b)
        return jnp.concatenate([x for c in range(n) for x in (ins_f[c], ins_b[c])], axis=0), m_f, m_b

    m0 = jnp.zeros((hg, lc), F32)
    m_in_c, m_f, m_b = scan_m(gc_rows, ncc, m0, m0)
    m_in_x, _, _ = scan_m(gx_rows, ncx, m_f, m_b)
    _, _, _, wkey_c, decay_c = derived(*gc_rows, m_in_c)
    for ref, val in zip((mrun_ref, wint_ref, floor_ref, wkey_ref, decay_ref), derived(*gx_rows, m_in_x)):
        ref[...] = val

    fwd_col = (lax.broadcasted_iota(jnp.int32, (lc, LANES), 1) & (ng - 1)) < hg

    def token_major_u(j):
        rows = pl.ds(pl.multiple_of(j * lc, lc), lc)
        gcol = gcx_ref[0, rows, :]
        hi, lo = split(_log_sigmoid(gcol))
        cat = jnp.concatenate([hi, lo], axis=1)
        bl = _dot(tril_b, cat)
        bu = _dot(triu_b, cat)
        b = jnp.where(fwd_col, bl[:, :LANES] + bl[:, LANES:], bu[:, :LANES] + bu[:, LANES:])
        ucol_ref[rows, :] = (gcol - pltpu.roll(b, LANES - ML_HEADS, axis=1)) * LOG2_E

    def advance(dirn, k4, vt, w_key, decay):
        st = st_ref[dirn]
        pieces, decays = [], []
        for h in range(ML_HEADS):
            c = hg * dirn + h
            pieces += [vt[h * ML_V_DIM:(h + 1) * ML_V_DIM].astype(F32) * w_key[c:c + 1], w_key[c:c + 1], pad_rows]
            decays.append(jnp.broadcast_to(jnp.concatenate([decay[c:c + 1]] * (qkw // lc), axis=1), (sr, qkw)))
        upd = _dot(jnp.concatenate(pieces, axis=0).astype(BF16), k4)
        st_ref[dirn] = jnp.concatenate(decays, axis=0) * st + jnp.where(own_block, upd, 0.0)

    for j in range(ncc):
        for dirn, cj in ((0, j), (1, ncc - 1 - j)):
            grows = slice(cj * ng, (cj + 1) * ng)
            advance(dirn, qkc_ref[0, cj * lc:(cj + 1) * lc, qkw:], vtc_ref[0, cj], wkey_c[grows], decay_c[grows])

    def scan_body(j, carry):
        for dirn, cj in ((0, j), (1, ncx - 1 - j)):
            rows = pl.ds(pl.multiple_of(cj * lc, lc), lc)
            grows = pl.ds(pl.multiple_of(cj * ng, ng), ng)
            sin_ref[dirn, cj] = st_ref[dirn].astype(BF16)
            advance(dirn, qkx_ref[0, rows, qkw:], vtx_ref[0, cj], wkey_ref[grows, :], decay_ref[grows, :])
        token_major_u(j)
        return carry

    lax.fori_loop(0, ncx, scan_body, 0, unroll=8)

    gain = gain_ref[...]

    def out_body(j, carry):
        rows = pl.ds(pl.multiple_of(j * lc, lc), lc)
        grows = pl.ds(pl.multiple_of(j * ng, ng), ng)
        q4 = qkx_ref[0, rows, :qkw]
        k4 = qkx_ref[0, rows, qkw:]
        vt = vtx_ref[0, j]
        u_col = ucol_ref[rows, :]
        m_run, w_int, floor = mrun_ref[grows, :], wint_ref[grows, :], floor_ref[grows, :]
        zero = jnp.zeros_like(q4)
        qs = jnp.concatenate([jnp.where(head_of_lane == h, q4, zero) for h in range(ML_HEADS)], axis=0)
        qk_t = _dot_nt(k4, qs)
        hsum = [None] * ML_HEADS
        for dirn in range(2):
            inter = _dot_nt(sin_ref[dirn, j], q4)
            valid = triu if dirn == 0 else tril
            for h in range(ML_HEADS):
                c = hg * dirn + h
                e = jnp.exp2(jnp.where(valid, u_col[:, c:c + 1] - m_run[c:c + 1], -jnp.inf))
                s_t = qk_t[:, h * lc:(h + 1) * lc] * e
                num = _dot(vt[h * ML_V_DIM:(h + 1) * ML_V_DIM], s_t.astype(BF16))
                num = num + w_int[c:c + 1] * inter[h * sr:h * sr + ML_V_DIM]
                den = (jnp.sum(s_t, axis=0, keepdims=True)
                       + w_int[c:c + 1] * inter[h * sr + ML_V_DIM:h * sr + ML_V_DIM + 1])
                hv = num * (1.0 / jnp.maximum(jnp.abs(den), floor[c:c + 1]))
                hsum[h] = hv if dirn == 0 else hsum[h] + hv
        for h in range(ML_HEADS):
            cols = slice(h * ML_V_DIM, (h + 1) * ML_V_DIM)
            hs = hsum[h]
            hn = hs * lax.rsqrt(jnp.mean(hs * hs, axis=0, keepdims=True) + EPS)
            out_ref[0, rows, cols] = (hn.T * gain[:, cols] * ox_ref[0, rows, cols].astype(F32)).astype(BF16)
        return carry

    lax.fori_loop(0, ncx, out_body, 0, unroll=8)


def _mlstm(qkx, vtx, ox, gcx, grx, qkc, vtc, grc, gain):
    bsz, s, _ = qkx.shape
    ncx, ncc = s // ML_CHUNK, qkc.shape[1] // ML_CHUNK
    per_b = lambda a: pl.BlockSpec((1,) + a.shape[1:], lambda b: (b,) + (0,) * (a.ndim - 1))
    ins = [qkx, vtx, ox, gcx, grx, qkc, vtc, grc]
    return pl.pallas_call(
        functools.partial(_mlstm_kernel, ncx=ncx, ncc=ncc),
        grid=(bsz,),
        in_specs=[per_b(a) for a in ins] + [pl.BlockSpec((1, ML_WIDTH), lambda b: (0, 0))],
        out_specs=pl.BlockSpec((1, s, ML_WIDTH), lambda b: (b, 0, 0)),
        out_shape=jax.ShapeDtypeStruct((bsz, s, ML_WIDTH), BF16),
        scratch_shapes=[pltpu.VMEM((2, ncx, ML_HEADS * ML_STATE_ROWS, ML_QK_WIDTH), BF16),
                        pltpu.VMEM((2, ML_HEADS * ML_STATE_ROWS, ML_QK_WIDTH), F32),
                        pltpu.VMEM((s, LANES), F32)]
                       + [pltpu.VMEM((ncx * ML_GATES, ML_CHUNK), F32)] * 5,
        compiler_params=pltpu.CompilerParams(dimension_semantics=("arbitrary",),
                                             vmem_limit_bytes=VMEM_LIMIT),
        name="mlstm",
    )(*ins, gain)


def _out_ffn_kernel(x_ref, att_ref, ml_ref, gtm_ref, shf_ref, scf_ref, gtf_ref,
                    gpm_ref, gpf_ref, gqf_ref, woa_ref, wom_ref, wfi_ref, wfo_ref, o_ref, *, hidden,
                    tiles_per_batch):
    r_mod = pl.ds((pl.program_id(0) // tiles_per_batch) % SUBLANES, 1)
    gtm, shf, scf, gtf = gtm_ref[r_mod, :], shf_ref[r_mod, :], scf_ref[r_mod, :], gtf_ref[r_mod, :]

    def group(g):
        r = slice(g * FFN_ROWS, (g + 1) * FFN_ROWS)
        mix = _dot(att_ref[r, :], woa_ref[...]) + _dot(ml_ref[r, :], wom_ref[...])
        yield
        x1 = x_ref[r, :] + gtm * _rms(mix, gpm_ref[...])
        h = (_rms(x1, gpf_ref[...]) * (1.0 + scf) + shf).astype(BF16)
        gu = _dot(h, wfi_ref[...])
        yield
        act = (_silu(gu[:, :hidden]) * gu[:, hidden:]).astype(BF16)
        fx = _dot(act, wfo_ref[...])
        yield
        o_ref[r, :] = x1 + gtf * _rms(fx, gqf_ref[...])

    done = object()
    active = [group(g) for g in range(x_ref.shape[0] // FFN_ROWS)]
    while active:
        active = [g for g in active if next(g, done) is not done]


def _out_ffn(x2, att2, ml2, mod, g_post_mix, g_pre_ffn, g_post_ffn, woa, wom, wfi, wfo, *, tiles_per_batch):
    t, d = x2.shape
    tm = FFN_TM
    hidden = wfo.shape[0]
    resident = lambda a: pl.BlockSpec(a.shape, lambda i: (0,) * a.ndim, pipeline_mode=pl.Buffered(1))
    mod_spec = lambda k: pl.BlockSpec((SUBLANES, d), lambda i: (i // tiles_per_batch // SUBLANES, k))
    row = pl.BlockSpec((1, d), lambda i: (0, 0))
    return pl.pallas_call(
        functools.partial(_out_ffn_kernel, hidden=hidden, tiles_per_batch=tiles_per_batch),
        grid=(t // tm,),
        in_specs=[pl.BlockSpec((tm, d), lambda i: (i, 0)),
                  pl.BlockSpec((tm, ATT_WIDTH), lambda i: (i, 0)),
                  pl.BlockSpec((tm, ML_WIDTH), lambda i: (i, 0)),
                  mod_spec(2), mod_spec(3), mod_spec(4), mod_spec(5), row, row, row,
                  resident(woa), resident(wom), resident(wfi), resident(wfo)],
        out_specs=pl.BlockSpec((tm, d), lambda i: (i, 0)),
        out_shape=jax.ShapeDtypeStruct((t, d), F32),
        compiler_params=pltpu.CompilerParams(dimension_semantics=("arbitrary",),
                                             vmem_limit_bytes=VMEM_LIMIT),
        name="out_ffn",
    )(x2, att2, ml2, mod, mod, mod, mod, g_post_mix, g_pre_ffn, g_post_ffn, woa, wom, wfi, wfo)


def _rope_tables(n_tokens):
    pos = jnp.arange(n_tokens)
    row = (pos // GRID_W).astype(F32)
    col = (pos % GRID_W).astype(F32)
    half = ATT_HEAD_DIM // 4
    inv_freq = jnp.power(ROPE_BASE, -jnp.arange(half, dtype=F32) / half)
    ang_r = row[:, None] * inv_freq
    ang_c = col[:, None] * inv_freq
    z = jnp.zeros_like(ang_r)
    reps = LANES // ATT_HEAD_DIM
    cos = jnp.tile(jnp.concatenate([jnp.cos(ang_r)] * 2 + [jnp.cos(ang_c)] * 2, axis=1), (1, reps))
    sin_lo = jnp.tile(jnp.concatenate([-jnp.sin(ang_r), z, -jnp.sin(ang_c), z], axis=1), (1, reps))
    sin_hi = jnp.tile(jnp.concatenate([z, jnp.sin(ang_r), z, jnp.sin(ang_c)], axis=1), (1, reps))
    return cos, sin_lo, sin_hi


def _permute_heads(w, axis):
    shape = w.shape
    grouped = shape[:axis] + (ATT_KV_HEADS, ATT_GROUP, ATT_HEAD_DIM) + shape[axis + 1:]
    return jnp.swapaxes(w.reshape(grouped), axis, axis + 1).reshape(shape)


def kernel(x, c, ctx, c_ctx, w_ada, b_ada, g_pre_mix, w_in, w_conv_qk, b_gates, attn_sink,
           g_mlstm_out, w_out, g_post_mix, g_pre_ffn, w_ffn_in, w_ffn_out, g_post_ffn):
    bsz, s, d = x.shape
    l = ctx.shape[1]
    assert w_ada.shape[0] == 1, "single-layer block"
    assert bsz < MOD_ROWS and s % INPROJ_TM == 0 and l % ML_CHUNK == 0 and s % FFN_TM == 0

    mod = _ada(c, c_ctx, w_ada[0], b_ada)

    w = w_in[0]
    o_q, o_k, o_v = 0, ATT_WIDTH, ATT_WIDTH + ATT_KV_WIDTH
    o_mq = o_v + ATT_KV_WIDTH
    o_mv = o_mq + 2 * ML_QK_WIDTH
    o_mo = o_mv + ML_WIDTH
    o_mg = o_mo + ML_WIDTH
    w_q = _permute_heads(w[:, o_q:o_k], 1) * (ATT_HEAD_DIM ** -0.5 * LOG2_E)
    w_g = jnp.pad(w[:, o_mg:], ((0, 0), (0, LANES - ML_GATES)))
    shared = [w[:, o_k:o_v], w_g, w[:, o_mq:o_mv]]
    w_lat = jnp.concatenate([w_q] + shared + [w[:, o_mo:o_mg]], axis=1).astype(BF16)
    wvt = jnp.concatenate([w[:, o_v:o_mq], w[:, o_mv:o_mo], w[:, o_mg:]], axis=1).T.astype(BF16)
    bg_row = jnp.pad(b_gates, ((0, 0), (0, LANES - ML_GATES)))
    bg_col = b_gates.reshape(ML_GATES, 1)
    wc = w_conv_qk[0]

    q, k, vt, qkx, vtx, ox, gcx, grx = _inproj(
        x, mod, None, g_pre_mix, w_lat, wvt, bg_row, bg_col, wc, _rope_tables(s),
        tm=INPROJ_TM, gm=INPROJ_ROWS, latent=True)
    kc, vct, qkc, vtc, _, grc = _inproj(
        ctx, mod, bsz, g_pre_mix, w_lat, wvt, bg_row, bg_col, wc, None,
        tm=l, gm=l, latent=False)

    sink_row = jnp.repeat(attn_sink[0][jnp.array(_HEAD_PERM)] * LOG2_E, ATT_BLOCK)[None, :]
    att = _attention(q, k, vt, kc, vct, sink_row)
    ml = _mlstm(qkx, vtx, ox, gcx, grx, qkc, vtc, grc, g_mlstm_out)

    wo = w_out[0]
    woa = _permute_heads(wo[:ATT_WIDTH], 0).astype(BF16)
    wom = wo[ATT_WIDTH:].astype(BF16)
    out = _out_ffn(x.reshape(bsz * s, d), att.reshape(bsz * s, ATT_WIDTH), ml.reshape(bsz * s, ML_WIDTH),
                   mod, g_post_mix, g_pre_ffn, g_post_ffn, woa, wom,
                   w_ffn_in[0].astype(BF16), w_ffn_out[0].astype(BF16), tiles_per_batch=s // FFN_TM)
    return out.reshape(bsz, s, d)
```

```python
import functools

import jax
import jax.numpy as jnp
from jax import lax
from jax.experimental import pallas as pl
from jax.experimental.pallas import tpu as pltpu

F32 = jnp.float32
BF16 = jnp.bfloat16

EPS = 1e-6
GRID_W = 64
ROPE_BASE = 10000.0
LOG2_E = 1.4426950408889634

ATT_HEADS = 8
ATT_KV_HEADS = 2
ATT_GROUP = ATT_HEADS // ATT_KV_HEADS
ATT_HEAD_DIM = 64
ROPE_PAIR = ATT_HEAD_DIM // 4
ATT_BLOCK = 128
ATT_WIDTH = ATT_HEADS * ATT_HEAD_DIM
ATT_KV_WIDTH = ATT_KV_HEADS * ATT_HEAD_DIM
ATT_UNIT_GROUPS = 2
ATT_ONES_ROWS = 16

ML_HEADS = 4
ML_V_DIM = 128
ML_QK_DIM = 64
ML_WIDTH = ML_HEADS * ML_V_DIM
ML_QK_WIDTH = ML_HEADS * ML_QK_DIM
ML_GATES = 4 * ML_HEADS
ML_CHUNK = 128

LANES = 128
SUBLANES = 8
VMEM_LIMIT = 56 * 1024 * 1024

INPROJ_TM = 1024
INPROJ_ROWS = 512
CTX_PER_STEP = 2
FFN_TM = 512
FFN_ROWS = 256
ADA_TN = 1536
MOD_ROWS = 16

_HEAD_PERM = tuple(h * ATT_GROUP + g for g in range(ATT_GROUP) for h in range(ATT_KV_HEADS))


def _silu(v):
    return v * jax.nn.sigmoid(v)


def _log_sigmoid(v):
    return jnp.minimum(v, 0.0) - jnp.log1p(jnp.exp(-jnp.abs(v)))


def _rms(v, g):
    return v * lax.rsqrt(jnp.mean(v * v, axis=-1, keepdims=True) + EPS) * g


def _dot(a, b):
    return jnp.dot(a, b, preferred_element_type=F32)


def _dot_nt(a, b):
    return lax.dot_general(a, b, (((1,), (1,)), ((), ())), preferred_element_type=F32)


def _ada_kernel(c_ref, cctx_ref, w_ref, b_ref, o_ref):
    pad = jnp.zeros((MOD_ROWS - c_ref.shape[0] - 1, c_ref.shape[1]), F32)
    a = _silu(jnp.concatenate([c_ref[...], cctx_ref[...], pad], axis=0))
    o_ref[...] = _dot(a.astype(BF16), w_ref[...].astype(BF16)) + b_ref[...]


def _ada(c, c_ctx, w, b):
    d, n = w.shape
    return pl.pallas_call(
        _ada_kernel,
        grid=(n // ADA_TN,),
        in_specs=[pl.BlockSpec(c.shape, lambda j: (0, 0)),
                  pl.BlockSpec((1, d), lambda j: (0, 0)),
                  pl.BlockSpec((d, ADA_TN), lambda j: (0, j)),
                  pl.BlockSpec((1, ADA_TN), lambda j: (0, j))],
        out_specs=pl.BlockSpec((MOD_ROWS, ADA_TN), lambda j: (0, j)),
        out_shape=jax.ShapeDtypeStruct((MOD_ROWS, n), F32),
        compiler_params=pltpu.CompilerParams(dimension_semantics=("arbitrary",),
                                             vmem_limit_bytes=VMEM_LIMIT),
        name="ada",
    )(c, c_ctx.reshape(1, d), w, b)


def _rope(v, cos, sin_lo, sin_hi):
    return (v * cos + pltpu.roll(v, LANES - ROPE_PAIR, axis=1) * sin_lo
            + pltpu.roll(v, ROPE_PAIR, axis=1) * sin_hi)


def _inproj_kernel(*refs, tm, gm, n_tiles, latent, mod_row):
    if latent:
        (x_ref, xp_ref, xn_ref, sh_ref, sc_ref, g_ref, w_ref, wvt_ref, bgr_ref, bgc_ref,
         wc_ref, cos_ref, sl_ref, shi_ref, gain_ref,
         q_ref, k_ref, vt_ref, qk_ref, mvt_ref, o_ref, gcol_ref, grow_ref) = refs
    else:
        (x_ref, xp_ref, xn_ref, sh_ref, sc_ref, g_ref, w_ref, wvt_ref, bgr_ref, bgc_ref,
         wc_ref,
         k_ref, vt_ref, qk_ref, mvt_ref, gcol_ref, grow_ref) = refs
    i = pl.program_id(0)
    n_groups = tm // gm
    r_mod = (pl.program_id(1) if mod_row is None else mod_row) % SUBLANES
    scale = g_ref[...] * (1.0 + sc_ref[pl.ds(r_mod, 1), :])
    shift = sh_ref[pl.ds(r_mod, 1), :]
    wc = wc_ref[...]
    row = lax.broadcasted_iota(jnp.int32, (gm, 1), 0)

    def project(lo, width, lhs):
        return _dot(lhs, w_ref[:, lo:lo + width])

    def group(r):
        rows = slice(r * gm, (r + 1) * gm)
        before = xp_ref[0] if r == 0 else x_ref[0, r * gm - SUBLANES:r * gm, :]
        after = xn_ref[0] if r == n_groups - 1 else x_ref[0, (r + 1) * gm:(r + 1) * gm + SUBLANES, :]
        keep_prev = (jnp.where(i == 0, 0.0, 1.0) if r == 0 else 1.0) if latent else 0.0
        keep_next = (jnp.where(i == n_tiles - 1, 0.0, 1.0) if r == n_groups - 1 else 1.0) if latent else 0.0
        xt = jnp.concatenate([x_ref[0, rows, :], before, after], axis=0)
        ms = jnp.mean(xt * xt, axis=-1, keepdims=True)
        hb = (xt * lax.rsqrt(ms + EPS) * scale + shift).astype(BF16)
        hm = hb[:gm]
        yield

        c = ATT_WIDTH
        if latent:
            r_q = project(0, ATT_WIDTH, hm)
        r_kg = project(c, ATT_KV_WIDTH + LANES, hm)
        c += ATT_KV_WIDTH + LANES
        yield

        if latent:
            cos, sl, shi = cos_ref[rows, :], sl_ref[rows, :], shi_ref[rows, :]
            for g in range(ATT_WIDTH // LANES):
                q_ref[0, rows, g * LANES:(g + 1) * LANES] = _rope(
                    r_q[:, g * LANES:(g + 1) * LANES], cos, sl, shi).astype(BF16)
        y = project(c, 2 * ML_QK_WIDTH, hb)
        c += 2 * ML_QK_WIDTH
        yield

        if latent:
            k_ref[0, rows, :] = _rope(r_kg[:, :LANES], cos, sl, shi).astype(BF16)
        else:
            k_ref[0, rows, :] = r_kg[:, :LANES].astype(BF16)
        gcol_ref[0, rows, :] = r_kg[:, LANES:] + bgr_ref[...]
        v_t = _dot_nt(wvt_ref[...], hm)
        yield

        ym = y[:gm]
        prev = jnp.where(row == 0, y[gm + SUBLANES - 1:gm + SUBLANES] * keep_prev,
                         pltpu.roll(ym, 1, axis=0))
        nxt = jnp.where(row == gm - 1, y[gm + SUBLANES:gm + SUBLANES + 1] * keep_next,
                        pltpu.roll(ym, gm - 1, axis=0))
        act = _silu(prev * wc[0:1] + ym * wc[1:2] + nxt * wc[2:3])
        qk_ref[0, rows, :ML_QK_WIDTH] = (act[:, :ML_QK_WIDTH] * (ML_QK_DIM ** -0.5)).astype(BF16)
        qk_ref[0, rows, ML_QK_WIDTH:] = act[:, ML_QK_WIDTH:].astype(BF16)
        if latent:
            r_o = project(c, ML_WIDTH, hm)
        yield

        g_t = v_t[ATT_KV_WIDTH + ML_WIDTH:] + bgc_ref[...]
        vt_ref[0, :, rows] = v_t[:ATT_KV_WIDTH].astype(BF16)
        for j in range(gm // ML_CHUNK):
            cols = slice(j * ML_CHUNK, (j + 1) * ML_CHUNK)
            mvt_ref[0, r * (gm // ML_CHUNK) + j] = v_t[ATT_KV_WIDTH:ATT_KV_WIDTH + ML_WIDTH, cols].astype(BF16)
            grow_ref[0, r * (gm // ML_CHUNK) + j] = g_t[:, cols]
        if latent:
            o_ref[0, rows, :] = (jax.nn.sigmoid(r_o) * gain_ref[...]).astype(BF16)

    done = object()
    waiting = [group(r) for r in range(n_groups)]
    active = []
    while waiting or active:
        if waiting:
            active.append(waiting.pop(0))
        active = [g for g in active if next(g, done) is not done]


def _inproj(x, mod, mod_row, g_pre, w_main, wvt, bg_row, bg_col, wc, rope_tabs, gain, *, tm, gm, latent):
    bsz, t, d = x.shape
    n_tiles = t // tm
    hb = tm // SUBLANES
    n_hblk = t // SUBLANES
    n = w_main.shape[1]
    mod_blk = (lambda b: b // SUBLANES) if mod_row is None else (lambda b: mod_row // SUBLANES)

    def const(shape):
        return pl.BlockSpec(shape, lambda i, b: (0,) * len(shape))

    in_specs = [
        pl.BlockSpec((1, tm, d), lambda i, b: (b, i, 0)),
        pl.BlockSpec((1, SUBLANES, d), lambda i, b: (b, jnp.maximum(i * hb - 1, 0), 0)),
        pl.BlockSpec((1, SUBLANES, d), lambda i, b: (b, jnp.minimum((i + 1) * hb, n_hblk - 1), 0)),
        pl.BlockSpec((SUBLANES, d), lambda i, b: (mod_blk(b), 0)),
        pl.BlockSpec((SUBLANES, d), lambda i, b: (mod_blk(b), 1)),
        const((1, d)), const((d, n)), const((ATT_KV_WIDTH + ML_WIDTH + ML_GATES, d)),
        const((1, LANES)), const((ML_GATES, 1)), const((3, 2 * ML_QK_WIDTH)),
    ]
    args = [x, x, x, mod, mod, g_pre, w_main, wvt, bg_row, bg_col, wc]
    tok = lambda width, dt: (pl.BlockSpec((1, tm, width), lambda i, b: (b, i, 0)),
                             jax.ShapeDtypeStruct((bsz, t, width), dt))
    chunked = lambda rows, dt: (pl.BlockSpec((1, tm // ML_CHUNK, rows, ML_CHUNK), lambda i, b: (b, i, 0, 0)),
                                jax.ShapeDtypeStruct((bsz, t // ML_CHUNK, rows, ML_CHUNK), dt))
    outs = []
    if latent:
        in_specs += [pl.BlockSpec((tm, LANES), lambda i, b: (i, 0))] * 3 + [const((1, ML_WIDTH))]
        args += list(rope_tabs) + [gain]
        outs.append(tok(ATT_WIDTH, BF16))
    outs.append(tok(ATT_KV_WIDTH, BF16))
    outs.append((pl.BlockSpec((1, LANES, tm), lambda i, b: (b, 0, i)),
                 jax.ShapeDtypeStruct((bsz, ATT_KV_WIDTH, t), BF16)))
    outs.append(tok(2 * ML_QK_WIDTH, BF16))
    outs.append(chunked(ML_WIDTH, BF16))
    if latent:
        outs.append(tok(ML_WIDTH, BF16))
    outs.append(tok(LANES, F32))
    outs.append(chunked(ML_GATES, F32))
    return pl.pallas_call(
        functools.partial(_inproj_kernel, tm=tm, gm=gm, n_tiles=n_tiles, latent=latent, mod_row=mod_row),
        grid=(n_tiles, bsz),
        in_specs=in_specs,
        out_specs=[o[0] for o in outs],
        out_shape=[o[1] for o in outs],
        compiler_params=pltpu.CompilerParams(dimension_semantics=("arbitrary", "arbitrary"),
                                             vmem_limit_bytes=VMEM_LIMIT),
        name="inproj_latent" if latent else "inproj_context",
    )(*args)


def _attn_kernel(q_ref, k_ref, kx_ref, vt_ref, vx_ref, sink_ref, o_ref):
    blk = ATT_BLOCK
    lane = lax.broadcasted_iota(jnp.int32, (blk, LANES), 1)
    zero = jnp.zeros((blk, LANES), BF16)
    half_groups = ATT_UNIT_GROUPS
    n_slots = ATT_KV_HEADS * half_groups
    n_parts = ATT_GROUP // half_groups
    sink = [sink_ref[:, part * n_slots * blk:(part + 1) * n_slots * blk] for part in range(n_parts)]

    def stack_heads(q, half):
        parts = []
        for g in range(half * half_groups, (half + 1) * half_groups):
            slab = q[:, g * LANES:(g + 1) * LANES]
            parts.append(jnp.where(lane < ATT_HEAD_DIM, slab, zero))
            parts.append(jnp.where(lane >= ATT_HEAD_DIM, slab, zero))
        return jnp.concatenate(parts, axis=0)

    key = lax.broadcasted_iota(jnp.int32, (blk, blk), 0)
    qry = lax.broadcasted_iota(jnp.int32, (blk, blk), 1)
    ninf = jnp.full((blk, blk), -jnp.inf, F32)
    bias_prev = jnp.where(key >= qry, 0.0, ninf)
    bias_next = jnp.where(key <= qry, 0.0, ninf)
    slots = lambda b: jnp.concatenate([b] * n_slots, axis=1)

    def scores(qs, k_prev, k_cur, k_next):
        return [_dot_nt(k_prev, qs), _dot_nt(k_cur, qs), _dot_nt(k_next, qs), _dot_nt(kx_ref[0], qs)]

    def softmax(s, b_prev, b_next, sink_h):
        s = jnp.concatenate([s[0] + slots(b_prev), s[1], s[2] + slots(b_next), s[3]], axis=0)
        m = jnp.maximum(sink_h, jnp.max(s, axis=0, keepdims=True))
        return jnp.exp2(s - m).astype(BF16), jnp.exp2(sink_h - m)

    n_keys = 3 * blk + kx_ref.shape[1]
    ones_rows = jnp.ones((ATT_ONES_ROWS, n_keys), BF16)

    def weighted_values(p, v_prev, v_cur, v_next):
        vt = jnp.concatenate([v_prev, v_cur, v_next, vx_ref[0]], axis=1)
        return _dot(jnp.concatenate([vt, ones_rows], axis=0), p)

    dim = lax.broadcasted_iota(jnp.int32, (LANES, blk), 0)

    def emit(rows, half, ot, p_sink):
        ot = ot[:LANES] * (1.0 / (ot[LANES:LANES + 1] + p_sink))
        for j in range(half_groups):
            g = half * half_groups + j
            a = ot[:, (2 * j) * blk:(2 * j + 1) * blk]
            b = ot[:, (2 * j + 1) * blk:(2 * j + 2) * blk]
            o_ref[0, rows, g * LANES:(g + 1) * LANES] = jnp.where(dim < ATT_HEAD_DIM, a, b).T.astype(BF16)

    nblk = q_ref.shape[1] // blk
    rows_of = [slice(b * blk, (b + 1) * blk) for b in range(nblk)]
    k_blocks = [k_ref[0, r, :] for r in rows_of]
    v_blocks = [vt_ref[0, :, r] for r in rows_of]
    near = lambda blocks, b: (blocks[max(b - 1, 0)], blocks[b], blocks[min(b + 1, nblk - 1)])
    bias = [(ninf if b == 0 else bias_prev, ninf if b == nblk - 1 else bias_next) for b in range(nblk)]
    units = [(b, half) for b in range(nblk) for half in range(n_parts)]

    def stage_scores(u):
        b, half = u
        return scores(stack_heads(q_ref[0, rows_of[b], :], half), *near(k_blocks, b))

    def stage_softmax(u, s):
        b, half = u
        return softmax(s, *bias[b], sink[half])

    def stage_values(u, p):
        return weighted_values(p, *near(v_blocks, u[0]))

    def stage_emit(u, ot, p_sink):
        emit(rows_of[u[0]], u[1], ot, p_sink)

    n_units = len(units)
    s, p, ot = {}, {}, {}
    s[0] = stage_scores(units[0])
    for t in range(n_units):
        if t + 1 < n_units:
            s[t + 1] = stage_scores(units[t + 1])
        p[t] = stage_softmax(units[t], s.pop(t))
        if t >= 1:
            ot[t - 1] = stage_values(units[t - 1], p[t - 1][0])
        if t >= 2:
            stage_emit(units[t - 2], ot.pop(t - 2), p.pop(t - 2)[1])
    ot[n_units - 1] = stage_values(units[n_units - 1], p[n_units - 1][0])
    for t in (n_units - 2, n_units - 1):
        stage_emit(units[t], ot.pop(t), p.pop(t)[1])


def _attention(q, k, vt, kx, vxt, sink_row):
    bsz, s, _ = q.shape
    l = kx.shape[1]
    per_b = lambda a: pl.BlockSpec((1,) + a.shape[1:], lambda b: (b, 0, 0))
    vx_spec = pl.BlockSpec((1, ATT_KV_WIDTH, l), lambda b: (b // CTX_PER_STEP, 0, b % CTX_PER_STEP))
    return pl.pallas_call(
        _attn_kernel,
        grid=(bsz,),
        in_specs=[per_b(q), per_b(k), per_b(kx), per_b(vt), vx_spec,
                  pl.BlockSpec(sink_row.shape, lambda b: (0, 0))],
        out_specs=pl.BlockSpec((1, s, ATT_WIDTH), lambda b: (b, 0, 0)),
        out_shape=jax.ShapeDtypeStruct((bsz, s, ATT_WIDTH), BF16),
        compiler_params=pltpu.CompilerParams(dimension_semantics=("arbitrary",),
                                             vmem_limit_bytes=VMEM_LIMIT),
        name="attention",
    )(q, k, kx, vt, vxt, sink_row)


BF16_SUBLANES = 16
ML_STATE_ROWS = ML_V_DIM + BF16_SUBLANES


def _mlstm_kernel(qkx_ref, vtx_ref, ox_ref, gcx_ref, grx_ref, qkc_ref, vtc_ref, grc_ref,
                  out_ref, sin_ref, st_ref, ucol_ref, mrun_ref, wint_ref, floor_ref,
                  wkey_ref, decay_ref, *, ncx, ncc):
    lc = ML_CHUNK
    qkw = ML_QK_WIDTH
    sr = ML_STATE_ROWS
    ng = ML_GATES
    hg = ML_GATES // 2
    st_ref[...] = jnp.zeros(st_ref.shape, F32)

    rr = lax.broadcasted_iota(jnp.int32, (lc, lc), 0)
    cc = lax.broadcasted_iota(jnp.int32, (lc, lc), 1)
    tril = rr >= cc
    triu = rr <= cc
    tril_b = jnp.where(tril, 1.0, 0.0).astype(BF16)
    triu_b = jnp.where(triu, 1.0, 0.0).astype(BF16)
    head_of_lane = lax.broadcasted_iota(jnp.int32, (lc, qkw), 1) // ML_QK_DIM
    head_of_state_lane = lax.broadcasted_iota(jnp.int32, (1, qkw), 1) // ML_QK_DIM
    pad_rows = jnp.zeros((BF16_SUBLANES - 1, lc), F32)

    def split(v):
        hi = v.astype(BF16)
        return hi, (v - hi.astype(F32)).astype(BF16)

    def gate_rows(g):
        n16 = g.shape[0]
        fwd_row = (lax.broadcasted_iota(jnp.int32, g.shape, 0) & (ng - 1)) < hg
        lane = lax.broadcasted_iota(jnp.int32, g.shape, 1)
        hi, lo = split(_log_sigmoid(g))
        cat = jnp.concatenate([hi, lo], axis=0)
        bu = _dot(cat, triu_b)
        bl = _dot(cat, tril_b)
        b = jnp.where(fwd_row, bu[:n16] + bu[n16:], bl[:n16] + bl[n16:])
        b = pltpu.roll(b, n16 - ML_HEADS, axis=0)
        u = g - b
        run_f = run_b = u
        k = 1
        while k < lc:
            run_f = jnp.maximum(run_f, jnp.where(lane >= k, pltpu.roll(run_f, k, axis=1), -jnp.inf))
            run_b = jnp.maximum(run_b, jnp.where(lane < lc - k, pltpu.roll(run_b, lc - k, axis=1), -jnp.inf))
            k *= 2
        run = jnp.where(fwd_row, run_f, run_b)

        def at_end(a):
            return jnp.where(fwd_row, jnp.broadcast_to(a[:, lc - 1:lc], a.shape),
                             jnp.broadcast_to(a[:, 0:1], a.shape))

        return u, b, run, at_end(b), at_end(run)

    def derived(u, b, run, b_tot, run_end, m_in):
        m_run = jnp.maximum(run, m_in)
        m_out = b_tot + jnp.maximum(run_end, m_in)
        return (m_run * LOG2_E, jnp.exp(m_in - m_run), jnp.exp(-(b + m_run)),
                jnp.exp(b_tot + u - m_out), jnp.exp(b_tot + m_in - m_out))

    gc_rows = gate_rows(grc_ref[0].reshape(ncc * ng, lc))
    gx_rows = gate_rows(grx_ref[0].reshape(ncx * ng, lc))

    def scan_m(rows, n, m_f, m_b):
        _, _, _, b_tot, run_end = rows
        part = lambda a, c, d: a[c * ng + d * hg:c * ng + (d + 1) * hg]
        ins_f, ins_b = [], [None] * n
        for c in range(n):
            ins_f.append(m_f)
            m_f = part(b_tot, c, 0) + jnp.maximum(part(run_end, c, 0), m_f)
        for c in reversed(range(n)):
            ins_b[c] = m_b
            m_b = part(b_tot, c, 1) + jnp.maximum(part(run_end, c, 1), m_b)
        return jnp.concatenate([x for c in range(n) for x in (ins_f[c], ins_b[c])], axis=0), m_f, m_b

    m0 = jnp.zeros((hg, lc), F32)
    m_in_c, m_f, m_b = scan_m(gc_rows, ncc, m0, m0)
    m_in_x, _, _ = scan_m(gx_rows, ncx, m_f, m_b)
    _, _, _, wkey_c, decay_c = derived(*gc_rows, m_in_c)
    for ref, val in zip((mrun_ref, wint_ref, floor_ref, wkey_ref, decay_ref), derived(*gx_rows, m_in_x)):
        ref[...] = val

    fwd_col = (lax.broadcasted_iota(jnp.int32, (lc, LANES), 1) & (ng - 1)) < hg

    def token_major_u(j):
        rows = pl.ds(pl.multiple_of(j * lc, lc), lc)
        gcol = gcx_ref[0, rows, :]
        hi, lo = split(_log_sigmoid(gcol))
        cat = jnp.concatenate([hi, lo], axis=1)
        bl = _dot(tril_b, cat)
        bu = _dot(triu_b, cat)
        b = jnp.where(fwd_col, bl[:, :LANES] + bl[:, LANES:], bu[:, :LANES] + bu[:, LANES:])
        ucol_ref[rows, :] = (gcol - pltpu.roll(b, LANES - ML_HEADS, axis=1)) * LOG2_E

    def advance(dirn, k4, vt, w_key, decay):
        st = st_ref[dirn]
        lhs, rhs, decay_row = [], [], None
        for h in range(ML_HEADS):
            c = hg * dirn + h
            lhs.append(jnp.concatenate([vt[h * ML_V_DIM:(h + 1) * ML_V_DIM].astype(F32) * w_key[c:c + 1],
                                        w_key[c:c + 1], pad_rows], axis=0).astype(BF16))
            rhs.append(jnp.where(head_of_lane == h, k4, jnp.zeros_like(k4)))
            d_h = jnp.concatenate([decay[c:c + 1]] * (qkw // lc), axis=1)
            decay_row = d_h if decay_row is None else jnp.where(head_of_state_lane == h, d_h, decay_row)
        upd = _dot(jnp.concatenate(lhs, axis=1), jnp.concatenate(rhs, axis=0))
        st_ref[dirn] = decay_row * st + upd

    for j in range(ncc):
        for dirn, cj in ((0, j), (1, ncc - 1 - j)):
            grows = slice(cj * ng, (cj + 1) * ng)
            advance(dirn, qkc_ref[0, cj * lc:(cj + 1) * lc, qkw:], vtc_ref[0, cj], wkey_c[grows], decay_c[grows])

    def scan_body(j, carry):
        for dirn, cj in ((0, j), (1, ncx - 1 - j)):
            rows = pl.ds(pl.multiple_of(cj * lc, lc), lc)
            grows = pl.ds(pl.multiple_of(cj * ng, ng), ng)
            sin_ref[dirn, cj] = st_ref[dirn].astype(BF16)
            advance(dirn, qkx_ref[0, rows, qkw:], vtx_ref[0, cj], wkey_ref[grows, :], decay_ref[grows, :])
        token_major_u(j)
        return carry

    lax.fori_loop(0, ncx, scan_body, 0, unroll=8)


    def out_body(j, carry):
        rows = pl.ds(pl.multiple_of(j * lc, lc), lc)
        grows = pl.ds(pl.multiple_of(j * ng, ng), ng)
        q4 = qkx_ref[0, rows, :qkw]
        k4 = qkx_ref[0, rows, qkw:]
        vt = vtx_ref[0, j]
        u_col = ucol_ref[rows, :]
        m_run, w_int, floor = mrun_ref[grows, :], wint_ref[grows, :], floor_ref[grows, :]
        zero = jnp.zeros_like(q4)
        qs = jnp.concatenate([jnp.where(head_of_lane == h, q4, zero) for h in range(ML_HEADS)], axis=0)
        both = _dot_nt(jnp.concatenate([k4, sin_ref[0, j], sin_ref[1, j]], axis=0), qs)
        qk_t = both[:lc]
        for h in range(ML_HEADS):
            cols_h = slice(h * lc, (h + 1) * lc)
            hs = None
            for dirn in range(2):
                c = hg * dirn + h
                inter = both[lc + dirn * sr:lc + (dirn + 1) * sr, cols_h]
                valid = triu if dirn == 0 else tril
                e = jnp.exp2(jnp.where(valid, u_col[:, c:c + 1] - m_run[c:c + 1], -jnp.inf))
                s_t = qk_t[:, cols_h] * e
                num = _dot(vt[h * ML_V_DIM:(h + 1) * ML_V_DIM], s_t.astype(BF16))
                num = num + w_int[c:c + 1] * inter[:ML_V_DIM]
                den = jnp.sum(s_t, axis=0, keepdims=True) + w_int[c:c + 1] * inter[ML_V_DIM:ML_V_DIM + 1]
                hv = num * (1.0 / jnp.maximum(jnp.abs(den), floor[c:c + 1]))
                hs = hv if hs is None else hs + hv
            cols = slice(h * ML_V_DIM, (h + 1) * ML_V_DIM)
            hn = hs * lax.rsqrt(jnp.mean(hs * hs, axis=0, keepdims=True) + EPS)
            out_ref[0, rows, cols] = (hn.T * ox_ref[0, rows, cols].astype(F32)).astype(BF16)
        return carry

    lax.fori_loop(0, ncx, out_body, 0, unroll=8)


def _mlstm(qkx, vtx, ox, gcx, grx, qkc, vtc, grc):
    bsz, s, _ = qkx.shape
    ncx, ncc = s // ML_CHUNK, qkc.shape[1] // ML_CHUNK
    per_b = lambda a: pl.BlockSpec((1,) + a.shape[1:], lambda b: (b,) + (0,) * (a.ndim - 1))
    ins = [qkx, vtx, ox, gcx, grx, qkc, vtc, grc]
    return pl.pallas_call(
        functools.partial(_mlstm_kernel, ncx=ncx, ncc=ncc),
        grid=(bsz,),
        in_specs=[per_b(a) for a in ins],
        out_specs=pl.BlockSpec((1, s, ML_WIDTH), lambda b: (b, 0, 0)),
        out_shape=jax.ShapeDtypeStruct((bsz, s, ML_WIDTH), BF16),
        scratch_shapes=[pltpu.VMEM((2, ncx, ML_STATE_ROWS, ML_QK_WIDTH), BF16),
                        pltpu.VMEM((2, ML_STATE_ROWS, ML_QK_WIDTH), F32),
                        pltpu.VMEM((s, LANES), F32)]
                       + [pltpu.VMEM((ncx * ML_GATES, ML_CHUNK), F32)] * 5,
        compiler_params=pltpu.CompilerParams(dimension_semantics=("arbitrary",),
                                             vmem_limit_bytes=VMEM_LIMIT),
        name="mlstm",
    )(*ins)


def _out_ffn_kernel(x_ref, att_ref, ml_ref, gtm_ref, shf_ref, scf_ref, gtf_ref,
                    gpm_ref, gpf_ref, gqf_ref, woa_ref, wom_ref, wfi_ref, wfo_ref, o_ref, *, hidden,
                    tiles_per_batch):
    r_mod = pl.ds((pl.program_id(0) // tiles_per_batch) % SUBLANES, 1)
    gtm, shf, scf, gtf = gtm_ref[r_mod, :], shf_ref[r_mod, :], scf_ref[r_mod, :], gtf_ref[r_mod, :]

    def group(g):
        r = slice(g * FFN_ROWS, (g + 1) * FFN_ROWS)
        mix = _dot(att_ref[r, :], woa_ref[...]) + _dot(ml_ref[r, :], wom_ref[...])
        yield
        x1 = x_ref[r, :] + gtm * _rms(mix, gpm_ref[...])
        h = (_rms(x1, gpf_ref[...]) * (1.0 + scf) + shf).astype(BF16)
        gu = _dot(h, wfi_ref[...])
        yield
        act = (_silu(gu[:, :hidden]) * gu[:, hidden:]).astype(BF16)
        fx = _dot(act, wfo_ref[...])
        yield
        o_ref[r, :] = x1 + gtf * _rms(fx, gqf_ref[...])

    done = object()
    active = [group(g) for g in range(x_ref.shape[0] // FFN_ROWS)]
    while active:
        active = [g for g in active if next(g, done) is not done]


def _out_ffn(x2, att2, ml2, mod, g_post_mix, g_pre_ffn, g_post_ffn, woa, wom, wfi, wfo, *, tiles_per_batch):
    t, d = x2.shape
    tm = FFN_TM
    hidden = wfo.shape[0]
    resident = lambda a: pl.BlockSpec(a.shape, lambda i: (0,) * a.ndim, pipeline_mode=pl.Buffered(1))
    mod_spec = lambda k: pl.BlockSpec((SUBLANES, d), lambda i: (i // tiles_per_batch // SUBLANES, k))
    row = pl.BlockSpec((1, d), lambda i: (0, 0))
    return pl.pallas_call(
        functools.partial(_out_ffn_kernel, hidden=hidden, tiles_per_batch=tiles_per_batch),
        grid=(t // tm,),
        in_specs=[pl.BlockSpec((tm, d), lambda i: (i, 0)),
                  pl.BlockSpec((tm, ATT_WIDTH), lambda i: (i, 0)),
                  pl.BlockSpec((tm, ML_WIDTH), lambda i: (i, 0)),
                  mod_spec(2), mod_spec(3), mod_spec(4), mod_spec(5), row, row, row,
                  resident(woa), resident(wom), resident(wfi), resident(wfo)],
        out_specs=pl.BlockSpec((tm, d), lambda i: (i, 0)),
        out_shape=jax.ShapeDtypeStruct((t, d), F32),
        compiler_params=pltpu.CompilerParams(dimension_semantics=("arbitrary",),
                                             vmem_limit_bytes=VMEM_LIMIT),
        name="out_ffn",
    )(x2, att2, ml2, mod, mod, mod, mod, g_post_mix, g_pre_ffn, g_post_ffn, woa, wom, wfi, wfo)


def _rope_tables(n_tokens):
    pos = jnp.arange(n_tokens)
    row = (pos // GRID_W).astype(F32)
    col = (pos % GRID_W).astype(F32)
    inv_freq = jnp.power(ROPE_BASE, -jnp.arange(ROPE_PAIR, dtype=F32) / ROPE_PAIR)
    ang_r = row[:, None] * inv_freq
    ang_c = col[:, None] * inv_freq
    cos_r, sin_r, cos_c, sin_c = jnp.cos(ang_r), jnp.sin(ang_r), jnp.cos(ang_c), jnp.sin(ang_c)
    z = jnp.zeros_like(ang_r)
    reps = LANES // ATT_HEAD_DIM
    cos = jnp.concatenate([cos_r, cos_r, cos_c, cos_c] * reps, axis=1)
    sin_lo = jnp.concatenate([-sin_r, z, -sin_c, z] * reps, axis=1)
    sin_hi = jnp.concatenate([z, sin_r, z, sin_c] * reps, axis=1)
    return cos, sin_lo, sin_hi


def _permute_heads(w, axis):
    shape = w.shape
    grouped = shape[:axis] + (ATT_KV_HEADS, ATT_GROUP, ATT_HEAD_DIM) + shape[axis + 1:]
    return jnp.swapaxes(w.reshape(grouped), axis, axis + 1).reshape(shape)


def kernel(x, c, ctx, c_ctx, w_ada, b_ada, g_pre_mix, w_in, w_conv_qk, b_gates, attn_sink,
           g_mlstm_out, w_out, g_post_mix, g_pre_ffn, w_ffn_in, w_ffn_out, g_post_ffn):
    bsz, s, d = x.shape
    l = ctx.shape[1]
    assert w_ada.shape[0] == 1, "single-layer block"
    assert bsz < MOD_ROWS and s % INPROJ_TM == 0 and l % ML_CHUNK == 0 and s % FFN_TM == 0
    assert bsz % CTX_PER_STEP == 0 and l % LANES == 0

    mod = _ada(c, c_ctx, w_ada[0], b_ada)

    w = w_in[0]
    o_q, o_k, o_v = 0, ATT_WIDTH, ATT_WIDTH + ATT_KV_WIDTH
    o_mq = o_v + ATT_KV_WIDTH
    o_mv = o_mq + 2 * ML_QK_WIDTH
    o_mo = o_mv + ML_WIDTH
    o_mg = o_mo + ML_WIDTH
    w_q = _permute_heads(w[:, o_q:o_k], 1) * (ATT_HEAD_DIM ** -0.5 * LOG2_E)
    w_g = jnp.pad(w[:, o_mg:], ((0, 0), (0, LANES - ML_GATES)))
    shared = [w[:, o_k:o_v], w_g, w[:, o_mq:o_mv]]
    w_lat = jnp.concatenate([w_q] + shared + [w[:, o_mo:o_mg]], axis=1).astype(BF16)
    wvt = jnp.concatenate([w[:, o_v:o_mq], w[:, o_mv:o_mo], w[:, o_mg:]], axis=1).T.astype(BF16)
    bg_row = jnp.pad(b_gates, ((0, 0), (0, LANES - ML_GATES)))
    bg_col = b_gates.reshape(ML_GATES, 1)
    wc = w_conv_qk[0]

    q, k, vt, qkx, vtx, ox, gcx, grx = _inproj(
        x, mod, None, g_pre_mix, w_lat, wvt, bg_row, bg_col, wc, _rope_tables(s), g_mlstm_out,
        tm=INPROJ_TM, gm=INPROJ_ROWS, latent=True)
    nctx = bsz // CTX_PER_STEP
    kc, vct, qkc, vtc, _, grc = _inproj(
        ctx.reshape(nctx, CTX_PER_STEP * l, d), mod, bsz, g_pre_mix, w_lat, wvt, bg_row, bg_col, wc, None, None,
        tm=CTX_PER_STEP * l, gm=l, latent=False)
    kc = kc.reshape(bsz, l, ATT_KV_WIDTH)
    qkc = qkc.reshape(bsz, l, 2 * ML_QK_WIDTH)
    vtc = vtc.reshape((bsz, l // ML_CHUNK) + vtc.shape[2:])
    grc = grc.reshape((bsz, l // ML_CHUNK) + grc.shape[2:])

    sink_row = jnp.repeat(attn_sink[0][jnp.array(_HEAD_PERM)] * LOG2_E, ATT_BLOCK)[None, :]
    att = _attention(q, k, vt, kc, vct, sink_row)
    ml = _mlstm(qkx, vtx, ox, gcx, grx, qkc, vtc, grc)

    wo = w_out[0]
    woa = _permute_heads(wo[:ATT_WIDTH], 0).astype(BF16)
    wom = wo[ATT_WIDTH:].astype(BF16)
    out = _out_ffn(x.reshape(bsz * s, d), att.reshape(bsz * s, ATT_WIDTH), ml.reshape(bsz * s, ML_WIDTH),
                   mod, g_post_mix, g_pre_ffn, g_post_ffn, woa, wom,
                   w_ffn_in[0].astype(BF16), w_ffn_out[0].astype(BF16), tiles_per_batch=s // FFN_TM)
    return out.reshape(bsz, s, d)
```

```python
import functools

import jax
import jax.numpy as jnp
from jax import lax
from jax.experimental import pallas as pl
from jax.experimental.pallas import tpu as pltpu

F32 = jnp.float32
BF16 = jnp.bfloat16

EPS = 1e-6
GRID_W = 64
ROPE_BASE = 10000.0
LOG2_E = 1.4426950408889634

ATT_HEADS = 8
ATT_KV_HEADS = 2
ATT_GROUP = ATT_HEADS // ATT_KV_HEADS
ATT_HEAD_DIM = 64
ROPE_PAIR = ATT_HEAD_DIM // 4
ATT_BLOCK = 128
ATT_WIDTH = ATT_HEADS * ATT_HEAD_DIM
ATT_KV_WIDTH = ATT_KV_HEADS * ATT_HEAD_DIM
ATT_UNIT_GROUPS = 2
ATT_ONES_ROWS = 16

ML_HEADS = 4
ML_V_DIM = 128
ML_QK_DIM = 64
ML_WIDTH = ML_HEADS * ML_V_DIM
ML_QK_WIDTH = ML_HEADS * ML_QK_DIM
ML_GATES = 4 * ML_HEADS
ML_CHUNK = 128

LANES = 128
SUBLANES = 8
VMEM_LIMIT = 56 * 1024 * 1024

INPROJ_TM = 1024
INPROJ_ROWS = 512
CTX_PER_STEP = 2
FFN_TM = 512
FFN_ROWS = 256
ADA_TN = 1536
MOD_ROWS = 16

_HEAD_PERM = tuple(h * ATT_GROUP + g for g in range(ATT_GROUP) for h in range(ATT_KV_HEADS))


def _silu(v):
    return v * jax.nn.sigmoid(v)


def _log_sigmoid(v):
    return jnp.minimum(v, 0.0) - jnp.log1p(jnp.exp(-jnp.abs(v)))


def _rms(v, g):
    return v * lax.rsqrt(jnp.mean(v * v, axis=-1, keepdims=True) + EPS) * g


def _dot(a, b):
    return jnp.dot(a, b, preferred_element_type=F32)


def _dot_nt(a, b):
    return lax.dot_general(a, b, (((1,), (1,)), ((), ())), preferred_element_type=F32)


def _ada_kernel(c_ref, cctx_ref, w_ref, b_ref, o_ref):
    pad = jnp.zeros((MOD_ROWS - c_ref.shape[0] - 1, c_ref.shape[1]), F32)
    a = _silu(jnp.concatenate([c_ref[...], cctx_ref[...], pad], axis=0))
    o_ref[...] = _dot(a.astype(BF16), w_ref[...].astype(BF16)) + b_ref[...]


def _ada(c, c_ctx, w, b):
    d, n = w.shape
    return pl.pallas_call(
        _ada_kernel,
        grid=(n // ADA_TN,),
        in_specs=[pl.BlockSpec(c.shape, lambda j: (0, 0)),
                  pl.BlockSpec((1, d), lambda j: (0, 0)),
                  pl.BlockSpec((d, ADA_TN), lambda j: (0, j)),
                  pl.BlockSpec((1, ADA_TN), lambda j: (0, j))],
        out_specs=pl.BlockSpec((MOD_ROWS, ADA_TN), lambda j: (0, j)),
        out_shape=jax.ShapeDtypeStruct((MOD_ROWS, n), F32),
        compiler_params=pltpu.CompilerParams(dimension_semantics=("arbitrary",),
                                             vmem_limit_bytes=VMEM_LIMIT),
        name="ada",
    )(c, c_ctx.reshape(1, d), w, b)


def _rope(v, cos, sin_lo, sin_hi):
    return (v * cos + pltpu.roll(v, LANES - ROPE_PAIR, axis=1) * sin_lo
            + pltpu.roll(v, ROPE_PAIR, axis=1) * sin_hi)


def _inproj_kernel(*refs, tm, gm, n_tiles, latent, mod_row):
    if latent:
        (x_ref, xp_ref, xn_ref, sh_ref, sc_ref, g_ref, w_ref, wvt_ref, bgr_ref, bgc_ref,
         wc_ref, cos_ref, sl_ref, shi_ref, gain_ref,
         q_ref, k_ref, vt_ref, qk_ref, mvt_ref, o_ref, gcol_ref, grow_ref) = refs
    else:
        (x_ref, xp_ref, xn_ref, sh_ref, sc_ref, g_ref, w_ref, wvt_ref, bgr_ref, bgc_ref,
         wc_ref,
         k_ref, vt_ref, qk_ref, mvt_ref, gcol_ref, grow_ref) = refs
    i = pl.program_id(0)
    n_groups = tm // gm
    r_mod = (pl.program_id(1) if mod_row is None else mod_row) % SUBLANES
    scale = g_ref[...] * (1.0 + sc_ref[pl.ds(r_mod, 1), :])
    shift = sh_ref[pl.ds(r_mod, 1), :]
    wc = wc_ref[...]
    row = lax.broadcasted_iota(jnp.int32, (gm, 1), 0)

    def project(lo, width, lhs):
        return _dot(lhs, w_ref[:, lo:lo + width])

    def group(r):
        rows = slice(r * gm, (r + 1) * gm)
        before = xp_ref[0] if r == 0 else x_ref[0, r * gm - SUBLANES:r * gm, :]
        after = xn_ref[0] if r == n_groups - 1 else x_ref[0, (r + 1) * gm:(r + 1) * gm + SUBLANES, :]
        keep_prev = (jnp.where(i == 0, 0.0, 1.0) if r == 0 else 1.0) if latent else 0.0
        keep_next = (jnp.where(i == n_tiles - 1, 0.0, 1.0) if r == n_groups - 1 else 1.0) if latent else 0.0
        xt = jnp.concatenate([x_ref[0, rows, :], before, after], axis=0)
        ms = jnp.mean(xt * xt, axis=-1, keepdims=True)
        hb = (xt * lax.rsqrt(ms + EPS) * scale + shift).astype(BF16)
        hm = hb[:gm]
        yield

        c = ATT_WIDTH
        if latent:
            r_q = project(0, ATT_WIDTH, hm)
        r_kg = project(c, ATT_KV_WIDTH + LANES, hm)
        c += ATT_KV_WIDTH + LANES
        yield

        if latent:
            cos, sl, shi = cos_ref[rows, :], sl_ref[rows, :], shi_ref[rows, :]
            for g in range(ATT_WIDTH // LANES):
                q_ref[0, rows, g * LANES:(g + 1) * LANES] = _rope(
                    r_q[:, g * LANES:(g + 1) * LANES], cos, sl, shi).astype(BF16)
        y = project(c, 2 * ML_QK_WIDTH, hb)
        c += 2 * ML_QK_WIDTH
        yield

        if latent:
            k_ref[0, rows, :] = _rope(r_kg[:, :LANES], cos, sl, shi).astype(BF16)
        else:
            k_ref[0, rows, :] = r_kg[:, :LANES].astype(BF16)
        gcol_ref[0, rows, :] = r_kg[:, LANES:] + bgr_ref[...]
        v_t = _dot_nt(wvt_ref[...], hm)
        yield

        ym = y[:gm]
        prev = jnp.where(row == 0, y[gm + SUBLANES - 1:gm + SUBLANES] * keep_prev,
                         pltpu.roll(ym, 1, axis=0))
        nxt = jnp.where(row == gm - 1, y[gm + SUBLANES:gm + SUBLANES + 1] * keep_next,
                        pltpu.roll(ym, gm - 1, axis=0))
        act = _silu(prev * wc[0:1] + ym * wc[1:2] + nxt * wc[2:3])
        qk_ref[0, rows, :ML_QK_WIDTH] = (act[:, :ML_QK_WIDTH] * (ML_QK_DIM ** -0.5)).astype(BF16)
        qk_ref[0, rows, ML_QK_WIDTH:] = act[:, ML_QK_WIDTH:].astype(BF16)
        if latent:
            r_o = project(c, ML_WIDTH, hm)
        yield

        g_t = v_t[ATT_KV_WIDTH + ML_WIDTH:] + bgc_ref[...]
        vt_ref[0, :, rows] = v_t[:ATT_KV_WIDTH].astype(BF16)
        for j in range(gm // ML_CHUNK):
            cols = slice(j * ML_CHUNK, (j + 1) * ML_CHUNK)
            mvt_ref[0, r * (gm // ML_CHUNK) + j] = v_t[ATT_KV_WIDTH:ATT_KV_WIDTH + ML_WIDTH, cols].astype(BF16)
            grow_ref[0, r * (gm // ML_CHUNK) + j] = g_t[:, cols]
        if latent:
            o_ref[0, rows, :] = (jax.nn.sigmoid(r_o) * gain_ref[...]).astype(BF16)

    done = object()
    waiting = [group(r) for r in range(n_groups)]
    active = []
    while waiting or active:
        if waiting:
            active.append(waiting.pop(0))
        active = [g for g in active if next(g, done) is not done]


def _inproj(x, mod, mod_row, g_pre, w_main, wvt, bg_row, bg_col, wc, rope_tabs, gain, *, tm, gm, latent):
    bsz, t, d = x.shape
    n_tiles = t // tm
    hb = tm // SUBLANES
    n_hblk = t // SUBLANES
    n = w_main.shape[1]
    mod_blk = (lambda b: b // SUBLANES) if mod_row is None else (lambda b: mod_row // SUBLANES)

    def const(shape):
        return pl.BlockSpec(shape, lambda i, b: (0,) * len(shape))

    in_specs = [
        pl.BlockSpec((1, tm, d), lambda i, b: (b, i, 0)),
        pl.BlockSpec((1, SUBLANES, d), lambda i, b: (b, jnp.maximum(i * hb - 1, 0), 0)),
        pl.BlockSpec((1, SUBLANES, d), lambda i, b: (b, jnp.minimum((i + 1) * hb, n_hblk - 1), 0)),
        pl.BlockSpec((SUBLANES, d), lambda i, b: (mod_blk(b), 0)),
        pl.BlockSpec((SUBLANES, d), lambda i, b: (mod_blk(b), 1)),
        const((1, d)), const((d, n)), const((ATT_KV_WIDTH + ML_WIDTH + ML_GATES, d)),
        const((1, LANES)), const((ML_GATES, 1)), const((3, 2 * ML_QK_WIDTH)),
    ]
    args = [x, x, x, mod, mod, g_pre, w_main, wvt, bg_row, bg_col, wc]
    tok = lambda width, dt: (pl.BlockSpec((1, tm, width), lambda i, b: (b, i, 0)),
                             jax.ShapeDtypeStruct((bsz, t, width), dt))
    chunked = lambda rows, dt: (pl.BlockSpec((1, tm // ML_CHUNK, rows, ML_CHUNK), lambda i, b: (b, i, 0, 0)),
                                jax.ShapeDtypeStruct((bsz, t // ML_CHUNK, rows, ML_CHUNK), dt))
    outs = []
    if latent:
        in_specs += [pl.BlockSpec((tm, LANES), lambda i, b: (i, 0))] * 3 + [const((1, ML_WIDTH))]
        args += list(rope_tabs) + [gain]
        outs.append(tok(ATT_WIDTH, BF16))
    outs.append(tok(ATT_KV_WIDTH, BF16))
    outs.append((pl.BlockSpec((1, LANES, tm), lambda i, b: (b, 0, i)),
                 jax.ShapeDtypeStruct((bsz, ATT_KV_WIDTH, t), BF16)))
    outs.append(tok(2 * ML_QK_WIDTH, BF16))
    outs.append(chunked(ML_WIDTH, BF16))
    if latent:
        outs.append(tok(ML_WIDTH, BF16))
    outs.append(tok(LANES, F32))
    outs.append(chunked(ML_GATES, F32))
    return pl.pallas_call(
        functools.partial(_inproj_kernel, tm=tm, gm=gm, n_tiles=n_tiles, latent=latent, mod_row=mod_row),
        grid=(n_tiles, bsz),
        in_specs=in_specs,
        out_specs=[o[0] for o in outs],
        out_shape=[o[1] for o in outs],
        compiler_params=pltpu.CompilerParams(dimension_semantics=("arbitrary", "arbitrary"),
                                             vmem_limit_bytes=VMEM_LIMIT),
        name="inproj_latent" if latent else "inproj_context",
    )(*args)


def _attn_kernel(q_ref, k_ref, kx_ref, vt_ref, vx_ref, sink_ref, o_ref):
    blk = ATT_BLOCK
    lane = lax.broadcasted_iota(jnp.int32, (blk, LANES), 1)
    zero = jnp.zeros((blk, LANES), BF16)
    half_groups = ATT_UNIT_GROUPS
    n_slots = ATT_KV_HEADS * half_groups
    n_parts = ATT_GROUP // half_groups
    sink = [sink_ref[:, part * n_slots * blk:(part + 1) * n_slots * blk] for part in range(n_parts)]

    def stack_heads(q, half):
        parts = []
        for g in range(half * half_groups, (half + 1) * half_groups):
            slab = q[:, g * LANES:(g + 1) * LANES]
            parts.append(jnp.where(lane < ATT_HEAD_DIM, slab, zero))
            parts.append(jnp.where(lane >= ATT_HEAD_DIM, slab, zero))
        return jnp.concatenate(parts, axis=0)

    key = lax.broadcasted_iota(jnp.int32, (blk, blk), 0)
    qry = lax.broadcasted_iota(jnp.int32, (blk, blk), 1)
    ninf = jnp.full((blk, blk), -jnp.inf, F32)
    bias_prev = jnp.where(key >= qry, 0.0, ninf)
    bias_next = jnp.where(key <= qry, 0.0, ninf)
    slots = lambda b: jnp.concatenate([b] * n_slots, axis=1)

    def scores(qs, k_prev, k_cur, k_next):
        return [_dot_nt(k_prev, qs), _dot_nt(k_cur, qs), _dot_nt(k_next, qs), _dot_nt(kx_ref[0], qs)]

    def softmax(s, b_prev, b_next, sink_h):
        s = jnp.concatenate([s[0] + slots(b_prev), s[1], s[2] + slots(b_next), s[3]], axis=0)
        m = jnp.maximum(sink_h, jnp.max(s, axis=0, keepdims=True))
        return jnp.exp2(s - m).astype(BF16), jnp.exp2(sink_h - m)

    n_keys = 3 * blk + kx_ref.shape[1]
    ones_rows = jnp.ones((ATT_ONES_ROWS, n_keys), BF16)

    def weighted_values(p, v_prev, v_cur, v_next):
        vt = jnp.concatenate([v_prev, v_cur, v_next, vx_ref[0]], axis=1)
        return _dot(jnp.concatenate([vt, ones_rows], axis=0), p)

    dim = lax.broadcasted_iota(jnp.int32, (LANES, blk), 0)

    def emit(rows, half, ot, p_sink):
        ot = ot[:LANES] * (1.0 / (ot[LANES:LANES + 1] + p_sink))
        for j in range(half_groups):
            g = half * half_groups + j
            a = ot[:, (2 * j) * blk:(2 * j + 1) * blk]
            b = ot[:, (2 * j + 1) * blk:(2 * j + 2) * blk]
            o_ref[0, rows, g * LANES:(g + 1) * LANES] = jnp.where(dim < ATT_HEAD_DIM, a, b).T.astype(BF16)

    nblk = q_ref.shape[1] // blk
    rows_of = [slice(b * blk, (b + 1) * blk) for b in range(nblk)]
    k_blocks = [k_ref[0, r, :] for r in rows_of]
    v_blocks = [vt_ref[0, :, r] for r in rows_of]
    near = lambda blocks, b: (blocks[max(b - 1, 0)], blocks[b], blocks[min(b + 1, nblk - 1)])
    bias = [(ninf if b == 0 else bias_prev, ninf if b == nblk - 1 else bias_next) for b in range(nblk)]
    units = [(b, half) for b in range(nblk) for half in range(n_parts)]

    def stage_scores(u):
        b, half = u
        return scores(stack_heads(q_ref[0, rows_of[b], :], half), *near(k_blocks, b))

    def stage_softmax(u, s):
        b, half = u
        return softmax(s, *bias[b], sink[half])

    def stage_values(u, p):
        return weighted_values(p, *near(v_blocks, u[0]))

    def stage_emit(u, ot, p_sink):
        emit(rows_of[u[0]], u[1], ot, p_sink)

    n_units = len(units)
    s, p, ot = {}, {}, {}
    s[0] = stage_scores(units[0])
    for t in range(n_units):
        if t + 1 < n_units:
            s[t + 1] = stage_scores(units[t + 1])
        p[t] = stage_softmax(units[t], s.pop(t))
        if t >= 1:
            ot[t - 1] = stage_values(units[t - 1], p[t - 1][0])
        if t >= 2:
            stage_emit(units[t - 2], ot.pop(t - 2), p.pop(t - 2)[1])
    ot[n_units - 1] = stage_values(units[n_units - 1], p[n_units - 1][0])
    for t in (n_units - 2, n_units - 1):
        stage_emit(units[t], ot.pop(t), p.pop(t)[1])


def _attention(q, k, vt, kx, vxt, sink_row):
    bsz, s, _ = q.shape
    l = kx.shape[1]
    per_b = lambda a: pl.BlockSpec((1,) + a.shape[1:], lambda b: (b, 0, 0))
    vx_spec = pl.BlockSpec((1, ATT_KV_WIDTH, l), lambda b: (b // CTX_PER_STEP, 0, b % CTX_PER_STEP))
    return pl.pallas_call(
        _attn_kernel,
        grid=(bsz,),
        in_specs=[per_b(q), per_b(k), per_b(kx), per_b(vt), vx_spec,
                  pl.BlockSpec(sink_row.shape, lambda b: (0, 0))],
        out_specs=pl.BlockSpec((1, s, ATT_WIDTH), lambda b: (b, 0, 0)),
        out_shape=jax.ShapeDtypeStruct((bsz, s, ATT_WIDTH), BF16),
        compiler_params=pltpu.CompilerParams(dimension_semantics=("arbitrary",),
                                             vmem_limit_bytes=VMEM_LIMIT),
        name="attention",
    )(q, k, kx, vt, vxt, sink_row)


BF16_SUBLANES = 16
ML_STATE_ROWS = ML_V_DIM + BF16_SUBLANES


def _mlstm_kernel(qkx_ref, vtx_ref, ox_ref, gcx_ref, grx_ref, qkc_ref, vtc_ref, grc_ref,
                  out_ref, sin_ref, st_ref, ucol_ref, mrun_ref, wint_ref, floor_ref,
                  wkey_ref, decay_ref, *, ncx, ncc):
    lc = ML_CHUNK
    qkw = ML_QK_WIDTH
    sr = ML_STATE_ROWS
    ng = ML_GATES
    hg = ML_GATES // 2
    st_ref[...] = jnp.zeros(st_ref.shape, F32)

    rr = lax.broadcasted_iota(jnp.int32, (lc, lc), 0)
    cc = lax.broadcasted_iota(jnp.int32, (lc, lc), 1)
    tril = rr >= cc
    triu = rr <= cc
    tril_b = jnp.where(tril, 1.0, 0.0).astype(BF16)
    triu_b = jnp.where(triu, 1.0, 0.0).astype(BF16)
    head_of_lane = lax.broadcasted_iota(jnp.int32, (lc, qkw), 1) // ML_QK_DIM
    head_of_state_lane = lax.broadcasted_iota(jnp.int32, (1, qkw), 1) // ML_QK_DIM
    pad_rows = jnp.zeros((BF16_SUBLANES - 1, lc), F32)

    def split(v):
        hi = v.astype(BF16)
        return hi, (v - hi.astype(F32)).astype(BF16)

    def gate_rows(g):
        n16 = g.shape[0]
        fwd_row = (lax.broadcasted_iota(jnp.int32, g.shape, 0) & (ng - 1)) < hg
        lane = lax.broadcasted_iota(jnp.int32, g.shape, 1)
        hi, lo = split(_log_sigmoid(g))
        cat = jnp.concatenate([hi, lo], axis=0)
        bu = _dot(cat, triu_b)
        bl = _dot(cat, tril_b)
        b = jnp.where(fwd_row, bu[:n16] + bu[n16:], bl[:n16] + bl[n16:])
        b = pltpu.roll(b, n16 - ML_HEADS, axis=0)
        u = g - b
        run_f = run_b = u
        k = 1
        while k < lc:
            run_f = jnp.maximum(run_f, jnp.where(lane >= k, pltpu.roll(run_f, k, axis=1), -jnp.inf))
            run_b = jnp.maximum(run_b, jnp.where(lane < lc - k, pltpu.roll(run_b, lc - k, axis=1), -jnp.inf))
            k *= 2
        run = jnp.where(fwd_row, run_f, run_b)

        def at_end(a):
            return jnp.where(fwd_row, jnp.broadcast_to(a[:, lc - 1:lc], a.shape),
                             jnp.broadcast_to(a[:, 0:1], a.shape))

        return u, b, run, at_end(b), at_end(run)

    def derived(u, b, run, b_tot, run_end, m_in):
        m_run = jnp.maximum(run, m_in)
        m_out = b_tot + jnp.maximum(run_end, m_in)
        return (m_run * LOG2_E, jnp.exp(m_in - m_run), jnp.exp(-(b + m_run)),
                jnp.exp(b_tot + u - m_out), jnp.exp(b_tot + m_in - m_out))

    gc_rows = gate_rows(grc_ref[0].reshape(ncc * ng, lc))
    gx_rows = gate_rows(grx_ref[0].reshape(ncx * ng, lc))

    def scan_m(rows, n, m_f, m_b):
        _, _, _, b_tot, run_end = rows
        part = lambda a, c, d: a[c * ng + d * hg:c * ng + (d + 1) * hg]
        ins_f, ins_b = [], [None] * n
        for c in range(n):
            ins_f.append(m_f)
            m_f = part(b_tot, c, 0) + jnp.maximum(part(run_end, c, 0), m_f)
        for c in reversed(range(n)):
            ins_b[c] = m_b
            m_b = part(b_tot, c, 1) + jnp.maximum(part(run_end, c, 1), m_b)
        return jnp.concatenate([x for c in range(n) for x in (ins_f[c], ins_b[c])], axis=0), m_f, m_b

    m0 = jnp.zeros((hg, lc), F32)
    m_in_c, m_f, m_b = scan_m(gc_rows, ncc, m0, m0)
    m_in_x, _, _ = scan_m(gx_rows, ncx, m_f, m_b)
    _, _, _, wkey_c, decay_c = derived(*gc_rows, m_in_c)
    for ref, val in zip((mrun_ref, wint_ref, floor_ref, wkey_ref, decay_ref), derived(*gx_rows, m_in_x)):
        ref[...] = val

    fwd_col = (lax.broadcasted_iota(jnp.int32, (lc, LANES), 1) & (ng - 1)) < hg

    def token_major_u(j):
        rows = pl.ds(pl.multiple_of(j * lc, lc), lc)
        gcol = gcx_ref[0, rows, :]
        hi, lo = split(_log_sigmoid(gcol))
        cat = jnp.concatenate([hi, lo], axis=1)
        bl = _dot(tril_b, cat)
        bu = _dot(triu_b, cat)
        b = jnp.where(fwd_col, bl[:, :LANES] + bl[:, LANES:], bu[:, :LANES] + bu[:, LANES:])
        ucol_ref[rows, :] = (gcol - pltpu.roll(b, LANES - ML_HEADS, axis=1)) * LOG2_E

    def advance(dirn, k4, vt, w_key, decay):
        st = st_ref[dirn]
        lhs, rhs, decay_row = [], [], None
        for h in range(ML_HEADS):
            c = hg * dirn + h
            lhs.append(jnp.concatenate([vt[h * ML_V_DIM:(h + 1) * ML_V_DIM].astype(F32) * w_key[c:c + 1],
                                        w_key[c:c + 1], pad_rows], axis=0).astype(BF16))
            rhs.append(jnp.where(head_of_lane == h, k4, jnp.zeros_like(k4)))
            d_h = jnp.concatenate([decay[c:c + 1]] * (qkw // lc), axis=1)
            decay_row = d_h if decay_row is None else jnp.where(head_of_state_lane == h, d_h, decay_row)
        upd = _dot(jnp.concatenate(lhs, axis=1), jnp.concatenate(rhs, axis=0))
        st_ref[dirn] = decay_row * st + upd

    for j in range(ncc):
        for dirn, cj in ((0, j), (1, ncc - 1 - j)):
            grows = slice(cj * ng, (cj + 1) * ng)
            advance(dirn, qkc_ref[0, cj * lc:(cj + 1) * lc, qkw:], vtc_ref[0, cj], wkey_c[grows], decay_c[grows])

    def scan_body(j, carry):
        for dirn, cj in ((0, j), (1, ncx - 1 - j)):
            rows = pl.ds(pl.multiple_of(cj * lc, lc), lc)
            grows = pl.ds(pl.multiple_of(cj * ng, ng), ng)
            sin_ref[dirn, cj] = st_ref[dirn].astype(BF16)
            advance(dirn, qkx_ref[0, rows, qkw:], vtx_ref[0, cj], wkey_ref[grows, :], decay_ref[grows, :])
        token_major_u(j)
        return carry

    lax.fori_loop(0, ncx, scan_body, 0, unroll=8)


    def out_body(j, carry):
        rows = pl.ds(pl.multiple_of(j * lc, lc), lc)
        grows = pl.ds(pl.multiple_of(j * ng, ng), ng)
        q4 = qkx_ref[0, rows, :qkw]
        k4 = qkx_ref[0, rows, qkw:]
        vt = vtx_ref[0, j]
        u_col = ucol_ref[rows, :]
        m_run, w_int, floor = mrun_ref[grows, :], wint_ref[grows, :], floor_ref[grows, :]
        zero = jnp.zeros_like(q4)
        qs = jnp.concatenate([jnp.where(head_of_lane == h, q4, zero) for h in range(ML_HEADS)], axis=0)
        both = _dot_nt(jnp.concatenate([k4, sin_ref[0, j], sin_ref[1, j]], axis=0), qs)
        qk_t = both[:lc]
        for h in range(ML_HEADS):
            cols_h = slice(h * lc, (h + 1) * lc)
            hs = None
            for dirn in range(2):
                c = hg * dirn + h
                inter = both[lc + dirn * sr:lc + (dirn + 1) * sr, cols_h]
                valid = triu if dirn == 0 else tril
                e = jnp.exp2(jnp.where(valid, u_col[:, c:c + 1] - m_run[c:c + 1], -jnp.inf))
                s_t = qk_t[:, cols_h] * e
                num = _dot(vt[h * ML_V_DIM:(h + 1) * ML_V_DIM], s_t.astype(BF16))
                num = num + w_int[c:c + 1] * inter[:ML_V_DIM]
                den = jnp.sum(s_t, axis=0, keepdims=True) + w_int[c:c + 1] * inter[ML_V_DIM:ML_V_DIM + 1]
                hv = num * (1.0 / jnp.maximum(jnp.abs(den), floor[c:c + 1]))
                hs = hv if hs is None else hs + hv
            cols = slice(h * ML_V_DIM, (h + 1) * ML_V_DIM)
            hn = hs * lax.rsqrt(jnp.mean(hs * hs, axis=0, keepdims=True) + EPS)
            out_ref[0, rows, cols] = (hn.T * ox_ref[0, rows, cols].astype(F32)).astype(BF16)
        return carry

    lax.fori_loop(0, ncx, out_body, 0, unroll=8)


def _mlstm(qkx, vtx, ox, gcx, grx, qkc, vtc, grc):
    bsz, s, _ = qkx.shape
    ncx, ncc = s // ML_CHUNK, qkc.shape[1] // ML_CHUNK
    per_b = lambda a: pl.BlockSpec((1,) + a.shape[1:], lambda b: (b,) + (0,) * (a.ndim - 1))
    ins = [qkx, vtx, ox, gcx, grx, qkc, vtc, grc]
    return pl.pallas_call(
        functools.partial(_mlstm_kernel, ncx=ncx, ncc=ncc),
        grid=(bsz,),
        in_specs=[per_b(a) for a in ins],
        out_specs=pl.BlockSpec((1, s, ML_WIDTH), lambda b: (b, 0, 0)),
        out_shape=jax.ShapeDtypeStruct((bsz, s, ML_WIDTH), BF16),
        scratch_shapes=[pltpu.VMEM((2, ncx, ML_STATE_ROWS, ML_QK_WIDTH), BF16),
                        pltpu.VMEM((2, ML_STATE_ROWS, ML_QK_WIDTH), F32),
                        pltpu.VMEM((s, LANES), F32)]
                       + [pltpu.VMEM((ncx * ML_GATES, ML_CHUNK), F32)] * 5,
        compiler_params=pltpu.CompilerParams(dimension_semantics=("arbitrary",),
                                             vmem_limit_bytes=VMEM_LIMIT),
        name="mlstm",
    )(*ins)


def _out_ffn_kernel(x_ref, att_ref, ml_ref, gtm_ref, shf_ref, scf_ref, gtf_ref,
                    gpm_ref, gpf_ref, gqf_ref, woa_ref, wom_ref, wfi_ref, wfo_ref, o_ref, *, hidden,
                    tiles_per_batch):
    r_mod = pl.ds((pl.program_id(0) // tiles_per_batch) % SUBLANES, 1)
    gtm, shf, scf, gtf = gtm_ref[r_mod, :], shf_ref[r_mod, :], scf_ref[r_mod, :], gtf_ref[r_mod, :]

    def group(g):
        r = slice(g * FFN_ROWS, (g + 1) * FFN_ROWS)
        mix = _dot(att_ref[r, :], woa_ref[...]) + _dot(ml_ref[r, :], wom_ref[...])
        yield
        x1 = x_ref[r, :] + gtm * _rms(mix, gpm_ref[...])
        h = (_rms(x1, gpf_ref[...]) * (1.0 + scf) + shf).astype(BF16)
        gu = _dot(h, wfi_ref[...])
        yield
        act = (_silu(gu[:, :hidden]) * gu[:, hidden:]).astype(BF16)
        fx = _dot(act, wfo_ref[...])
        yield
        o_ref[r, :] = x1 + gtf * _rms(fx, gqf_ref[...])

    done = object()
    active = [group(g) for g in range(x_ref.shape[0] // FFN_ROWS)]
    while active:
        active = [g for g in active if next(g, done) is not done]


def _out_ffn(x2, att2, ml2, mod, g_post_mix, g_pre_ffn, g_post_ffn, woa, wom, wfi, wfo, *, tiles_per_batch):
    t, d = x2.shape
    tm = FFN_TM
    hidden = wfo.shape[0]
    resident = lambda a: pl.BlockSpec(a.shape, lambda i: (0,) * a.ndim, pipeline_mode=pl.Buffered(1))
    mod_spec = lambda k: pl.BlockSpec((SUBLANES, d), lambda i: (i // tiles_per_batch // SUBLANES, k))
    row = pl.BlockSpec((1, d), lambda i: (0, 0))
    return pl.pallas_call(
        functools.partial(_out_ffn_kernel, hidden=hidden, tiles_per_batch=tiles_per_batch),
        grid=(t // tm,),
        in_specs=[pl.BlockSpec((tm, d), lambda i: (i, 0)),
                  pl.BlockSpec((tm, ATT_WIDTH), lambda i: (i, 0)),
                  pl.BlockSpec((tm, ML_WIDTH), lambda i: (i, 0)),
                  mod_spec(2), mod_spec(3), mod_spec(4), mod_spec(5), row, row, row,
                  resident(woa), resident(wom), resident(wfi), resident(wfo)],
        out_specs=pl.BlockSpec((tm, d), lambda i: (i, 0)),
        out_shape=jax.ShapeDtypeStruct((t, d), F32),
        compiler_params=pltpu.CompilerParams(dimension_semantics=("arbitrary",),
                                             vmem_limit_bytes=VMEM_LIMIT),
        name="out_ffn",
    )(x2, att2, ml2, mod, mod, mod, mod, g_post_mix, g_pre_ffn, g_post_ffn, woa, wom, wfi, wfo)


def _rope_tables(n_tokens):
    pos = jnp.arange(n_tokens)
    row = (pos // GRID_W).astype(F32)
    col = (pos % GRID_W).astype(F32)
    inv_freq = jnp.power(ROPE_BASE, -jnp.arange(ROPE_PAIR, dtype=F32) / ROPE_PAIR)
    ang_r = row[:, None] * inv_freq
    ang_c = col[:, None] * inv_freq
    z = jnp.zeros_like(ang_r)
    reps = LANES // ATT_HEAD_DIM
    cos = jnp.tile(jnp.concatenate([jnp.cos(ang_r)] * 2 + [jnp.cos(ang_c)] * 2, axis=1), (1, reps))
    sin_lo = jnp.tile(jnp.concatenate([-jnp.sin(ang_r), z, -jnp.sin(ang_c), z], axis=1), (1, reps))
    sin_hi = jnp.tile(jnp.concatenate([z, jnp.sin(ang_r), z, jnp.sin(ang_c)], axis=1), (1, reps))
    return cos, sin_lo, sin_hi


def _permute_heads(w, axis):
    shape = w.shape
    grouped = shape[:axis] + (ATT_KV_HEADS, ATT_GROUP, ATT_HEAD_DIM) + shape[axis + 1:]
    return jnp.swapaxes(w.reshape(grouped), axis, axis + 1).reshape(shape)


def kernel(x, c, ctx, c_ctx, w_ada, b_ada, g_pre_mix, w_in, w_conv_qk, b_gates, attn_sink,
           g_mlstm_out, w_out, g_post_mix, g_pre_ffn, w_ffn_in, w_ffn_out, g_post_ffn):
    bsz, s, d = x.shape
    l = ctx.shape[1]
    assert w_ada.shape[0] == 1, "single-layer block"
    assert bsz < MOD_ROWS and s % INPROJ_TM == 0 and l % ML_CHUNK == 0 and s % FFN_TM == 0
    assert bsz % CTX_PER_STEP == 0 and l % LANES == 0

    mod = _ada(c, c_ctx, w_ada[0], b_ada)

    w = w_in[0]
    o_q, o_k, o_v = 0, ATT_WIDTH, ATT_WIDTH + ATT_KV_WIDTH
    o_mq = o_v + ATT_KV_WIDTH
    o_mv = o_mq + 2 * ML_QK_WIDTH
    o_mo = o_mv + ML_WIDTH
    o_mg = o_mo + ML_WIDTH
    w_q = _permute_heads(w[:, o_q:o_k], 1) * (ATT_HEAD_DIM ** -0.5 * LOG2_E)
    w_g = jnp.pad(w[:, o_mg:], ((0, 0), (0, LANES - ML_GATES)))
    shared = [w[:, o_k:o_v], w_g, w[:, o_mq:o_mv]]
    w_lat = jnp.concatenate([w_q] + shared + [w[:, o_mo:o_mg]], axis=1).astype(BF16)
    wvt = jnp.concatenate([w[:, o_v:o_mq], w[:, o_mv:o_mo], w[:, o_mg:]], axis=1).T.astype(BF16)
    bg_row = jnp.pad(b_gates, ((0, 0), (0, LANES - ML_GATES)))
    bg_col = b_gates.reshape(ML_GATES, 1)
    wc = w_conv_qk[0]

    q, k, vt, qkx, vtx, ox, gcx, grx = _inproj(
        x, mod, None, g_pre_mix, w_lat, wvt, bg_row, bg_col, wc, _rope_tables(s), g_mlstm_out,
        tm=INPROJ_TM, gm=INPROJ_ROWS, latent=True)
    nctx = bsz // CTX_PER_STEP
    kc, vct, qkc, vtc, _, grc = _inproj(
        ctx.reshape(nctx, CTX_PER_STEP * l, d), mod, bsz, g_pre_mix, w_lat, wvt, bg_row, bg_col, wc, None, None,
        tm=CTX_PER_STEP * l, gm=l, latent=False)
    kc = kc.reshape(bsz, l, ATT_KV_WIDTH)
    qkc = qkc.reshape(bsz, l, 2 * ML_QK_WIDTH)
    vtc = vtc.reshape((bsz, l // ML_CHUNK) + vtc.shape[2:])
    grc = grc.reshape((bsz, l // ML_CHUNK) + grc.shape[2:])

    sink_row = jnp.repeat(attn_sink[0][jnp.array(_HEAD_PERM)] * LOG2_E, ATT_BLOCK)[None, :]
    att = _attention(q, k, vt, kc, vct, sink_row)
    ml = _mlstm(qkx, vtx, ox, gcx, grx, qkc, vtc, grc)

    wo = w_out[0]
    woa = _permute_heads(wo[:ATT_WIDTH], 0).astype(BF16)
    wom = wo[ATT_WIDTH:].astype(BF16)
    out = _out_ffn(x.reshape(bsz * s, d), att.reshape(bsz * s, ATT_WIDTH), ml.reshape(bsz * s, ML_WIDTH),
                   mod, g_post_mix, g_pre_ffn, g_post_ffn, woa, wom,
                   w_ffn_in[0].astype(BF16), w_ffn_out[0].astype(BF16), tiles_per_batch=s // FFN_TM)
    return out.reshape(bsz, s, d)
```

```python
import functools

import jax
import jax.numpy as jnp
from jax import lax
from jax.experimental import pallas as pl
from jax.experimental.pallas import tpu as pltpu

F32 = jnp.float32
BF16 = jnp.bfloat16

EPS = 1e-6
GRID_W = 64
ROPE_BASE = 10000.0
LOG2_E = 1.4426950408889634

ATT_HEADS = 8
ATT_KV_HEADS = 2
ATT_GROUP = ATT_HEADS // ATT_KV_HEADS
ATT_HEAD_DIM = 64
ROPE_PAIR = ATT_HEAD_DIM // 4
ATT_BLOCK = 128
ATT_WIDTH = ATT_HEADS * ATT_HEAD_DIM
ATT_KV_WIDTH = ATT_KV_HEADS * ATT_HEAD_DIM
ATT_UNIT_GROUPS = 2
ATT_ONES_ROWS = 16

ML_HEADS = 4
ML_V_DIM = 128
ML_QK_DIM = 64
ML_WIDTH = ML_HEADS * ML_V_DIM
ML_QK_WIDTH = ML_HEADS * ML_QK_DIM
ML_GATES = 4 * ML_HEADS
ML_CHUNK = 128

LANES = 128
SUBLANES = 8
VMEM_LIMIT = 56 * 1024 * 1024

INPROJ_TM = 1024
INPROJ_ROWS = 512
CTX_PER_STEP = 2
FFN_TM = 1024
FFN_ROWS = 256
FFN_PAIR_PERIOD = 3
ADA_TN = 1536
MOD_ROWS = 16

_HEAD_PERM = tuple(h * ATT_GROUP + g for g in range(ATT_GROUP) for h in range(ATT_KV_HEADS))


def _silu(v):
    return v * jax.nn.sigmoid(v)


def _log_sigmoid(v):
    return jnp.minimum(v, 0.0) - jnp.log1p(jnp.exp(-jnp.abs(v)))


def _rms(v, g):
    return v * lax.rsqrt(jnp.mean(v * v, axis=-1, keepdims=True) + EPS) * g


def _dot(a, b):
    return jnp.dot(a, b, preferred_element_type=F32)


def _dot_nt(a, b):
    return lax.dot_general(a, b, (((1,), (1,)), ((), ())), preferred_element_type=F32)


def _ada_kernel(c_ref, cctx_ref, w_ref, b_ref, o_ref):
    pad = jnp.zeros((MOD_ROWS - c_ref.shape[0] - 1, c_ref.shape[1]), F32)
    a = _silu(jnp.concatenate([c_ref[...], cctx_ref[...], pad], axis=0))
    o_ref[...] = _dot(a.astype(BF16), w_ref[...].astype(BF16)) + b_ref[...]


def _ada(c, c_ctx, w, b):
    d, n = w.shape
    return pl.pallas_call(
        _ada_kernel,
        grid=(n // ADA_TN,),
        in_specs=[pl.BlockSpec(c.shape, lambda j: (0, 0)),
                  pl.BlockSpec((1, d), lambda j: (0, 0)),
                  pl.BlockSpec((d, ADA_TN), lambda j: (0, j)),
                  pl.BlockSpec((1, ADA_TN), lambda j: (0, j))],
        out_specs=pl.BlockSpec((MOD_ROWS, ADA_TN), lambda j: (0, j)),
        out_shape=jax.ShapeDtypeStruct((MOD_ROWS, n), F32),
        compiler_params=pltpu.CompilerParams(dimension_semantics=("arbitrary",),
                                             vmem_limit_bytes=VMEM_LIMIT),
        name="ada",
    )(c, c_ctx.reshape(1, d), w, b)


def _rope(v, cos, sin_lo, sin_hi):
    return (v * cos + pltpu.roll(v, LANES - ROPE_PAIR, axis=1) * sin_lo
            + pltpu.roll(v, ROPE_PAIR, axis=1) * sin_hi)


def _inproj_kernel(*refs, tm, gm, n_tiles, latent, mod_row):
    if latent:
        (x_ref, xp_ref, xn_ref, sh_ref, sc_ref, g_ref, w_ref, wvt_ref, bgr_ref, bgc_ref,
         wc_ref, cos_ref, sl_ref, shi_ref, gain_ref,
         q_ref, k_ref, vt_ref, qk_ref, mvt_ref, o_ref, gcol_ref, grow_ref) = refs
    else:
        (x_ref, xp_ref, xn_ref, sh_ref, sc_ref, g_ref, w_ref, wvt_ref, bgr_ref, bgc_ref,
         wc_ref,
         k_ref, vt_ref, qk_ref, mvt_ref, gcol_ref, grow_ref) = refs
    i = pl.program_id(0)
    n_groups = tm // gm
    r_mod = (pl.program_id(1) if mod_row is None else mod_row) % SUBLANES
    scale = g_ref[...] * (1.0 + sc_ref[pl.ds(r_mod, 1), :])
    shift = sh_ref[pl.ds(r_mod, 1), :]
    wc = wc_ref[...]
    row = lax.broadcasted_iota(jnp.int32, (gm, 1), 0)

    def project(lo, width, lhs):
        return _dot(lhs, w_ref[:, lo:lo + width])

    def group(r):
        rows = slice(r * gm, (r + 1) * gm)
        before = xp_ref[0] if r == 0 else x_ref[0, r * gm - SUBLANES:r * gm, :]
        after = xn_ref[0] if r == n_groups - 1 else x_ref[0, (r + 1) * gm:(r + 1) * gm + SUBLANES, :]
        keep_prev = (jnp.where(i == 0, 0.0, 1.0) if r == 0 else 1.0) if latent else 0.0
        keep_next = (jnp.where(i == n_tiles - 1, 0.0, 1.0) if r == n_groups - 1 else 1.0) if latent else 0.0
        xt = jnp.concatenate([x_ref[0, rows, :], before, after], axis=0)
        ms = jnp.mean(xt * xt, axis=-1, keepdims=True)
        hb = (xt * lax.rsqrt(ms + EPS) * scale + shift).astype(BF16)
        hm = hb[:gm]
        yield

        c = ATT_WIDTH
        if latent:
            r_q = project(0, ATT_WIDTH, hm)
        r_kg = project(c, ATT_KV_WIDTH + LANES, hm)
        c += ATT_KV_WIDTH + LANES
        yield

        if latent:
            cos, sl, shi = cos_ref[rows, :], sl_ref[rows, :], shi_ref[rows, :]
            for g in range(ATT_WIDTH // LANES):
                q_ref[0, rows, g * LANES:(g + 1) * LANES] = _rope(
                    r_q[:, g * LANES:(g + 1) * LANES], cos, sl, shi).astype(BF16)
        y = project(c, 2 * ML_QK_WIDTH, hb)
        c += 2 * ML_QK_WIDTH
        yield

        if latent:
            k_ref[0, rows, :] = _rope(r_kg[:, :LANES], cos, sl, shi).astype(BF16)
        else:
            k_ref[0, rows, :] = r_kg[:, :LANES].astype(BF16)
        gcol_ref[0, rows, :] = r_kg[:, LANES:] + bgr_ref[...]
        v_t = _dot_nt(wvt_ref[...], hm)
        yield

        ym = y[:gm]
        prev = jnp.where(row == 0, y[gm + SUBLANES - 1:gm + SUBLANES] * keep_prev,
                         pltpu.roll(ym, 1, axis=0))
        nxt = jnp.where(row == gm - 1, y[gm + SUBLANES:gm + SUBLANES + 1] * keep_next,
                        pltpu.roll(ym, gm - 1, axis=0))
        act = _silu(prev * wc[0:1] + ym * wc[1:2] + nxt * wc[2:3])
        qk_ref[0, rows, :ML_QK_WIDTH] = (act[:, :ML_QK_WIDTH] * (ML_QK_DIM ** -0.5)).astype(BF16)
        qk_ref[0, rows, ML_QK_WIDTH:] = act[:, ML_QK_WIDTH:].astype(BF16)
        if latent:
            r_o = project(c, ML_WIDTH, hm)
        yield

        g_t = v_t[ATT_KV_WIDTH + ML_WIDTH:] + bgc_ref[...]
        vt_ref[0, :, rows] = v_t[:ATT_KV_WIDTH].astype(BF16)
        for j in range(gm // ML_CHUNK):
            cols = slice(j * ML_CHUNK, (j + 1) * ML_CHUNK)
            mvt_ref[0, r * (gm // ML_CHUNK) + j] = v_t[ATT_KV_WIDTH:ATT_KV_WIDTH + ML_WIDTH, cols].astype(BF16)
            grow_ref[0, r * (gm // ML_CHUNK) + j] = g_t[:, cols]
        if latent:
            o_ref[0, rows, :] = (jax.nn.sigmoid(r_o) * gain_ref[...]).astype(BF16)

    done = object()
    waiting = [group(r) for r in range(n_groups)]
    active = []
    while waiting or active:
        if waiting:
            active.append(waiting.pop(0))
        active = [g for g in active if next(g, done) is not done]


def _inproj(x, mod, mod_row, g_pre, w_main, wvt, bg_row, bg_col, wc, rope_tabs, gain, *, tm, gm, latent):
    bsz, t, d = x.shape
    n_tiles = t // tm
    hb = tm // SUBLANES
    n_hblk = t // SUBLANES
    n = w_main.shape[1]
    mod_blk = (lambda b: b // SUBLANES) if mod_row is None else (lambda b: mod_row // SUBLANES)

    def const(shape):
        return pl.BlockSpec(shape, lambda i, b: (0,) * len(shape))

    in_specs = [
        pl.BlockSpec((1, tm, d), lambda i, b: (b, i, 0)),
        pl.BlockSpec((1, SUBLANES, d), lambda i, b: (b, jnp.maximum(i * hb - 1, 0), 0)),
        pl.BlockSpec((1, SUBLANES, d), lambda i, b: (b, jnp.minimum((i + 1) * hb, n_hblk - 1), 0)),
        pl.BlockSpec((SUBLANES, d), lambda i, b: (mod_blk(b), 0)),
        pl.BlockSpec((SUBLANES, d), lambda i, b: (mod_blk(b), 1)),
        const((1, d)), const((d, n)), const((ATT_KV_WIDTH + ML_WIDTH + ML_GATES, d)),
        const((1, LANES)), const((ML_GATES, 1)), const((3, 2 * ML_QK_WIDTH)),
    ]
    args = [x, x, x, mod, mod, g_pre, w_main, wvt, bg_row, bg_col, wc]
    tok = lambda width, dt: (pl.BlockSpec((1, tm, width), lambda i, b: (b, i, 0)),
                             jax.ShapeDtypeStruct((bsz, t, width), dt))
    chunked = lambda rows, dt: (pl.BlockSpec((1, tm // ML_CHUNK, rows, ML_CHUNK), lambda i, b: (b, i, 0, 0)),
                                jax.ShapeDtypeStruct((bsz, t // ML_CHUNK, rows, ML_CHUNK), dt))
    outs = []
    if latent:
        in_specs += [pl.BlockSpec((tm, LANES), lambda i, b: (i, 0))] * 3 + [const((1, ML_WIDTH))]
        args += list(rope_tabs) + [gain]
        outs.append(tok(ATT_WIDTH, BF16))
    outs.append(tok(ATT_KV_WIDTH, BF16))
    outs.append((pl.BlockSpec((1, LANES, tm), lambda i, b: (b, 0, i)),
                 jax.ShapeDtypeStruct((bsz, ATT_KV_WIDTH, t), BF16)))
    outs.append(tok(2 * ML_QK_WIDTH, BF16))
    outs.append(chunked(ML_WIDTH, BF16))
    if latent:
        outs.append(tok(ML_WIDTH, BF16))
    outs.append(tok(LANES, F32))
    outs.append(chunked(ML_GATES, F32))
    return pl.pallas_call(
        functools.partial(_inproj_kernel, tm=tm, gm=gm, n_tiles=n_tiles, latent=latent, mod_row=mod_row),
        grid=(n_tiles, bsz),
        in_specs=in_specs,
        out_specs=[o[0] for o in outs],
        out_shape=[o[1] for o in outs],
        compiler_params=pltpu.CompilerParams(dimension_semantics=("arbitrary", "arbitrary"),
                                             vmem_limit_bytes=VMEM_LIMIT),
        name="inproj_latent" if latent else "inproj_context",
    )(*args)


def _attn_kernel(q_ref, k_ref, kx_ref, vt_ref, vx_ref, sink_ref, o_ref):
    blk = ATT_BLOCK
    lane = lax.broadcasted_iota(jnp.int32, (blk, LANES), 1)
    zero = jnp.zeros((blk, LANES), BF16)
    half_groups = ATT_UNIT_GROUPS
    n_slots = ATT_KV_HEADS * half_groups
    n_parts = ATT_GROUP // half_groups
    sink = [sink_ref[:, part * n_slots * blk:(part + 1) * n_slots * blk] for part in range(n_parts)]

    def stack_heads(q, half):
        parts = []
        for g in range(half * half_groups, (half + 1) * half_groups):
            slab = q[:, g * LANES:(g + 1) * LANES]
            parts.append(jnp.where(lane < ATT_HEAD_DIM, slab, zero))
            parts.append(jnp.where(lane >= ATT_HEAD_DIM, slab, zero))
        return jnp.concatenate(parts, axis=0)

    key = lax.broadcasted_iota(jnp.int32, (blk, blk), 0)
    qry = lax.broadcasted_iota(jnp.int32, (blk, blk), 1)
    ninf = jnp.full((blk, blk), -jnp.inf, F32)
    bias_prev = jnp.where(key >= qry, 0.0, ninf)
    bias_next = jnp.where(key <= qry, 0.0, ninf)
    slots = lambda b: jnp.concatenate([b] * n_slots, axis=1)

    def scores(qs, k_prev, k_cur, k_next):
        return [_dot_nt(k_prev, qs), _dot_nt(k_cur, qs), _dot_nt(k_next, qs), _dot_nt(kx_ref[0], qs)]

    def softmax(s, b_prev, b_next, sink_h):
        s = jnp.concatenate([s[0] + slots(b_prev), s[1], s[2] + slots(b_next), s[3]], axis=0)
        m = jnp.maximum(sink_h, jnp.max(s, axis=0, keepdims=True))
        return jnp.exp2(s - m).astype(BF16), jnp.exp2(sink_h - m)

    n_keys = 3 * blk + kx_ref.shape[1]
    ones_rows = jnp.ones((ATT_ONES_ROWS, n_keys), BF16)

    def weighted_values(p, v_prev, v_cur, v_next):
        vt = jnp.concatenate([v_prev, v_cur, v_next, vx_ref[0]], axis=1)
        return _dot(jnp.concatenate([vt, ones_rows], axis=0), p)

    dim = lax.broadcasted_iota(jnp.int32, (LANES, blk), 0)

    def emit(rows, half, ot, p_sink):
        ot = ot[:LANES] * (1.0 / (ot[LANES:LANES + 1] + p_sink))
        for j in range(half_groups):
            g = half * half_groups + j
            a = ot[:, (2 * j) * blk:(2 * j + 1) * blk]
            b = ot[:, (2 * j + 1) * blk:(2 * j + 2) * blk]
            o_ref[0, rows, g * LANES:(g + 1) * LANES] = jnp.where(dim < ATT_HEAD_DIM, a, b).T.astype(BF16)

    nblk = q_ref.shape[1] // blk
    rows_of = [slice(b * blk, (b + 1) * blk) for b in range(nblk)]
    k_blocks = [k_ref[0, r, :] for r in rows_of]
    v_blocks = [vt_ref[0, :, r] for r in rows_of]
    near = lambda blocks, b: (blocks[max(b - 1, 0)], blocks[b], blocks[min(b + 1, nblk - 1)])
    bias = [(ninf if b == 0 else bias_prev, ninf if b == nblk - 1 else bias_next) for b in range(nblk)]
    units = [(b, half) for b in range(nblk) for half in range(n_parts)]

    def stage_scores(u):
        b, half = u
        return scores(stack_heads(q_ref[0, rows_of[b], :], half), *near(k_blocks, b))

    def stage_softmax(u, s):
        b, half = u
        return softmax(s, *bias[b], sink[half])

    def stage_values(u, p):
        return weighted_values(p, *near(v_blocks, u[0]))

    def stage_emit(u, ot, p_sink):
        emit(rows_of[u[0]], u[1], ot, p_sink)

    n_units = len(units)
    s, p, ot = {}, {}, {}
    s[0] = stage_scores(units[0])
    for t in range(n_units):
        if t + 1 < n_units:
            s[t + 1] = stage_scores(units[t + 1])
        p[t] = stage_softmax(units[t], s.pop(t))
        if t >= 1:
            ot[t - 1] = stage_values(units[t - 1], p[t - 1][0])
        if t >= 2:
            stage_emit(units[t - 2], ot.pop(t - 2), p.pop(t - 2)[1])
    ot[n_units - 1] = stage_values(units[n_units - 1], p[n_units - 1][0])
    for t in (n_units - 2, n_units - 1):
        stage_emit(units[t], ot.pop(t), p.pop(t)[1])


def _attention(q, k, vt, kx, vxt, sink_row):
    bsz, s, _ = q.shape
    l = kx.shape[1]
    per_b = lambda a: pl.BlockSpec((1,) + a.shape[1:], lambda b: (b, 0, 0))
    vx_spec = pl.BlockSpec((1, ATT_KV_WIDTH, l), lambda b: (b // CTX_PER_STEP, 0, b % CTX_PER_STEP))
    return pl.pallas_call(
        _attn_kernel,
        grid=(bsz,),
        in_specs=[per_b(q), per_b(k), per_b(kx), per_b(vt), vx_spec,
                  pl.BlockSpec(sink_row.shape, lambda b: (0, 0))],
        out_specs=pl.BlockSpec((1, s, ATT_WIDTH), lambda b: (b, 0, 0)),
        out_shape=jax.ShapeDtypeStruct((bsz, s, ATT_WIDTH), BF16),
        compiler_params=pltpu.CompilerParams(dimension_semantics=("arbitrary",),
                                             vmem_limit_bytes=VMEM_LIMIT),
        name="attention",
    )(q, k, kx, vt, vxt, sink_row)


BF16_SUBLANES = 16
ML_STATE_ROWS = ML_V_DIM + BF16_SUBLANES


def _mlstm_kernel(qkx_ref, vtx_ref, ox_ref, gcx_ref, grx_ref, qkc_ref, vtc_ref, grc_ref,
                  out_ref, sin_ref, st_ref, ucol_ref, mrun_ref, wint_ref, floor_ref,
                  wkey_ref, decay_ref, *, ncx, ncc):
    lc = ML_CHUNK
    qkw = ML_QK_WIDTH
    sr = ML_STATE_ROWS
    ng = ML_GATES
    hg = ML_GATES // 2
    st_ref[...] = jnp.zeros(st_ref.shape, F32)

    rr = lax.broadcasted_iota(jnp.int32, (lc, lc), 0)
    cc = lax.broadcasted_iota(jnp.int32, (lc, lc), 1)
    tril = rr >= cc
    triu = rr <= cc
    tril_b = jnp.where(tril, 1.0, 0.0).astype(BF16)
    triu_b = jnp.where(triu, 1.0, 0.0).astype(BF16)
    head_of_lane = lax.broadcasted_iota(jnp.int32, (lc, qkw), 1) // ML_QK_DIM
    head_of_state_lane = lax.broadcasted_iota(jnp.int32, (1, qkw), 1) // ML_QK_DIM
    pad_rows = jnp.zeros((BF16_SUBLANES - 1, lc), F32)

    def split(v):
        hi = v.astype(BF16)
        return hi, (v - hi.astype(F32)).astype(BF16)

    def gate_rows(g):
        n16 = g.shape[0]
        fwd_row = (lax.broadcasted_iota(jnp.int32, g.shape, 0) & (ng - 1)) < hg
        lane = lax.broadcasted_iota(jnp.int32, g.shape, 1)
        hi, lo = split(_log_sigmoid(g))
        cat = jnp.concatenate([hi, lo], axis=0)
        bu = _dot(cat, triu_b)
        bl = _dot(cat, tril_b)
        b = jnp.where(fwd_row, bu[:n16] + bu[n16:], bl[:n16] + bl[n16:])
        b = pltpu.roll(b, n16 - ML_HEADS, axis=0)
        u = g - b
        run_f = run_b = u
        k = 1
        while k < lc:
            run_f = jnp.maximum(run_f, jnp.where(lane >= k, pltpu.roll(run_f, k, axis=1), -jnp.inf))
            run_b = jnp.maximum(run_b, jnp.where(lane < lc - k, pltpu.roll(run_b, lc - k, axis=1), -jnp.inf))
            k *= 2
        run = jnp.where(fwd_row, run_f, run_b)

        def at_end(a):
            return jnp.where(fwd_row, jnp.broadcast_to(a[:, lc - 1:lc], a.shape),
                             jnp.broadcast_to(a[:, 0:1], a.shape))

        return u, b, run, at_end(b), at_end(run)

    def derived(u, b, run, b_tot, run_end, m_in):
        m_run = jnp.maximum(run, m_in)
        m_out = b_tot + jnp.maximum(run_end, m_in)
        return (m_run * LOG2_E, jnp.exp(m_in - m_run), jnp.exp(-(b + m_run)),
                jnp.exp(b_tot + u - m_out), jnp.exp(b_tot + m_in - m_out))

    gc_rows = gate_rows(grc_ref[0].reshape(ncc * ng, lc))
    gx_rows = gate_rows(grx_ref[0].reshape(ncx * ng, lc))

    def scan_m(rows, n, m_f, m_b):
        _, _, _, b_tot, run_end = rows
        part = lambda a, c, d: a[c * ng + d * hg:c * ng + (d + 1) * hg]
        ins_f, ins_b = [], [None] * n
        for c in range(n):
            ins_f.append(m_f)
            m_f = part(b_tot, c, 0) + jnp.maximum(part(run_end, c, 0), m_f)
        for c in reversed(range(n)):
            ins_b[c] = m_b
            m_b = part(b_tot, c, 1) + jnp.maximum(part(run_end, c, 1), m_b)
        return jnp.concatenate([x for c in range(n) for x in (ins_f[c], ins_b[c])], axis=0), m_f, m_b

    m0 = jnp.zeros((hg, lc), F32)
    m_in_c, m_f, m_b = scan_m(gc_rows, ncc, m0, m0)
    m_in_x, _, _ = scan_m(gx_rows, ncx, m_f, m_b)
    _, _, _, wkey_c, decay_c = derived(*gc_rows, m_in_c)
    for ref, val in zip((mrun_ref, wint_ref, floor_ref, wkey_ref, decay_ref), derived(*gx_rows, m_in_x)):
        ref[...] = val

    fwd_col = (lax.broadcasted_iota(jnp.int32, (lc, LANES), 1) & (ng - 1)) < hg

    def token_major_u(j):
        rows = pl.ds(pl.multiple_of(j * lc, lc), lc)
        gcol = gcx_ref[0, rows, :]
        hi, lo = split(_log_sigmoid(gcol))
        cat = jnp.concatenate([hi, lo], axis=1)
        bl = _dot(tril_b, cat)
        bu = _dot(triu_b, cat)
        b = jnp.where(fwd_col, bl[:, :LANES] + bl[:, LANES:], bu[:, :LANES] + bu[:, LANES:])
        ucol_ref[rows, :] = (gcol - pltpu.roll(b, LANES - ML_HEADS, axis=1)) * LOG2_E

    def advance(dirn, k4, vt, w_key, decay):
        st = st_ref[dirn]
        lhs, rhs, decay_row = [], [], None
        for h in range(ML_HEADS):
            c = hg * dirn + h
            lhs.append(jnp.concatenate([vt[h * ML_V_DIM:(h + 1) * ML_V_DIM].astype(F32) * w_key[c:c + 1],
                                        w_key[c:c + 1], pad_rows], axis=0).astype(BF16))
            rhs.append(jnp.where(head_of_lane == h, k4, jnp.zeros_like(k4)))
            d_h = jnp.concatenate([decay[c:c + 1]] * (qkw // lc), axis=1)
            decay_row = d_h if decay_row is None else jnp.where(head_of_state_lane == h, d_h, decay_row)
        upd = _dot(jnp.concatenate(lhs, axis=1), jnp.concatenate(rhs, axis=0))
        st_ref[dirn] = decay_row * st + upd

    for j in range(ncc):
        for dirn, cj in ((0, j), (1, ncc - 1 - j)):
            grows = slice(cj * ng, (cj + 1) * ng)
            advance(dirn, qkc_ref[0, cj * lc:(cj + 1) * lc, qkw:], vtc_ref[0, cj], wkey_c[grows], decay_c[grows])

    def scan_body(j, carry):
        for dirn, cj in ((0, j), (1, ncx - 1 - j)):
            rows = pl.ds(pl.multiple_of(cj * lc, lc), lc)
            grows = pl.ds(pl.multiple_of(cj * ng, ng), ng)
            sin_ref[dirn, cj] = st_ref[dirn].astype(BF16)
            advance(dirn, qkx_ref[0, rows, qkw:], vtx_ref[0, cj], wkey_ref[grows, :], decay_ref[grows, :])
        token_major_u(j)
        return carry

    lax.fori_loop(0, ncx, scan_body, 0, unroll=8)


    def out_body(j, carry):
        rows = pl.ds(pl.multiple_of(j * lc, lc), lc)
        grows = pl.ds(pl.multiple_of(j * ng, ng), ng)
        q4 = qkx_ref[0, rows, :qkw]
        k4 = qkx_ref[0, rows, qkw:]
        vt = vtx_ref[0, j]
        u_col = ucol_ref[rows, :]
        m_run, w_int, floor = mrun_ref[grows, :], wint_ref[grows, :], floor_ref[grows, :]
        zero = jnp.zeros_like(q4)
        qs = jnp.concatenate([jnp.where(head_of_lane == h, q4, zero) for h in range(ML_HEADS)], axis=0)
        both = _dot_nt(jnp.concatenate([k4, sin_ref[0, j], sin_ref[1, j]], axis=0), qs)
        qk_t = both[:lc]
        for h in range(ML_HEADS):
            cols_h = slice(h * lc, (h + 1) * lc)
            hs = None
            for dirn in range(2):
                c = hg * dirn + h
                inter = both[lc + dirn * sr:lc + (dirn + 1) * sr, cols_h]
                valid = triu if dirn == 0 else tril
                e = jnp.exp2(jnp.where(valid, u_col[:, c:c + 1] - m_run[c:c + 1], -jnp.inf))
                s_t = qk_t[:, cols_h] * e
                num = _dot(vt[h * ML_V_DIM:(h + 1) * ML_V_DIM], s_t.astype(BF16))
                num = num + w_int[c:c + 1] * inter[:ML_V_DIM]
                den = jnp.sum(s_t, axis=0, keepdims=True) + w_int[c:c + 1] * inter[ML_V_DIM:ML_V_DIM + 1]
                hv = num * (1.0 / jnp.maximum(jnp.abs(den), floor[c:c + 1]))
                hs = hv if hs is None else hs + hv
            cols = slice(h * ML_V_DIM, (h + 1) * ML_V_DIM)
            hn = hs * lax.rsqrt(jnp.mean(hs * hs, axis=0, keepdims=True) + EPS)
            out_ref[0, rows, cols] = (hn.T * ox_ref[0, rows, cols].astype(F32)).astype(BF16)
        return carry

    lax.fori_loop(0, ncx, out_body, 0, unroll=8)


def _mlstm(qkx, vtx, ox, gcx, grx, qkc, vtc, grc):
    bsz, s, _ = qkx.shape
    ncx, ncc = s // ML_CHUNK, qkc.shape[1] // ML_CHUNK
    per_b = lambda a: pl.BlockSpec((1,) + a.shape[1:], lambda b: (b,) + (0,) * (a.ndim - 1))
    ins = [qkx, vtx, ox, gcx, grx, qkc, vtc, grc]
    return pl.pallas_call(
        functools.partial(_mlstm_kernel, ncx=ncx, ncc=ncc),
        grid=(bsz,),
        in_specs=[per_b(a) for a in ins],
        out_specs=pl.BlockSpec((1, s, ML_WIDTH), lambda b: (b, 0, 0)),
        out_shape=jax.ShapeDtypeStruct((bsz, s, ML_WIDTH), BF16),
        scratch_shapes=[pltpu.VMEM((2, ncx, ML_STATE_ROWS, ML_QK_WIDTH), BF16),
                        pltpu.VMEM((2, ML_STATE_ROWS, ML_QK_WIDTH), F32),
                        pltpu.VMEM((s, LANES), F32)]
                       + [pltpu.VMEM((ncx * ML_GATES, ML_CHUNK), F32)] * 5,
        compiler_params=pltpu.CompilerParams(dimension_semantics=("arbitrary",),
                                             vmem_limit_bytes=VMEM_LIMIT),
        name="mlstm",
    )(*ins)


def _out_ffn_kernel(x_ref, att_ref, ml_ref, gtm_ref, shf_ref, scf_ref, gtf_ref,
                    gpm_ref, gpf_ref, gqf_ref, woa_ref, wom_ref, wfi_ref, wfo_ref, o_ref, *, hidden,
                    tiles_per_batch):
    r_mod = pl.ds((pl.program_id(0) // tiles_per_batch) % SUBLANES, 1)
    gtm, shf, scf, gtf = gtm_ref[r_mod, :], shf_ref[r_mod, :], scf_ref[r_mod, :], gtf_ref[r_mod, :]

    def group(g):
        r = slice(g * FFN_ROWS, (g + 1) * FFN_ROWS)
        mix = _dot(att_ref[r, :], woa_ref[...]) + _dot(ml_ref[r, :], wom_ref[...])
        yield
        x1 = x_ref[r, :] + gtm * _rms(mix, gpm_ref[...])
        h = (_rms(x1, gpf_ref[...]) * (1.0 + scf) + shf).astype(BF16)
        gu = _dot(h, wfi_ref[...])
        yield
        act = (_silu(gu[:, :hidden]) * gu[:, hidden:]).astype(BF16)
        fx = _dot(act, wfo_ref[...])
        yield
        o_ref[r, :] = x1 + gtf * _rms(fx, gqf_ref[...])

    done = object()
    waiting = [group(g) for g in range(x_ref.shape[0] // FFN_ROWS)]
    active, rounds = [], 0
    while waiting or active:
        if rounds % FFN_PAIR_PERIOD == 0:
            active += [waiting.pop(0) for _ in range(min(2, len(waiting)))]
        active = [g for g in active if next(g, done) is not done]
        rounds += 1


def _out_ffn(x2, att2, ml2, mod, g_post_mix, g_pre_ffn, g_post_ffn, woa, wom, wfi, wfo, *, tiles_per_batch):
    t, d = x2.shape
    tm = FFN_TM
    hidden = wfo.shape[0]
    resident = lambda a: pl.BlockSpec(a.shape, lambda i: (0,) * a.ndim, pipeline_mode=pl.Buffered(1))
    mod_spec = lambda k: pl.BlockSpec((SUBLANES, d), lambda i: (i // tiles_per_batch // SUBLANES, k))
    row = pl.BlockSpec((1, d), lambda i: (0, 0))
    return pl.pallas_call(
        functools.partial(_out_ffn_kernel, hidden=hidden, tiles_per_batch=tiles_per_batch),
        grid=(t // tm,),
        in_specs=[pl.BlockSpec((tm, d), lambda i: (i, 0)),
                  pl.BlockSpec((tm, ATT_WIDTH), lambda i: (i, 0)),
                  pl.BlockSpec((tm, ML_WIDTH), lambda i: (i, 0)),
                  mod_spec(2), mod_spec(3), mod_spec(4), mod_spec(5), row, row, row,
                  resident(woa), resident(wom), resident(wfi), resident(wfo)],
        out_specs=pl.BlockSpec((tm, d), lambda i: (i, 0)),
        out_shape=jax.ShapeDtypeStruct((t, d), F32),
        compiler_params=pltpu.CompilerParams(dimension_semantics=("arbitrary",),
                                             vmem_limit_bytes=VMEM_LIMIT),
        name="out_ffn",
    )(x2, att2, ml2, mod, mod, mod, mod, g_post_mix, g_pre_ffn, g_post_ffn, woa, wom, wfi, wfo)


def _rope_tables(n_tokens):
    pos = jnp.arange(n_tokens)
    row = (pos // GRID_W).astype(F32)
    col = (pos % GRID_W).astype(F32)
    inv_freq = jnp.power(ROPE_BASE, -jnp.arange(ROPE_PAIR, dtype=F32) / ROPE_PAIR)
    ang_r = row[:, None] * inv_freq
    ang_c = col[:, None] * inv_freq
    z = jnp.zeros_like(ang_r)
    reps = LANES // ATT_HEAD_DIM
    cos = jnp.tile(jnp.concatenate([jnp.cos(ang_r)] * 2 + [jnp.cos(ang_c)] * 2, axis=1), (1, reps))
    sin_lo = jnp.tile(jnp.concatenate([-jnp.sin(ang_r), z, -jnp.sin(ang_c), z], axis=1), (1, reps))
    sin_hi = jnp.tile(jnp.concatenate([z, jnp.sin(ang_r), z, jnp.sin(ang_c)], axis=1), (1, reps))
    return cos, sin_lo, sin_hi


def _permute_heads(w, axis):
    shape = w.shape
    grouped = shape[:axis] + (ATT_KV_HEADS, ATT_GROUP, ATT_HEAD_DIM) + shape[axis + 1:]
    return jnp.swapaxes(w.reshape(grouped), axis, axis + 1).reshape(shape)


def kernel(x, c, ctx, c_ctx, w_ada, b_ada, g_pre_mix, w_in, w_conv_qk, b_gates, attn_sink,
           g_mlstm_out, w_out, g_post_mix, g_pre_ffn, w_ffn_in, w_ffn_out, g_post_ffn):
    bsz, s, d = x.shape
    l = ctx.shape[1]
    assert w_ada.shape[0] == 1, "single-layer block"
    assert bsz < MOD_ROWS and s % INPROJ_TM == 0 and l % ML_CHUNK == 0 and s % FFN_TM == 0
    assert bsz % CTX_PER_STEP == 0 and l % LANES == 0

    mod = _ada(c, c_ctx, w_ada[0], b_ada)

    w = w_in[0]
    o_q, o_k, o_v = 0, ATT_WIDTH, ATT_WIDTH + ATT_KV_WIDTH
    o_mq = o_v + ATT_KV_WIDTH
    o_mv = o_mq + 2 * ML_QK_WIDTH
    o_mo = o_mv + ML_WIDTH
    o_mg = o_mo + ML_WIDTH
    w_q = _permute_heads(w[:, o_q:o_k], 1) * (ATT_HEAD_DIM ** -0.5 * LOG2_E)
    w_g = jnp.pad(w[:, o_mg:], ((0, 0), (0, LANES - ML_GATES)))
    shared = [w[:, o_k:o_v], w_g, w[:, o_mq:o_mv]]
    w_lat = jnp.concatenate([w_q] + shared + [w[:, o_mo:o_mg]], axis=1).astype(BF16)
    wvt = jnp.concatenate([w[:, o_v:o_mq], w[:, o_mv:o_mo], w[:, o_mg:]], axis=1).T.astype(BF16)
    bg_row = jnp.pad(b_gates, ((0, 0), (0, LANES - ML_GATES)))
    bg_col = b_gates.reshape(ML_GATES, 1)
    wc = w_conv_qk[0]

    q, k, vt, qkx, vtx, ox, gcx, grx = _inproj(
        x, mod, None, g_pre_mix, w_lat, wvt, bg_row, bg_col, wc, _rope_tables(s), g_mlstm_out,
        tm=INPROJ_TM, gm=INPROJ_ROWS, latent=True)
    nctx = bsz // CTX_PER_STEP
    kc, vct, qkc, vtc, _, grc = _inproj(
        ctx.reshape(nctx, CTX_PER_STEP * l, d), mod, bsz, g_pre_mix, w_lat, wvt, bg_row, bg_col, wc, None, None,
        tm=CTX_PER_STEP * l, gm=l, latent=False)
    kc = kc.reshape(bsz, l, ATT_KV_WIDTH)
    qkc = qkc.reshape(bsz, l, 2 * ML_QK_WIDTH)
    vtc = vtc.reshape((bsz, l // ML_CHUNK) + vtc.shape[2:])
    grc = grc.reshape((bsz, l // ML_CHUNK) + grc.shape[2:])

    sink_row = jnp.repeat(attn_sink[0][jnp.array(_HEAD_PERM)] * LOG2_E, ATT_BLOCK)[None, :]
    att = _attention(q, k, vt, kc, vct, sink_row)
    ml = _mlstm(qkx, vtx, ox, gcx, grx, qkc, vtc, grc)

    wo = w_out[0]
    woa = _permute_heads(wo[:ATT_WIDTH], 0).astype(BF16)
    wom = wo[ATT_WIDTH:].astype(BF16)
    out = _out_ffn(x.reshape(bsz * s, d), att.reshape(bsz * s, ATT_WIDTH), ml.reshape(bsz * s, ML_WIDTH),
                   mod, g_post_mix, g_pre_ffn, g_post_ffn, woa, wom,
                   w_ffn_in[0].astype(BF16), w_ffn_out[0].astype(BF16), tiles_per_batch=s // FFN_TM)
    return out.reshape(bsz, s, d)
```

```python
import functools

import jax
import jax.numpy as jnp
from jax import lax
from jax.experimental import pallas as pl
from jax.experimental.pallas import tpu as pltpu

F32 = jnp.float32
BF16 = jnp.bfloat16

EPS = 1e-6
GRID_W = 64
ROPE_BASE = 10000.0
LOG2_E = 1.4426950408889634

ATT_HEADS = 8
ATT_KV_HEADS = 2
ATT_GROUP = ATT_HEADS // ATT_KV_HEADS
ATT_HEAD_DIM = 64
ROPE_PAIR = ATT_HEAD_DIM // 4
ATT_BLOCK = 128
ATT_WIDTH = ATT_HEADS * ATT_HEAD_DIM
ATT_KV_WIDTH = ATT_KV_HEADS * ATT_HEAD_DIM
ATT_UNIT_GROUPS = 2
ATT_ONES_ROWS = 16

ML_HEADS = 4
ML_V_DIM = 128
ML_QK_DIM = 64
ML_WIDTH = ML_HEADS * ML_V_DIM
ML_QK_WIDTH = ML_HEADS * ML_QK_DIM
ML_GATES = 4 * ML_HEADS
ML_CHUNK = 128

LANES = 128
SUBLANES = 8
VMEM_LIMIT = 56 * 1024 * 1024

INPROJ_TM = 1024
INPROJ_ROWS = 256
CTX_PER_STEP = 2
FFN_TM = 1024
FFN_ROWS = 256
FFN_PAIR_PERIOD = 3
ADA_TN = 768
MOD_ROWS = 16

_HEAD_PERM = tuple(h * ATT_GROUP + g for g in range(ATT_GROUP) for h in range(ATT_KV_HEADS))


def _silu(v):
    return v * jax.nn.sigmoid(v)


def _log_sigmoid(v):
    return jnp.minimum(v, 0.0) - jnp.log1p(jnp.exp(-jnp.abs(v)))


def _rms(v, g):
    return v * lax.rsqrt(jnp.mean(v * v, axis=-1, keepdims=True) + EPS) * g


def _dot(a, b):
    return jnp.dot(a, b, preferred_element_type=F32)


def _dot_nt(a, b):
    return lax.dot_general(a, b, (((1,), (1,)), ((), ())), preferred_element_type=F32)


def _ada_kernel(c_ref, cctx_ref, w_ref, b_ref, o_ref):
    pad = jnp.zeros((MOD_ROWS - c_ref.shape[0] - 1, c_ref.shape[1]), F32)
    a = _silu(jnp.concatenate([c_ref[...], cctx_ref[...], pad], axis=0))
    o_ref[...] = _dot(a.astype(BF16), w_ref[...].astype(BF16)) + b_ref[...]


def _ada(c, c_ctx, w, b):
    d, n = w.shape
    return pl.pallas_call(
        _ada_kernel,
        grid=(n // ADA_TN,),
        in_specs=[pl.BlockSpec(c.shape, lambda j: (0, 0)),
                  pl.BlockSpec((1, d), lambda j: (0, 0)),
                  pl.BlockSpec((d, ADA_TN), lambda j: (0, j)),
                  pl.BlockSpec((1, ADA_TN), lambda j: (0, j))],
        out_specs=pl.BlockSpec((MOD_ROWS, ADA_TN), lambda j: (0, j)),
        out_shape=jax.ShapeDtypeStruct((MOD_ROWS, n), F32),
        compiler_params=pltpu.CompilerParams(dimension_semantics=("arbitrary",),
                                             vmem_limit_bytes=VMEM_LIMIT),
        name="ada",
    )(c, c_ctx.reshape(1, d), w, b)


def _rope(v, cos, sin_lo, sin_hi):
    return (v * cos + pltpu.roll(v, LANES - ROPE_PAIR, axis=1) * sin_lo
            + pltpu.roll(v, ROPE_PAIR, axis=1) * sin_hi)


def _inproj_kernel(*refs, tm, gm, n_tiles, latent, mod_row):
    if latent:
        (x_ref, xp_ref, xn_ref, sh_ref, sc_ref, g_ref, w_ref, wvt_ref, bgr_ref, bgc_ref,
         wc_ref, cos_ref, sl_ref, shi_ref, gain_ref,
         q_ref, k_ref, vt_ref, qk_ref, mvt_ref, o_ref, gcol_ref, grow_ref) = refs
    else:
        (x_ref, xp_ref, xn_ref, sh_ref, sc_ref, g_ref, w_ref, wvt_ref, bgr_ref, bgc_ref,
         wc_ref,
         k_ref, vt_ref, qk_ref, mvt_ref, gcol_ref, grow_ref) = refs
    i = pl.program_id(0)
    n_groups = tm // gm
    r_mod = (pl.program_id(1) if mod_row is None else mod_row) % SUBLANES
    scale = g_ref[...] * (1.0 + sc_ref[pl.ds(r_mod, 1), :])
    shift = sh_ref[pl.ds(r_mod, 1), :]
    wc = wc_ref[...]
    row = lax.broadcasted_iota(jnp.int32, (gm, 1), 0)

    def project(lo, width, lhs):
        return _dot(lhs, w_ref[:, lo:lo + width])

    def group(r):
        rows = slice(r * gm, (r + 1) * gm)
        before = xp_ref[0] if r == 0 else x_ref[0, r * gm - SUBLANES:r * gm, :]
        after = xn_ref[0] if r == n_groups - 1 else x_ref[0, (r + 1) * gm:(r + 1) * gm + SUBLANES, :]
        keep_prev = (jnp.where(i == 0, 0.0, 1.0) if r == 0 else 1.0) if latent else 0.0
        keep_next = (jnp.where(i == n_tiles - 1, 0.0, 1.0) if r == n_groups - 1 else 1.0) if latent else 0.0
        xt = jnp.concatenate([x_ref[0, rows, :], before, after], axis=0)
        ms = jnp.mean(xt * xt, axis=-1, keepdims=True)
        hb = (xt * lax.rsqrt(ms + EPS) * scale + shift).astype(BF16)
        hm = hb[:gm]
        yield

        c = ATT_WIDTH
        if latent:
            r_q = project(0, ATT_WIDTH, hm)
        r_kg = project(c, ATT_KV_WIDTH + LANES, hm)
        c += ATT_KV_WIDTH + LANES
        yield

        if latent:
            cos, sl, shi = cos_ref[rows, :], sl_ref[rows, :], shi_ref[rows, :]
            for g in range(ATT_WIDTH // LANES):
                q_ref[0, rows, g * LANES:(g + 1) * LANES] = _rope(
                    r_q[:, g * LANES:(g + 1) * LANES], cos, sl, shi).astype(BF16)
        y = project(c, 2 * ML_QK_WIDTH, hb)
        c += 2 * ML_QK_WIDTH
        yield

        if latent:
            k_ref[0, rows, :] = _rope(r_kg[:, :LANES], cos, sl, shi).astype(BF16)
        else:
            k_ref[0, rows, :] = r_kg[:, :LANES].astype(BF16)
        gcol_ref[0, rows, :] = r_kg[:, LANES:] + bgr_ref[...]
        v_t = _dot_nt(wvt_ref[...], hm)
        yield

        ym = y[:gm]
        prev = jnp.where(row == 0, y[gm + SUBLANES - 1:gm + SUBLANES] * keep_prev,
                         pltpu.roll(ym, 1, axis=0))
        nxt = jnp.where(row == gm - 1, y[gm + SUBLANES:gm + SUBLANES + 1] * keep_next,
                        pltpu.roll(ym, gm - 1, axis=0))
        act = _silu(prev * wc[0:1] + ym * wc[1:2] + nxt * wc[2:3])
        qk_ref[0, rows, :ML_QK_WIDTH] = (act[:, :ML_QK_WIDTH] * (ML_QK_DIM ** -0.5)).astype(BF16)
        qk_ref[0, rows, ML_QK_WIDTH:] = act[:, ML_QK_WIDTH:].astype(BF16)
        if latent:
            r_o = project(c, ML_WIDTH, hm)
        yield

        g_t = v_t[ATT_KV_WIDTH + ML_WIDTH:] + bgc_ref[...]
        vt_ref[0, :, rows] = v_t[:ATT_KV_WIDTH].astype(BF16)
        for j in range(gm // ML_CHUNK):
            cols = slice(j * ML_CHUNK, (j + 1) * ML_CHUNK)
            mvt_ref[0, r * (gm // ML_CHUNK) + j] = v_t[ATT_KV_WIDTH:ATT_KV_WIDTH + ML_WIDTH, cols].astype(BF16)
            grow_ref[0, r * (gm // ML_CHUNK) + j] = g_t[:, cols]
        if latent:
            o_ref[0, rows, :] = (jax.nn.sigmoid(r_o) * gain_ref[...]).astype(BF16)

    done = object()
    waiting = [group(r) for r in range(n_groups)]
    active = []
    while waiting or active:
        if waiting:
            active.append(waiting.pop(0))
        active = [g for g in active if next(g, done) is not done]


def _inproj(x, mod, mod_row, g_pre, w_main, wvt, bg_row, bg_col, wc, rope_tabs, gain, *, tm, gm, latent):
    bsz, t, d = x.shape
    n_tiles = t // tm
    hb = tm // SUBLANES
    n_hblk = t // SUBLANES
    n = w_main.shape[1]
    mod_blk = (lambda b: b // SUBLANES) if mod_row is None else (lambda b: mod_row // SUBLANES)

    def const(shape):
        return pl.BlockSpec(shape, lambda i, b: (0,) * len(shape))

    in_specs = [
        pl.BlockSpec((1, tm, d), lambda i, b: (b, i, 0)),
        pl.BlockSpec((1, SUBLANES, d), lambda i, b: (b, jnp.maximum(i * hb - 1, 0), 0)),
        pl.BlockSpec((1, SUBLANES, d), lambda i, b: (b, jnp.minimum((i + 1) * hb, n_hblk - 1), 0)),
        pl.BlockSpec((SUBLANES, d), lambda i, b: (mod_blk(b), 0)),
        pl.BlockSpec((SUBLANES, d), lambda i, b: (mod_blk(b), 1)),
        const((1, d)), const((d, n)), const((ATT_KV_WIDTH + ML_WIDTH + ML_GATES, d)),
        const((1, LANES)), const((ML_GATES, 1)), const((3, 2 * ML_QK_WIDTH)),
    ]
    args = [x, x, x, mod, mod, g_pre, w_main, wvt, bg_row, bg_col, wc]
    tok = lambda width, dt: (pl.BlockSpec((1, tm, width), lambda i, b: (b, i, 0)),
                             jax.ShapeDtypeStruct((bsz, t, width), dt))
    chunked = lambda rows, dt: (pl.BlockSpec((1, tm // ML_CHUNK, rows, ML_CHUNK), lambda i, b: (b, i, 0, 0)),
                                jax.ShapeDtypeStruct((bsz, t // ML_CHUNK, rows, ML_CHUNK), dt))
    outs = []
    if latent:
        in_specs += [pl.BlockSpec((tm, LANES), lambda i, b: (i, 0))] * 3 + [const((1, ML_WIDTH))]
        args += list(rope_tabs) + [gain]
        outs.append(tok(ATT_WIDTH, BF16))
    outs.append(tok(ATT_KV_WIDTH, BF16))
    outs.append((pl.BlockSpec((1, LANES, tm), lambda i, b: (b, 0, i)),
                 jax.ShapeDtypeStruct((bsz, ATT_KV_WIDTH, t), BF16)))
    outs.append(tok(2 * ML_QK_WIDTH, BF16))
    outs.append(chunked(ML_WIDTH, BF16))
    if latent:
        outs.append(tok(ML_WIDTH, BF16))
    outs.append(tok(LANES, F32))
    outs.append(chunked(ML_GATES, F32))
    return pl.pallas_call(
        functools.partial(_inproj_kernel, tm=tm, gm=gm, n_tiles=n_tiles, latent=latent, mod_row=mod_row),
        grid=(n_tiles, bsz),
        in_specs=in_specs,
        out_specs=[o[0] for o in outs],
        out_shape=[o[1] for o in outs],
        compiler_params=pltpu.CompilerParams(dimension_semantics=("arbitrary", "arbitrary"),
                                             vmem_limit_bytes=VMEM_LIMIT),
        name="inproj_latent" if latent else "inproj_context",
    )(*args)


def _attn_kernel(q_ref, k_ref, kx_ref, vt_ref, vx_ref, sink_ref, o_ref):
    blk = ATT_BLOCK
    lane = lax.broadcasted_iota(jnp.int32, (blk, LANES), 1)
    zero = jnp.zeros((blk, LANES), BF16)
    half_groups = ATT_UNIT_GROUPS
    n_slots = ATT_KV_HEADS * half_groups
    n_parts = ATT_GROUP // half_groups
    sink = [sink_ref[:, part * n_slots * blk:(part + 1) * n_slots * blk] for part in range(n_parts)]

    def stack_heads(q, half):
        parts = []
        for g in range(half * half_groups, (half + 1) * half_groups):
            slab = q[:, g * LANES:(g + 1) * LANES]
            parts.append(jnp.where(lane < ATT_HEAD_DIM, slab, zero))
            parts.append(jnp.where(lane >= ATT_HEAD_DIM, slab, zero))
        return jnp.concatenate(parts, axis=0)

    key = lax.broadcasted_iota(jnp.int32, (blk, blk), 0)
    qry = lax.broadcasted_iota(jnp.int32, (blk, blk), 1)
    ninf = jnp.full((blk, blk), -jnp.inf, F32)
    bias_prev = jnp.where(key >= qry, 0.0, ninf)
    bias_next = jnp.where(key <= qry, 0.0, ninf)
    slots = lambda b: jnp.concatenate([b] * n_slots, axis=1)

    def scores(qs, k_prev, k_cur, k_next):
        return [_dot_nt(k_prev, qs), _dot_nt(k_cur, qs), _dot_nt(k_next, qs), _dot_nt(kx_ref[0], qs)]

    def softmax(s, b_prev, b_next, sink_h):
        s = jnp.concatenate([s[0] + slots(b_prev), s[1], s[2] + slots(b_next), s[3]], axis=0)
        m = jnp.maximum(sink_h, jnp.max(s, axis=0, keepdims=True))
        return jnp.exp2(s - m).astype(BF16), jnp.exp2(sink_h - m)

    n_keys = 3 * blk + kx_ref.shape[1]
    ones_rows = jnp.ones((ATT_ONES_ROWS, n_keys), BF16)

    def weighted_values(p, v_prev, v_cur, v_next):
        vt = jnp.concatenate([v_prev, v_cur, v_next, vx_ref[0]], axis=1)
        return _dot(jnp.concatenate([vt, ones_rows], axis=0), p)

    dim = lax.broadcasted_iota(jnp.int32, (LANES, blk), 0)

    def emit(rows, half, ot, p_sink):
        ot = ot[:LANES] * (1.0 / (ot[LANES:LANES + 1] + p_sink))
        for j in range(half_groups):
            g = half * half_groups + j
            a = ot[:, (2 * j) * blk:(2 * j + 1) * blk]
            b = ot[:, (2 * j + 1) * blk:(2 * j + 2) * blk]
            o_ref[0, rows, g * LANES:(g + 1) * LANES] = jnp.where(dim < ATT_HEAD_DIM, a, b).T.astype(BF16)

    nblk = q_ref.shape[1] // blk
    rows_of = [slice(b * blk, (b + 1) * blk) for b in range(nblk)]
    k_blocks = [k_ref[0, r, :] for r in rows_of]
    v_blocks = [vt_ref[0, :, r] for r in rows_of]
    near = lambda blocks, b: (blocks[max(b - 1, 0)], blocks[b], blocks[min(b + 1, nblk - 1)])
    bias = [(ninf if b == 0 else bias_prev, ninf if b == nblk - 1 else bias_next) for b in range(nblk)]
    units = [(b, half) for b in range(nblk) for half in range(n_parts)]

    def stage_scores(u):
        b, half = u
        return scores(stack_heads(q_ref[0, rows_of[b], :], half), *near(k_blocks, b))

    def stage_softmax(u, s):
        b, half = u
        return softmax(s, *bias[b], sink[half])

    def stage_values(u, p):
        return weighted_values(p, *near(v_blocks, u[0]))

    def stage_emit(u, ot, p_sink):
        emit(rows_of[u[0]], u[1], ot, p_sink)

    n_units = len(units)
    s, p, ot = {}, {}, {}
    s[0] = stage_scores(units[0])
    for t in range(n_units):
        if t + 1 < n_units:
            s[t + 1] = stage_scores(units[t + 1])
        p[t] = stage_softmax(units[t], s.pop(t))
        if t >= 1:
            ot[t - 1] = stage_values(units[t - 1], p[t - 1][0])
        if t >= 2:
            stage_emit(units[t - 2], ot.pop(t - 2), p.pop(t - 2)[1])
    ot[n_units - 1] = stage_values(units[n_units - 1], p[n_units - 1][0])
    for t in (n_units - 2, n_units - 1):
        stage_emit(units[t], ot.pop(t), p.pop(t)[1])


def _attention(q, k, vt, kx, vxt, sink_row):
    bsz, s, _ = q.shape
    l = kx.shape[1]
    per_b = lambda a: pl.BlockSpec((1,) + a.shape[1:], lambda b: (b, 0, 0))
    vx_spec = pl.BlockSpec((1, ATT_KV_WIDTH, l), lambda b: (b // CTX_PER_STEP, 0, b % CTX_PER_STEP))
    return pl.pallas_call(
        _attn_kernel,
        grid=(bsz,),
        in_specs=[per_b(q), per_b(k), per_b(kx), per_b(vt), vx_spec,
                  pl.BlockSpec(sink_row.shape, lambda b: (0, 0))],
        out_specs=pl.BlockSpec((1, s, ATT_WIDTH), lambda b: (b, 0, 0)),
        out_shape=jax.ShapeDtypeStruct((bsz, s, ATT_WIDTH), BF16),
        compiler_params=pltpu.CompilerParams(dimension_semantics=("arbitrary",),
                                             vmem_limit_bytes=VMEM_LIMIT),
        name="attention",
    )(q, k, kx, vt, vxt, sink_row)


BF16_SUBLANES = 16
ML_STATE_ROWS = ML_V_DIM + BF16_SUBLANES


def _mlstm_kernel(qkx_ref, vtx_ref, ox_ref, gcx_ref, grx_ref, qkc_ref, vtc_ref, grc_ref,
                  out_ref, sin_ref, st_ref, ucol_ref, mrun_ref, wint_ref, floor_ref,
                  wkey_ref, decay_ref, *, ncx, ncc):
    lc = ML_CHUNK
    qkw = ML_QK_WIDTH
    sr = ML_STATE_ROWS
    ng = ML_GATES
    hg = ML_GATES // 2
    st_ref[...] = jnp.zeros(st_ref.shape, F32)

    rr = lax.broadcasted_iota(jnp.int32, (lc, lc), 0)
    cc = lax.broadcasted_iota(jnp.int32, (lc, lc), 1)
    tril = rr >= cc
    triu = rr <= cc
    tril_b = jnp.where(tril, 1.0, 0.0).astype(BF16)
    triu_b = jnp.where(triu, 1.0, 0.0).astype(BF16)
    head_of_lane = lax.broadcasted_iota(jnp.int32, (lc, qkw), 1) // ML_QK_DIM
    head_of_state_lane = lax.broadcasted_iota(jnp.int32, (1, qkw), 1) // ML_QK_DIM
    pad_rows = jnp.zeros((BF16_SUBLANES - 1, lc), F32)

    def split(v):
        hi = v.astype(BF16)
        return hi, (v - hi.astype(F32)).astype(BF16)

    def gate_rows(g):
        n16 = g.shape[0]
        fwd_row = (lax.broadcasted_iota(jnp.int32, g.shape, 0) & (ng - 1)) < hg
        lane = lax.broadcasted_iota(jnp.int32, g.shape, 1)
        hi, lo = split(_log_sigmoid(g))
        cat = jnp.concatenate([hi, lo], axis=0)
        bu = _dot(cat, triu_b)
        bl = _dot(cat, tril_b)
        b = jnp.where(fwd_row, bu[:n16] + bu[n16:], bl[:n16] + bl[n16:])
        b = pltpu.roll(b, n16 - ML_HEADS, axis=0)
        u = g - b
        run_f = run_b = u
        k = 1
        while k < lc:
            run_f = jnp.maximum(run_f, jnp.where(lane >= k, pltpu.roll(run_f, k, axis=1), -jnp.inf))
            run_b = jnp.maximum(run_b, jnp.where(lane < lc - k, pltpu.roll(run_b, lc - k, axis=1), -jnp.inf))
            k *= 2
        run = jnp.where(fwd_row, run_f, run_b)

        def at_end(a):
            return jnp.where(fwd_row, jnp.broadcast_to(a[:, lc - 1:lc], a.shape),
                             jnp.broadcast_to(a[:, 0:1], a.shape))

        return u, b, run, at_end(b), at_end(run)

    def derived(u, b, run, b_tot, run_end, m_in):
        m_run = jnp.maximum(run, m_in)
        m_out = b_tot + jnp.maximum(run_end, m_in)
        return (m_run * LOG2_E, jnp.exp(m_in - m_run), jnp.exp(-(b + m_run)),
                jnp.exp(b_tot + u - m_out), jnp.exp(b_tot + m_in - m_out))

    gc_rows = gate_rows(grc_ref[0].reshape(ncc * ng, lc))
    gx_rows = gate_rows(grx_ref[0].reshape(ncx * ng, lc))

    def scan_m(rows, n, m_f, m_b):
        _, _, _, b_tot, run_end = rows
        part = lambda a, c, d: a[c * ng + d * hg:c * ng + (d + 1) * hg]
        ins_f, ins_b = [], [None] * n
        for c in range(n):
            ins_f.append(m_f)
            m_f = part(b_tot, c, 0) + jnp.maximum(part(run_end, c, 0), m_f)
        for c in reversed(range(n)):
            ins_b[c] = m_b
            m_b = part(b_tot, c, 1) + jnp.maximum(part(run_end, c, 1), m_b)
        return jnp.concatenate([x for c in range(n) for x in (ins_f[c], ins_b[c])], axis=0), m_f, m_b

    m0 = jnp.zeros((hg, lc), F32)
    m_in_c, m_f, m_b = scan_m(gc_rows, ncc, m0, m0)
    m_in_x, _, _ = scan_m(gx_rows, ncx, m_f, m_b)
    _, _, _, wkey_c, decay_c = derived(*gc_rows, m_in_c)
    for ref, val in zip((mrun_ref, wint_ref, floor_ref, wkey_ref, decay_ref), derived(*gx_rows, m_in_x)):
        ref[...] = val

    fwd_col = (lax.broadcasted_iota(jnp.int32, (lc, LANES), 1) & (ng - 1)) < hg

    def token_major_u(j):
        rows = pl.ds(pl.multiple_of(j * lc, lc), lc)
        gcol = gcx_ref[0, rows, :]
        hi, lo = split(_log_sigmoid(gcol))
        cat = jnp.concatenate([hi, lo], axis=1)
        bl = _dot(tril_b, cat)
        bu = _dot(triu_b, cat)
        b = jnp.where(fwd_col, bl[:, :LANES] + bl[:, LANES:], bu[:, :LANES] + bu[:, LANES:])
        ucol_ref[rows, :] = (gcol - pltpu.roll(b, LANES - ML_HEADS, axis=1)) * LOG2_E

    def advance(dirn, k4, vt, w_key, decay):
        st = st_ref[dirn]
        lhs, rhs, decay_row = [], [], None
        for h in range(ML_HEADS):
            c = hg * dirn + h
            lhs.append(jnp.concatenate([vt[h * ML_V_DIM:(h + 1) * ML_V_DIM].astype(F32) * w_key[c:c + 1],
                                        w_key[c:c + 1], pad_rows], axis=0).astype(BF16))
            rhs.append(jnp.where(head_of_lane == h, k4, jnp.zeros_like(k4)))
            d_h = jnp.concatenate([decay[c:c + 1]] * (qkw // lc), axis=1)
            decay_row = d_h if decay_row is None else jnp.where(head_of_state_lane == h, d_h, decay_row)
        upd = _dot(jnp.concatenate(lhs, axis=1), jnp.concatenate(rhs, axis=0))
        st_ref[dirn] = decay_row * st + upd

    for j in range(ncc):
        for dirn, cj in ((0, j), (1, ncc - 1 - j)):
            grows = slice(cj * ng, (cj + 1) * ng)
            advance(dirn, qkc_ref[0, cj * lc:(cj + 1) * lc, qkw:], vtc_ref[0, cj], wkey_c[grows], decay_c[grows])

    def scan_body(j, carry):
        for dirn, cj in ((0, j), (1, ncx - 1 - j)):
            rows = pl.ds(pl.multiple_of(cj * lc, lc), lc)
            grows = pl.ds(pl.multiple_of(cj * ng, ng), ng)
            sin_ref[dirn, cj] = st_ref[dirn].astype(BF16)
            advance(dirn, qkx_ref[0, rows, qkw:], vtx_ref[0, cj], wkey_ref[grows, :], decay_ref[grows, :])
        token_major_u(j)
        return carry

    lax.fori_loop(0, ncx, scan_body, 0, unroll=8)


    def out_body(j, carry):
        rows = pl.ds(pl.multiple_of(j * lc, lc), lc)
        grows = pl.ds(pl.multiple_of(j * ng, ng), ng)
        q4 = qkx_ref[0, rows, :qkw]
        k4 = qkx_ref[0, rows, qkw:]
        vt = vtx_ref[0, j]
        u_col = ucol_ref[rows, :]
        m_run, w_int, floor = mrun_ref[grows, :], wint_ref[grows, :], floor_ref[grows, :]
        zero = jnp.zeros_like(q4)
        qs = jnp.concatenate([jnp.where(head_of_lane == h, q4, zero) for h in range(ML_HEADS)], axis=0)
        both = _dot_nt(jnp.concatenate([k4, sin_ref[0, j], sin_ref[1, j]], axis=0), qs)
        qk_t = both[:lc]
        for h in range(ML_HEADS):
            cols_h = slice(h * lc, (h + 1) * lc)
            hs = None
            for dirn in range(2):
                c = hg * dirn + h
                inter = both[lc + dirn * sr:lc + (dirn + 1) * sr, cols_h]
                valid = triu if dirn == 0 else tril
                e = jnp.exp2(jnp.where(valid, u_col[:, c:c + 1] - m_run[c:c + 1], -jnp.inf))
                s_t = qk_t[:, cols_h] * e
                num = _dot(vt[h * ML_V_DIM:(h + 1) * ML_V_DIM], s_t.astype(BF16))
                num = num + w_int[c:c + 1] * inter[:ML_V_DIM]
                den = jnp.sum(s_t, axis=0, keepdims=True) + w_int[c:c + 1] * inter[ML_V_DIM:ML_V_DIM + 1]
                hv = num * (1.0 / jnp.maximum(jnp.abs(den), floor[c:c + 1]))
                hs = hv if hs is None else hs + hv
            cols = slice(h * ML_V_DIM, (h + 1) * ML_V_DIM)
            hn = hs * lax.rsqrt(jnp.mean(hs * hs, axis=0, keepdims=True) + EPS)
            out_ref[0, rows, cols] = (hn.T * ox_ref[0, rows, cols].astype(F32)).astype(BF16)
        return carry

    lax.fori_loop(0, ncx, out_body, 0, unroll=8)


def _mlstm(qkx, vtx, ox, gcx, grx, qkc, vtc, grc):
    bsz, s, _ = qkx.shape
    ncx, ncc = s // ML_CHUNK, qkc.shape[1] // ML_CHUNK
    per_b = lambda a: pl.BlockSpec((1,) + a.shape[1:], lambda b: (b,) + (0,) * (a.ndim - 1))
    ins = [qkx, vtx, ox, gcx, grx, qkc, vtc, grc]
    return pl.pallas_call(
        functools.partial(_mlstm_kernel, ncx=ncx, ncc=ncc),
        grid=(bsz,),
        in_specs=[per_b(a) for a in ins],
        out_specs=pl.BlockSpec((1, s, ML_WIDTH), lambda b: (b, 0, 0)),
        out_shape=jax.ShapeDtypeStruct((bsz, s, ML_WIDTH), BF16),
        scratch_shapes=[pltpu.VMEM((2, ncx, ML_STATE_ROWS, ML_QK_WIDTH), BF16),
                        pltpu.VMEM((2, ML_STATE_ROWS, ML_QK_WIDTH), F32),
                        pltpu.VMEM((s, LANES), F32)]
                       + [pltpu.VMEM((ncx * ML_GATES, ML_CHUNK), F32)] * 5,
        compiler_params=pltpu.CompilerParams(dimension_semantics=("arbitrary",),
                                             vmem_limit_bytes=VMEM_LIMIT),
        name="mlstm",
    )(*ins)


def _out_ffn_kernel(x_ref, att_ref, ml_ref, gtm_ref, shf_ref, scf_ref, gtf_ref,
                    gpm_ref, gpf_ref, gqf_ref, woa_ref, wom_ref, wfi_ref, wfo_ref, o_ref, *, hidden,
                    tiles_per_batch):
    r_mod = pl.ds((pl.program_id(0) // tiles_per_batch) % SUBLANES, 1)
    gtm, shf, scf, gtf = gtm_ref[r_mod, :], shf_ref[r_mod, :], scf_ref[r_mod, :], gtf_ref[r_mod, :]

    def group(g):
        r = slice(g * FFN_ROWS, (g + 1) * FFN_ROWS)
        mix = _dot(att_ref[r, :], woa_ref[...]) + _dot(ml_ref[r, :], wom_ref[...])
        yield
        x1 = x_ref[r, :] + gtm * _rms(mix, gpm_ref[...])
        h = (_rms(x1, gpf_ref[...]) * (1.0 + scf) + shf).astype(BF16)
        gu = _dot(h, wfi_ref[...])
        yield
        act = (_silu(gu[:, :hidden]) * gu[:, hidden:]).astype(BF16)
        fx = _dot(act, wfo_ref[...])
        yield
        o_ref[r, :] = x1 + gtf * _rms(fx, gqf_ref[...])

    done = object()
    waiting = [group(g) for g in range(x_ref.shape[0] // FFN_ROWS)]
    active, rounds = [], 0
    while waiting or active:
        if rounds % FFN_PAIR_PERIOD == 0:
            active += [waiting.pop(0) for _ in range(min(2, len(waiting)))]
        active = [g for g in active if next(g, done) is not done]
        rounds += 1


def _out_ffn(x2, att2, ml2, mod, g_post_mix, g_pre_ffn, g_post_ffn, woa, wom, wfi, wfo, *, tiles_per_batch):
    t, d = x2.shape
    tm = FFN_TM
    hidden = wfo.shape[0]
    resident = lambda a: pl.BlockSpec(a.shape, lambda i: (0,) * a.ndim, pipeline_mode=pl.Buffered(1))
    mod_spec = lambda k: pl.BlockSpec((SUBLANES, d), lambda i: (i // tiles_per_batch // SUBLANES, k))
    row = pl.BlockSpec((1, d), lambda i: (0, 0))
    return pl.pallas_call(
        functools.partial(_out_ffn_kernel, hidden=hidden, tiles_per_batch=tiles_per_batch),
        grid=(t // tm,),
        in_specs=[pl.BlockSpec((tm, d), lambda i: (i, 0)),
                  pl.BlockSpec((tm, ATT_WIDTH), lambda i: (i, 0)),
                  pl.BlockSpec((tm, ML_WIDTH), lambda i: (i, 0)),
                  mod_spec(2), mod_spec(3), mod_spec(4), mod_spec(5), row, row, row,
                  resident(woa), resident(wom), resident(wfi), resident(wfo)],
        out_specs=pl.BlockSpec((tm, d), lambda i: (i, 0)),
        out_shape=jax.ShapeDtypeStruct((t, d), F32),
        compiler_params=pltpu.CompilerParams(dimension_semantics=("arbitrary",),
                                             vmem_limit_bytes=VMEM_LIMIT),
        name="out_ffn",
    )(x2, att2, ml2, mod, mod, mod, mod, g_post_mix, g_pre_ffn, g_post_ffn, woa, wom, wfi, wfo)


def _rope_tables(n_tokens):
    pos = jnp.arange(n_tokens)
    row = (pos // GRID_W).astype(F32)
    col = (pos % GRID_W).astype(F32)
    inv_freq = jnp.power(ROPE_BASE, -jnp.arange(ROPE_PAIR, dtype=F32) / ROPE_PAIR)
    ang_r = row[:, None] * inv_freq
    ang_c = col[:, None] * inv_freq
    z = jnp.zeros_like(ang_r)
    reps = LANES // ATT_HEAD_DIM
    cos = jnp.tile(jnp.concatenate([jnp.cos(ang_r)] * 2 + [jnp.cos(ang_c)] * 2, axis=1), (1, reps))
    sin_lo = jnp.tile(jnp.concatenate([-jnp.sin(ang_r), z, -jnp.sin(ang_c), z], axis=1), (1, reps))
    sin_hi = jnp.tile(jnp.concatenate([z, jnp.sin(ang_r), z, jnp.sin(ang_c)], axis=1), (1, reps))
    return cos, sin_lo, sin_hi


def _permute_heads(w, axis):
    shape = w.shape
    grouped = shape[:axis] + (ATT_KV_HEADS, ATT_GROUP, ATT_HEAD_DIM) + shape[axis + 1:]
    return jnp.swapaxes(w.reshape(grouped), axis, axis + 1).reshape(shape)


def kernel(x, c, ctx, c_ctx, w_ada, b_ada, g_pre_mix, w_in, w_conv_qk, b_gates, attn_sink,
           g_mlstm_out, w_out, g_post_mix, g_pre_ffn, w_ffn_in, w_ffn_out, g_post_ffn):
    bsz, s, d = x.shape
    l = ctx.shape[1]
    assert w_ada.shape[0] == 1, "single-layer block"
    assert bsz < MOD_ROWS and s % INPROJ_TM == 0 and l % ML_CHUNK == 0 and s % FFN_TM == 0
    assert bsz % CTX_PER_STEP == 0 and l % LANES == 0

    mod = _ada(c, c_ctx, w_ada[0], b_ada)

    w = w_in[0]
    o_q, o_k, o_v = 0, ATT_WIDTH, ATT_WIDTH + ATT_KV_WIDTH
    o_mq = o_v + ATT_KV_WIDTH
    o_mv = o_mq + 2 * ML_QK_WIDTH
    o_mo = o_mv + ML_WIDTH
    o_mg = o_mo + ML_WIDTH
    w_q = _permute_heads(w[:, o_q:o_k], 1) * (ATT_HEAD_DIM ** -0.5 * LOG2_E)
    w_g = jnp.pad(w[:, o_mg:], ((0, 0), (0, LANES - ML_GATES)))
    shared = [w[:, o_k:o_v], w_g, w[:, o_mq:o_mv]]
    w_lat = jnp.concatenate([w_q] + shared + [w[:, o_mo:o_mg]], axis=1).astype(BF16)
    wvt = jnp.concatenate([w[:, o_v:o_mq], w[:, o_mv:o_mo], w[:, o_mg:]], axis=1).T.astype(BF16)
    bg_row = jnp.pad(b_gates, ((0, 0), (0, LANES - ML_GATES)))
    bg_col = b_gates.reshape(ML_GATES, 1)
    wc = w_conv_qk[0]

    q, k, vt, qkx, vtx, ox, gcx, grx = _inproj(
        x, mod, None, g_pre_mix, w_lat, wvt, bg_row, bg_col, wc, _rope_tables(s), g_mlstm_out,
        tm=INPROJ_TM, gm=INPROJ_ROWS, latent=True)
    nctx = bsz // CTX_PER_STEP
    kc, vct, qkc, vtc, _, grc = _inproj(
        ctx.reshape(nctx, CTX_PER_STEP * l, d), mod, bsz, g_pre_mix, w_lat, wvt, bg_row, bg_col, wc, None, None,
        tm=CTX_PER_STEP * l, gm=l, latent=False)
    kc = kc.reshape(bsz, l, ATT_KV_WIDTH)
    qkc = qkc.reshape(bsz, l, 2 * ML_QK_WIDTH)
    vtc = vtc.reshape((bsz, l // ML_CHUNK) + vtc.shape[2:])
    grc = grc.reshape((bsz, l // ML_CHUNK) + grc.shape[2:])

    sink_row = jnp.repeat(attn_sink[0][jnp.array(_HEAD_PERM)] * LOG2_E, ATT_BLOCK)[None, :]
    att = _attention(q, k, vt, kc, vct, sink_row)
    ml = _mlstm(qkx, vtx, ox, gcx, grx, qkc, vtc, grc)

    wo = w_out[0]
    woa = _permute_heads(wo[:ATT_WIDTH], 0).astype(BF16)
    wom = wo[ATT_WIDTH:].astype(BF16)
    out = _out_ffn(x.reshape(bsz * s, d), att.reshape(bsz * s, ATT_WIDTH), ml.reshape(bsz * s, ML_WIDTH),
                   mod, g_post_mix, g_pre_ffn, g_post_ffn, woa, wom,
                   w_ffn_in[0].astype(BF16), w_ffn_out[0].astype(BF16), tiles_per_batch=s // FFN_TM)
    return out.reshape(bsz, s, d)
```

```python
import functools

import jax
import jax.numpy as jnp
from jax import lax
from jax.experimental import pallas as pl
from jax.experimental.pallas import tpu as pltpu

F32 = jnp.float32
BF16 = jnp.bfloat16

EPS = 1e-6
GRID_W = 64
ROPE_BASE = 10000.0
LOG2_E = 1.4426950408889634

ATT_HEADS = 8
ATT_KV_HEADS = 2
ATT_GROUP = ATT_HEADS // ATT_KV_HEADS
ATT_HEAD_DIM = 64
ROPE_PAIR = ATT_HEAD_DIM // 4
ATT_BLOCK = 128
ATT_WIDTH = ATT_HEADS * ATT_HEAD_DIM
ATT_KV_WIDTH = ATT_KV_HEADS * ATT_HEAD_DIM
ATT_UNIT_GROUPS = 2
ATT_ONES_ROWS = 16

ML_HEADS = 4
ML_V_DIM = 128
ML_QK_DIM = 64
ML_WIDTH = ML_HEADS * ML_V_DIM
ML_QK_WIDTH = ML_HEADS * ML_QK_DIM
ML_GATES = 4 * ML_HEADS
ML_CHUNK = 128

LANES = 128
SUBLANES = 8
VMEM_LIMIT = 56 * 1024 * 1024

INPROJ_TM = 1024
INPROJ_ROWS = 256
CTX_PER_STEP = 2
FFN_TM = 1024
FFN_ROWS = 256
FFN_PAIR_PERIOD = 3
ADA_TN = 1536
MOD_ROWS = 16

_HEAD_PERM = tuple(h * ATT_GROUP + g for g in range(ATT_GROUP) for h in range(ATT_KV_HEADS))


def _silu(v):
    return v * jax.nn.sigmoid(v)


def _log_sigmoid(v):
    return jnp.minimum(v, 0.0) - jnp.log1p(jnp.exp(-jnp.abs(v)))


def _rms(v, g):
    return v * lax.rsqrt(jnp.mean(v * v, axis=-1, keepdims=True) + EPS) * g


def _dot(a, b):
    return jnp.dot(a, b, preferred_element_type=F32)


def _dot_nt(a, b):
    return lax.dot_general(a, b, (((1,), (1,)), ((), ())), preferred_element_type=F32)


def _ada_kernel(c_ref, cctx_ref, w_ref, b_ref, o_ref):
    pad = jnp.zeros((MOD_ROWS - c_ref.shape[0] - 1, c_ref.shape[1]), F32)
    a = _silu(jnp.concatenate([c_ref[...], cctx_ref[...], pad], axis=0))
    o_ref[...] = _dot(a.astype(BF16), w_ref[...].astype(BF16)) + b_ref[...]


def _ada(c, c_ctx, w, b):
    d, n = w.shape
    return pl.pallas_call(
        _ada_kernel,
        grid=(n // ADA_TN,),
        in_specs=[pl.BlockSpec(c.shape, lambda j: (0, 0)),
                  pl.BlockSpec((1, d), lambda j: (0, 0)),
                  pl.BlockSpec((d, ADA_TN), lambda j: (0, j)),
                  pl.BlockSpec((1, ADA_TN), lambda j: (0, j))],
        out_specs=pl.BlockSpec((MOD_ROWS, ADA_TN), lambda j: (0, j)),
        out_shape=jax.ShapeDtypeStruct((MOD_ROWS, n), F32),
        compiler_params=pltpu.CompilerParams(dimension_semantics=("arbitrary",),
                                             vmem_limit_bytes=VMEM_LIMIT),
        name="ada",
    )(c, c_ctx.reshape(1, d), w, b)


def _rope(v, cos, sin_lo, sin_hi):
    return (v * cos + pltpu.roll(v, LANES - ROPE_PAIR, axis=1) * sin_lo
            + pltpu.roll(v, ROPE_PAIR, axis=1) * sin_hi)


def _inproj_kernel(*refs, tm, gm, n_tiles, latent, mod_row):
    if latent:
        (x_ref, xp_ref, xn_ref, sh_ref, sc_ref, g_ref, w_ref, wvt_ref, bgr_ref, bgc_ref,
         wc_ref, cos_ref, sl_ref, shi_ref, gain_ref,
         q_ref, k_ref, vt_ref, qk_ref, mvt_ref, o_ref, gcol_ref, grow_ref) = refs
    else:
        (x_ref, xp_ref, xn_ref, sh_ref, sc_ref, g_ref, w_ref, wvt_ref, bgr_ref, bgc_ref,
         wc_ref,
         k_ref, vt_ref, qk_ref, mvt_ref, gcol_ref, grow_ref) = refs
    i = pl.program_id(0)
    n_groups = tm // gm
    r_mod = (pl.program_id(1) if mod_row is None else mod_row) % SUBLANES
    scale = g_ref[...] * (1.0 + sc_ref[pl.ds(r_mod, 1), :])
    shift = sh_ref[pl.ds(r_mod, 1), :]
    wc = wc_ref[...]
    row = lax.broadcasted_iota(jnp.int32, (gm, 1), 0)

    def project(lo, width, lhs):
        return _dot(lhs, w_ref[:, lo:lo + width])

    def group(r):
        rows = slice(r * gm, (r + 1) * gm)
        before = xp_ref[0] if r == 0 else x_ref[0, r * gm - SUBLANES:r * gm, :]
        after = xn_ref[0] if r == n_groups - 1 else x_ref[0, (r + 1) * gm:(r + 1) * gm + SUBLANES, :]
        keep_prev = (jnp.where(i == 0, 0.0, 1.0) if r == 0 else 1.0) if latent else 0.0
        keep_next = (jnp.where(i == n_tiles - 1, 0.0, 1.0) if r == n_groups - 1 else 1.0) if latent else 0.0
        xt = jnp.concatenate([x_ref[0, rows, :], before, after], axis=0)
        ms = jnp.mean(xt * xt, axis=-1, keepdims=True)
        hb = (xt * lax.rsqrt(ms + EPS) * scale + shift).astype(BF16)
        hm = hb[:gm]
        yield

        c = ATT_WIDTH
        if latent:
            r_q = project(0, ATT_WIDTH, hm)
        r_kg = project(c, ATT_KV_WIDTH + LANES, hm)
        c += ATT_KV_WIDTH + LANES
        yield

        if latent:
            cos, sl, shi = cos_ref[rows, :], sl_ref[rows, :], shi_ref[rows, :]
            for g in range(ATT_WIDTH // LANES):
                q_ref[0, rows, g * LANES:(g + 1) * LANES] = _rope(
                    r_q[:, g * LANES:(g + 1) * LANES], cos, sl, shi).astype(BF16)
        y = project(c, 2 * ML_QK_WIDTH, hb)
        c += 2 * ML_QK_WIDTH
        yield

        if latent:
            k_ref[0, rows, :] = _rope(r_kg[:, :LANES], cos, sl, shi).astype(BF16)
        else:
            k_ref[0, rows, :] = r_kg[:, :LANES].astype(BF16)
        gcol_ref[0, rows, :] = r_kg[:, LANES:] + bgr_ref[...]
        v_t = _dot_nt(wvt_ref[...], hm)
        yield

        ym = y[:gm]
        prev = jnp.where(row == 0, y[gm + SUBLANES - 1:gm + SUBLANES] * keep_prev,
                         pltpu.roll(ym, 1, axis=0))
        nxt = jnp.where(row == gm - 1, y[gm + SUBLANES:gm + SUBLANES + 1] * keep_next,
                        pltpu.roll(ym, gm - 1, axis=0))
        act = _silu(prev * wc[0:1] + ym * wc[1:2] + nxt * wc[2:3])
        qk_ref[0, rows, :ML_QK_WIDTH] = (act[:, :ML_QK_WIDTH] * (ML_QK_DIM ** -0.5)).astype(BF16)
        qk_ref[0, rows, ML_QK_WIDTH:] = act[:, ML_QK_WIDTH:].astype(BF16)
        if latent:
            r_o = project(c, ML_WIDTH, hm)
        yield

        g_t = v_t[ATT_KV_WIDTH + ML_WIDTH:] + bgc_ref[...]
        vt_ref[0, :, rows] = v_t[:ATT_KV_WIDTH].astype(BF16)
        for j in range(gm // ML_CHUNK):
            cols = slice(j * ML_CHUNK, (j + 1) * ML_CHUNK)
            mvt_ref[0, r * (gm // ML_CHUNK) + j] = v_t[ATT_KV_WIDTH:ATT_KV_WIDTH + ML_WIDTH, cols].astype(BF16)
            grow_ref[0, r * (gm // ML_CHUNK) + j] = g_t[:, cols]
        if latent:
            o_ref[0, rows, :] = (jax.nn.sigmoid(r_o) * gain_ref[...]).astype(BF16)

    done = object()
    waiting = [group(r) for r in range(n_groups)]
    active = []
    while waiting or active:
        if waiting:
            active.append(waiting.pop(0))
        active = [g for g in active if next(g, done) is not done]


def _inproj(x, mod, mod_row, g_pre, w_main, wvt, bg_row, bg_col, wc, rope_tabs, gain, *, tm, gm, latent):
    bsz, t, d = x.shape
    n_tiles = t // tm
    hb = tm // SUBLANES
    n_hblk = t // SUBLANES
    n = w_main.shape[1]
    mod_blk = (lambda b: b // SUBLANES) if mod_row is None else (lambda b: mod_row // SUBLANES)

    def const(shape):
        return pl.BlockSpec(shape, lambda i, b: (0,) * len(shape))

    in_specs = [
        pl.BlockSpec((1, tm, d), lambda i, b: (b, i, 0)),
        pl.BlockSpec((1, SUBLANES, d), lambda i, b: (b, jnp.maximum(i * hb - 1, 0), 0)),
        pl.BlockSpec((1, SUBLANES, d), lambda i, b: (b, jnp.minimum((i + 1) * hb, n_hblk - 1), 0)),
        pl.BlockSpec((SUBLANES, d), lambda i, b: (mod_blk(b), 0)),
        pl.BlockSpec((SUBLANES, d), lambda i, b: (mod_blk(b), 1)),
        const((1, d)), const((d, n)), const((ATT_KV_WIDTH + ML_WIDTH + ML_GATES, d)),
        const((1, LANES)), const((ML_GATES, 1)), const((3, 2 * ML_QK_WIDTH)),
    ]
    args = [x, x, x, mod, mod, g_pre, w_main, wvt, bg_row, bg_col, wc]
    tok = lambda width, dt: (pl.BlockSpec((1, tm, width), lambda i, b: (b, i, 0)),
                             jax.ShapeDtypeStruct((bsz, t, width), dt))
    chunked = lambda rows, dt: (pl.BlockSpec((1, tm // ML_CHUNK, rows, ML_CHUNK), lambda i, b: (b, i, 0, 0)),
                                jax.ShapeDtypeStruct((bsz, t // ML_CHUNK, rows, ML_CHUNK), dt))
    outs = []
    if latent:
        in_specs += [pl.BlockSpec((tm, LANES), lambda i, b: (i, 0))] * 3 + [const((1, ML_WIDTH))]
        args += list(rope_tabs) + [gain]
        outs.append(tok(ATT_WIDTH, BF16))
    outs.append(tok(ATT_KV_WIDTH, BF16))
    outs.append((pl.BlockSpec((1, LANES, tm), lambda i, b: (b, 0, i)),
                 jax.ShapeDtypeStruct((bsz, ATT_KV_WIDTH, t), BF16)))
    outs.append(tok(2 * ML_QK_WIDTH, BF16))
    outs.append(chunked(ML_WIDTH, BF16))
    if latent:
        outs.append(tok(ML_WIDTH, BF16))
    outs.append(tok(LANES, F32))
    outs.append(chunked(ML_GATES, F32))
    return pl.pallas_call(
        functools.partial(_inproj_kernel, tm=tm, gm=gm, n_tiles=n_tiles, latent=latent, mod_row=mod_row),
        grid=(n_tiles, bsz),
        in_specs=in_specs,
        out_specs=[o[0] for o in outs],
        out_shape=[o[1] for o in outs],
        compiler_params=pltpu.CompilerParams(dimension_semantics=("arbitrary", "arbitrary"),
                                             vmem_limit_bytes=VMEM_LIMIT),
        name="inproj_latent" if latent else "inproj_context",
    )(*args)


def _attn_kernel(q_ref, k_ref, kx_ref, vt_ref, vx_ref, sink_ref, o_ref):
    blk = ATT_BLOCK
    lane = lax.broadcasted_iota(jnp.int32, (blk, LANES), 1)
    zero = jnp.zeros((blk, LANES), BF16)
    half_groups = ATT_UNIT_GROUPS
    n_slots = ATT_KV_HEADS * half_groups
    n_parts = ATT_GROUP // half_groups
    sink = [sink_ref[:, part * n_slots * blk:(part + 1) * n_slots * blk] for part in range(n_parts)]

    def stack_heads(q, half):
        parts = []
        for g in range(half * half_groups, (half + 1) * half_groups):
            slab = q[:, g * LANES:(g + 1) * LANES]
            parts.append(jnp.where(lane < ATT_HEAD_DIM, slab, zero))
            parts.append(jnp.where(lane >= ATT_HEAD_DIM, slab, zero))
        return jnp.concatenate(parts, axis=0)

    key = lax.broadcasted_iota(jnp.int32, (blk, blk), 0)
    qry = lax.broadcasted_iota(jnp.int32, (blk, blk), 1)
    ninf = jnp.full((blk, blk), -jnp.inf, F32)
    bias_prev = jnp.where(key >= qry, 0.0, ninf)
    bias_next = jnp.where(key <= qry, 0.0, ninf)
    slots = lambda b: jnp.concatenate([b] * n_slots, axis=1)

    def scores(qs, k_prev, k_cur, k_next):
        return [_dot_nt(k_prev, qs), _dot_nt(k_cur, qs), _dot_nt(k_next, qs), _dot_nt(kx_ref[0], qs)]

    def softmax(s, b_prev, b_next, sink_h):
        s = jnp.concatenate([s[0] + slots(b_prev), s[1], s[2] + slots(b_next), s[3]], axis=0)
        m = jnp.maximum(sink_h, jnp.max(s, axis=0, keepdims=True))
        return jnp.exp2(s - m).astype(BF16), jnp.exp2(sink_h - m)

    n_keys = 3 * blk + kx_ref.shape[1]
    ones_rows = jnp.ones((ATT_ONES_ROWS, n_keys), BF16)

    def weighted_values(p, v_prev, v_cur, v_next):
        vt = jnp.concatenate([v_prev, v_cur, v_next, vx_ref[0]], axis=1)
        return _dot(jnp.concatenate([vt, ones_rows], axis=0), p)

    dim = lax.broadcasted_iota(jnp.int32, (LANES, blk), 0)

    def emit(rows, half, ot, p_sink):
        ot = ot[:LANES] * (1.0 / (ot[LANES:LANES + 1] + p_sink))
        for j in range(half_groups):
            g = half * half_groups + j
            a = ot[:, (2 * j) * blk:(2 * j + 1) * blk]
            b = ot[:, (2 * j + 1) * blk:(2 * j + 2) * blk]
            o_ref[0, rows, g * LANES:(g + 1) * LANES] = jnp.where(dim < ATT_HEAD_DIM, a, b).T.astype(BF16)

    nblk = q_ref.shape[1] // blk
    rows_of = [slice(b * blk, (b + 1) * blk) for b in range(nblk)]
    k_blocks = [k_ref[0, r, :] for r in rows_of]
    v_blocks = [vt_ref[0, :, r] for r in rows_of]
    near = lambda blocks, b: (blocks[max(b - 1, 0)], blocks[b], blocks[min(b + 1, nblk - 1)])
    bias = [(ninf if b == 0 else bias_prev, ninf if b == nblk - 1 else bias_next) for b in range(nblk)]
    units = [(b, half) for b in range(nblk) for half in range(n_parts)]

    def stage_scores(u):
        b, half = u
        return scores(stack_heads(q_ref[0, rows_of[b], :], half), *near(k_blocks, b))

    def stage_softmax(u, s):
        b, half = u
        return softmax(s, *bias[b], sink[half])

    def stage_values(u, p):
        return weighted_values(p, *near(v_blocks, u[0]))

    def stage_emit(u, ot, p_sink):
        emit(rows_of[u[0]], u[1], ot, p_sink)

    n_units = len(units)
    s, p, ot = {}, {}, {}
    s[0] = stage_scores(units[0])
    for t in range(n_units):
        if t + 1 < n_units:
            s[t + 1] = stage_scores(units[t + 1])
        p[t] = stage_softmax(units[t], s.pop(t))
        if t >= 1:
            ot[t - 1] = stage_values(units[t - 1], p[t - 1][0])
        if t >= 2:
            stage_emit(units[t - 2], ot.pop(t - 2), p.pop(t - 2)[1])
    ot[n_units - 1] = stage_values(units[n_units - 1], p[n_units - 1][0])
    for t in (n_units - 2, n_units - 1):
        stage_emit(units[t], ot.pop(t), p.pop(t)[1])


def _attention(q, k, vt, kx, vxt, sink_row):
    bsz, s, _ = q.shape
    l = kx.shape[1]
    per_b = lambda a: pl.BlockSpec((1,) + a.shape[1:], lambda b: (b, 0, 0))
    vx_spec = pl.BlockSpec((1, ATT_KV_WIDTH, l), lambda b: (b // CTX_PER_STEP, 0, b % CTX_PER_STEP))
    return pl.pallas_call(
        _attn_kernel,
        grid=(bsz,),
        in_specs=[per_b(q), per_b(k), per_b(kx), per_b(vt), vx_spec,
                  pl.BlockSpec(sink_row.shape, lambda b: (0, 0))],
        out_specs=pl.BlockSpec((1, s, ATT_WIDTH), lambda b: (b, 0, 0)),
        out_shape=jax.ShapeDtypeStruct((bsz, s, ATT_WIDTH), BF16),
        compiler_params=pltpu.CompilerParams(dimension_semantics=("arbitrary",),
                                             vmem_limit_bytes=VMEM_LIMIT),
        name="attention",
    )(q, k, kx, vt, vxt, sink_row)


BF16_SUBLANES = 16
ML_STATE_ROWS = ML_V_DIM + BF16_SUBLANES


def _mlstm_kernel(qkx_ref, vtx_ref, ox_ref, gcx_ref, grx_ref, qkc_ref, vtc_ref, grc_ref,
                  out_ref, sin_ref, st_ref, ucol_ref, mrun_ref, wint_ref, floor_ref,
                  wkey_ref, decay_ref, *, ncx, ncc):
    lc = ML_CHUNK
    qkw = ML_QK_WIDTH
    sr = ML_STATE_ROWS
    ng = ML_GATES
    hg = ML_GATES // 2
    st_ref[...] = jnp.zeros(st_ref.shape, F32)

    rr = lax.broadcasted_iota(jnp.int32, (lc, lc), 0)
    cc = lax.broadcasted_iota(jnp.int32, (lc, lc), 1)
    tril = rr >= cc
    triu = rr <= cc
    tril_b = jnp.where(tril, 1.0, 0.0).astype(BF16)
    triu_b = jnp.where(triu, 1.0, 0.0).astype(BF16)
    head_of_lane = lax.broadcasted_iota(jnp.int32, (lc, qkw), 1) // ML_QK_DIM
    head_of_state_lane = lax.broadcasted_iota(jnp.int32, (1, qkw), 1) // ML_QK_DIM
    pad_rows = jnp.zeros((BF16_SUBLANES - 1, lc), F32)

    def split(v):
        hi = v.astype(BF16)
        return hi, (v - hi.astype(F32)).astype(BF16)

    def gate_rows(g):
        n16 = g.shape[0]
        fwd_row = (lax.broadcasted_iota(jnp.int32, g.shape, 0) & (ng - 1)) < hg
        lane = lax.broadcasted_iota(jnp.int32, g.shape, 1)
        hi, lo = split(_log_sigmoid(g))
        cat = jnp.concatenate([hi, lo], axis=0)
        bu = _dot(cat, triu_b)
        bl = _dot(cat, tril_b)
        b = jnp.where(fwd_row, bu[:n16] + bu[n16:], bl[:n16] + bl[n16:])
        b = pltpu.roll(b, n16 - ML_HEADS, axis=0)
        u = g - b
        run_f = run_b = u
        k = 1
        while k < lc:
            run_f = jnp.maximum(run_f, jnp.where(lane >= k, pltpu.roll(run_f, k, axis=1), -jnp.inf))
            run_b = jnp.maximum(run_b, jnp.where(lane < lc - k, pltpu.roll(run_b, lc - k, axis=1), -jnp.inf))
            k *= 2
        run = jnp.where(fwd_row, run_f, run_b)

        def at_end(a):
            return jnp.where(fwd_row, jnp.broadcast_to(a[:, lc - 1:lc], a.shape),
                             jnp.broadcast_to(a[:, 0:1], a.shape))

        return u, b, run, at_end(b), at_end(run)

    def derived(u, b, run, b_tot, run_end, m_in):
        m_run = jnp.maximum(run, m_in)
        m_out = b_tot + jnp.maximum(run_end, m_in)
        return (m_run * LOG2_E, jnp.exp(m_in - m_run), jnp.exp(-(b + m_run)),
                jnp.exp(b_tot + u - m_out), jnp.exp(b_tot + m_in - m_out))

    gc_rows = gate_rows(grc_ref[0].reshape(ncc * ng, lc))
    gx_rows = gate_rows(grx_ref[0].reshape(ncx * ng, lc))

    def scan_m(rows, n, m_f, m_b):
        _, _, _, b_tot, run_end = rows
        part = lambda a, c, d: a[c * ng + d * hg:c * ng + (d + 1) * hg]
        ins_f, ins_b = [], [None] * n
        for c in range(n):
            ins_f.append(m_f)
            m_f = part(b_tot, c, 0) + jnp.maximum(part(run_end, c, 0), m_f)
        for c in reversed(range(n)):
            ins_b[c] = m_b
            m_b = part(b_tot, c, 1) + jnp.maximum(part(run_end, c, 1), m_b)
        return jnp.concatenate([x for c in range(n) for x in (ins_f[c], ins_b[c])], axis=0), m_f, m_b

    m0 = jnp.zeros((hg, lc), F32)
    m_in_c, m_f, m_b = scan_m(gc_rows, ncc, m0, m0)
    m_in_x, _, _ = scan_m(gx_rows, ncx, m_f, m_b)
    _, _, _, wkey_c, decay_c = derived(*gc_rows, m_in_c)
    for ref, val in zip((mrun_ref, wint_ref, floor_ref, wkey_ref, decay_ref), derived(*gx_rows, m_in_x)):
        ref[...] = val

    fwd_col = (lax.broadcasted_iota(jnp.int32, (lc, LANES), 1) & (ng - 1)) < hg

    def token_major_u(j):
        rows = pl.ds(pl.multiple_of(j * lc, lc), lc)
        gcol = gcx_ref[0, rows, :]
        hi, lo = split(_log_sigmoid(gcol))
        cat = jnp.concatenate([hi, lo], axis=1)
        bl = _dot(tril_b, cat)
        bu = _dot(triu_b, cat)
        b = jnp.where(fwd_col, bl[:, :LANES] + bl[:, LANES:], bu[:, :LANES] + bu[:, LANES:])
        ucol_ref[rows, :] = (gcol - pltpu.roll(b, LANES - ML_HEADS, axis=1)) * LOG2_E

    def advance(dirn, k4, vt, w_key, decay):
        st = st_ref[dirn]
        lhs, rhs, decay_row = [], [], None
        for h in range(ML_HEADS):
            c = hg * dirn + h
            lhs.append(jnp.concatenate([vt[h * ML_V_DIM:(h + 1) * ML_V_DIM].astype(F32) * w_key[c:c + 1],
                                        w_key[c:c + 1], pad_rows], axis=0).astype(BF16))
            rhs.append(jnp.where(head_of_lane == h, k4, jnp.zeros_like(k4)))
            d_h = jnp.concatenate([decay[c:c + 1]] * (qkw // lc), axis=1)
            decay_row = d_h if decay_row is None else jnp.where(head_of_state_lane == h, d_h, decay_row)
        upd = _dot(jnp.concatenate(lhs, axis=1), jnp.concatenate(rhs, axis=0))
        st_ref[dirn] = decay_row * st + upd

    for j in range(ncc):
        for dirn, cj in ((0, j), (1, ncc - 1 - j)):
            grows = slice(cj * ng, (cj + 1) * ng)
            advance(dirn, qkc_ref[0, cj * lc:(cj + 1) * lc, qkw:], vtc_ref[0, cj], wkey_c[grows], decay_c[grows])

    def scan_body(j, carry):
        for dirn, cj in ((0, j), (1, ncx - 1 - j)):
            rows = pl.ds(pl.multiple_of(cj * lc, lc), lc)
            grows = pl.ds(pl.multiple_of(cj * ng, ng), ng)
            sin_ref[dirn, cj] = st_ref[dirn].astype(BF16)
            advance(dirn, qkx_ref[0, rows, qkw:], vtx_ref[0, cj], wkey_ref[grows, :], decay_ref[grows, :])
        token_major_u(j)
        return carry

    lax.fori_loop(0, ncx, scan_body, 0, unroll=16)


    def out_body(j, carry):
        rows = pl.ds(pl.multiple_of(j * lc, lc), lc)
        grows = pl.ds(pl.multiple_of(j * ng, ng), ng)
        q4 = qkx_ref[0, rows, :qkw]
        k4 = qkx_ref[0, rows, qkw:]
        vt = vtx_ref[0, j]
        u_col = ucol_ref[rows, :]
        m_run, w_int, floor = mrun_ref[grows, :], wint_ref[grows, :], floor_ref[grows, :]
        zero = jnp.zeros_like(q4)
        qs = jnp.concatenate([jnp.where(head_of_lane == h, q4, zero) for h in range(ML_HEADS)], axis=0)
        both = _dot_nt(jnp.concatenate([k4, sin_ref[0, j], sin_ref[1, j]], axis=0), qs)
        qk_t = both[:lc]
        for h in range(ML_HEADS):
            cols_h = slice(h * lc, (h + 1) * lc)
            hs = None
            for dirn in range(2):
                c = hg * dirn + h
                inter = both[lc + dirn * sr:lc + (dirn + 1) * sr, cols_h]
                valid = triu if dirn == 0 else tril
                e = jnp.exp2(jnp.where(valid, u_col[:, c:c + 1] - m_run[c:c + 1], -jnp.inf))
                s_t = qk_t[:, cols_h] * e
                num = _dot(vt[h * ML_V_DIM:(h + 1) * ML_V_DIM], s_t.astype(BF16))
                num = num + w_int[c:c + 1] * inter[:ML_V_DIM]
                den = jnp.sum(s_t, axis=0, keepdims=True) + w_int[c:c + 1] * inter[ML_V_DIM:ML_V_DIM + 1]
                hv = num * (1.0 / jnp.maximum(jnp.abs(den), floor[c:c + 1]))
                hs = hv if hs is None else hs + hv
            cols = slice(h * ML_V_DIM, (h + 1) * ML_V_DIM)
            hn = hs * lax.rsqrt(jnp.mean(hs * hs, axis=0, keepdims=True) + EPS)
            out_ref[0, rows, cols] = (hn.T * ox_ref[0, rows, cols].astype(F32)).astype(BF16)
        return carry

    lax.fori_loop(0, ncx, out_body, 0, unroll=16)


def _mlstm(qkx, vtx, ox, gcx, grx, qkc, vtc, grc):
    bsz, s, _ = qkx.shape
    ncx, ncc = s // ML_CHUNK, qkc.shape[1] // ML_CHUNK
    per_b = lambda a: pl.BlockSpec((1,) + a.shape[1:], lambda b: (b,) + (0,) * (a.ndim - 1))
    ins = [qkx, vtx, ox, gcx, grx, qkc, vtc, grc]
    return pl.pallas_call(
        functools.partial(_mlstm_kernel, ncx=ncx, ncc=ncc),
        grid=(bsz,),
        in_specs=[per_b(a) for a in ins],
        out_specs=pl.BlockSpec((1, s, ML_WIDTH), lambda b: (b, 0, 0)),
        out_shape=jax.ShapeDtypeStruct((bsz, s, ML_WIDTH), BF16),
        scratch_shapes=[pltpu.VMEM((2, ncx, ML_STATE_ROWS, ML_QK_WIDTH), BF16),
                        pltpu.VMEM((2, ML_STATE_ROWS, ML_QK_WIDTH), F32),
                        pltpu.VMEM((s, LANES), F32)]
                       + [pltpu.VMEM((ncx * ML_GATES, ML_CHUNK), F32)] * 5,
        compiler_params=pltpu.CompilerParams(dimension_semantics=("arbitrary",),
                                             vmem_limit_bytes=VMEM_LIMIT),
        name="mlstm",
    )(*ins)


def _out_ffn_kernel(x_ref, att_ref, ml_ref, gtm_ref, shf_ref, scf_ref, gtf_ref,
                    gpm_ref, gpf_ref, gqf_ref, woa_ref, wom_ref, wfi_ref, wfo_ref, o_ref, *, hidden,
                    tiles_per_batch):
    r_mod = pl.ds((pl.program_id(0) // tiles_per_batch) % SUBLANES, 1)
    gtm, shf, scf, gtf = gtm_ref[r_mod, :], shf_ref[r_mod, :], scf_ref[r_mod, :], gtf_ref[r_mod, :]

    def group(g):
        r = slice(g * FFN_ROWS, (g + 1) * FFN_ROWS)
        mix = _dot(att_ref[r, :], woa_ref[...]) + _dot(ml_ref[r, :], wom_ref[...])
        yield
        x1 = x_ref[r, :] + gtm * _rms(mix, gpm_ref[...])
        h = (_rms(x1, gpf_ref[...]) * (1.0 + scf) + shf).astype(BF16)
        gu = _dot(h, wfi_ref[...])
        yield
        act = (_silu(gu[:, :hidden]) * gu[:, hidden:]).astype(BF16)
        fx = _dot(act, wfo_ref[...])
        yield
        o_ref[r, :] = x1 + gtf * _rms(fx, gqf_ref[...])

    done = object()
    waiting = [group(g) for g in range(x_ref.shape[0] // FFN_ROWS)]
    active, rounds = [], 0
    while waiting or active:
        if rounds % FFN_PAIR_PERIOD == 0:
            active += [waiting.pop(0) for _ in range(min(2, len(waiting)))]
        active = [g for g in active if next(g, done) is not done]
        rounds += 1


def _out_ffn(x2, att2, ml2, mod, g_post_mix, g_pre_ffn, g_post_ffn, woa, wom, wfi, wfo, *, tiles_per_batch):
    t, d = x2.shape
    tm = FFN_TM
    hidden = wfo.shape[0]
    resident = lambda a: pl.BlockSpec(a.shape, lambda i: (0,) * a.ndim, pipeline_mode=pl.Buffered(1))
    mod_spec = lambda k: pl.BlockSpec((SUBLANES, d), lambda i: (i // tiles_per_batch // SUBLANES, k))
    row = pl.BlockSpec((1, d), lambda i: (0, 0))
    return pl.pallas_call(
        functools.partial(_out_ffn_kernel, hidden=hidden, tiles_per_batch=tiles_per_batch),
        grid=(t // tm,),
        in_specs=[pl.BlockSpec((tm, d), lambda i: (i, 0)),
                  pl.BlockSpec((tm, ATT_WIDTH), lambda i: (i, 0)),
                  pl.BlockSpec((tm, ML_WIDTH), lambda i: (i, 0)),
                  mod_spec(2), mod_spec(3), mod_spec(4), mod_spec(5), row, row, row,
                  resident(woa), resident(wom), resident(wfi), resident(wfo)],
        out_specs=pl.BlockSpec((tm, d), lambda i: (i, 0)),
        out_shape=jax.ShapeDtypeStruct((t, d), F32),
        compiler_params=pltpu.CompilerParams(dimension_semantics=("arbitrary",),
                                             vmem_limit_bytes=VMEM_LIMIT),
        name="out_ffn",
    )(x2, att2, ml2, mod, mod, mod, mod, g_post_mix, g_pre_ffn, g_post_ffn, woa, wom, wfi, wfo)


def _rope_tables(n_tokens):
    pos = jnp.arange(n_tokens)
    row = (pos // GRID_W).astype(F32)
    col = (pos % GRID_W).astype(F32)
    inv_freq = jnp.power(ROPE_BASE, -jnp.arange(ROPE_PAIR, dtype=F32) / ROPE_PAIR)
    ang_r = row[:, None] * inv_freq
    ang_c = col[:, None] * inv_freq
    z = jnp.zeros_like(ang_r)
    reps = LANES // ATT_HEAD_DIM
    cos = jnp.tile(jnp.concatenate([jnp.cos(ang_r)] * 2 + [jnp.cos(ang_c)] * 2, axis=1), (1, reps))
    sin_lo = jnp.tile(jnp.concatenate([-jnp.sin(ang_r), z, -jnp.sin(ang_c), z], axis=1), (1, reps))
    sin_hi = jnp.tile(jnp.concatenate([z, jnp.sin(ang_r), z, jnp.sin(ang_c)], axis=1), (1, reps))
    return cos, sin_lo, sin_hi


def _permute_heads(w, axis):
    shape = w.shape
    grouped = shape[:axis] + (ATT_KV_HEADS, ATT_GROUP, ATT_HEAD_DIM) + shape[axis + 1:]
    return jnp.swapaxes(w.reshape(grouped), axis, axis + 1).reshape(shape)


def kernel(x, c, ctx, c_ctx, w_ada, b_ada, g_pre_mix, w_in, w_conv_qk, b_gates, attn_sink,
           g_mlstm_out, w_out, g_post_mix, g_pre_ffn, w_ffn_in, w_ffn_out, g_post_ffn):
    bsz, s, d = x.shape
    l = ctx.shape[1]
    assert w_ada.shape[0] == 1, "single-layer block"
    assert bsz < MOD_ROWS and s % INPROJ_TM == 0 and l % ML_CHUNK == 0 and s % FFN_TM == 0
    assert bsz % CTX_PER_STEP == 0 and l % LANES == 0

    mod = _ada(c, c_ctx, w_ada[0], b_ada)

    w = w_in[0]
    o_q, o_k, o_v = 0, ATT_WIDTH, ATT_WIDTH + ATT_KV_WIDTH
    o_mq = o_v + ATT_KV_WIDTH
    o_mv = o_mq + 2 * ML_QK_WIDTH
    o_mo = o_mv + ML_WIDTH
    o_mg = o_mo + ML_WIDTH
    w_q = _permute_heads(w[:, o_q:o_k], 1) * (ATT_HEAD_DIM ** -0.5 * LOG2_E)
    w_g = jnp.pad(w[:, o_mg:], ((0, 0), (0, LANES - ML_GATES)))
    shared = [w[:, o_k:o_v], w_g, w[:, o_mq:o_mv]]
    w_lat = jnp.concatenate([w_q] + shared + [w[:, o_mo:o_mg]], axis=1).astype(BF16)
    wvt = jnp.concatenate([w[:, o_v:o_mq], w[:, o_mv:o_mo], w[:, o_mg:]], axis=1).T.astype(BF16)
    bg_row = jnp.pad(b_gates, ((0, 0), (0, LANES - ML_GATES)))
    bg_col = b_gates.reshape(ML_GATES, 1)
    wc = w_conv_qk[0]

    q, k, vt, qkx, vtx, ox, gcx, grx = _inproj(
        x, mod, None, g_pre_mix, w_lat, wvt, bg_row, bg_col, wc, _rope_tables(s), g_mlstm_out,
        tm=INPROJ_TM, gm=INPROJ_ROWS, latent=True)
    nctx = bsz // CTX_PER_STEP
    kc, vct, qkc, vtc, _, grc = _inproj(
        ctx.reshape(nctx, CTX_PER_STEP * l, d), mod, bsz, g_pre_mix, w_lat, wvt, bg_row, bg_col, wc, None, None,
        tm=CTX_PER_STEP * l, gm=l, latent=False)
    kc = kc.reshape(bsz, l, ATT_KV_WIDTH)
    qkc = qkc.reshape(bsz, l, 2 * ML_QK_WIDTH)
    vtc = vtc.reshape((bsz, l // ML_CHUNK) + vtc.shape[2:])
    grc = grc.reshape((bsz, l // ML_CHUNK) + grc.shape[2:])

    sink_row = jnp.repeat(attn_sink[0][jnp.array(_HEAD_PERM)] * LOG2_E, ATT_BLOCK)[None, :]
    att = _attention(q, k, vt, kc, vct, sink_row)
    ml = _mlstm(qkx, vtx, ox, gcx, grx, qkc, vtc, grc)

    wo = w_out[0]
    woa = _permute_heads(wo[:ATT_WIDTH], 0).astype(BF16)
    wom = wo[ATT_WIDTH:].astype(BF16)
    out = _out_ffn(x.reshape(bsz * s, d), att.reshape(bsz * s, ATT_WIDTH), ml.reshape(bsz * s, ML_WIDTH),
                   mod, g_post_mix, g_pre_ffn, g_post_ffn, woa, wom,
                   w_ffn_in[0].astype(BF16), w_ffn_out[0].astype(BF16), tiles_per_batch=s // FFN_TM)
    return out.reshape(bsz, s, d)
```

```python
import functools

import jax
import jax.numpy as jnp
from jax import lax
from jax.experimental import pallas as pl
from jax.experimental.pallas import tpu as pltpu

F32 = jnp.float32
BF16 = jnp.bfloat16

EPS = 1e-6
GRID_W = 64
ROPE_BASE = 10000.0
LOG2_E = 1.4426950408889634

ATT_HEADS = 8
ATT_KV_HEADS = 2
ATT_GROUP = ATT_HEADS // ATT_KV_HEADS
ATT_HEAD_DIM = 64
ROPE_PAIR = ATT_HEAD_DIM // 4
ATT_BLOCK = 128
ATT_WIDTH = ATT_HEADS * ATT_HEAD_DIM
ATT_KV_WIDTH = ATT_KV_HEADS * ATT_HEAD_DIM
ATT_UNIT_GROUPS = 2
ATT_ONES_ROWS = 16

ML_HEADS = 4
ML_V_DIM = 128
ML_QK_DIM = 64
ML_WIDTH = ML_HEADS * ML_V_DIM
ML_QK_WIDTH = ML_HEADS * ML_QK_DIM
ML_GATES = 4 * ML_HEADS
ML_CHUNK = 128

LANES = 128
SUBLANES = 8
VMEM_LIMIT = 56 * 1024 * 1024

INPROJ_TM = 1024
INPROJ_ROWS = 256
CTX_PER_STEP = 4
FFN_TM = 1024
FFN_ROWS = 256
FFN_PAIR_PERIOD = 3
ADA_TN = 1536
MOD_ROWS = 16

_HEAD_PERM = tuple(h * ATT_GROUP + g for g in range(ATT_GROUP) for h in range(ATT_KV_HEADS))


def _silu(v):
    return v * jax.nn.sigmoid(v)


def _log_sigmoid(v):
    return jnp.minimum(v, 0.0) - jnp.log1p(jnp.exp(-jnp.abs(v)))


def _rms(v, g):
    return v * lax.rsqrt(jnp.mean(v * v, axis=-1, keepdims=True) + EPS) * g


def _dot(a, b):
    return jnp.dot(a, b, preferred_element_type=F32)


def _dot_nt(a, b):
    return lax.dot_general(a, b, (((1,), (1,)), ((), ())), preferred_element_type=F32)


def _ada_kernel(c_ref, cctx_ref, w_ref, b_ref, o_ref):
    pad = jnp.zeros((MOD_ROWS - c_ref.shape[0] - 1, c_ref.shape[1]), F32)
    a = _silu(jnp.concatenate([c_ref[...], cctx_ref[...], pad], axis=0))
    o_ref[...] = _dot(a.astype(BF16), w_ref[...].astype(BF16)) + b_ref[...]


def _ada(c, c_ctx, w, b):
    d, n = w.shape
    return pl.pallas_call(
        _ada_kernel,
        grid=(n // ADA_TN,),
        in_specs=[pl.BlockSpec(c.shape, lambda j: (0, 0)),
                  pl.BlockSpec((1, d), lambda j: (0, 0)),
                  pl.BlockSpec((d, ADA_TN), lambda j: (0, j)),
                  pl.BlockSpec((1, ADA_TN), lambda j: (0, j))],
        out_specs=pl.BlockSpec((MOD_ROWS, ADA_TN), lambda j: (0, j)),
        out_shape=jax.ShapeDtypeStruct((MOD_ROWS, n), F32),
        compiler_params=pltpu.CompilerParams(dimension_semantics=("arbitrary",),
                                             vmem_limit_bytes=VMEM_LIMIT),
        name="ada",
    )(c, c_ctx.reshape(1, d), w, b)


def _rope(v, cos, sin_lo, sin_hi):
    return (v * cos + pltpu.roll(v, LANES - ROPE_PAIR, axis=1) * sin_lo
            + pltpu.roll(v, ROPE_PAIR, axis=1) * sin_hi)


def _inproj_kernel(*refs, tm, gm, n_tiles, latent, mod_row):
    if latent:
        (x_ref, xp_ref, xn_ref, sh_ref, sc_ref, g_ref, w_ref, wvt_ref, bgr_ref, bgc_ref,
         wc_ref, cos_ref, sl_ref, shi_ref, gain_ref,
         q_ref, k_ref, vt_ref, qk_ref, mvt_ref, o_ref, gcol_ref, grow_ref) = refs
    else:
        (x_ref, xp_ref, xn_ref, sh_ref, sc_ref, g_ref, w_ref, wvt_ref, bgr_ref, bgc_ref,
         wc_ref,
         k_ref, vt_ref, qk_ref, mvt_ref, gcol_ref, grow_ref) = refs
    i = pl.program_id(0)
    n_groups = tm // gm
    r_mod = (pl.program_id(1) if mod_row is None else mod_row) % SUBLANES
    scale = g_ref[...] * (1.0 + sc_ref[pl.ds(r_mod, 1), :])
    shift = sh_ref[pl.ds(r_mod, 1), :]
    wc = wc_ref[...]
    row = lax.broadcasted_iota(jnp.int32, (gm, 1), 0)

    def project(lo, width, lhs):
        return _dot(lhs, w_ref[:, lo:lo + width])

    def group(r):
        rows = slice(r * gm, (r + 1) * gm)
        before = xp_ref[0] if r == 0 else x_ref[0, r * gm - SUBLANES:r * gm, :]
        after = xn_ref[0] if r == n_groups - 1 else x_ref[0, (r + 1) * gm:(r + 1) * gm + SUBLANES, :]
        keep_prev = (jnp.where(i == 0, 0.0, 1.0) if r == 0 else 1.0) if latent else 0.0
        keep_next = (jnp.where(i == n_tiles - 1, 0.0, 1.0) if r == n_groups - 1 else 1.0) if latent else 0.0
        xt = jnp.concatenate([x_ref[0, rows, :], before, after], axis=0)
        ms = jnp.mean(xt * xt, axis=-1, keepdims=True)
        hb = (xt * lax.rsqrt(ms + EPS) * scale + shift).astype(BF16)
        hm = hb[:gm]
        yield

        c = ATT_WIDTH
        if latent:
            r_q = project(0, ATT_WIDTH, hm)
        r_kg = project(c, ATT_KV_WIDTH + LANES, hm)
        c += ATT_KV_WIDTH + LANES
        yield

        if latent:
            cos, sl, shi = cos_ref[rows, :], sl_ref[rows, :], shi_ref[rows, :]
            for g in range(ATT_WIDTH // LANES):
                q_ref[0, rows, g * LANES:(g + 1) * LANES] = _rope(
                    r_q[:, g * LANES:(g + 1) * LANES], cos, sl, shi).astype(BF16)
        y = project(c, 2 * ML_QK_WIDTH, hb)
        c += 2 * ML_QK_WIDTH
        yield

        if latent:
            k_ref[0, rows, :] = _rope(r_kg[:, :LANES], cos, sl, shi).astype(BF16)
        else:
            k_ref[0, rows, :] = r_kg[:, :LANES].astype(BF16)
        gcol_ref[0, rows, :] = r_kg[:, LANES:] + bgr_ref[...]
        v_t = _dot_nt(wvt_ref[...], hm)
        yield

        ym = y[:gm]
        prev = jnp.where(row == 0, y[gm + SUBLANES - 1:gm + SUBLANES] * keep_prev,
                         pltpu.roll(ym, 1, axis=0))
        nxt = jnp.where(row == gm - 1, y[gm + SUBLANES:gm + SUBLANES + 1] * keep_next,
                        pltpu.roll(ym, gm - 1, axis=0))
        act = _silu(prev * wc[0:1] + ym * wc[1:2] + nxt * wc[2:3])
        qk_ref[0, rows, :ML_QK_WIDTH] = (act[:, :ML_QK_WIDTH] * (ML_QK_DIM ** -0.5)).astype(BF16)
        qk_ref[0, rows, ML_QK_WIDTH:] = act[:, ML_QK_WIDTH:].astype(BF16)
        if latent:
            r_o = project(c, ML_WIDTH, hm)
        yield

        g_t = v_t[ATT_KV_WIDTH + ML_WIDTH:] + bgc_ref[...]
        vt_ref[0, :, rows] = v_t[:ATT_KV_WIDTH].astype(BF16)
        for j in range(gm // ML_CHUNK):
            cols = slice(j * ML_CHUNK, (j + 1) * ML_CHUNK)
            mvt_ref[0, r * (gm // ML_CHUNK) + j] = v_t[ATT_KV_WIDTH:ATT_KV_WIDTH + ML_WIDTH, cols].astype(BF16)
            grow_ref[0, r * (gm // ML_CHUNK) + j] = g_t[:, cols]
        if latent:
            o_ref[0, rows, :] = (jax.nn.sigmoid(r_o) * gain_ref[...]).astype(BF16)

    done = object()
    waiting = [group(r) for r in range(n_groups)]
    active = []
    while waiting or active:
        if waiting:
            active.append(waiting.pop(0))
        active = [g for g in active if next(g, done) is not done]


def _inproj(x, mod, mod_row, g_pre, w_main, wvt, bg_row, bg_col, wc, rope_tabs, gain, *, tm, gm, latent):
    bsz, t, d = x.shape
    n_tiles = t // tm
    hb = tm // SUBLANES
    n_hblk = t // SUBLANES
    n = w_main.shape[1]
    mod_blk = (lambda b: b // SUBLANES) if mod_row is None else (lambda b: mod_row // SUBLANES)

    def const(shape):
        return pl.BlockSpec(shape, lambda i, b: (0,) * len(shape))

    in_specs = [
        pl.BlockSpec((1, tm, d), lambda i, b: (b, i, 0)),
        pl.BlockSpec((1, SUBLANES, d), lambda i, b: (b, jnp.maximum(i * hb - 1, 0), 0)),
        pl.BlockSpec((1, SUBLANES, d), lambda i, b: (b, jnp.minimum((i + 1) * hb, n_hblk - 1), 0)),
        pl.BlockSpec((SUBLANES, d), lambda i, b: (mod_blk(b), 0)),
        pl.BlockSpec((SUBLANES, d), lambda i, b: (mod_blk(b), 1)),
        const((1, d)), const((d, n)), const((ATT_KV_WIDTH + ML_WIDTH + ML_GATES, d)),
        const((1, LANES)), const((ML_GATES, 1)), const((3, 2 * ML_QK_WIDTH)),
    ]
    args = [x, x, x, mod, mod, g_pre, w_main, wvt, bg_row, bg_col, wc]
    tok = lambda width, dt: (pl.BlockSpec((1, tm, width), lambda i, b: (b, i, 0)),
                             jax.ShapeDtypeStruct((bsz, t, width), dt))
    chunked = lambda rows, dt: (pl.BlockSpec((1, tm // ML_CHUNK, rows, ML_CHUNK), lambda i, b: (b, i, 0, 0)),
                                jax.ShapeDtypeStruct((bsz, t // ML_CHUNK, rows, ML_CHUNK), dt))
    outs = []
    if latent:
        in_specs += [pl.BlockSpec((tm, LANES), lambda i, b: (i, 0))] * 3 + [const((1, ML_WIDTH))]
        args += list(rope_tabs) + [gain]
        outs.append(tok(ATT_WIDTH, BF16))
    outs.append(tok(ATT_KV_WIDTH, BF16))
    outs.append((pl.BlockSpec((1, LANES, tm), lambda i, b: (b, 0, i)),
                 jax.ShapeDtypeStruct((bsz, ATT_KV_WIDTH, t), BF16)))
    outs.append(tok(2 * ML_QK_WIDTH, BF16))
    outs.append(chunked(ML_WIDTH, BF16))
    if latent:
        outs.append(tok(ML_WIDTH, BF16))
    outs.append(tok(LANES, F32))
    outs.append(chunked(ML_GATES, F32))
    return pl.pallas_call(
        functools.partial(_inproj_kernel, tm=tm, gm=gm, n_tiles=n_tiles, latent=latent, mod_row=mod_row),
        grid=(n_tiles, bsz),
        in_specs=in_specs,
        out_specs=[o[0] for o in outs],
        out_shape=[o[1] for o in outs],
        compiler_params=pltpu.CompilerParams(dimension_semantics=("arbitrary", "arbitrary"),
                                             vmem_limit_bytes=VMEM_LIMIT),
        name="inproj_latent" if latent else "inproj_context",
    )(*args)


def _attn_kernel(q_ref, k_ref, kx_ref, vt_ref, vx_ref, sink_ref, o_ref):
    blk = ATT_BLOCK
    lane = lax.broadcasted_iota(jnp.int32, (blk, LANES), 1)
    zero = jnp.zeros((blk, LANES), BF16)
    half_groups = ATT_UNIT_GROUPS
    n_slots = ATT_KV_HEADS * half_groups
    n_parts = ATT_GROUP // half_groups
    sink = [sink_ref[:, part * n_slots * blk:(part + 1) * n_slots * blk] for part in range(n_parts)]

    def stack_heads(q, half):
        parts = []
        for g in range(half * half_groups, (half + 1) * half_groups):
            slab = q[:, g * LANES:(g + 1) * LANES]
            parts.append(jnp.where(lane < ATT_HEAD_DIM, slab, zero))
            parts.append(jnp.where(lane >= ATT_HEAD_DIM, slab, zero))
        return jnp.concatenate(parts, axis=0)

    key = lax.broadcasted_iota(jnp.int32, (blk, blk), 0)
    qry = lax.broadcasted_iota(jnp.int32, (blk, blk), 1)
    ninf = jnp.full((blk, blk), -jnp.inf, F32)
    bias_prev = jnp.where(key >= qry, 0.0, ninf)
    bias_next = jnp.where(key <= qry, 0.0, ninf)
    slots = lambda b: jnp.concatenate([b] * n_slots, axis=1)

    def scores(qs, k_prev, k_cur, k_next):
        return [_dot_nt(k_prev, qs), _dot_nt(k_cur, qs), _dot_nt(k_next, qs), _dot_nt(kx_ref[0], qs)]

    def softmax(s, b_prev, b_next, sink_h):
        s = jnp.concatenate([s[0] + slots(b_prev), s[1], s[2] + slots(b_next), s[3]], axis=0)
        m = jnp.maximum(sink_h, jnp.max(s, axis=0, keepdims=True))
        return jnp.exp2(s - m).astype(BF16), jnp.exp2(sink_h - m)

    n_keys = 3 * blk + kx_ref.shape[1]
    ones_rows = jnp.ones((ATT_ONES_ROWS, n_keys), BF16)

    def weighted_values(p, v_prev, v_cur, v_next):
        vt = jnp.concatenate([v_prev, v_cur, v_next, vx_ref[0]], axis=1)
        return _dot(jnp.concatenate([vt, ones_rows], axis=0), p)

    dim = lax.broadcasted_iota(jnp.int32, (LANES, blk), 0)

    def emit(rows, half, ot, p_sink):
        ot = ot[:LANES] * (1.0 / (ot[LANES:LANES + 1] + p_sink))
        for j in range(half_groups):
            g = half * half_groups + j
            a = ot[:, (2 * j) * blk:(2 * j + 1) * blk]
            b = ot[:, (2 * j + 1) * blk:(2 * j + 2) * blk]
            o_ref[0, rows, g * LANES:(g + 1) * LANES] = jnp.where(dim < ATT_HEAD_DIM, a, b).T.astype(BF16)

    nblk = q_ref.shape[1] // blk
    rows_of = [slice(b * blk, (b + 1) * blk) for b in range(nblk)]
    k_blocks = [k_ref[0, r, :] for r in rows_of]
    v_blocks = [vt_ref[0, :, r] for r in rows_of]
    near = lambda blocks, b: (blocks[max(b - 1, 0)], blocks[b], blocks[min(b + 1, nblk - 1)])
    bias = [(ninf if b == 0 else bias_prev, ninf if b == nblk - 1 else bias_next) for b in range(nblk)]
    units = [(b, half) for b in range(nblk) for half in range(n_parts)]

    def stage_scores(u):
        b, half = u
        return scores(stack_heads(q_ref[0, rows_of[b], :], half), *near(k_blocks, b))

    def stage_softmax(u, s):
        b, half = u
        return softmax(s, *bias[b], sink[half])

    def stage_values(u, p):
        return weighted_values(p, *near(v_blocks, u[0]))

    def stage_emit(u, ot, p_sink):
        emit(rows_of[u[0]], u[1], ot, p_sink)

    n_units = len(units)
    s, p, ot = {}, {}, {}
    s[0] = stage_scores(units[0])
    for t in range(n_units):
        if t + 1 < n_units:
            s[t + 1] = stage_scores(units[t + 1])
        p[t] = stage_softmax(units[t], s.pop(t))
        if t >= 1:
            ot[t - 1] = stage_values(units[t - 1], p[t - 1][0])
        if t >= 2:
            stage_emit(units[t - 2], ot.pop(t - 2), p.pop(t - 2)[1])
    ot[n_units - 1] = stage_values(units[n_units - 1], p[n_units - 1][0])
    for t in (n_units - 2, n_units - 1):
        stage_emit(units[t], ot.pop(t), p.pop(t)[1])


def _attention(q, k, vt, kx, vxt, sink_row):
    bsz, s, _ = q.shape
    l = kx.shape[1]
    per_b = lambda a: pl.BlockSpec((1,) + a.shape[1:], lambda b: (b, 0, 0))
    vx_spec = pl.BlockSpec((1, ATT_KV_WIDTH, l), lambda b: (b // CTX_PER_STEP, 0, b % CTX_PER_STEP))
    return pl.pallas_call(
        _attn_kernel,
        grid=(bsz,),
        in_specs=[per_b(q), per_b(k), per_b(kx), per_b(vt), vx_spec,
                  pl.BlockSpec(sink_row.shape, lambda b: (0, 0))],
        out_specs=pl.BlockSpec((1, s, ATT_WIDTH), lambda b: (b, 0, 0)),
        out_shape=jax.ShapeDtypeStruct((bsz, s, ATT_WIDTH), BF16),
        compiler_params=pltpu.CompilerParams(dimension_semantics=("arbitrary",),
                                             vmem_limit_bytes=VMEM_LIMIT),
        name="attention",
    )(q, k, kx, vt, vxt, sink_row)


BF16_SUBLANES = 16
ML_STATE_ROWS = ML_V_DIM + BF16_SUBLANES


def _mlstm_kernel(qkx_ref, vtx_ref, ox_ref, gcx_ref, grx_ref, qkc_ref, vtc_ref, grc_ref,
                  out_ref, sin_ref, st_ref, ucol_ref, mrun_ref, wint_ref, floor_ref,
                  wkey_ref, decay_ref, *, ncx, ncc):
    lc = ML_CHUNK
    qkw = ML_QK_WIDTH
    sr = ML_STATE_ROWS
    ng = ML_GATES
    hg = ML_GATES // 2
    st_ref[...] = jnp.zeros(st_ref.shape, F32)

    rr = lax.broadcasted_iota(jnp.int32, (lc, lc), 0)
    cc = lax.broadcasted_iota(jnp.int32, (lc, lc), 1)
    tril = rr >= cc
    triu = rr <= cc
    tril_b = jnp.where(tril, 1.0, 0.0).astype(BF16)
    triu_b = jnp.where(triu, 1.0, 0.0).astype(BF16)
    head_of_lane = lax.broadcasted_iota(jnp.int32, (lc, qkw), 1) // ML_QK_DIM
    head_of_state_lane = lax.broadcasted_iota(jnp.int32, (1, qkw), 1) // ML_QK_DIM
    pad_rows = jnp.zeros((BF16_SUBLANES - 1, lc), F32)

    def split(v):
        hi = v.astype(BF16)
        return hi, (v - hi.astype(F32)).astype(BF16)

    def gate_rows(g):
        n16 = g.shape[0]
        fwd_row = (lax.broadcasted_iota(jnp.int32, g.shape, 0) & (ng - 1)) < hg
        lane = lax.broadcasted_iota(jnp.int32, g.shape, 1)
        hi, lo = split(_log_sigmoid(g))
        cat = jnp.concatenate([hi, lo], axis=0)
        bu = _dot(cat, triu_b)
        bl = _dot(cat, tril_b)
        b = jnp.where(fwd_row, bu[:n16] + bu[n16:], bl[:n16] + bl[n16:])
        b = pltpu.roll(b, n16 - ML_HEADS, axis=0)
        u = g - b
        run_f = run_b = u
        k = 1
        while k < lc:
            run_f = jnp.maximum(run_f, jnp.where(lane >= k, pltpu.roll(run_f, k, axis=1), -jnp.inf))
            run_b = jnp.maximum(run_b, jnp.where(lane < lc - k, pltpu.roll(run_b, lc - k, axis=1), -jnp.inf))
            k *= 2
        run = jnp.where(fwd_row, run_f, run_b)

        def at_end(a):
            return jnp.where(fwd_row, jnp.broadcast_to(a[:, lc - 1:lc], a.shape),
                             jnp.broadcast_to(a[:, 0:1], a.shape))

        return u, b, run, at_end(b), at_end(run)

    def derived(u, b, run, b_tot, run_end, m_in):
        m_run = jnp.maximum(run, m_in)
        m_out = b_tot + jnp.maximum(run_end, m_in)
        return (m_run * LOG2_E, jnp.exp(m_in - m_run), jnp.exp(-(b + m_run)),
                jnp.exp(b_tot + u - m_out), jnp.exp(b_tot + m_in - m_out))

    gc_rows = gate_rows(grc_ref[0].reshape(ncc * ng, lc))
    gx_rows = gate_rows(grx_ref[0].reshape(ncx * ng, lc))

    def scan_m(rows, n, m_f, m_b):
        _, _, _, b_tot, run_end = rows
        part = lambda a, c, d: a[c * ng + d * hg:c * ng + (d + 1) * hg]
        ins_f, ins_b = [], [None] * n
        for c in range(n):
            ins_f.append(m_f)
            m_f = part(b_tot, c, 0) + jnp.maximum(part(run_end, c, 0), m_f)
        for c in reversed(range(n)):
            ins_b[c] = m_b
            m_b = part(b_tot, c, 1) + jnp.maximum(part(run_end, c, 1), m_b)
        return jnp.concatenate([x for c in range(n) for x in (ins_f[c], ins_b[c])], axis=0), m_f, m_b

    m0 = jnp.zeros((hg, lc), F32)
    m_in_c, m_f, m_b = scan_m(gc_rows, ncc, m0, m0)
    m_in_x, _, _ = scan_m(gx_rows, ncx, m_f, m_b)
    _, _, _, wkey_c, decay_c = derived(*gc_rows, m_in_c)
    for ref, val in zip((mrun_ref, wint_ref, floor_ref, wkey_ref, decay_ref), derived(*gx_rows, m_in_x)):
        ref[...] = val

    fwd_col = (lax.broadcasted_iota(jnp.int32, (lc, LANES), 1) & (ng - 1)) < hg

    def token_major_u(j):
        rows = pl.ds(pl.multiple_of(j * lc, lc), lc)
        gcol = gcx_ref[0, rows, :]
        hi, lo = split(_log_sigmoid(gcol))
        cat = jnp.concatenate([hi, lo], axis=1)
        bl = _dot(tril_b, cat)
        bu = _dot(triu_b, cat)
        b = jnp.where(fwd_col, bl[:, :LANES] + bl[:, LANES:], bu[:, :LANES] + bu[:, LANES:])
        ucol_ref[rows, :] = (gcol - pltpu.roll(b, LANES - ML_HEADS, axis=1)) * LOG2_E

    def advance(dirn, k4, vt, w_key, decay):
        st = st_ref[dirn]
        lhs, rhs, decay_row = [], [], None
        for h in range(ML_HEADS):
            c = hg * dirn + h
            lhs.append(jnp.concatenate([vt[h * ML_V_DIM:(h + 1) * ML_V_DIM].astype(F32) * w_key[c:c + 1],
                                        w_key[c:c + 1], pad_rows], axis=0).astype(BF16))
            rhs.append(jnp.where(head_of_lane == h, k4, jnp.zeros_like(k4)))
            d_h = jnp.concatenate([decay[c:c + 1]] * (qkw // lc), axis=1)
            decay_row = d_h if decay_row is None else jnp.where(head_of_state_lane == h, d_h, decay_row)
        upd = _dot(jnp.concatenate(lhs, axis=1), jnp.concatenate(rhs, axis=0))
        st_ref[dirn] = decay_row * st + upd

    for j in range(ncc):
        for dirn, cj in ((0, j), (1, ncc - 1 - j)):
            grows = slice(cj * ng, (cj + 1) * ng)
            advance(dirn, qkc_ref[0, cj * lc:(cj + 1) * lc, qkw:], vtc_ref[0, cj], wkey_c[grows], decay_c[grows])

    def scan_body(j, carry):
        for dirn, cj in ((0, j), (1, ncx - 1 - j)):
            rows = pl.ds(pl.multiple_of(cj * lc, lc), lc)
            grows = pl.ds(pl.multiple_of(cj * ng, ng), ng)
            sin_ref[dirn, cj] = st_ref[dirn].astype(BF16)
            advance(dirn, qkx_ref[0, rows, qkw:], vtx_ref[0, cj], wkey_ref[grows, :], decay_ref[grows, :])
        token_major_u(j)
        return carry

    lax.fori_loop(0, ncx, scan_body, 0, unroll=16)


    def out_body(j, carry):
        rows = pl.ds(pl.multiple_of(j * lc, lc), lc)
        grows = pl.ds(pl.multiple_of(j * ng, ng), ng)
        q4 = qkx_ref[0, rows, :qkw]
        k4 = qkx_ref[0, rows, qkw:]
        vt = vtx_ref[0, j]
        u_col = ucol_ref[rows, :]
        m_run, w_int, floor = mrun_ref[grows, :], wint_ref[grows, :], floor_ref[grows, :]
        zero = jnp.zeros_like(q4)
        qs = jnp.concatenate([jnp.where(head_of_lane == h, q4, zero) for h in range(ML_HEADS)], axis=0)
        both = _dot_nt(jnp.concatenate([k4, sin_ref[0, j], sin_ref[1, j]], axis=0), qs)
        qk_t = both[:lc]
        for h in range(ML_HEADS):
            cols_h = slice(h * lc, (h + 1) * lc)
            hs = None
            for dirn in range(2):
                c = hg * dirn + h
                inter = both[lc + dirn * sr:lc + (dirn + 1) * sr, cols_h]
                valid = triu if dirn == 0 else tril
                e = jnp.exp2(jnp.where(valid, u_col[:, c:c + 1] - m_run[c:c + 1], -jnp.inf))
                s_t = qk_t[:, cols_h] * e
                num = _dot(vt[h * ML_V_DIM:(h + 1) * ML_V_DIM], s_t.astype(BF16))
                num = num + w_int[c:c + 1] * inter[:ML_V_DIM]
                den = jnp.sum(s_t, axis=0, keepdims=True) + w_int[c:c + 1] * inter[ML_V_DIM:ML_V_DIM + 1]
                hv = num * (1.0 / jnp.maximum(jnp.abs(den), floor[c:c + 1]))
                hs = hv if hs is None else hs + hv
            cols = slice(h * ML_V_DIM, (h + 1) * ML_V_DIM)
            hn = hs * lax.rsqrt(jnp.mean(hs * hs, axis=0, keepdims=True) + EPS)
            out_ref[0, rows, cols] = (hn.T * ox_ref[0, rows, cols].astype(F32)).astype(BF16)
        return carry

    lax.fori_loop(0, ncx, out_body, 0, unroll=16)


def _mlstm(qkx, vtx, ox, gcx, grx, qkc, vtc, grc):
    bsz, s, _ = qkx.shape
    ncx, ncc = s // ML_CHUNK, qkc.shape[1] // ML_CHUNK
    per_b = lambda a: pl.BlockSpec((1,) + a.shape[1:], lambda b: (b,) + (0,) * (a.ndim - 1))
    ins = [qkx, vtx, ox, gcx, grx, qkc, vtc, grc]
    return pl.pallas_call(
        functools.partial(_mlstm_kernel, ncx=ncx, ncc=ncc),
        grid=(bsz,),
        in_specs=[per_b(a) for a in ins],
        out_specs=pl.BlockSpec((1, s, ML_WIDTH), lambda b: (b, 0, 0)),
        out_shape=jax.ShapeDtypeStruct((bsz, s, ML_WIDTH), BF16),
        scratch_shapes=[pltpu.VMEM((2, ncx, ML_STATE_ROWS, ML_QK_WIDTH), BF16),
                        pltpu.VMEM((2, ML_STATE_ROWS, ML_QK_WIDTH), F32),
                        pltpu.VMEM((s, LANES), F32)]
                       + [pltpu.VMEM((ncx * ML_GATES, ML_CHUNK), F32)] * 5,
        compiler_params=pltpu.CompilerParams(dimension_semantics=("arbitrary",),
                                             vmem_limit_bytes=VMEM_LIMIT),
        name="mlstm",
    )(*ins)


def _out_ffn_kernel(x_ref, att_ref, ml_ref, gtm_ref, shf_ref, scf_ref, gtf_ref,
                    gpm_ref, gpf_ref, gqf_ref, woa_ref, wom_ref, wfi_ref, wfo_ref, o_ref, *, hidden,
                    tiles_per_batch):
    r_mod = pl.ds((pl.program_id(0) // tiles_per_batch) % SUBLANES, 1)
    gtm, shf, scf, gtf = gtm_ref[r_mod, :], shf_ref[r_mod, :], scf_ref[r_mod, :], gtf_ref[r_mod, :]

    def group(g):
        r = slice(g * FFN_ROWS, (g + 1) * FFN_ROWS)
        mix = _dot(att_ref[r, :], woa_ref[...]) + _dot(ml_ref[r, :], wom_ref[...])
        yield
        x1 = x_ref[r, :] + gtm * _rms(mix, gpm_ref[...])
        h = (_rms(x1, gpf_ref[...]) * (1.0 + scf) + shf).astype(BF16)
        gu = _dot(h, wfi_ref[...])
        yield
        act = (_silu(gu[:, :hidden]) * gu[:, hidden:]).astype(BF16)
        fx = _dot(act, wfo_ref[...])
        yield
        o_ref[r, :] = x1 + gtf * _rms(fx, gqf_ref[...])

    done = object()
    waiting = [group(g) for g in range(x_ref.shape[0] // FFN_ROWS)]
    active, rounds = [], 0
    while waiting or active:
        if rounds % FFN_PAIR_PERIOD == 0:
            active += [waiting.pop(0) for _ in range(min(2, len(waiting)))]
        active = [g for g in active if next(g, done) is not done]
        rounds += 1


def _out_ffn(x2, att2, ml2, mod, g_post_mix, g_pre_ffn, g_post_ffn, woa, wom, wfi, wfo, *, tiles_per_batch):
    t, d = x2.shape
    tm = FFN_TM
    hidden = wfo.shape[0]
    resident = lambda a: pl.BlockSpec(a.shape, lambda i: (0,) * a.ndim, pipeline_mode=pl.Buffered(1))
    mod_spec = lambda k: pl.BlockSpec((SUBLANES, d), lambda i: (i // tiles_per_batch // SUBLANES, k))
    row = pl.BlockSpec((1, d), lambda i: (0, 0))
    return pl.pallas_call(
        functools.partial(_out_ffn_kernel, hidden=hidden, tiles_per_batch=tiles_per_batch),
        grid=(t // tm,),
        in_specs=[pl.BlockSpec((tm, d), lambda i: (i, 0)),
                  pl.BlockSpec((tm, ATT_WIDTH), lambda i: (i, 0)),
                  pl.BlockSpec((tm, ML_WIDTH), lambda i: (i, 0)),
                  mod_spec(2), mod_spec(3), mod_spec(4), mod_spec(5), row, row, row,
                  resident(woa), resident(wom), resident(wfi), resident(wfo)],
        out_specs=pl.BlockSpec((tm, d), lambda i: (i, 0)),
        out_shape=jax.ShapeDtypeStruct((t, d), F32),
        compiler_params=pltpu.CompilerParams(dimension_semantics=("arbitrary",),
                                             vmem_limit_bytes=VMEM_LIMIT),
        name="out_ffn",
    )(x2, att2, ml2, mod, mod, mod, mod, g_post_mix, g_pre_ffn, g_post_ffn, woa, wom, wfi, wfo)


def _rope_tables(n_tokens):
    pos = jnp.arange(n_tokens)
    row = (pos // GRID_W).astype(F32)
    col = (pos % GRID_W).astype(F32)
    inv_freq = jnp.power(ROPE_BASE, -jnp.arange(ROPE_PAIR, dtype=F32) / ROPE_PAIR)
    ang_r = row[:, None] * inv_freq
    ang_c = col[:, None] * inv_freq
    z = jnp.zeros_like(ang_r)
    reps = LANES // ATT_HEAD_DIM
    cos = jnp.tile(jnp.concatenate([jnp.cos(ang_r)] * 2 + [jnp.cos(ang_c)] * 2, axis=1), (1, reps))
    sin_lo = jnp.tile(jnp.concatenate([-jnp.sin(ang_r), z, -jnp.sin(ang_c), z], axis=1), (1, reps))
    sin_hi = jnp.tile(jnp.concatenate([z, jnp.sin(ang_r), z, jnp.sin(ang_c)], axis=1), (1, reps))
    return cos, sin_lo, sin_hi


def _permute_heads(w, axis):
    shape = w.shape
    grouped = shape[:axis] + (ATT_KV_HEADS, ATT_GROUP, ATT_HEAD_DIM) + shape[axis + 1:]
    return jnp.swapaxes(w.reshape(grouped), axis, axis + 1).reshape(shape)


def kernel(x, c, ctx, c_ctx, w_ada, b_ada, g_pre_mix, w_in, w_conv_qk, b_gates, attn_sink,
           g_mlstm_out, w_out, g_post_mix, g_pre_ffn, w_ffn_in, w_ffn_out, g_post_ffn):
    bsz, s, d = x.shape
    l = ctx.shape[1]
    assert w_ada.shape[0] == 1, "single-layer block"
    assert bsz < MOD_ROWS and s % INPROJ_TM == 0 and l % ML_CHUNK == 0 and s % FFN_TM == 0
    assert bsz % CTX_PER_STEP == 0 and l % LANES == 0

    mod = _ada(c, c_ctx, w_ada[0], b_ada)

    w = w_in[0]
    o_q, o_k, o_v = 0, ATT_WIDTH, ATT_WIDTH + ATT_KV_WIDTH
    o_mq = o_v + ATT_KV_WIDTH
    o_mv = o_mq + 2 * ML_QK_WIDTH
    o_mo = o_mv + ML_WIDTH
    o_mg = o_mo + ML_WIDTH
    w_q = _permute_heads(w[:, o_q:o_k], 1) * (ATT_HEAD_DIM ** -0.5 * LOG2_E)
    w_g = jnp.pad(w[:, o_mg:], ((0, 0), (0, LANES - ML_GATES)))
    shared = [w[:, o_k:o_v], w_g, w[:, o_mq:o_mv]]
    w_lat = jnp.concatenate([w_q] + shared + [w[:, o_mo:o_mg]], axis=1).astype(BF16)
    wvt = jnp.concatenate([w[:, o_v:o_mq], w[:, o_mv:o_mo], w[:, o_mg:]], axis=1).T.astype(BF16)
    bg_row = jnp.pad(b_gates, ((0, 0), (0, LANES - ML_GATES)))
    bg_col = b_gates.reshape(ML_GATES, 1)
    wc = w_conv_qk[0]

    q, k, vt, qkx, vtx, ox, gcx, grx = _inproj(
        x, mod, None, g_pre_mix, w_lat, wvt, bg_row, bg_col, wc, _rope_tables(s), g_mlstm_out,
        tm=INPROJ_TM, gm=INPROJ_ROWS, latent=True)
    nctx = bsz // CTX_PER_STEP
    kc, vct, qkc, vtc, _, grc = _inproj(
        ctx.reshape(nctx, CTX_PER_STEP * l, d), mod, bsz, g_pre_mix, w_lat, wvt, bg_row, bg_col, wc, None, None,
        tm=CTX_PER_STEP * l, gm=l, latent=False)
    kc = kc.reshape(bsz, l, ATT_KV_WIDTH)
    qkc = qkc.reshape(bsz, l, 2 * ML_QK_WIDTH)
    vtc = vtc.reshape((bsz, l // ML_CHUNK) + vtc.shape[2:])
    grc = grc.reshape((bsz, l // ML_CHUNK) + grc.shape[2:])

    sink_row = jnp.repeat(attn_sink[0][jnp.array(_HEAD_PERM)] * LOG2_E, ATT_BLOCK)[None, :]
    att = _attention(q, k, vt, kc, vct, sink_row)
    ml = _mlstm(qkx, vtx, ox, gcx, grx, qkc, vtc, grc)

    wo = w_out[0]
    woa = _permute_heads(wo[:ATT_WIDTH], 0).astype(BF16)
    wom = wo[ATT_WIDTH:].astype(BF16)
    out = _out_ffn(x.reshape(bsz * s, d), att.reshape(bsz * s, ATT_WIDTH), ml.reshape(bsz * s, ML_WIDTH),
                   mod, g_post_mix, g_pre_ffn, g_post_ffn, woa, wom,
                   w_ffn_in[0].astype(BF16), w_ffn_out[0].astype(BF16), tiles_per_batch=s // FFN_TM)
    return out.reshape(bsz, s, d)
```

```python
import functools

import jax
import jax.numpy as jnp
from jax import lax
from jax.experimental import pallas as pl
from jax.experimental.pallas import tpu as pltpu

F32 = jnp.float32
BF16 = jnp.bfloat16

EPS = 1e-6
GRID_W = 64
ROPE_BASE = 10000.0
LOG2_E = 1.4426950408889634

ATT_HEADS = 8
ATT_KV_HEADS = 2
ATT_GROUP = ATT_HEADS // ATT_KV_HEADS
ATT_HEAD_DIM = 64
ROPE_PAIR = ATT_HEAD_DIM // 4
ATT_BLOCK = 128
ATT_WIDTH = ATT_HEADS * ATT_HEAD_DIM
ATT_KV_WIDTH = ATT_KV_HEADS * ATT_HEAD_DIM
ATT_UNIT_GROUPS = 2
ATT_ONES_ROWS = 16

ML_HEADS = 4
ML_V_DIM = 128
ML_QK_DIM = 64
ML_WIDTH = ML_HEADS * ML_V_DIM
ML_QK_WIDTH = ML_HEADS * ML_QK_DIM
ML_GATES = 4 * ML_HEADS
ML_CHUNK = 128

LANES = 128
SUBLANES = 8
VMEM_LIMIT = 56 * 1024 * 1024

INPROJ_TM = 1024
INPROJ_ROWS = 256
CTX_PER_STEP = 2
FFN_TM = 1024
FFN_ROWS = 256
FFN_PAIR_PERIOD = 3
ADA_TN = 1536
MOD_ROWS = 16

_HEAD_PERM = tuple(h * ATT_GROUP + g for g in range(ATT_GROUP) for h in range(ATT_KV_HEADS))


def _silu(v):
    return v * jax.nn.sigmoid(v)


def _log_sigmoid(v):
    return jnp.minimum(v, 0.0) - jnp.log1p(jnp.exp(-jnp.abs(v)))


def _rms(v, g):
    return v * lax.rsqrt(jnp.mean(v * v, axis=-1, keepdims=True) + EPS) * g


def _dot(a, b):
    return jnp.dot(a, b, preferred_element_type=F32)


def _dot_nt(a, b):
    return lax.dot_general(a, b, (((1,), (1,)), ((), ())), preferred_element_type=F32)


def _ada_kernel(c_ref, cctx_ref, w_ref, b_ref, o_ref):
    pad = jnp.zeros((MOD_ROWS - c_ref.shape[0] - 1, c_ref.shape[1]), F32)
    a = _silu(jnp.concatenate([c_ref[...], cctx_ref[...], pad], axis=0))
    o_ref[...] = _dot(a.astype(BF16), w_ref[...].astype(BF16)) + b_ref[...]


def _ada(c, c_ctx, w, b):
    d, n = w.shape
    return pl.pallas_call(
        _ada_kernel,
        grid=(n // ADA_TN,),
        in_specs=[pl.BlockSpec(c.shape, lambda j: (0, 0)),
                  pl.BlockSpec((1, d), lambda j: (0, 0)),
                  pl.BlockSpec((d, ADA_TN), lambda j: (0, j)),
                  pl.BlockSpec((1, ADA_TN), lambda j: (0, j))],
        out_specs=pl.BlockSpec((MOD_ROWS, ADA_TN), lambda j: (0, j)),
        out_shape=jax.ShapeDtypeStruct((MOD_ROWS, n), F32),
        compiler_params=pltpu.CompilerParams(dimension_semantics=("arbitrary",),
                                             vmem_limit_bytes=VMEM_LIMIT),
        name="ada",
    )(c, c_ctx.reshape(1, d), w, b)


def _rope(v, cos, sin_lo, sin_hi):
    return (v * cos + pltpu.roll(v, LANES - ROPE_PAIR, axis=1) * sin_lo
            + pltpu.roll(v, ROPE_PAIR, axis=1) * sin_hi)


def _inproj_kernel(*refs, tm, gm, n_tiles, latent, mod_row):
    if latent:
        (x_ref, xp_ref, xn_ref, sh_ref, sc_ref, g_ref, w_ref, wvt_ref, bgr_ref, bgc_ref,
         wc_ref, cos_ref, sl_ref, shi_ref, gain_ref,
         q_ref, k_ref, vt_ref, qk_ref, mvt_ref, o_ref, gcol_ref, grow_ref) = refs
    else:
        (x_ref, xp_ref, xn_ref, sh_ref, sc_ref, g_ref, w_ref, wvt_ref, bgr_ref, bgc_ref,
         wc_ref,
         k_ref, vt_ref, qk_ref, mvt_ref, gcol_ref, grow_ref) = refs
    i = pl.program_id(0)
    n_groups = tm // gm
    r_mod = (pl.program_id(1) if mod_row is None else mod_row) % SUBLANES
    scale = g_ref[...] * (1.0 + sc_ref[pl.ds(r_mod, 1), :])
    shift = sh_ref[pl.ds(r_mod, 1), :]
    wc = wc_ref[...]
    row = lax.broadcasted_iota(jnp.int32, (gm, 1), 0)

    def project(lo, width, lhs):
        return _dot(lhs, w_ref[:, lo:lo + width])

    def group(r):
        rows = slice(r * gm, (r + 1) * gm)
        before = xp_ref[0] if r == 0 else x_ref[0, r * gm - SUBLANES:r * gm, :]
        after = xn_ref[0] if r == n_groups - 1 else x_ref[0, (r + 1) * gm:(r + 1) * gm + SUBLANES, :]
        keep_prev = (jnp.where(i == 0, 0.0, 1.0) if r == 0 else 1.0) if latent else 0.0
        keep_next = (jnp.where(i == n_tiles - 1, 0.0, 1.0) if r == n_groups - 1 else 1.0) if latent else 0.0
        xt = jnp.concatenate([x_ref[0, rows, :], before, after], axis=0)
        ms = jnp.mean(xt * xt, axis=-1, keepdims=True)
        hb = (xt * lax.rsqrt(ms + EPS) * scale + shift).astype(BF16)
        hm = hb[:gm]
        yield

        c = ATT_WIDTH
        if latent:
            r_q = project(0, ATT_WIDTH, hm)
        r_kg = project(c, ATT_KV_WIDTH + LANES, hm)
        c += ATT_KV_WIDTH + LANES
        yield

        if latent:
            cos, sl, shi = cos_ref[rows, :], sl_ref[rows, :], shi_ref[rows, :]
            for g in range(ATT_WIDTH // LANES):
                q_ref[0, rows, g * LANES:(g + 1) * LANES] = _rope(
                    r_q[:, g * LANES:(g + 1) * LANES], cos, sl, shi).astype(BF16)
        y = project(c, 2 * ML_QK_WIDTH, hb)
        c += 2 * ML_QK_WIDTH
        yield

        if latent:
            k_ref[0, rows, :] = _rope(r_kg[:, :LANES], cos, sl, shi).astype(BF16)
        else:
            k_ref[0, rows, :] = r_kg[:, :LANES].astype(BF16)
        gcol_ref[0, rows, :] = r_kg[:, LANES:] + bgr_ref[...]
        v_t = _dot_nt(wvt_ref[...], hm)
        yield

        ym = y[:gm]
        prev = jnp.where(row == 0, y[gm + SUBLANES - 1:gm + SUBLANES] * keep_prev,
                         pltpu.roll(ym, 1, axis=0))
        nxt = jnp.where(row == gm - 1, y[gm + SUBLANES:gm + SUBLANES + 1] * keep_next,
                        pltpu.roll(ym, gm - 1, axis=0))
        act = _silu(prev * wc[0:1] + ym * wc[1:2] + nxt * wc[2:3])
        qk_ref[0, rows, :ML_QK_WIDTH] = (act[:, :ML_QK_WIDTH] * (ML_QK_DIM ** -0.5)).astype(BF16)
        qk_ref[0, rows, ML_QK_WIDTH:] = act[:, ML_QK_WIDTH:].astype(BF16)
        if latent:
            r_o = project(c, ML_WIDTH, hm)
        yield

        g_t = v_t[ATT_KV_WIDTH + ML_WIDTH:] + bgc_ref[...]
        vt_ref[0, :, rows] = v_t[:ATT_KV_WIDTH].astype(BF16)
        for j in range(gm // ML_CHUNK):
            cols = slice(j * ML_CHUNK, (j + 1) * ML_CHUNK)
            mvt_ref[0, r * (gm // ML_CHUNK) + j] = v_t[ATT_KV_WIDTH:ATT_KV_WIDTH + ML_WIDTH, cols].astype(BF16)
            grow_ref[0, r * (gm // ML_CHUNK) + j] = g_t[:, cols]
        if latent:
            o_ref[0, rows, :] = (jax.nn.sigmoid(r_o) * gain_ref[...]).astype(BF16)

    done = object()
    waiting = [group(r) for r in range(n_groups)]
    active = []
    while waiting or active:
        if waiting:
            active.append(waiting.pop(0))
        active = [g for g in active if next(g, done) is not done]


def _inproj(x, mod, mod_row, g_pre, w_main, wvt, bg_row, bg_col, wc, rope_tabs, gain, *, tm, gm, latent):
    bsz, t, d = x.shape
    n_tiles = t // tm
    hb = tm // SUBLANES
    n_hblk = t // SUBLANES
    n = w_main.shape[1]
    mod_blk = (lambda b: b // SUBLANES) if mod_row is None else (lambda b: mod_row // SUBLANES)

    def const(shape):
        return pl.BlockSpec(shape, lambda i, b: (0,) * len(shape))

    in_specs = [
        pl.BlockSpec((1, tm, d), lambda i, b: (b, i, 0)),
        pl.BlockSpec((1, SUBLANES, d), lambda i, b: (b, jnp.maximum(i * hb - 1, 0), 0)),
        pl.BlockSpec((1, SUBLANES, d), lambda i, b: (b, jnp.minimum((i + 1) * hb, n_hblk - 1), 0)),
        pl.BlockSpec((SUBLANES, d), lambda i, b: (mod_blk(b), 0)),
        pl.BlockSpec((SUBLANES, d), lambda i, b: (mod_blk(b), 1)),
        const((1, d)), const((d, n)), const((ATT_KV_WIDTH + ML_WIDTH + ML_GATES, d)),
        const((1, LANES)), const((ML_GATES, 1)), const((3, 2 * ML_QK_WIDTH)),
    ]
    args = [x, x, x, mod, mod, g_pre, w_main, wvt, bg_row, bg_col, wc]
    tok = lambda width, dt: (pl.BlockSpec((1, tm, width), lambda i, b: (b, i, 0)),
                             jax.ShapeDtypeStruct((bsz, t, width), dt))
    chunked = lambda rows, dt: (pl.BlockSpec((1, tm // ML_CHUNK, rows, ML_CHUNK), lambda i, b: (b, i, 0, 0)),
                                jax.ShapeDtypeStruct((bsz, t // ML_CHUNK, rows, ML_CHUNK), dt))
    outs = []
    if latent:
        in_specs += [pl.BlockSpec((tm, LANES), lambda i, b: (i, 0))] * 3 + [const((1, ML_WIDTH))]
        args += list(rope_tabs) + [gain]
        outs.append(tok(ATT_WIDTH, BF16))
    outs.append(tok(ATT_KV_WIDTH, BF16))
    outs.append((pl.BlockSpec((1, LANES, tm), lambda i, b: (b, 0, i)),
                 jax.ShapeDtypeStruct((bsz, ATT_KV_WIDTH, t), BF16)))
    outs.append(tok(2 * ML_QK_WIDTH, BF16))
    outs.append(chunked(ML_WIDTH, BF16))
    if latent:
        outs.append(tok(ML_WIDTH, BF16))
    outs.append(tok(LANES, F32))
    outs.append(chunked(ML_GATES, F32))
    return pl.pallas_call(
        functools.partial(_inproj_kernel, tm=tm, gm=gm, n_tiles=n_tiles, latent=latent, mod_row=mod_row),
        grid=(n_tiles, bsz),
        in_specs=in_specs,
        out_specs=[o[0] for o in outs],
        out_shape=[o[1] for o in outs],
        compiler_params=pltpu.CompilerParams(dimension_semantics=("arbitrary", "arbitrary"),
                                             vmem_limit_bytes=VMEM_LIMIT),
        name="inproj_latent" if latent else "inproj_context",
    )(*args)


def _attn_kernel(q_ref, k_ref, kx_ref, vt_ref, vx_ref, sink_ref, o_ref):
    blk = ATT_BLOCK
    lane = lax.broadcasted_iota(jnp.int32, (blk, LANES), 1)
    zero = jnp.zeros((blk, LANES), BF16)
    half_groups = ATT_UNIT_GROUPS
    n_slots = ATT_KV_HEADS * half_groups
    n_parts = ATT_GROUP // half_groups
    sink = [sink_ref[:, part * n_slots * blk:(part + 1) * n_slots * blk] for part in range(n_parts)]

    def stack_heads(q, half):
        parts = []
        for g in range(half * half_groups, (half + 1) * half_groups):
            slab = q[:, g * LANES:(g + 1) * LANES]
            parts.append(jnp.where(lane < ATT_HEAD_DIM, slab, zero))
            parts.append(jnp.where(lane >= ATT_HEAD_DIM, slab, zero))
        return jnp.concatenate(parts, axis=0)

    key = lax.broadcasted_iota(jnp.int32, (blk, blk), 0)
    qry = lax.broadcasted_iota(jnp.int32, (blk, blk), 1)
    ninf = jnp.full((blk, blk), -jnp.inf, F32)
    bias_prev = jnp.where(key >= qry, 0.0, ninf)
    bias_next = jnp.where(key <= qry, 0.0, ninf)
    slots = lambda b: jnp.concatenate([b] * n_slots, axis=1)

    def scores(qs, k_prev, k_cur, k_next):
        return [_dot_nt(k_prev, qs), _dot_nt(k_cur, qs), _dot_nt(k_next, qs), _dot_nt(kx_ref[0], qs)]

    def softmax(s, b_prev, b_next, sink_h):
        s = jnp.concatenate([s[0] + slots(b_prev), s[1], s[2] + slots(b_next), s[3]], axis=0)
        m = jnp.maximum(sink_h, jnp.max(s, axis=0, keepdims=True))
        return jnp.exp2(s - m).astype(BF16), jnp.exp2(sink_h - m)

    n_keys = 3 * blk + kx_ref.shape[1]
    ones_rows = jnp.ones((ATT_ONES_ROWS, n_keys), BF16)

    def weighted_values(p, v_prev, v_cur, v_next):
        vt = jnp.concatenate([v_prev, v_cur, v_next, vx_ref[0]], axis=1)
        return _dot(jnp.concatenate([vt, ones_rows], axis=0), p)

    dim = lax.broadcasted_iota(jnp.int32, (LANES, blk), 0)

    def emit(rows, half, ot, p_sink):
        ot = ot[:LANES] * (1.0 / (ot[LANES:LANES + 1] + p_sink))
        for j in range(half_groups):
            g = half * half_groups + j
            a = ot[:, (2 * j) * blk:(2 * j + 1) * blk]
            b = ot[:, (2 * j + 1) * blk:(2 * j + 2) * blk]
            o_ref[0, rows, g * LANES:(g + 1) * LANES] = jnp.where(dim < ATT_HEAD_DIM, a, b).T.astype(BF16)

    nblk = q_ref.shape[1] // blk
    rows_of = [slice(b * blk, (b + 1) * blk) for b in range(nblk)]
    k_blocks = [k_ref[0, r, :] for r in rows_of]
    v_blocks = [vt_ref[0, :, r] for r in rows_of]
    near = lambda blocks, b: (blocks[max(b - 1, 0)], blocks[b], blocks[min(b + 1, nblk - 1)])
    bias = [(ninf if b == 0 else bias_prev, ninf if b == nblk - 1 else bias_next) for b in range(nblk)]
    units = [(b, half) for b in range(nblk) for half in range(n_parts)]

    def stage_scores(u):
        b, half = u
        return scores(stack_heads(q_ref[0, rows_of[b], :], half), *near(k_blocks, b))

    def stage_softmax(u, s):
        b, half = u
        return softmax(s, *bias[b], sink[half])

    def stage_values(u, p):
        return weighted_values(p, *near(v_blocks, u[0]))

    def stage_emit(u, ot, p_sink):
        emit(rows_of[u[0]], u[1], ot, p_sink)

    n_units = len(units)
    s, p, ot = {}, {}, {}
    s[0] = stage_scores(units[0])
    for t in range(n_units):
        if t + 1 < n_units:
            s[t + 1] = stage_scores(units[t + 1])
        p[t] = stage_softmax(units[t], s.pop(t))
        if t >= 1:
            ot[t - 1] = stage_values(units[t - 1], p[t - 1][0])
        if t >= 2:
            stage_emit(units[t - 2], ot.pop(t - 2), p.pop(t - 2)[1])
    ot[n_units - 1] = stage_values(units[n_units - 1], p[n_units - 1][0])
    for t in (n_units - 2, n_units - 1):
        stage_emit(units[t], ot.pop(t), p.pop(t)[1])


def _attention(q, k, vt, kx, vxt, sink_row):
    bsz, s, _ = q.shape
    l = kx.shape[1]
    per_b = lambda a: pl.BlockSpec((1,) + a.shape[1:], lambda b: (b, 0, 0))
    vx_spec = pl.BlockSpec((1, ATT_KV_WIDTH, l), lambda b: (b // CTX_PER_STEP, 0, b % CTX_PER_STEP))
    return pl.pallas_call(
        _attn_kernel,
        grid=(bsz,),
        in_specs=[per_b(q), per_b(k), per_b(kx), per_b(vt), vx_spec,
                  pl.BlockSpec(sink_row.shape, lambda b: (0, 0))],
        out_specs=pl.BlockSpec((1, s, ATT_WIDTH), lambda b: (b, 0, 0)),
        out_shape=jax.ShapeDtypeStruct((bsz, s, ATT_WIDTH), BF16),
        compiler_params=pltpu.CompilerParams(dimension_semantics=("arbitrary",),
                                             vmem_limit_bytes=VMEM_LIMIT),
        name="attention",
    )(q, k, kx, vt, vxt, sink_row)


BF16_SUBLANES = 16
ML_STATE_ROWS = ML_V_DIM + BF16_SUBLANES


def _mlstm_kernel(qkx_ref, vtx_ref, ox_ref, gcx_ref, grx_ref, qkc_ref, vtc_ref, grc_ref,
                  out_ref, sin_ref, st_ref, ucol_ref, mrun_ref, wint_ref, floor_ref,
                  wkey_ref, decay_ref, *, ncx, ncc):
    lc = ML_CHUNK
    qkw = ML_QK_WIDTH
    sr = ML_STATE_ROWS
    ng = ML_GATES
    hg = ML_GATES // 2
    st_ref[...] = jnp.zeros(st_ref.shape, F32)

    rr = lax.broadcasted_iota(jnp.int32, (lc, lc), 0)
    cc = lax.broadcasted_iota(jnp.int32, (lc, lc), 1)
    tril = rr >= cc
    triu = rr <= cc
    tril_b = jnp.where(tril, 1.0, 0.0).astype(BF16)
    triu_b = jnp.where(triu, 1.0, 0.0).astype(BF16)
    head_of_lane = lax.broadcasted_iota(jnp.int32, (lc, qkw), 1) // ML_QK_DIM
    head_of_state_lane = lax.broadcasted_iota(jnp.int32, (1, qkw), 1) // ML_QK_DIM

    def split(v):
        hi = v.astype(BF16)
        return hi, (v - hi.astype(F32)).astype(BF16)

    def gate_rows(g):
        n16 = g.shape[0]
        fwd_row = (lax.broadcasted_iota(jnp.int32, g.shape, 0) & (ng - 1)) < hg
        lane = lax.broadcasted_iota(jnp.int32, g.shape, 1)
        hi, lo = split(_log_sigmoid(g))
        cat = jnp.concatenate([hi, lo], axis=0)
        bu = _dot(cat, triu_b)
        bl = _dot(cat, tril_b)
        b = jnp.where(fwd_row, bu[:n16] + bu[n16:], bl[:n16] + bl[n16:])
        b = pltpu.roll(b, n16 - ML_HEADS, axis=0)
        u = g - b
        run_f = run_b = u
        k = 1
        while k < lc:
            run_f = jnp.maximum(run_f, jnp.where(lane >= k, pltpu.roll(run_f, k, axis=1), -jnp.inf))
            run_b = jnp.maximum(run_b, jnp.where(lane < lc - k, pltpu.roll(run_b, lc - k, axis=1), -jnp.inf))
            k *= 2
        run = jnp.where(fwd_row, run_f, run_b)

        def at_end(a):
            return jnp.where(fwd_row, jnp.broadcast_to(a[:, lc - 1:lc], a.shape),
                             jnp.broadcast_to(a[:, 0:1], a.shape))

        return u, b, run, at_end(b), at_end(run)

    def derived(u, b, run, b_tot, run_end, m_in):
        m_run = jnp.maximum(run, m_in)
        m_out = b_tot + jnp.maximum(run_end, m_in)
        return (m_run * LOG2_E, jnp.exp(m_in - m_run), jnp.exp(-(b + m_run)),
                jnp.exp(b_tot + u - m_out), jnp.exp(b_tot + m_in - m_out))

    gc_rows = gate_rows(grc_ref[0].reshape(ncc * ng, lc))
    gx_rows = gate_rows(grx_ref[0].reshape(ncx * ng, lc))

    def scan_m(rows, n, m_f, m_b):
        _, _, _, b_tot, run_end = rows
        part = lambda a, c, d: a[c * ng + d * hg:c * ng + (d + 1) * hg]
        ins_f, ins_b = [], [None] * n
        for c in range(n):
            ins_f.append(m_f)
            m_f = part(b_tot, c, 0) + jnp.maximum(part(run_end, c, 0), m_f)
        for c in reversed(range(n)):
            ins_b[c] = m_b
            m_b = part(b_tot, c, 1) + jnp.maximum(part(run_end, c, 1), m_b)
        return jnp.concatenate([x for c in range(n) for x in (ins_f[c], ins_b[c])], axis=0), m_f, m_b

    m0 = jnp.zeros((hg, lc), F32)
    m_in_c, m_f, m_b = scan_m(gc_rows, ncc, m0, m0)
    m_in_x, _, _ = scan_m(gx_rows, ncx, m_f, m_b)
    _, _, _, wkey_c, decay_c = derived(*gc_rows, m_in_c)
    for ref, val in zip((mrun_ref, wint_ref, floor_ref, wkey_ref, decay_ref), derived(*gx_rows, m_in_x)):
        ref[...] = val

    fwd_col = (lax.broadcasted_iota(jnp.int32, (lc, LANES), 1) & (ng - 1)) < hg

    def token_major_u(j):
        rows = pl.ds(pl.multiple_of(j * lc, lc), lc)
        gcol = gcx_ref[0, rows, :]
        hi, lo = split(_log_sigmoid(gcol))
        cat = jnp.concatenate([hi, lo], axis=1)
        bl = _dot(tril_b, cat)
        bu = _dot(triu_b, cat)
        b = jnp.where(fwd_col, bl[:, :LANES] + bl[:, LANES:], bu[:, :LANES] + bu[:, LANES:])
        ucol_ref[rows, :] = (gcol - pltpu.roll(b, LANES - ML_HEADS, axis=1)) * LOG2_E

    def advance(dirn, k4, vt, w_key, decay):
        st = st_ref[dirn]
        lhs, rhs, decay_row = [], [], None
        for h in range(ML_HEADS):
            c = hg * dirn + h
            wk = w_key[c:c + 1].astype(BF16)
            lhs.append(jnp.concatenate([vt[h * ML_V_DIM:(h + 1) * ML_V_DIM] * wk,
                                        jnp.broadcast_to(wk, (BF16_SUBLANES, lc))], axis=0))
            rhs.append(jnp.where(head_of_lane == h, k4, jnp.zeros_like(k4)))
            d_h = jnp.concatenate([decay[c:c + 1]] * (qkw // lc), axis=1)
            decay_row = d_h if decay_row is None else jnp.where(head_of_state_lane == h, d_h, decay_row)
        upd = _dot(jnp.concatenate(lhs, axis=1), jnp.concatenate(rhs, axis=0))
        st_ref[dirn] = decay_row * st + upd

    for j in range(ncc):
        for dirn, cj in ((0, j), (1, ncc - 1 - j)):
            grows = slice(cj * ng, (cj + 1) * ng)
            advance(dirn, qkc_ref[0, cj * lc:(cj + 1) * lc, qkw:], vtc_ref[0, cj], wkey_c[grows], decay_c[grows])

    def scan_body(j, carry):
        for dirn, cj in ((0, j), (1, ncx - 1 - j)):
            rows = pl.ds(pl.multiple_of(cj * lc, lc), lc)
            grows = pl.ds(pl.multiple_of(cj * ng, ng), ng)
            sin_ref[dirn, cj] = st_ref[dirn].astype(BF16)
            advance(dirn, qkx_ref[0, rows, qkw:], vtx_ref[0, cj], wkey_ref[grows, :], decay_ref[grows, :])
        token_major_u(j)
        return carry

    lax.fori_loop(0, ncx, scan_body, 0, unroll=16)


    def out_body(j, carry):
        rows = pl.ds(pl.multiple_of(j * lc, lc), lc)
        grows = pl.ds(pl.multiple_of(j * ng, ng), ng)
        q4 = qkx_ref[0, rows, :qkw]
        k4 = qkx_ref[0, rows, qkw:]
        vt = vtx_ref[0, j]
        u_col = ucol_ref[rows, :]
        m_run, w_int, floor = mrun_ref[grows, :], wint_ref[grows, :], floor_ref[grows, :]
        zero = jnp.zeros_like(q4)
        qs = jnp.concatenate([jnp.where(head_of_lane == h, q4, zero) for h in range(ML_HEADS)], axis=0)
        both = _dot_nt(jnp.concatenate([k4, sin_ref[0, j], sin_ref[1, j]], axis=0), qs)
        qk_t = both[:lc]
        for h in range(ML_HEADS):
            cols_h = slice(h * lc, (h + 1) * lc)
            hs = None
            for dirn in range(2):
                c = hg * dirn + h
                inter = both[lc + dirn * sr:lc + (dirn + 1) * sr, cols_h]
                valid = triu if dirn == 0 else tril
                e = jnp.exp2(jnp.where(valid, u_col[:, c:c + 1] - m_run[c:c + 1], -jnp.inf))
                s_t = qk_t[:, cols_h] * e
                num = _dot(vt[h * ML_V_DIM:(h + 1) * ML_V_DIM], s_t.astype(BF16))
                num = num + w_int[c:c + 1] * inter[:ML_V_DIM]
                den = jnp.sum(s_t, axis=0, keepdims=True) + w_int[c:c + 1] * inter[ML_V_DIM:ML_V_DIM + 1]
                hv = num * (1.0 / jnp.maximum(jnp.abs(den), floor[c:c + 1]))
                hs = hv if hs is None else hs + hv
            cols = slice(h * ML_V_DIM, (h + 1) * ML_V_DIM)
            hn = hs * lax.rsqrt(jnp.mean(hs * hs, axis=0, keepdims=True) + EPS)
            out_ref[0, rows, cols] = (hn.T * ox_ref[0, rows, cols].astype(F32)).astype(BF16)
        return carry

    lax.fori_loop(0, ncx, out_body, 0, unroll=16)


def _mlstm(qkx, vtx, ox, gcx, grx, qkc, vtc, grc):
    bsz, s, _ = qkx.shape
    ncx, ncc = s // ML_CHUNK, qkc.shape[1] // ML_CHUNK
    per_b = lambda a: pl.BlockSpec((1,) + a.shape[1:], lambda b: (b,) + (0,) * (a.ndim - 1))
    ins = [qkx, vtx, ox, gcx, grx, qkc, vtc, grc]
    return pl.pallas_call(
        functools.partial(_mlstm_kernel, ncx=ncx, ncc=ncc),
        grid=(bsz,),
        in_specs=[per_b(a) for a in ins],
        out_specs=pl.BlockSpec((1, s, ML_WIDTH), lambda b: (b, 0, 0)),
        out_shape=jax.ShapeDtypeStruct((bsz, s, ML_WIDTH), BF16),
        scratch_shapes=[pltpu.VMEM((2, ncx, ML_STATE_ROWS, ML_QK_WIDTH), BF16),
                        pltpu.VMEM((2, ML_STATE_ROWS, ML_QK_WIDTH), F32),
                        pltpu.VMEM((s, LANES), F32)]
                       + [pltpu.VMEM((ncx * ML_GATES, ML_CHUNK), F32)] * 5,
        compiler_params=pltpu.CompilerParams(dimension_semantics=("arbitrary",),
                                             vmem_limit_bytes=VMEM_LIMIT),
        name="mlstm",
    )(*ins)


def _out_ffn_kernel(x_ref, att_ref, ml_ref, gtm_ref, shf_ref, scf_ref, gtf_ref,
                    gpm_ref, gpf_ref, gqf_ref, woa_ref, wom_ref, wfi_ref, wfo_ref, o_ref, *, hidden,
                    tiles_per_batch):
    r_mod = pl.ds((pl.program_id(0) // tiles_per_batch) % SUBLANES, 1)
    gtm, shf, scf, gtf = gtm_ref[r_mod, :], shf_ref[r_mod, :], scf_ref[r_mod, :], gtf_ref[r_mod, :]

    def group(g):
        r = slice(g * FFN_ROWS, (g + 1) * FFN_ROWS)
        mix = _dot(att_ref[r, :], woa_ref[...]) + _dot(ml_ref[r, :], wom_ref[...])
        yield
        x1 = x_ref[r, :] + gtm * _rms(mix, gpm_ref[...])
        h = (_rms(x1, gpf_ref[...]) * (1.0 + scf) + shf).astype(BF16)
        gu = _dot(h, wfi_ref[...])
        yield
        act = (_silu(gu[:, :hidden]) * gu[:, hidden:]).astype(BF16)
        fx = _dot(act, wfo_ref[...])
        yield
        o_ref[r, :] = x1 + gtf * _rms(fx, gqf_ref[...])

    done = object()
    waiting = [group(g) for g in range(x_ref.shape[0] // FFN_ROWS)]
    active, rounds = [], 0
    while waiting or active:
        if rounds % FFN_PAIR_PERIOD == 0:
            active += [waiting.pop(0) for _ in range(min(2, len(waiting)))]
        active = [g for g in active if next(g, done) is not done]
        rounds += 1


def _out_ffn(x2, att2, ml2, mod, g_post_mix, g_pre_ffn, g_post_ffn, woa, wom, wfi, wfo, *, tiles_per_batch):
    t, d = x2.shape
    tm = FFN_TM
    hidden = wfo.shape[0]
    resident = lambda a: pl.BlockSpec(a.shape, lambda i: (0,) * a.ndim, pipeline_mode=pl.Buffered(1))
    mod_spec = lambda k: pl.BlockSpec((SUBLANES, d), lambda i: (i // tiles_per_batch // SUBLANES, k))
    row = pl.BlockSpec((1, d), lambda i: (0, 0))
    return pl.pallas_call(
        functools.partial(_out_ffn_kernel, hidden=hidden, tiles_per_batch=tiles_per_batch),
        grid=(t // tm,),
        in_specs=[pl.BlockSpec((tm, d), lambda i: (i, 0)),
                  pl.BlockSpec((tm, ATT_WIDTH), lambda i: (i, 0)),
                  pl.BlockSpec((tm, ML_WIDTH), lambda i: (i, 0)),
                  mod_spec(2), mod_spec(3), mod_spec(4), mod_spec(5), row, row, row,
                  resident(woa), resident(wom), resident(wfi), resident(wfo)],
        out_specs=pl.BlockSpec((tm, d), lambda i: (i, 0)),
        out_shape=jax.ShapeDtypeStruct((t, d), F32),
        compiler_params=pltpu.CompilerParams(dimension_semantics=("arbitrary",),
                                             vmem_limit_bytes=VMEM_LIMIT),
        name="out_ffn",
    )(x2, att2, ml2, mod, mod, mod, mod, g_post_mix, g_pre_ffn, g_post_ffn, woa, wom, wfi, wfo)


def _rope_tables(n_tokens):
    pos = jnp.arange(n_tokens)
    row = (pos // GRID_W).astype(F32)
    col = (pos % GRID_W).astype(F32)
    inv_freq = jnp.power(ROPE_BASE, -jnp.arange(ROPE_PAIR, dtype=F32) / ROPE_PAIR)
    ang_r = row[:, None] * inv_freq
    ang_c = col[:, None] * inv_freq
    z = jnp.zeros_like(ang_r)
    reps = LANES // ATT_HEAD_DIM
    cos = jnp.tile(jnp.concatenate([jnp.cos(ang_r)] * 2 + [jnp.cos(ang_c)] * 2, axis=1), (1, reps))
    sin_lo = jnp.tile(jnp.concatenate([-jnp.sin(ang_r), z, -jnp.sin(ang_c), z], axis=1), (1, reps))
    sin_hi = jnp.tile(jnp.concatenate([z, jnp.sin(ang_r), z, jnp.sin(ang_c)], axis=1), (1, reps))
    return cos, sin_lo, sin_hi


def _permute_heads(w, axis):
    shape = w.shape
    grouped = shape[:axis] + (ATT_KV_HEADS, ATT_GROUP, ATT_HEAD_DIM) + shape[axis + 1:]
    return jnp.swapaxes(w.reshape(grouped), axis, axis + 1).reshape(shape)


def kernel(x, c, ctx, c_ctx, w_ada, b_ada, g_pre_mix, w_in, w_conv_qk, b_gates, attn_sink,
           g_mlstm_out, w_out, g_post_mix, g_pre_ffn, w_ffn_in, w_ffn_out, g_post_ffn):
    bsz, s, d = x.shape
    l = ctx.shape[1]
    assert w_ada.shape[0] == 1, "single-layer block"
    assert bsz < MOD_ROWS and s % INPROJ_TM == 0 and l % ML_CHUNK == 0 and s % FFN_TM == 0
    assert bsz % CTX_PER_STEP == 0 and l % LANES == 0

    mod = _ada(c, c_ctx, w_ada[0], b_ada)

    w = w_in[0]
    o_q, o_k, o_v = 0, ATT_WIDTH, ATT_WIDTH + ATT_KV_WIDTH
    o_mq = o_v + ATT_KV_WIDTH
    o_mv = o_mq + 2 * ML_QK_WIDTH
    o_mo = o_mv + ML_WIDTH
    o_mg = o_mo + ML_WIDTH
    w_q = _permute_heads(w[:, o_q:o_k], 1) * (ATT_HEAD_DIM ** -0.5 * LOG2_E)
    w_g = jnp.pad(w[:, o_mg:], ((0, 0), (0, LANES - ML_GATES)))
    shared = [w[:, o_k:o_v], w_g, w[:, o_mq:o_mv]]
    w_lat = jnp.concatenate([w_q] + shared + [w[:, o_mo:o_mg]], axis=1).astype(BF16)
    wvt = jnp.concatenate([w[:, o_v:o_mq], w[:, o_mv:o_mo], w[:, o_mg:]], axis=1).T.astype(BF16)
    bg_row = jnp.pad(b_gates, ((0, 0), (0, LANES - ML_GATES)))
    bg_col = b_gates.reshape(ML_GATES, 1)
    wc = w_conv_qk[0]

    q, k, vt, qkx, vtx, ox, gcx, grx = _inproj(
        x, mod, None, g_pre_mix, w_lat, wvt, bg_row, bg_col, wc, _rope_tables(s), g_mlstm_out,
        tm=INPROJ_TM, gm=INPROJ_ROWS, latent=True)
    nctx = bsz // CTX_PER_STEP
    kc, vct, qkc, vtc, _, grc = _inproj(
        ctx.reshape(nctx, CTX_PER_STEP * l, d), mod, bsz, g_pre_mix, w_lat, wvt, bg_row, bg_col, wc, None, None,
        tm=CTX_PER_STEP * l, gm=l, latent=False)
    kc = kc.reshape(bsz, l, ATT_KV_WIDTH)
    qkc = qkc.reshape(bsz, l, 2 * ML_QK_WIDTH)
    vtc = vtc.reshape((bsz, l // ML_CHUNK) + vtc.shape[2:])
    grc = grc.reshape((bsz, l // ML_CHUNK) + grc.shape[2:])

    sink_row = jnp.repeat(attn_sink[0][jnp.array(_HEAD_PERM)] * LOG2_E, ATT_BLOCK)[None, :]
    att = _attention(q, k, vt, kc, vct, sink_row)
    ml = _mlstm(qkx, vtx, ox, gcx, grx, qkc, vtc, grc)

    wo = w_out[0]
    woa = _permute_heads(wo[:ATT_WIDTH], 0).astype(BF16)
    wom = wo[ATT_WIDTH:].astype(BF16)
    out = _out_ffn(x.reshape(bsz * s, d), att.reshape(bsz * s, ATT_WIDTH), ml.reshape(bsz * s, ML_WIDTH),
                   mod, g_post_mix, g_pre_ffn, g_post_ffn, woa, wom,
                   w_ffn_in[0].astype(BF16), w_ffn_out[0].astype(BF16), tiles_per_batch=s // FFN_TM)
    return out.reshape(bsz, s, d)
```

```python
import functools

import jax
import jax.numpy as jnp
from jax import lax
from jax.experimental import pallas as pl
from jax.experimental.pallas import tpu as pltpu

F32 = jnp.float32
BF16 = jnp.bfloat16

EPS = 1e-6
GRID_W = 64
ROPE_BASE = 10000.0
LOG2_E = 1.4426950408889634

ATT_HEADS = 8
ATT_KV_HEADS = 2
ATT_GROUP = ATT_HEADS // ATT_KV_HEADS
ATT_HEAD_DIM = 64
ROPE_PAIR = ATT_HEAD_DIM // 4
ATT_BLOCK = 128
ATT_WIDTH = ATT_HEADS * ATT_HEAD_DIM
ATT_KV_WIDTH = ATT_KV_HEADS * ATT_HEAD_DIM
ATT_UNIT_GROUPS = 2
ATT_ONES_ROWS = 16

ML_HEADS = 4
ML_V_DIM = 128
ML_QK_DIM = 64
ML_WIDTH = ML_HEADS * ML_V_DIM
ML_QK_WIDTH = ML_HEADS * ML_QK_DIM
ML_GATES = 4 * ML_HEADS
ML_CHUNK = 128

LANES = 128
SUBLANES = 8
VMEM_LIMIT = 56 * 1024 * 1024

INPROJ_TM = 2048
INPROJ_ROWS = 256
CTX_PER_STEP = 2
FFN_TM = 1024
FFN_ROWS = 256
FFN_PAIR_PERIOD = 3
ADA_TN = 1536
MOD_ROWS = 16

_HEAD_PERM = tuple(h * ATT_GROUP + g for g in range(ATT_GROUP) for h in range(ATT_KV_HEADS))


def _silu(v):
    return v * jax.nn.sigmoid(v)


def _log_sigmoid(v):
    return jnp.minimum(v, 0.0) - jnp.log1p(jnp.exp(-jnp.abs(v)))


def _rms(v, g):
    return v * lax.rsqrt(jnp.mean(v * v, axis=-1, keepdims=True) + EPS) * g


def _dot(a, b):
    return jnp.dot(a, b, preferred_element_type=F32)


def _dot_nt(a, b):
    return lax.dot_general(a, b, (((1,), (1,)), ((), ())), preferred_element_type=F32)


def _ada_kernel(c_ref, cctx_ref, w_ref, b_ref, o_ref):
    pad = jnp.zeros((MOD_ROWS - c_ref.shape[0] - 1, c_ref.shape[1]), F32)
    a = _silu(jnp.concatenate([c_ref[...], cctx_ref[...], pad], axis=0))
    o_ref[...] = _dot(a.astype(BF16), w_ref[...].astype(BF16)) + b_ref[...]


def _ada(c, c_ctx, w, b):
    d, n = w.shape
    return pl.pallas_call(
        _ada_kernel,
        grid=(n // ADA_TN,),
        in_specs=[pl.BlockSpec(c.shape, lambda j: (0, 0)),
                  pl.BlockSpec((1, d), lambda j: (0, 0)),
                  pl.BlockSpec((d, ADA_TN), lambda j: (0, j)),
                  pl.BlockSpec((1, ADA_TN), lambda j: (0, j))],
        out_specs=pl.BlockSpec((MOD_ROWS, ADA_TN), lambda j: (0, j)),
        out_shape=jax.ShapeDtypeStruct((MOD_ROWS, n), F32),
        compiler_params=pltpu.CompilerParams(dimension_semantics=("arbitrary",),
                                             vmem_limit_bytes=VMEM_LIMIT),
        name="ada",
    )(c, c_ctx.reshape(1, d), w, b)


def _rope(v, cos, sin_lo, sin_hi):
    return (v * cos + pltpu.roll(v, LANES - ROPE_PAIR, axis=1) * sin_lo
            + pltpu.roll(v, ROPE_PAIR, axis=1) * sin_hi)


def _inproj_kernel(*refs, tm, gm, n_tiles, latent, mod_row):
    if latent:
        (x_ref, xp_ref, xn_ref, sh_ref, sc_ref, g_ref, w_ref, wvt_ref, bgr_ref, bgc_ref,
         wc_ref, cos_ref, sl_ref, shi_ref, gain_ref,
         q_ref, k_ref, vt_ref, qk_ref, mvt_ref, o_ref, gcol_ref, grow_ref) = refs
    else:
        (x_ref, xp_ref, xn_ref, sh_ref, sc_ref, g_ref, w_ref, wvt_ref, bgr_ref, bgc_ref,
         wc_ref,
         k_ref, vt_ref, qk_ref, mvt_ref, gcol_ref, grow_ref) = refs
    i = pl.program_id(0)
    n_groups = tm // gm
    r_mod = (pl.program_id(1) if mod_row is None else mod_row) % SUBLANES
    scale = g_ref[...] * (1.0 + sc_ref[pl.ds(r_mod, 1), :])
    shift = sh_ref[pl.ds(r_mod, 1), :]
    wc = wc_ref[...]
    row = lax.broadcasted_iota(jnp.int32, (gm, 1), 0)

    def project(lo, width, lhs):
        return _dot(lhs, w_ref[:, lo:lo + width])

    def group(r):
        rows = slice(r * gm, (r + 1) * gm)
        before = xp_ref[0] if r == 0 else x_ref[0, r * gm - SUBLANES:r * gm, :]
        after = xn_ref[0] if r == n_groups - 1 else x_ref[0, (r + 1) * gm:(r + 1) * gm + SUBLANES, :]
        keep_prev = (jnp.where(i == 0, 0.0, 1.0) if r == 0 else 1.0) if latent else 0.0
        keep_next = (jnp.where(i == n_tiles - 1, 0.0, 1.0) if r == n_groups - 1 else 1.0) if latent else 0.0
        xt = jnp.concatenate([x_ref[0, rows, :], before, after], axis=0)
        ms = jnp.mean(xt * xt, axis=-1, keepdims=True)
        hb = (xt * lax.rsqrt(ms + EPS) * scale + shift).astype(BF16)
        hm = hb[:gm]
        yield

        c = ATT_WIDTH
        if latent:
            r_q = project(0, ATT_WIDTH, hm)
        r_kg = project(c, ATT_KV_WIDTH + LANES, hm)
        c += ATT_KV_WIDTH + LANES
        yield

        if latent:
            cos, sl, shi = cos_ref[rows, :], sl_ref[rows, :], shi_ref[rows, :]
            for g in range(ATT_WIDTH // LANES):
                q_ref[0, rows, g * LANES:(g + 1) * LANES] = _rope(
                    r_q[:, g * LANES:(g + 1) * LANES], cos, sl, shi).astype(BF16)
        y = project(c, 2 * ML_QK_WIDTH, hb)
        c += 2 * ML_QK_WIDTH
        yield

        if latent:
            k_ref[0, rows, :] = _rope(r_kg[:, :LANES], cos, sl, shi).astype(BF16)
        else:
            k_ref[0, rows, :] = r_kg[:, :LANES].astype(BF16)
        gcol_ref[0, rows, :] = r_kg[:, LANES:] + bgr_ref[...]
        v_t = _dot_nt(wvt_ref[...], hm)
        yield

        ym = y[:gm]
        prev = jnp.where(row == 0, y[gm + SUBLANES - 1:gm + SUBLANES] * keep_prev,
                         pltpu.roll(ym, 1, axis=0))
        nxt = jnp.where(row == gm - 1, y[gm + SUBLANES:gm + SUBLANES + 1] * keep_next,
                        pltpu.roll(ym, gm - 1, axis=0))
        act = _silu(prev * wc[0:1] + ym * wc[1:2] + nxt * wc[2:3])
        qk_ref[0, rows, :ML_QK_WIDTH] = (act[:, :ML_QK_WIDTH] * (ML_QK_DIM ** -0.5)).astype(BF16)
        qk_ref[0, rows, ML_QK_WIDTH:] = act[:, ML_QK_WIDTH:].astype(BF16)
        if latent:
            r_o = project(c, ML_WIDTH, hm)
        yield

        g_t = v_t[ATT_KV_WIDTH + ML_WIDTH:] + bgc_ref[...]
        vt_ref[0, :, rows] = v_t[:ATT_KV_WIDTH].astype(BF16)
        for j in range(gm // ML_CHUNK):
            cols = slice(j * ML_CHUNK, (j + 1) * ML_CHUNK)
            mvt_ref[0, r * (gm // ML_CHUNK) + j] = v_t[ATT_KV_WIDTH:ATT_KV_WIDTH + ML_WIDTH, cols].astype(BF16)
            grow_ref[0, r * (gm // ML_CHUNK) + j] = g_t[:, cols]
        if latent:
            o_ref[0, rows, :] = (jax.nn.sigmoid(r_o) * gain_ref[...]).astype(BF16)

    done = object()
    waiting = [group(r) for r in range(n_groups)]
    active = []
    while waiting or active:
        if waiting:
            active.append(waiting.pop(0))
        active = [g for g in active if next(g, done) is not done]


def _inproj(x, mod, mod_row, g_pre, w_main, wvt, bg_row, bg_col, wc, rope_tabs, gain, *, tm, gm, latent):
    bsz, t, d = x.shape
    n_tiles = t // tm
    hb = tm // SUBLANES
    n_hblk = t // SUBLANES
    n = w_main.shape[1]
    mod_blk = (lambda b: b // SUBLANES) if mod_row is None else (lambda b: mod_row // SUBLANES)

    def const(shape):
        return pl.BlockSpec(shape, lambda i, b: (0,) * len(shape))

    in_specs = [
        pl.BlockSpec((1, tm, d), lambda i, b: (b, i, 0)),
        pl.BlockSpec((1, SUBLANES, d), lambda i, b: (b, jnp.maximum(i * hb - 1, 0), 0)),
        pl.BlockSpec((1, SUBLANES, d), lambda i, b: (b, jnp.minimum((i + 1) * hb, n_hblk - 1), 0)),
        pl.BlockSpec((SUBLANES, d), lambda i, b: (mod_blk(b), 0)),
        pl.BlockSpec((SUBLANES, d), lambda i, b: (mod_blk(b), 1)),
        const((1, d)), const((d, n)), const((ATT_KV_WIDTH + ML_WIDTH + ML_GATES, d)),
        const((1, LANES)), const((ML_GATES, 1)), const((3, 2 * ML_QK_WIDTH)),
    ]
    args = [x, x, x, mod, mod, g_pre, w_main, wvt, bg_row, bg_col, wc]
    tok = lambda width, dt: (pl.BlockSpec((1, tm, width), lambda i, b: (b, i, 0)),
                             jax.ShapeDtypeStruct((bsz, t, width), dt))
    chunked = lambda rows, dt: (pl.BlockSpec((1, tm // ML_CHUNK, rows, ML_CHUNK), lambda i, b: (b, i, 0, 0)),
                                jax.ShapeDtypeStruct((bsz, t // ML_CHUNK, rows, ML_CHUNK), dt))
    outs = []
    if latent:
        in_specs += [pl.BlockSpec((tm, LANES), lambda i, b: (i, 0))] * 3 + [const((1, ML_WIDTH))]
        args += list(rope_tabs) + [gain]
        outs.append(tok(ATT_WIDTH, BF16))
    outs.append(tok(ATT_KV_WIDTH, BF16))
    outs.append((pl.BlockSpec((1, LANES, tm), lambda i, b: (b, 0, i)),
                 jax.ShapeDtypeStruct((bsz, ATT_KV_WIDTH, t), BF16)))
    outs.append(tok(2 * ML_QK_WIDTH, BF16))
    outs.append(chunked(ML_WIDTH, BF16))
    if latent:
        outs.append(tok(ML_WIDTH, BF16))
    outs.append(tok(LANES, F32))
    outs.append(chunked(ML_GATES, F32))
    return pl.pallas_call(
        functools.partial(_inproj_kernel, tm=tm, gm=gm, n_tiles=n_tiles, latent=latent, mod_row=mod_row),
        grid=(n_tiles, bsz),
        in_specs=in_specs,
        out_specs=[o[0] for o in outs],
        out_shape=[o[1] for o in outs],
        compiler_params=pltpu.CompilerParams(dimension_semantics=("arbitrary", "arbitrary"),
                                             vmem_limit_bytes=VMEM_LIMIT),
        name="inproj_latent" if latent else "inproj_context",
    )(*args)


def _attn_kernel(q_ref, k_ref, kx_ref, vt_ref, vx_ref, sink_ref, o_ref):
    blk = ATT_BLOCK
    lane = lax.broadcasted_iota(jnp.int32, (blk, LANES), 1)
    zero = jnp.zeros((blk, LANES), BF16)
    half_groups = ATT_UNIT_GROUPS
    n_slots = ATT_KV_HEADS * half_groups
    n_parts = ATT_GROUP // half_groups
    sink = [sink_ref[:, part * n_slots * blk:(part + 1) * n_slots * blk] for part in range(n_parts)]

    def stack_heads(q, half):
        parts = []
        for g in range(half * half_groups, (half + 1) * half_groups):
            slab = q[:, g * LANES:(g + 1) * LANES]
            parts.append(jnp.where(lane < ATT_HEAD_DIM, slab, zero))
            parts.append(jnp.where(lane >= ATT_HEAD_DIM, slab, zero))
        return jnp.concatenate(parts, axis=0)

    key = lax.broadcasted_iota(jnp.int32, (blk, blk), 0)
    qry = lax.broadcasted_iota(jnp.int32, (blk, blk), 1)
    ninf = jnp.full((blk, blk), -jnp.inf, F32)
    bias_prev = jnp.where(key >= qry, 0.0, ninf)
    bias_next = jnp.where(key <= qry, 0.0, ninf)
    slots = lambda b: jnp.concatenate([b] * n_slots, axis=1)

    def scores(qs, k_prev, k_cur, k_next):
        return [_dot_nt(k_prev, qs), _dot_nt(k_cur, qs), _dot_nt(k_next, qs), _dot_nt(kx_ref[0], qs)]

    def softmax(s, b_prev, b_next, sink_h):
        s = jnp.concatenate([s[0] + slots(b_prev), s[1], s[2] + slots(b_next), s[3]], axis=0)
        m = jnp.maximum(sink_h, jnp.max(s, axis=0, keepdims=True))
        return jnp.exp2(s - m).astype(BF16), jnp.exp2(sink_h - m)

    n_keys = 3 * blk + kx_ref.shape[1]
    ones_rows = jnp.ones((ATT_ONES_ROWS, n_keys), BF16)

    def weighted_values(p, v_prev, v_cur, v_next):
        vt = jnp.concatenate([v_prev, v_cur, v_next, vx_ref[0]], axis=1)
        return _dot(jnp.concatenate([vt, ones_rows], axis=0), p)

    dim = lax.broadcasted_iota(jnp.int32, (LANES, blk), 0)

    def emit(rows, half, ot, p_sink):
        ot = ot[:LANES] * (1.0 / (ot[LANES:LANES + 1] + p_sink))
        for j in range(half_groups):
            g = half * half_groups + j
            a = ot[:, (2 * j) * blk:(2 * j + 1) * blk]
            b = ot[:, (2 * j + 1) * blk:(2 * j + 2) * blk]
            o_ref[0, rows, g * LANES:(g + 1) * LANES] = jnp.where(dim < ATT_HEAD_DIM, a, b).T.astype(BF16)

    nblk = q_ref.shape[1] // blk
    rows_of = [slice(b * blk, (b + 1) * blk) for b in range(nblk)]
    k_blocks = [k_ref[0, r, :] for r in rows_of]
    v_blocks = [vt_ref[0, :, r] for r in rows_of]
    near = lambda blocks, b: (blocks[max(b - 1, 0)], blocks[b], blocks[min(b + 1, nblk - 1)])
    bias = [(ninf if b == 0 else bias_prev, ninf if b == nblk - 1 else bias_next) for b in range(nblk)]
    units = [(b, half) for b in range(nblk) for half in range(n_parts)]

    def stage_scores(u):
        b, half = u
        return scores(stack_heads(q_ref[0, rows_of[b], :], half), *near(k_blocks, b))

    def stage_softmax(u, s):
        b, half = u
        return softmax(s, *bias[b], sink[half])

    def stage_values(u, p):
        return weighted_values(p, *near(v_blocks, u[0]))

    def stage_emit(u, ot, p_sink):
        emit(rows_of[u[0]], u[1], ot, p_sink)

    n_units = len(units)
    s, p, ot = {}, {}, {}
    s[0] = stage_scores(units[0])
    for t in range(n_units):
        if t + 1 < n_units:
            s[t + 1] = stage_scores(units[t + 1])
        p[t] = stage_softmax(units[t], s.pop(t))
        if t >= 1:
            ot[t - 1] = stage_values(units[t - 1], p[t - 1][0])
        if t >= 2:
            stage_emit(units[t - 2], ot.pop(t - 2), p.pop(t - 2)[1])
    ot[n_units - 1] = stage_values(units[n_units - 1], p[n_units - 1][0])
    for t in (n_units - 2, n_units - 1):
        stage_emit(units[t], ot.pop(t), p.pop(t)[1])


def _attention(q, k, vt, kx, vxt, sink_row):
    bsz, s, _ = q.shape
    l = kx.shape[1]
    per_b = lambda a: pl.BlockSpec((1,) + a.shape[1:], lambda b: (b, 0, 0))
    vx_spec = pl.BlockSpec((1, ATT_KV_WIDTH, l), lambda b: (b // CTX_PER_STEP, 0, b % CTX_PER_STEP))
    return pl.pallas_call(
        _attn_kernel,
        grid=(bsz,),
        in_specs=[per_b(q), per_b(k), per_b(kx), per_b(vt), vx_spec,
                  pl.BlockSpec(sink_row.shape, lambda b: (0, 0))],
        out_specs=pl.BlockSpec((1, s, ATT_WIDTH), lambda b: (b, 0, 0)),
        out_shape=jax.ShapeDtypeStruct((bsz, s, ATT_WIDTH), BF16),
        compiler_params=pltpu.CompilerParams(dimension_semantics=("arbitrary",),
                                             vmem_limit_bytes=VMEM_LIMIT),
        name="attention",
    )(q, k, kx, vt, vxt, sink_row)


BF16_SUBLANES = 16
ML_STATE_ROWS = ML_V_DIM + BF16_SUBLANES


def _mlstm_kernel(qkx_ref, vtx_ref, ox_ref, gcx_ref, grx_ref, qkc_ref, vtc_ref, grc_ref,
                  out_ref, sin_ref, st_ref, ucol_ref, mrun_ref, wint_ref, floor_ref,
                  wkey_ref, decay_ref, *, ncx, ncc):
    lc = ML_CHUNK
    qkw = ML_QK_WIDTH
    sr = ML_STATE_ROWS
    ng = ML_GATES
    hg = ML_GATES // 2
    st_ref[...] = jnp.zeros(st_ref.shape, F32)

    rr = lax.broadcasted_iota(jnp.int32, (lc, lc), 0)
    cc = lax.broadcasted_iota(jnp.int32, (lc, lc), 1)
    tril = rr >= cc
    triu = rr <= cc
    tril_b = jnp.where(tril, 1.0, 0.0).astype(BF16)
    triu_b = jnp.where(triu, 1.0, 0.0).astype(BF16)
    head_of_lane = lax.broadcasted_iota(jnp.int32, (lc, qkw), 1) // ML_QK_DIM
    head_of_state_lane = lax.broadcasted_iota(jnp.int32, (1, qkw), 1) // ML_QK_DIM

    def split(v):
        hi = v.astype(BF16)
        return hi, (v - hi.astype(F32)).astype(BF16)

    def gate_rows(g):
        n16 = g.shape[0]
        fwd_row = (lax.broadcasted_iota(jnp.int32, g.shape, 0) & (ng - 1)) < hg
        lane = lax.broadcasted_iota(jnp.int32, g.shape, 1)
        hi, lo = split(_log_sigmoid(g))
        cat = jnp.concatenate([hi, lo], axis=0)
        bu = _dot(cat, triu_b)
        bl = _dot(cat, tril_b)
        b = jnp.where(fwd_row, bu[:n16] + bu[n16:], bl[:n16] + bl[n16:])
        b = pltpu.roll(b, n16 - ML_HEADS, axis=0)
        u = g - b
        run_f = run_b = u
        k = 1
        while k < lc:
            run_f = jnp.maximum(run_f, jnp.where(lane >= k, pltpu.roll(run_f, k, axis=1), -jnp.inf))
            run_b = jnp.maximum(run_b, jnp.where(lane < lc - k, pltpu.roll(run_b, lc - k, axis=1), -jnp.inf))
            k *= 2
        run = jnp.where(fwd_row, run_f, run_b)

        def at_end(a):
            return jnp.where(fwd_row, jnp.broadcast_to(a[:, lc - 1:lc], a.shape),
                             jnp.broadcast_to(a[:, 0:1], a.shape))

        return u, b, run, at_end(b), at_end(run)

    def derived(u, b, run, b_tot, run_end, m_in):
        m_run = jnp.maximum(run, m_in)
        m_out = b_tot + jnp.maximum(run_end, m_in)
        return (m_run * LOG2_E, jnp.exp(m_in - m_run), jnp.exp(-(b + m_run)),
                jnp.exp(b_tot + u - m_out), jnp.exp(b_tot + m_in - m_out))

    gc_rows = gate_rows(grc_ref[0].reshape(ncc * ng, lc))
    gx_rows = gate_rows(grx_ref[0].reshape(ncx * ng, lc))

    def scan_m(rows, n, m_f, m_b):
        _, _, _, b_tot, run_end = rows
        part = lambda a, c, d: a[c * ng + d * hg:c * ng + (d + 1) * hg]
        ins_f, ins_b = [], [None] * n
        for c in range(n):
            ins_f.append(m_f)
            m_f = part(b_tot, c, 0) + jnp.maximum(part(run_end, c, 0), m_f)
        for c in reversed(range(n)):
            ins_b[c] = m_b
            m_b = part(b_tot, c, 1) + jnp.maximum(part(run_end, c, 1), m_b)
        return jnp.concatenate([x for c in range(n) for x in (ins_f[c], ins_b[c])], axis=0), m_f, m_b

    m0 = jnp.zeros((hg, lc), F32)
    m_in_c, m_f, m_b = scan_m(gc_rows, ncc, m0, m0)
    m_in_x, _, _ = scan_m(gx_rows, ncx, m_f, m_b)
    _, _, _, wkey_c, decay_c = derived(*gc_rows, m_in_c)
    for ref, val in zip((mrun_ref, wint_ref, floor_ref, wkey_ref, decay_ref), derived(*gx_rows, m_in_x)):
        ref[...] = val

    fwd_col = (lax.broadcasted_iota(jnp.int32, (lc, LANES), 1) & (ng - 1)) < hg

    def token_major_u(j):
        rows = pl.ds(pl.multiple_of(j * lc, lc), lc)
        gcol = gcx_ref[0, rows, :]
        hi, lo = split(_log_sigmoid(gcol))
        cat = jnp.concatenate([hi, lo], axis=1)
        bl = _dot(tril_b, cat)
        bu = _dot(triu_b, cat)
        b = jnp.where(fwd_col, bl[:, :LANES] + bl[:, LANES:], bu[:, :LANES] + bu[:, LANES:])
        ucol_ref[rows, :] = (gcol - pltpu.roll(b, LANES - ML_HEADS, axis=1)) * LOG2_E

    def advance(dirn, k4, vt, w_key, decay):
        st = st_ref[dirn]
        lhs, rhs, decay_row = [], [], None
        for h in range(ML_HEADS):
            c = hg * dirn + h
            wk = w_key[c:c + 1].astype(BF16)
            lhs.append(jnp.concatenate([vt[h * ML_V_DIM:(h + 1) * ML_V_DIM] * wk,
                                        jnp.broadcast_to(wk, (BF16_SUBLANES, lc))], axis=0))
            rhs.append(jnp.where(head_of_lane == h, k4, jnp.zeros_like(k4)))
            d_h = jnp.concatenate([decay[c:c + 1]] * (qkw // lc), axis=1)
            decay_row = d_h if decay_row is None else jnp.where(head_of_state_lane == h, d_h, decay_row)
        upd = _dot(jnp.concatenate(lhs, axis=1), jnp.concatenate(rhs, axis=0))
        st_ref[dirn] = decay_row * st + upd

    for j in range(ncc):
        for dirn, cj in ((0, j), (1, ncc - 1 - j)):
            grows = slice(cj * ng, (cj + 1) * ng)
            advance(dirn, qkc_ref[0, cj * lc:(cj + 1) * lc, qkw:], vtc_ref[0, cj], wkey_c[grows], decay_c[grows])

    def scan_body(j, carry):
        for dirn, cj in ((0, j), (1, ncx - 1 - j)):
            rows = pl.ds(pl.multiple_of(cj * lc, lc), lc)
            grows = pl.ds(pl.multiple_of(cj * ng, ng), ng)
            sin_ref[dirn, cj] = st_ref[dirn].astype(BF16)
            advance(dirn, qkx_ref[0, rows, qkw:], vtx_ref[0, cj], wkey_ref[grows, :], decay_ref[grows, :])
        token_major_u(j)
        return carry

    lax.fori_loop(0, ncx, scan_body, 0, unroll=16)


    def out_body(j, carry):
        rows = pl.ds(pl.multiple_of(j * lc, lc), lc)
        grows = pl.ds(pl.multiple_of(j * ng, ng), ng)
        q4 = qkx_ref[0, rows, :qkw]
        k4 = qkx_ref[0, rows, qkw:]
        vt = vtx_ref[0, j]
        u_col = ucol_ref[rows, :]
        m_run, w_int, floor = mrun_ref[grows, :], wint_ref[grows, :], floor_ref[grows, :]
        zero = jnp.zeros_like(q4)
        qs = jnp.concatenate([jnp.where(head_of_lane == h, q4, zero) for h in range(ML_HEADS)], axis=0)
        both = _dot_nt(jnp.concatenate([k4, sin_ref[0, j], sin_ref[1, j]], axis=0), qs)
        qk_t = both[:lc]
        for h in range(ML_HEADS):
            cols_h = slice(h * lc, (h + 1) * lc)
            hs = None
            for dirn in range(2):
                c = hg * dirn + h
                inter = both[lc + dirn * sr:lc + (dirn + 1) * sr, cols_h]
                valid = triu if dirn == 0 else tril
                e = jnp.exp2(jnp.where(valid, u_col[:, c:c + 1] - m_run[c:c + 1], -jnp.inf))
                s_t = qk_t[:, cols_h] * e
                num = _dot(vt[h * ML_V_DIM:(h + 1) * ML_V_DIM], s_t.astype(BF16))
                num = num + w_int[c:c + 1] * inter[:ML_V_DIM]
                den = jnp.sum(s_t, axis=0, keepdims=True) + w_int[c:c + 1] * inter[ML_V_DIM:ML_V_DIM + 1]
                hv = num * (1.0 / jnp.maximum(jnp.abs(den), floor[c:c + 1]))
                hs = hv if hs is None else hs + hv
            cols = slice(h * ML_V_DIM, (h + 1) * ML_V_DIM)
            hn = hs * lax.rsqrt(jnp.mean(hs * hs, axis=0, keepdims=True) + EPS)
            out_ref[0, rows, cols] = (hn.T * ox_ref[0, rows, cols].astype(F32)).astype(BF16)
        return carry

    lax.fori_loop(0, ncx, out_body, 0, unroll=16)


def _mlstm(qkx, vtx, ox, gcx, grx, qkc, vtc, grc):
    bsz, s, _ = qkx.shape
    ncx, ncc = s // ML_CHUNK, qkc.shape[1] // ML_CHUNK
    per_b = lambda a: pl.BlockSpec((1,) + a.shape[1:], lambda b: (b,) + (0,) * (a.ndim - 1))
    ins = [qkx, vtx, ox, gcx, grx, qkc, vtc, grc]
    return pl.pallas_call(
        functools.partial(_mlstm_kernel, ncx=ncx, ncc=ncc),
        grid=(bsz,),
        in_specs=[per_b(a) for a in ins],
        out_specs=pl.BlockSpec((1, s, ML_WIDTH), lambda b: (b, 0, 0)),
        out_shape=jax.ShapeDtypeStruct((bsz, s, ML_WIDTH), BF16),
        scratch_shapes=[pltpu.VMEM((2, ncx, ML_STATE_ROWS, ML_QK_WIDTH), BF16),
                        pltpu.VMEM((2, ML_STATE_ROWS, ML_QK_WIDTH), F32),
                        pltpu.VMEM((s, LANES), F32)]
                       + [pltpu.VMEM((ncx * ML_GATES, ML_CHUNK), F32)] * 5,
        compiler_params=pltpu.CompilerParams(dimension_semantics=("arbitrary",),
                                             vmem_limit_bytes=VMEM_LIMIT),
        name="mlstm",
    )(*ins)


def _out_ffn_kernel(x_ref, att_ref, ml_ref, gtm_ref, shf_ref, scf_ref, gtf_ref,
                    gpm_ref, gpf_ref, gqf_ref, woa_ref, wom_ref, wfi_ref, wfo_ref, o_ref, *, hidden,
                    tiles_per_batch):
    r_mod = pl.ds((pl.program_id(0) // tiles_per_batch) % SUBLANES, 1)
    gtm, shf, scf, gtf = gtm_ref[r_mod, :], shf_ref[r_mod, :], scf_ref[r_mod, :], gtf_ref[r_mod, :]

    def group(g):
        r = slice(g * FFN_ROWS, (g + 1) * FFN_ROWS)
        mix = _dot(att_ref[r, :], woa_ref[...]) + _dot(ml_ref[r, :], wom_ref[...])
        yield
        x1 = x_ref[r, :] + gtm * _rms(mix, gpm_ref[...])
        h = (_rms(x1, gpf_ref[...]) * (1.0 + scf) + shf).astype(BF16)
        gu = _dot(h, wfi_ref[...])
        yield
        act = (_silu(gu[:, :hidden]) * gu[:, hidden:]).astype(BF16)
        fx = _dot(act, wfo_ref[...])
        yield
        o_ref[r, :] = x1 + gtf * _rms(fx, gqf_ref[...])

    done = object()
    waiting = [group(g) for g in range(x_ref.shape[0] // FFN_ROWS)]
    active, rounds = [], 0
    while waiting or active:
        if rounds % FFN_PAIR_PERIOD == 0:
            active += [waiting.pop(0) for _ in range(min(2, len(waiting)))]
        active = [g for g in active if next(g, done) is not done]
        rounds += 1


def _out_ffn(x2, att2, ml2, mod, g_post_mix, g_pre_ffn, g_post_ffn, woa, wom, wfi, wfo, *, tiles_per_batch):
    t, d = x2.shape
    tm = FFN_TM
    hidden = wfo.shape[0]
    resident = lambda a: pl.BlockSpec(a.shape, lambda i: (0,) * a.ndim, pipeline_mode=pl.Buffered(1))
    mod_spec = lambda k: pl.BlockSpec((SUBLANES, d), lambda i: (i // tiles_per_batch // SUBLANES, k))
    row = pl.BlockSpec((1, d), lambda i: (0, 0))
    return pl.pallas_call(
        functools.partial(_out_ffn_kernel, hidden=hidden, tiles_per_batch=tiles_per_batch),
        grid=(t // tm,),
        in_specs=[pl.BlockSpec((tm, d), lambda i: (i, 0)),
                  pl.BlockSpec((tm, ATT_WIDTH), lambda i: (i, 0)),
                  pl.BlockSpec((tm, ML_WIDTH), lambda i: (i, 0)),
                  mod_spec(2), mod_spec(3), mod_spec(4), mod_spec(5), row, row, row,
                  resident(woa), resident(wom), resident(wfi), resident(wfo)],
        out_specs=pl.BlockSpec((tm, d), lambda i: (i, 0)),
        out_shape=jax.ShapeDtypeStruct((t, d), F32),
        compiler_params=pltpu.CompilerParams(dimension_semantics=("arbitrary",),
                                             vmem_limit_bytes=VMEM_LIMIT),
        name="out_ffn",
    )(x2, att2, ml2, mod, mod, mod, mod, g_post_mix, g_pre_ffn, g_post_ffn, woa, wom, wfi, wfo)


def _rope_tables(n_tokens):
    pos = jnp.arange(n_tokens)
    row = (pos // GRID_W).astype(F32)
    col = (pos % GRID_W).astype(F32)
    inv_freq = jnp.power(ROPE_BASE, -jnp.arange(ROPE_PAIR, dtype=F32) / ROPE_PAIR)
    ang_r = row[:, None] * inv_freq
    ang_c = col[:, None] * inv_freq
    z = jnp.zeros_like(ang_r)
    reps = LANES // ATT_HEAD_DIM
    cos = jnp.tile(jnp.concatenate([jnp.cos(ang_r)] * 2 + [jnp.cos(ang_c)] * 2, axis=1), (1, reps))
    sin_lo = jnp.tile(jnp.concatenate([-jnp.sin(ang_r), z, -jnp.sin(ang_c), z], axis=1), (1, reps))
    sin_hi = jnp.tile(jnp.concatenate([z, jnp.sin(ang_r), z, jnp.sin(ang_c)], axis=1), (1, reps))
    return cos, sin_lo, sin_hi


def _permute_heads(w, axis):
    shape = w.shape
    grouped = shape[:axis] + (ATT_KV_HEADS, ATT_GROUP, ATT_HEAD_DIM) + shape[axis + 1:]
    return jnp.swapaxes(w.reshape(grouped), axis, axis + 1).reshape(shape)


def kernel(x, c, ctx, c_ctx, w_ada, b_ada, g_pre_mix, w_in, w_conv_qk, b_gates, attn_sink,
           g_mlstm_out, w_out, g_post_mix, g_pre_ffn, w_ffn_in, w_ffn_out, g_post_ffn):
    bsz, s, d = x.shape
    l = ctx.shape[1]
    assert w_ada.shape[0] == 1, "single-layer block"
    assert bsz < MOD_ROWS and s % INPROJ_TM == 0 and l % ML_CHUNK == 0 and s % FFN_TM == 0
    assert bsz % CTX_PER_STEP == 0 and l % LANES == 0

    mod = _ada(c, c_ctx, w_ada[0], b_ada)

    w = w_in[0]
    o_q, o_k, o_v = 0, ATT_WIDTH, ATT_WIDTH + ATT_KV_WIDTH
    o_mq = o_v + ATT_KV_WIDTH
    o_mv = o_mq + 2 * ML_QK_WIDTH
    o_mo = o_mv + ML_WIDTH
    o_mg = o_mo + ML_WIDTH
    w_q = _permute_heads(w[:, o_q:o_k], 1) * (ATT_HEAD_DIM ** -0.5 * LOG2_E)
    w_g = jnp.pad(w[:, o_mg:], ((0, 0), (0, LANES - ML_GATES)))
    shared = [w[:, o_k:o_v], w_g, w[:, o_mq:o_mv]]
    w_lat = jnp.concatenate([w_q] + shared + [w[:, o_mo:o_mg]], axis=1).astype(BF16)
    wvt = jnp.concatenate([w[:, o_v:o_mq], w[:, o_mv:o_mo], w[:, o_mg:]], axis=1).T.astype(BF16)
    bg_row = jnp.pad(b_gates, ((0, 0), (0, LANES - ML_GATES)))
    bg_col = b_gates.reshape(ML_GATES, 1)
    wc = w_conv_qk[0]

    q, k, vt, qkx, vtx, ox, gcx, grx = _inproj(
        x, mod, None, g_pre_mix, w_lat, wvt, bg_row, bg_col, wc, _rope_tables(s), g_mlstm_out,
        tm=INPROJ_TM, gm=INPROJ_ROWS, latent=True)
    nctx = bsz // CTX_PER_STEP
    kc, vct, qkc, vtc, _, grc = _inproj(
        ctx.reshape(nctx, CTX_PER_STEP * l, d), mod, bsz, g_pre_mix, w_lat, wvt, bg_row, bg_col, wc, None, None,
        tm=CTX_PER_STEP * l, gm=l, latent=False)
    kc = kc.reshape(bsz, l, ATT_KV_WIDTH)
    qkc = qkc.reshape(bsz, l, 2 * ML_QK_WIDTH)
    vtc = vtc.reshape((bsz, l // ML_CHUNK) + vtc.shape[2:])
    grc = grc.reshape((bsz, l // ML_CHUNK) + grc.shape[2:])

    sink_row = jnp.repeat(attn_sink[0][jnp.array(_HEAD_PERM)] * LOG2_E, ATT_BLOCK)[None, :]
    att = _attention(q, k, vt, kc, vct, sink_row)
    ml = _mlstm(qkx, vtx, ox, gcx, grx, qkc, vtc, grc)

    wo = w_out[0]
    woa = _permute_heads(wo[:ATT_WIDTH], 0).astype(BF16)
    wom = wo[ATT_WIDTH:].astype(BF16)
    out = _out_ffn(x.reshape(bsz * s, d), att.reshape(bsz * s, ATT_WIDTH), ml.reshape(bsz * s, ML_WIDTH),
                   mod, g_post_mix, g_pre_ffn, g_post_ffn, woa, wom,
                   w_ffn_in[0].astype(BF16), w_ffn_out[0].astype(BF16), tiles_per_batch=s // FFN_TM)
    return out.reshape(bsz, s, d)
```

```python
import functools

import jax
import jax.numpy as jnp
from jax import lax
from jax.experimental import pallas as pl
from jax.experimental.pallas import tpu as pltpu

F32 = jnp.float32
BF16 = jnp.bfloat16

EPS = 1e-6
GRID_W = 64
ROPE_BASE = 10000.0
LOG2_E = 1.4426950408889634

ATT_HEADS = 8
ATT_KV_HEADS = 2
ATT_GROUP = ATT_HEADS // ATT_KV_HEADS
ATT_HEAD_DIM = 64
ROPE_PAIR = ATT_HEAD_DIM // 4
ATT_BLOCK = 128
ATT_WIDTH = ATT_HEADS * ATT_HEAD_DIM
ATT_KV_WIDTH = ATT_KV_HEADS * ATT_HEAD_DIM
ATT_UNIT_GROUPS = 2
ATT_ONES_ROWS = 16

ML_HEADS = 4
ML_V_DIM = 128
ML_QK_DIM = 64
ML_WIDTH = ML_HEADS * ML_V_DIM
ML_QK_WIDTH = ML_HEADS * ML_QK_DIM
ML_GATES = 4 * ML_HEADS
ML_CHUNK = 128

LANES = 128
SUBLANES = 8
VMEM_LIMIT = 56 * 1024 * 1024

INPROJ_TM = 2048
INPROJ_ROWS = 256
CTX_PER_STEP = 2
FFN_TM = 1024
FFN_ROWS = 256
FFN_PAIR_PERIOD = 3
ADA_TN = 1536
MOD_ROWS = 16

_HEAD_PERM = tuple(h * ATT_GROUP + g for g in range(ATT_GROUP) for h in range(ATT_KV_HEADS))


def _silu(v):
    return v * jax.nn.sigmoid(v)


def _log_sigmoid(v):
    return jnp.minimum(v, 0.0) - jnp.log1p(jnp.exp(-jnp.abs(v)))


def _rms(v, g):
    return v * lax.rsqrt(jnp.mean(v * v, axis=-1, keepdims=True) + EPS) * g


def _dot(a, b):
    return jnp.dot(a, b, preferred_element_type=F32)


def _dot_nt(a, b):
    return lax.dot_general(a, b, (((1,), (1,)), ((), ())), preferred_element_type=F32)


def _ada_kernel(c_ref, cctx_ref, w_ref, b_ref, o_ref):
    pad = jnp.zeros((MOD_ROWS - c_ref.shape[0] - 1, c_ref.shape[1]), F32)
    a = _silu(jnp.concatenate([c_ref[...], cctx_ref[...], pad], axis=0))
    o_ref[...] = _dot(a.astype(BF16), w_ref[...].astype(BF16)) + b_ref[...]


def _ada(c, c_ctx, w, b):
    d, n = w.shape
    return pl.pallas_call(
        _ada_kernel,
        grid=(n // ADA_TN,),
        in_specs=[pl.BlockSpec(c.shape, lambda j: (0, 0)),
                  pl.BlockSpec((1, d), lambda j: (0, 0)),
                  pl.BlockSpec((d, ADA_TN), lambda j: (0, j)),
                  pl.BlockSpec((1, ADA_TN), lambda j: (0, j))],
        out_specs=pl.BlockSpec((MOD_ROWS, ADA_TN), lambda j: (0, j)),
        out_shape=jax.ShapeDtypeStruct((MOD_ROWS, n), F32),
        compiler_params=pltpu.CompilerParams(dimension_semantics=("arbitrary",),
                                             vmem_limit_bytes=VMEM_LIMIT),
        name="ada",
    )(c, c_ctx.reshape(1, d), w, b)


def _rope(v, cos, sin_lo, sin_hi):
    return (v * cos + pltpu.roll(v, LANES - ROPE_PAIR, axis=1) * sin_lo
            + pltpu.roll(v, ROPE_PAIR, axis=1) * sin_hi)


def _inproj_kernel(*refs, tm, gm, n_tiles, latent, mod_row):
    if latent:
        (x_ref, xp_ref, xn_ref, sh_ref, sc_ref, g_ref, w_ref, wvt_ref, bgr_ref, bgc_ref,
         wc_ref, cos_ref, sl_ref, shi_ref, gain_ref,
         q_ref, k_ref, vt_ref, qk_ref, mvt_ref, o_ref, gcol_ref, grow_ref) = refs
    else:
        (x_ref, xp_ref, xn_ref, sh_ref, sc_ref, g_ref, w_ref, wvt_ref, bgr_ref, bgc_ref,
         wc_ref,
         k_ref, vt_ref, qk_ref, mvt_ref, gcol_ref, grow_ref) = refs
    i = pl.program_id(0)
    n_groups = tm // gm
    r_mod = (pl.program_id(1) if mod_row is None else mod_row) % SUBLANES
    scale = g_ref[...] * (1.0 + sc_ref[pl.ds(r_mod, 1), :])
    shift = sh_ref[pl.ds(r_mod, 1), :]
    wc = wc_ref[...]
    row = lax.broadcasted_iota(jnp.int32, (gm, 1), 0)

    def project(lo, width, lhs):
        return _dot(lhs, w_ref[:, lo:lo + width])

    def group(r):
        rows = slice(r * gm, (r + 1) * gm)
        before = xp_ref[0] if r == 0 else x_ref[0, r * gm - SUBLANES:r * gm, :]
        after = xn_ref[0] if r == n_groups - 1 else x_ref[0, (r + 1) * gm:(r + 1) * gm + SUBLANES, :]
        keep_prev = (jnp.where(i == 0, 0.0, 1.0) if r == 0 else 1.0) if latent else 0.0
        keep_next = (jnp.where(i == n_tiles - 1, 0.0, 1.0) if r == n_groups - 1 else 1.0) if latent else 0.0
        xt = jnp.concatenate([x_ref[0, rows, :], before, after], axis=0)
        ms = jnp.mean(xt * xt, axis=-1, keepdims=True)
        hb = (xt * lax.rsqrt(ms + EPS) * scale + shift).astype(BF16)
        hm = hb[:gm]
        yield

        c = ATT_WIDTH
        if latent:
            r_q = project(0, ATT_WIDTH, hm)
        r_kg = project(c, ATT_KV_WIDTH + LANES, hm)
        c += ATT_KV_WIDTH + LANES
        yield

        if latent:
            cos, sl, shi = cos_ref[rows, :], sl_ref[rows, :], shi_ref[rows, :]
            for g in range(ATT_WIDTH // LANES):
                q_ref[0, rows, g * LANES:(g + 1) * LANES] = _rope(
                    r_q[:, g * LANES:(g + 1) * LANES], cos, sl, shi).astype(BF16)
        y = project(c, 2 * ML_QK_WIDTH, hb)
        c += 2 * ML_QK_WIDTH
        yield

        if latent:
            k_ref[0, rows, :] = _rope(r_kg[:, :LANES], cos, sl, shi).astype(BF16)
        else:
            k_ref[0, rows, :] = r_kg[:, :LANES].astype(BF16)
        gcol_ref[0, rows, :] = r_kg[:, LANES:] + bgr_ref[...]
        v_t = _dot_nt(wvt_ref[...], hm)
        yield

        ym = y[:gm]
        prev = jnp.where(row == 0, y[gm + SUBLANES - 1:gm + SUBLANES] * keep_prev,
                         pltpu.roll(ym, 1, axis=0))
        nxt = jnp.where(row == gm - 1, y[gm + SUBLANES:gm + SUBLANES + 1] * keep_next,
                        pltpu.roll(ym, gm - 1, axis=0))
        act = _silu(prev * wc[0:1] + ym * wc[1:2] + nxt * wc[2:3])
        qk_ref[0, rows, :ML_QK_WIDTH] = (act[:, :ML_QK_WIDTH] * (ML_QK_DIM ** -0.5)).astype(BF16)
        qk_ref[0, rows, ML_QK_WIDTH:] = act[:, ML_QK_WIDTH:].astype(BF16)
        if latent:
            r_o = project(c, ML_WIDTH, hm)
        yield

        g_t = v_t[ATT_KV_WIDTH + ML_WIDTH:] + bgc_ref[...]
        vt_ref[0, :, rows] = v_t[:ATT_KV_WIDTH].astype(BF16)
        for j in range(gm // ML_CHUNK):
            cols = slice(j * ML_CHUNK, (j + 1) * ML_CHUNK)
            mvt_ref[0, r * (gm // ML_CHUNK) + j] = v_t[ATT_KV_WIDTH:ATT_KV_WIDTH + ML_WIDTH, cols].astype(BF16)
            grow_ref[0, r * (gm // ML_CHUNK) + j] = g_t[:, cols]
        if latent:
            o_ref[0, rows, :] = (jax.nn.sigmoid(r_o) * gain_ref[...]).astype(BF16)

    done = object()
    waiting = [group(r) for r in range(n_groups)]
    active = []
    while waiting or active:
        if waiting:
            active.append(waiting.pop(0))
        active = [g for g in active if next(g, done) is not done]


def _inproj(x, mod, mod_row, g_pre, w_main, wvt, bg_row, bg_col, wc, rope_tabs, gain, *, tm, gm, latent):
    bsz, t, d = x.shape
    n_tiles = t // tm
    hb = tm // SUBLANES
    n_hblk = t // SUBLANES
    n = w_main.shape[1]
    mod_blk = (lambda b: b // SUBLANES) if mod_row is None else (lambda b: mod_row // SUBLANES)

    def const(shape):
        return pl.BlockSpec(shape, lambda i, b: (0,) * len(shape))

    in_specs = [
        pl.BlockSpec((1, tm, d), lambda i, b: (b, i, 0)),
        pl.BlockSpec((1, SUBLANES, d), lambda i, b: (b, jnp.maximum(i * hb - 1, 0), 0)),
        pl.BlockSpec((1, SUBLANES, d), lambda i, b: (b, jnp.minimum((i + 1) * hb, n_hblk - 1), 0)),
        pl.BlockSpec((SUBLANES, d), lambda i, b: (mod_blk(b), 0)),
        pl.BlockSpec((SUBLANES, d), lambda i, b: (mod_blk(b), 1)),
        const((1, d)), const((d, n)), const((ATT_KV_WIDTH + ML_WIDTH + ML_GATES, d)),
        const((1, LANES)), const((ML_GATES, 1)), const((3, 2 * ML_QK_WIDTH)),
    ]
    args = [x, x, x, mod, mod, g_pre, w_main, wvt, bg_row, bg_col, wc]
    tok = lambda width, dt: (pl.BlockSpec((1, tm, width), lambda i, b: (b, i, 0)),
                             jax.ShapeDtypeStruct((bsz, t, width), dt))
    chunked = lambda rows, dt: (pl.BlockSpec((1, tm // ML_CHUNK, rows, ML_CHUNK), lambda i, b: (b, i, 0, 0)),
                                jax.ShapeDtypeStruct((bsz, t // ML_CHUNK, rows, ML_CHUNK), dt))
    outs = []
    if latent:
        in_specs += [pl.BlockSpec((tm, LANES), lambda i, b: (i, 0))] * 3 + [const((1, ML_WIDTH))]
        args += list(rope_tabs) + [gain]
        outs.append(tok(ATT_WIDTH, BF16))
    outs.append(tok(ATT_KV_WIDTH, BF16))
    outs.append((pl.BlockSpec((1, LANES, tm), lambda i, b: (b, 0, i)),
                 jax.ShapeDtypeStruct((bsz, ATT_KV_WIDTH, t), BF16)))
    outs.append(tok(2 * ML_QK_WIDTH, BF16))
    outs.append(chunked(ML_WIDTH, BF16))
    if latent:
        outs.append(tok(ML_WIDTH, BF16))
    outs.append(tok(LANES, F32))
    outs.append(chunked(ML_GATES, F32))
    return pl.pallas_call(
        functools.partial(_inproj_kernel, tm=tm, gm=gm, n_tiles=n_tiles, latent=latent, mod_row=mod_row),
        grid=(n_tiles, bsz),
        in_specs=in_specs,
        out_specs=[o[0] for o in outs],
        out_shape=[o[1] for o in outs],
        compiler_params=pltpu.CompilerParams(dimension_semantics=("arbitrary", "arbitrary"),
                                             vmem_limit_bytes=VMEM_LIMIT),
        name="inproj_latent" if latent else "inproj_context",
    )(*args)


def _attn_kernel(q_ref, k_ref, kx_ref, vt_ref, vx_ref, sink_ref, o_ref):
    blk = ATT_BLOCK
    lane = lax.broadcasted_iota(jnp.int32, (blk, LANES), 1)
    zero = jnp.zeros((blk, LANES), BF16)
    half_groups = ATT_UNIT_GROUPS
    n_slots = ATT_KV_HEADS * half_groups
    n_parts = ATT_GROUP // half_groups
    sink = [sink_ref[:, part * n_slots * blk:(part + 1) * n_slots * blk] for part in range(n_parts)]

    def stack_heads(q, half):
        parts = []
        for g in range(half * half_groups, (half + 1) * half_groups):
            slab = q[:, g * LANES:(g + 1) * LANES]
            parts.append(jnp.where(lane < ATT_HEAD_DIM, slab, zero))
            parts.append(jnp.where(lane >= ATT_HEAD_DIM, slab, zero))
        return jnp.concatenate(parts, axis=0)

    key = lax.broadcasted_iota(jnp.int32, (blk, blk), 0)
    qry = lax.broadcasted_iota(jnp.int32, (blk, blk), 1)
    ninf = jnp.full((blk, blk), -jnp.inf, F32)
    bias_prev = jnp.where(key >= qry, 0.0, ninf)
    bias_next = jnp.where(key <= qry, 0.0, ninf)
    slots = lambda b: jnp.concatenate([b] * n_slots, axis=1)

    def scores(qs, k_prev, k_cur, k_next):
        return [_dot_nt(k_prev, qs), _dot_nt(k_cur, qs), _dot_nt(k_next, qs), _dot_nt(kx_ref[0], qs)]

    def softmax(s, b_prev, b_next, sink_h):
        s = jnp.concatenate([s[0] + slots(b_prev), s[1], s[2] + slots(b_next), s[3]], axis=0)
        m = jnp.maximum(sink_h, jnp.max(s, axis=0, keepdims=True))
        return jnp.exp2(s - m).astype(BF16), jnp.exp2(sink_h - m)

    n_keys = 3 * blk + kx_ref.shape[1]
    ones_rows = jnp.ones((ATT_ONES_ROWS, n_keys), BF16)

    def weighted_values(p, v_prev, v_cur, v_next):
        vt = jnp.concatenate([v_prev, v_cur, v_next, vx_ref[0]], axis=1)
        return _dot(jnp.concatenate([vt, ones_rows], axis=0), p)

    dim = lax.broadcasted_iota(jnp.int32, (LANES, blk), 0)

    def emit(rows, half, ot, p_sink):
        ot = ot[:LANES] * (1.0 / (ot[LANES:LANES + 1] + p_sink))
        for j in range(half_groups):
            g = half * half_groups + j
            a = ot[:, (2 * j) * blk:(2 * j + 1) * blk]
            b = ot[:, (2 * j + 1) * blk:(2 * j + 2) * blk]
            o_ref[0, rows, g * LANES:(g + 1) * LANES] = jnp.where(dim < ATT_HEAD_DIM, a, b).T.astype(BF16)

    nblk = q_ref.shape[1] // blk
    rows_of = [slice(b * blk, (b + 1) * blk) for b in range(nblk)]
    k_blocks = [k_ref[0, r, :] for r in rows_of]
    v_blocks = [vt_ref[0, :, r] for r in rows_of]
    near = lambda blocks, b: (blocks[max(b - 1, 0)], blocks[b], blocks[min(b + 1, nblk - 1)])
    bias = [(ninf if b == 0 else bias_prev, ninf if b == nblk - 1 else bias_next) for b in range(nblk)]
    units = [(b, half) for b in range(nblk) for half in range(n_parts)]

    def stage_scores(u):
        b, half = u
        return scores(stack_heads(q_ref[0, rows_of[b], :], half), *near(k_blocks, b))

    def stage_softmax(u, s):
        b, half = u
        return softmax(s, *bias[b], sink[half])

    def stage_values(u, p):
        return weighted_values(p, *near(v_blocks, u[0]))

    def stage_emit(u, ot, p_sink):
        emit(rows_of[u[0]], u[1], ot, p_sink)

    n_units = len(units)
    s, p, ot = {}, {}, {}
    s[0] = stage_scores(units[0])
    for t in range(n_units):
        if t + 1 < n_units:
            s[t + 1] = stage_scores(units[t + 1])
        p[t] = stage_softmax(units[t], s.pop(t))
        if t >= 1:
            ot[t - 1] = stage_values(units[t - 1], p[t - 1][0])
        if t >= 2:
            stage_emit(units[t - 2], ot.pop(t - 2), p.pop(t - 2)[1])
    ot[n_units - 1] = stage_values(units[n_units - 1], p[n_units - 1][0])
    for t in (n_units - 2, n_units - 1):
        stage_emit(units[t], ot.pop(t), p.pop(t)[1])


def _attention(q, k, vt, kx, vxt, sink_row):
    bsz, s, _ = q.shape
    l = kx.shape[1]
    per_b = lambda a: pl.BlockSpec((1,) + a.shape[1:], lambda b: (b, 0, 0))
    vx_spec = pl.BlockSpec((1, ATT_KV_WIDTH, l), lambda b: (b // CTX_PER_STEP, 0, b % CTX_PER_STEP))
    return pl.pallas_call(
        _attn_kernel,
        grid=(bsz,),
        in_specs=[per_b(q), per_b(k), per_b(kx), per_b(vt), vx_spec,
                  pl.BlockSpec(sink_row.shape, lambda b: (0, 0))],
        out_specs=pl.BlockSpec((1, s, ATT_WIDTH), lambda b: (b, 0, 0)),
        out_shape=jax.ShapeDtypeStruct((bsz, s, ATT_WIDTH), BF16),
        compiler_params=pltpu.CompilerParams(dimension_semantics=("arbitrary",),
                                             vmem_limit_bytes=VMEM_LIMIT),
        name="attention",
    )(q, k, kx, vt, vxt, sink_row)


BF16_SUBLANES = 16
ML_STATE_ROWS = ML_V_DIM + BF16_SUBLANES


def _mlstm_kernel(qkx_ref, vtx_ref, ox_ref, gcx_ref, grx_ref, qkc_ref, vtc_ref, grc_ref,
                  out_ref, sin_ref, st_ref, ucol_ref, mrun_ref, wint_ref, floor_ref,
                  wkey_ref, decay_ref, *, ncx, ncc):
    lc = ML_CHUNK
    qkw = ML_QK_WIDTH
    sr = ML_STATE_ROWS
    ng = ML_GATES
    hg = ML_GATES // 2
    st_ref[...] = jnp.zeros(st_ref.shape, F32)

    rr = lax.broadcasted_iota(jnp.int32, (lc, lc), 0)
    cc = lax.broadcasted_iota(jnp.int32, (lc, lc), 1)
    tril = rr >= cc
    triu = rr <= cc
    tril_b = jnp.where(tril, 1.0, 0.0).astype(BF16)
    triu_b = jnp.where(triu, 1.0, 0.0).astype(BF16)
    head_of_lane = lax.broadcasted_iota(jnp.int32, (lc, qkw), 1) // ML_QK_DIM
    head_of_state_lane = lax.broadcasted_iota(jnp.int32, (1, qkw), 1) // ML_QK_DIM

    def split(v):
        hi = v.astype(BF16)
        return hi, (v - hi.astype(F32)).astype(BF16)

    def gate_rows(g):
        n16 = g.shape[0]
        fwd_row = (lax.broadcasted_iota(jnp.int32, g.shape, 0) & (ng - 1)) < hg
        lane = lax.broadcasted_iota(jnp.int32, g.shape, 1)
        hi, lo = split(_log_sigmoid(g))
        cat = jnp.concatenate([hi, lo], axis=0)
        bu = _dot(cat, triu_b)
        bl = _dot(cat, tril_b)
        b = jnp.where(fwd_row, bu[:n16] + bu[n16:], bl[:n16] + bl[n16:])
        b = pltpu.roll(b, n16 - ML_HEADS, axis=0)
        u = g - b
        run_f = run_b = u
        k = 1
        while k < lc:
            run_f = jnp.maximum(run_f, jnp.where(lane >= k, pltpu.roll(run_f, k, axis=1), -jnp.inf))
            run_b = jnp.maximum(run_b, jnp.where(lane < lc - k, pltpu.roll(run_b, lc - k, axis=1), -jnp.inf))
            k *= 2
        run = jnp.where(fwd_row, run_f, run_b)

        def at_end(a):
            return jnp.where(fwd_row, jnp.broadcast_to(a[:, lc - 1:lc], a.shape),
                             jnp.broadcast_to(a[:, 0:1], a.shape))

        return u, b, run, at_end(b), at_end(run)

    def derived(u, b, run, b_tot, run_end, m_in):
        m_run = jnp.maximum(run, m_in)
        m_out = b_tot + jnp.maximum(run_end, m_in)
        return (m_run * LOG2_E, jnp.exp(m_in - m_run), jnp.exp(-(b + m_run)),
                jnp.exp(b_tot + u - m_out), jnp.exp(b_tot + m_in - m_out))

    gc_rows = gate_rows(grc_ref[0].reshape(ncc * ng, lc))
    gx_rows = gate_rows(grx_ref[0].reshape(ncx * ng, lc))

    def scan_m(rows, n, m_f, m_b):
        _, _, _, b_tot, run_end = rows
        part = lambda a, c, d: a[c * ng + d * hg:c * ng + (d + 1) * hg]
        ins_f, ins_b = [], [None] * n
        for c in range(n):
            ins_f.append(m_f)
            m_f = part(b_tot, c, 0) + jnp.maximum(part(run_end, c, 0), m_f)
        for c in reversed(range(n)):
            ins_b[c] = m_b
            m_b = part(b_tot, c, 1) + jnp.maximum(part(run_end, c, 1), m_b)
        return jnp.concatenate([x for c in range(n) for x in (ins_f[c], ins_b[c])], axis=0), m_f, m_b

    m0 = jnp.zeros((hg, lc), F32)
    m_in_c, m_f, m_b = scan_m(gc_rows, ncc, m0, m0)
    m_in_x, _, _ = scan_m(gx_rows, ncx, m_f, m_b)
    _, _, _, wkey_c, decay_c = derived(*gc_rows, m_in_c)
    for ref, val in zip((mrun_ref, wint_ref, floor_ref, wkey_ref, decay_ref), derived(*gx_rows, m_in_x)):
        ref[...] = val

    fwd_col = (lax.broadcasted_iota(jnp.int32, (lc, LANES), 1) & (ng - 1)) < hg

    def token_major_u(j):
        rows = pl.ds(pl.multiple_of(j * lc, lc), lc)
        gcol = gcx_ref[0, rows, :]
        hi, lo = split(_log_sigmoid(gcol))
        cat = jnp.concatenate([hi, lo], axis=1)
        bl = _dot(tril_b, cat)
        bu = _dot(triu_b, cat)
        b = jnp.where(fwd_col, bl[:, :LANES] + bl[:, LANES:], bu[:, :LANES] + bu[:, LANES:])
        ucol_ref[rows, :] = (gcol - pltpu.roll(b, LANES - ML_HEADS, axis=1)) * LOG2_E

    def advance(dirn, k4, vt, w_key, decay):
        st = st_ref[dirn]
        lhs, rhs, decay_row = [], [], None
        for h in range(ML_HEADS):
            c = hg * dirn + h
            wk = w_key[c:c + 1].astype(BF16)
            lhs.append(jnp.concatenate([vt[h * ML_V_DIM:(h + 1) * ML_V_DIM] * wk,
                                        jnp.broadcast_to(wk, (BF16_SUBLANES, lc))], axis=0))
            rhs.append(jnp.where(head_of_lane == h, k4, jnp.zeros_like(k4)))
            d_h = jnp.concatenate([decay[c:c + 1]] * (qkw // lc), axis=1)
            decay_row = d_h if decay_row is None else jnp.where(head_of_state_lane == h, d_h, decay_row)
        upd = _dot(jnp.concatenate(lhs, axis=1), jnp.concatenate(rhs, axis=0))
        st_ref[dirn] = decay_row * st + upd

    for j in range(ncc):
        for dirn, cj in ((0, j), (1, ncc - 1 - j)):
            grows = slice(cj * ng, (cj + 1) * ng)
            advance(dirn, qkc_ref[0, cj * lc:(cj + 1) * lc, qkw:], vtc_ref[0, cj], wkey_c[grows], decay_c[grows])

    def scan_body(j, carry):
        for dirn, cj in ((0, j), (1, ncx - 1 - j)):
            rows = pl.ds(pl.multiple_of(cj * lc, lc), lc)
            grows = pl.ds(pl.multiple_of(cj * ng, ng), ng)
            sin_ref[dirn, cj] = st_ref[dirn].astype(BF16)
            advance(dirn, qkx_ref[0, rows, qkw:], vtx_ref[0, cj], wkey_ref[grows, :], decay_ref[grows, :])
        token_major_u(j)
        return carry

    lax.fori_loop(0, ncx, scan_body, 0, unroll=16)


    def out_body(j, carry):
        rows = pl.ds(pl.multiple_of(j * lc, lc), lc)
        grows = pl.ds(pl.multiple_of(j * ng, ng), ng)
        q4 = qkx_ref[0, rows, :qkw]
        k4 = qkx_ref[0, rows, qkw:]
        vt = vtx_ref[0, j]
        u_col = ucol_ref[rows, :]
        m_run, w_int, floor = mrun_ref[grows, :], wint_ref[grows, :], floor_ref[grows, :]
        zero = jnp.zeros_like(q4)
        qs = jnp.concatenate([jnp.where(head_of_lane == h, q4, zero) for h in range(ML_HEADS)], axis=0)
        both = _dot_nt(jnp.concatenate([k4, sin_ref[0, j], sin_ref[1, j]], axis=0), qs)
        qk_t = both[:lc]
        for h in range(ML_HEADS):
            cols_h = slice(h * lc, (h + 1) * lc)
            s_dirs = []
            for dirn in range(2):
                c = hg * dirn + h
                valid = triu if dirn == 0 else tril
                e = jnp.exp2(jnp.where(valid, u_col[:, c:c + 1] - m_run[c:c + 1], -jnp.inf))
                s_dirs.append(qk_t[:, cols_h] * e)
            nums = _dot(vt[h * ML_V_DIM:(h + 1) * ML_V_DIM],
                        jnp.concatenate([s.astype(BF16) for s in s_dirs], axis=1))
            hs = None
            for dirn in range(2):
                c = hg * dirn + h
                inter = both[lc + dirn * sr:lc + (dirn + 1) * sr, cols_h]
                num = nums[:, dirn * lc:(dirn + 1) * lc] + w_int[c:c + 1] * inter[:ML_V_DIM]
                den = (jnp.sum(s_dirs[dirn], axis=0, keepdims=True)
                       + w_int[c:c + 1] * inter[ML_V_DIM:ML_V_DIM + 1])
                hv = num * (1.0 / jnp.maximum(jnp.abs(den), floor[c:c + 1]))
                hs = hv if hs is None else hs + hv
            cols = slice(h * ML_V_DIM, (h + 1) * ML_V_DIM)
            hn = hs * lax.rsqrt(jnp.mean(hs * hs, axis=0, keepdims=True) + EPS)
            out_ref[0, rows, cols] = (hn.T * ox_ref[0, rows, cols].astype(F32)).astype(BF16)
        return carry

    lax.fori_loop(0, ncx, out_body, 0, unroll=16)


def _mlstm(qkx, vtx, ox, gcx, grx, qkc, vtc, grc):
    bsz, s, _ = qkx.shape
    ncx, ncc = s // ML_CHUNK, qkc.shape[1] // ML_CHUNK
    per_b = lambda a: pl.BlockSpec((1,) + a.shape[1:], lambda b: (b,) + (0,) * (a.ndim - 1))
    ins = [qkx, vtx, ox, gcx, grx, qkc, vtc, grc]
    return pl.pallas_call(
        functools.partial(_mlstm_kernel, ncx=ncx, ncc=ncc),
        grid=(bsz,),
        in_specs=[per_b(a) for a in ins],
        out_specs=pl.BlockSpec((1, s, ML_WIDTH), lambda b: (b, 0, 0)),
        out_shape=jax.ShapeDtypeStruct((bsz, s, ML_WIDTH), BF16),
        scratch_shapes=[pltpu.VMEM((2, ncx, ML_STATE_ROWS, ML_QK_WIDTH), BF16),
                        pltpu.VMEM((2, ML_STATE_ROWS, ML_QK_WIDTH), F32),
                        pltpu.VMEM((s, LANES), F32)]
                       + [pltpu.VMEM((ncx * ML_GATES, ML_CHUNK), F32)] * 5,
        compiler_params=pltpu.CompilerParams(dimension_semantics=("arbitrary",),
                                             vmem_limit_bytes=VMEM_LIMIT),
        name="mlstm",
    )(*ins)


def _out_ffn_kernel(x_ref, att_ref, ml_ref, gtm_ref, shf_ref, scf_ref, gtf_ref,
                    gpm_ref, gpf_ref, gqf_ref, woa_ref, wom_ref, wfi_ref, wfo_ref, o_ref, *, hidden,
                    tiles_per_batch):
    r_mod = pl.ds((pl.program_id(0) // tiles_per_batch) % SUBLANES, 1)
    gtm, shf, scf, gtf = gtm_ref[r_mod, :], shf_ref[r_mod, :], scf_ref[r_mod, :], gtf_ref[r_mod, :]

    def group(g):
        r = slice(g * FFN_ROWS, (g + 1) * FFN_ROWS)
        mix = _dot(att_ref[r, :], woa_ref[...]) + _dot(ml_ref[r, :], wom_ref[...])
        yield
        x1 = x_ref[r, :] + gtm * _rms(mix, gpm_ref[...])
        h = (_rms(x1, gpf_ref[...]) * (1.0 + scf) + shf).astype(BF16)
        gu = _dot(h, wfi_ref[...])
        yield
        act = (_silu(gu[:, :hidden]) * gu[:, hidden:]).astype(BF16)
        fx = _dot(act, wfo_ref[...])
        yield
        o_ref[r, :] = x1 + gtf * _rms(fx, gqf_ref[...])

    done = object()
    waiting = [group(g) for g in range(x_ref.shape[0] // FFN_ROWS)]
    active, rounds = [], 0
    while waiting or active:
        if rounds % FFN_PAIR_PERIOD == 0:
            active += [waiting.pop(0) for _ in range(min(2, len(waiting)))]
        active = [g for g in active if next(g, done) is not done]
        rounds += 1


def _out_ffn(x2, att2, ml2, mod, g_post_mix, g_pre_ffn, g_post_ffn, woa, wom, wfi, wfo, *, tiles_per_batch):
    t, d = x2.shape
    tm = FFN_TM
    hidden = wfo.shape[0]
    resident = lambda a: pl.BlockSpec(a.shape, lambda i: (0,) * a.ndim, pipeline_mode=pl.Buffered(1))
    mod_spec = lambda k: pl.BlockSpec((SUBLANES, d), lambda i: (i // tiles_per_batch // SUBLANES, k))
    row = pl.BlockSpec((1, d), lambda i: (0, 0))
    return pl.pallas_call(
        functools.partial(_out_ffn_kernel, hidden=hidden, tiles_per_batch=tiles_per_batch),
        grid=(t // tm,),
        in_specs=[pl.BlockSpec((tm, d), lambda i: (i, 0)),
                  pl.BlockSpec((tm, ATT_WIDTH), lambda i: (i, 0)),
                  pl.BlockSpec((tm, ML_WIDTH), lambda i: (i, 0)),
                  mod_spec(2), mod_spec(3), mod_spec(4), mod_spec(5), row, row, row,
                  resident(woa), resident(wom), resident(wfi), resident(wfo)],
        out_specs=pl.BlockSpec((tm, d), lambda i: (i, 0)),
        out_shape=jax.ShapeDtypeStruct((t, d), F32),
        compiler_params=pltpu.CompilerParams(dimension_semantics=("arbitrary",),
                                             vmem_limit_bytes=VMEM_LIMIT),
        name="out_ffn",
    )(x2, att2, ml2, mod, mod, mod, mod, g_post_mix, g_pre_ffn, g_post_ffn, woa, wom, wfi, wfo)


def _rope_tables(n_tokens):
    pos = jnp.arange(n_tokens)
    row = (pos // GRID_W).astype(F32)
    col = (pos % GRID_W).astype(F32)
    inv_freq = jnp.power(ROPE_BASE, -jnp.arange(ROPE_PAIR, dtype=F32) / ROPE_PAIR)
    ang_r = row[:, None] * inv_freq
    ang_c = col[:, None] * inv_freq
    z = jnp.zeros_like(ang_r)
    reps = LANES // ATT_HEAD_DIM
    cos = jnp.tile(jnp.concatenate([jnp.cos(ang_r)] * 2 + [jnp.cos(ang_c)] * 2, axis=1), (1, reps))
    sin_lo = jnp.tile(jnp.concatenate([-jnp.sin(ang_r), z, -jnp.sin(ang_c), z], axis=1), (1, reps))
    sin_hi = jnp.tile(jnp.concatenate([z, jnp.sin(ang_r), z, jnp.sin(ang_c)], axis=1), (1, reps))
    return cos, sin_lo, sin_hi


def _permute_heads(w, axis):
    shape = w.shape
    grouped = shape[:axis] + (ATT_KV_HEADS, ATT_GROUP, ATT_HEAD_DIM) + shape[axis + 1:]
    return jnp.swapaxes(w.reshape(grouped), axis, axis + 1).reshape(shape)


def kernel(x, c, ctx, c_ctx, w_ada, b_ada, g_pre_mix, w_in, w_conv_qk, b_gates, attn_sink,
           g_mlstm_out, w_out, g_post_mix, g_pre_ffn, w_ffn_in, w_ffn_out, g_post_ffn):
    bsz, s, d = x.shape
    l = ctx.shape[1]
    assert w_ada.shape[0] == 1, "single-layer block"
    assert bsz < MOD_ROWS and s % INPROJ_TM == 0 and l % ML_CHUNK == 0 and s % FFN_TM == 0
    assert bsz % CTX_PER_STEP == 0 and l % LANES == 0

    mod = _ada(c, c_ctx, w_ada[0], b_ada)

    w = w_in[0]
    o_q, o_k, o_v = 0, ATT_WIDTH, ATT_WIDTH + ATT_KV_WIDTH
    o_mq = o_v + ATT_KV_WIDTH
    o_mv = o_mq + 2 * ML_QK_WIDTH
    o_mo = o_mv + ML_WIDTH
    o_mg = o_mo + ML_WIDTH
    w_q = _permute_heads(w[:, o_q:o_k], 1) * (ATT_HEAD_DIM ** -0.5 * LOG2_E)
    w_g = jnp.pad(w[:, o_mg:], ((0, 0), (0, LANES - ML_GATES)))
    shared = [w[:, o_k:o_v], w_g, w[:, o_mq:o_mv]]
    w_lat = jnp.concatenate([w_q] + shared + [w[:, o_mo:o_mg]], axis=1).astype(BF16)
    wvt = jnp.concatenate([w[:, o_v:o_mq], w[:, o_mv:o_mo], w[:, o_mg:]], axis=1).T.astype(BF16)
    bg_row = jnp.pad(b_gates, ((0, 0), (0, LANES - ML_GATES)))
    bg_col = b_gates.reshape(ML_GATES, 1)
    wc = w_conv_qk[0]

    q, k, vt, qkx, vtx, ox, gcx, grx = _inproj(
        x, mod, None, g_pre_mix, w_lat, wvt, bg_row, bg_col, wc, _rope_tables(s), g_mlstm_out,
        tm=INPROJ_TM, gm=INPROJ_ROWS, latent=True)
    nctx = bsz // CTX_PER_STEP
    kc, vct, qkc, vtc, _, grc = _inproj(
        ctx.reshape(nctx, CTX_PER_STEP * l, d), mod, bsz, g_pre_mix, w_lat, wvt, bg_row, bg_col, wc, None, None,
        tm=CTX_PER_STEP * l, gm=l, latent=False)
    kc = kc.reshape(bsz, l, ATT_KV_WIDTH)
    qkc = qkc.reshape(bsz, l, 2 * ML_QK_WIDTH)
    vtc = vtc.reshape((bsz, l // ML_CHUNK) + vtc.shape[2:])
    grc = grc.reshape((bsz, l // ML_CHUNK) + grc.shape[2:])

    sink_row = jnp.repeat(attn_sink[0][jnp.array(_HEAD_PERM)] * LOG2_E, ATT_BLOCK)[None, :]
    att = _attention(q, k, vt, kc, vct, sink_row)
    ml = _mlstm(qkx, vtx, ox, gcx, grx, qkc, vtc, grc)

    wo = w_out[0]
    woa = _permute_heads(wo[:ATT_WIDTH], 0).astype(BF16)
    wom = wo[ATT_WIDTH:].astype(BF16)
    out = _out_ffn(x.reshape(bsz * s, d), att.reshape(bsz * s, ATT_WIDTH), ml.reshape(bsz * s, ML_WIDTH),
                   mod, g_post_mix, g_pre_ffn, g_post_ffn, woa, wom,
                   w_ffn_in[0].astype(BF16), w_ffn_out[0].astype(BF16), tiles_per_batch=s // FFN_TM)
    return out.reshape(bsz, s, d)
```

```python
import functools

import jax
import jax.numpy as jnp
from jax import lax
from jax.experimental import pallas as pl
from jax.experimental.pallas import tpu as pltpu

F32 = jnp.float32
BF16 = jnp.bfloat16

EPS = 1e-6
GRID_W = 64
ROPE_BASE = 10000.0
LOG2_E = 1.4426950408889634

ATT_HEADS = 8
ATT_KV_HEADS = 2
ATT_GROUP = ATT_HEADS // ATT_KV_HEADS
ATT_HEAD_DIM = 64
ROPE_PAIR = ATT_HEAD_DIM // 4
ATT_BLOCK = 128
ATT_WIDTH = ATT_HEADS * ATT_HEAD_DIM
ATT_KV_WIDTH = ATT_KV_HEADS * ATT_HEAD_DIM
ATT_UNIT_GROUPS = 2
ATT_ONES_ROWS = 16

ML_HEADS = 4
ML_V_DIM = 128
ML_QK_DIM = 64
ML_WIDTH = ML_HEADS * ML_V_DIM
ML_QK_WIDTH = ML_HEADS * ML_QK_DIM
ML_GATES = 4 * ML_HEADS
ML_CHUNK = 128

LANES = 128
SUBLANES = 8
VMEM_LIMIT = 56 * 1024 * 1024

INPROJ_TM = 2048
INPROJ_ROWS = 256
CTX_PER_STEP = 2
FFN_TM = 1024
FFN_ROWS = 256
FFN_PAIR_PERIOD = 3
ADA_TN = 1536
MOD_ROWS = 16

_HEAD_PERM = tuple(h * ATT_GROUP + g for g in range(ATT_GROUP) for h in range(ATT_KV_HEADS))


def _silu(v):
    return v * jax.nn.sigmoid(v)


def _log_sigmoid(v):
    return jnp.minimum(v, 0.0) - jnp.log1p(jnp.exp(-jnp.abs(v)))


def _rms(v, g):
    return v * lax.rsqrt(jnp.mean(v * v, axis=-1, keepdims=True) + EPS) * g


def _dot(a, b):
    return jnp.dot(a, b, preferred_element_type=F32)


def _dot_nt(a, b):
    return lax.dot_general(a, b, (((1,), (1,)), ((), ())), preferred_element_type=F32)


def _ada_kernel(c_ref, cctx_ref, w_ref, b_ref, o_ref):
    pad = jnp.zeros((MOD_ROWS - c_ref.shape[0] - 1, c_ref.shape[1]), F32)
    a = _silu(jnp.concatenate([c_ref[...], cctx_ref[...], pad], axis=0))
    o_ref[...] = _dot(a.astype(BF16), w_ref[...].astype(BF16)) + b_ref[...]


def _ada(c, c_ctx, w, b):
    d, n = w.shape
    return pl.pallas_call(
        _ada_kernel,
        grid=(n // ADA_TN,),
        in_specs=[pl.BlockSpec(c.shape, lambda j: (0, 0)),
                  pl.BlockSpec((1, d), lambda j: (0, 0)),
                  pl.BlockSpec((d, ADA_TN), lambda j: (0, j)),
                  pl.BlockSpec((1, ADA_TN), lambda j: (0, j))],
        out_specs=pl.BlockSpec((MOD_ROWS, ADA_TN), lambda j: (0, j)),
        out_shape=jax.ShapeDtypeStruct((MOD_ROWS, n), F32),
        compiler_params=pltpu.CompilerParams(dimension_semantics=("arbitrary",),
                                             vmem_limit_bytes=VMEM_LIMIT),
        name="ada",
    )(c, c_ctx.reshape(1, d), w, b)


def _rope(v, cos, sin_lo, sin_hi):
    return (v * cos + pltpu.roll(v, LANES - ROPE_PAIR, axis=1) * sin_lo
            + pltpu.roll(v, ROPE_PAIR, axis=1) * sin_hi)


def _inproj_kernel(*refs, tm, gm, n_tiles, latent, mod_row):
    if latent:
        (x_ref, xp_ref, xn_ref, sh_ref, sc_ref, g_ref, w_ref, wvt_ref, bgr_ref, bgc_ref,
         wc_ref, cos_ref, sl_ref, shi_ref, gain_ref,
         q_ref, k_ref, vt_ref, qk_ref, mvt_ref, o_ref, gcol_ref, grow_ref) = refs
    else:
        (x_ref, xp_ref, xn_ref, sh_ref, sc_ref, g_ref, w_ref, wvt_ref, bgr_ref, bgc_ref,
         wc_ref,
         k_ref, vt_ref, qk_ref, mvt_ref, gcol_ref, grow_ref) = refs
    i = pl.program_id(0)
    n_groups = tm // gm
    r_mod = (pl.program_id(1) if mod_row is None else mod_row) % SUBLANES
    scale = g_ref[...] * (1.0 + sc_ref[pl.ds(r_mod, 1), :])
    shift = sh_ref[pl.ds(r_mod, 1), :]
    wc = wc_ref[...]
    row = lax.broadcasted_iota(jnp.int32, (gm, 1), 0)

    def project(lo, width, lhs):
        return _dot(lhs, w_ref[:, lo:lo + width])

    def group(r):
        rows = slice(r * gm, (r + 1) * gm)
        before = xp_ref[0] if r == 0 else x_ref[0, r * gm - SUBLANES:r * gm, :]
        after = xn_ref[0] if r == n_groups - 1 else x_ref[0, (r + 1) * gm:(r + 1) * gm + SUBLANES, :]
        keep_prev = (jnp.where(i == 0, 0.0, 1.0) if r == 0 else 1.0) if latent else 0.0
        keep_next = (jnp.where(i == n_tiles - 1, 0.0, 1.0) if r == n_groups - 1 else 1.0) if latent else 0.0
        xt = jnp.concatenate([x_ref[0, rows, :], before, after], axis=0)
        ms = jnp.mean(xt * xt, axis=-1, keepdims=True)
        hb = (xt * lax.rsqrt(ms + EPS) * scale + shift).astype(BF16)
        hm = hb[:gm]
        yield

        c = ATT_WIDTH
        if latent:
            r_q = project(0, ATT_WIDTH, hm)
        r_kg = project(c, ATT_KV_WIDTH + LANES, hm)
        c += ATT_KV_WIDTH + LANES
        yield

        if latent:
            cos, sl, shi = cos_ref[rows, :], sl_ref[rows, :], shi_ref[rows, :]
            for g in range(ATT_WIDTH // LANES):
                q_ref[0, rows, g * LANES:(g + 1) * LANES] = _rope(
                    r_q[:, g * LANES:(g + 1) * LANES], cos, sl, shi).astype(BF16)
        y = project(c, 2 * ML_QK_WIDTH, hb)
        c += 2 * ML_QK_WIDTH
        yield

        if latent:
            k_ref[0, rows, :] = _rope(r_kg[:, :LANES], cos, sl, shi).astype(BF16)
        else:
            k_ref[0, rows, :] = r_kg[:, :LANES].astype(BF16)
        gcol_ref[0, rows, :] = r_kg[:, LANES:] + bgr_ref[...]
        v_t = _dot_nt(wvt_ref[...], hm)
        yield

        ym = y[:gm]
        prev = jnp.where(row == 0, y[gm + SUBLANES - 1:gm + SUBLANES] * keep_prev,
                         pltpu.roll(ym, 1, axis=0))
        nxt = jnp.where(row == gm - 1, y[gm + SUBLANES:gm + SUBLANES + 1] * keep_next,
                        pltpu.roll(ym, gm - 1, axis=0))
        act = _silu(prev * wc[0:1] + ym * wc[1:2] + nxt * wc[2:3])
        qk_ref[0, rows, :ML_QK_WIDTH] = (act[:, :ML_QK_WIDTH] * (ML_QK_DIM ** -0.5)).astype(BF16)
        qk_ref[0, rows, ML_QK_WIDTH:] = act[:, ML_QK_WIDTH:].astype(BF16)
        if latent:
            r_o = project(c, ML_WIDTH, hm)
        yield

        g_t = v_t[ATT_KV_WIDTH + ML_WIDTH:] + bgc_ref[...]
        vt_ref[0, :, rows] = v_t[:ATT_KV_WIDTH].astype(BF16)
        for j in range(gm // ML_CHUNK):
            cols = slice(j * ML_CHUNK, (j + 1) * ML_CHUNK)
            mvt_ref[0, r * (gm // ML_CHUNK) + j] = v_t[ATT_KV_WIDTH:ATT_KV_WIDTH + ML_WIDTH, cols].astype(BF16)
            grow_ref[0, r * (gm // ML_CHUNK) + j] = g_t[:, cols]
        if latent:
            o_ref[0, rows, :] = (jax.nn.sigmoid(r_o) * gain_ref[...]).astype(BF16)

    done = object()
    waiting = [group(r) for r in range(n_groups)]
    active = []
    while waiting or active:
        if waiting:
            active.append(waiting.pop(0))
        active = [g for g in active if next(g, done) is not done]


def _inproj(x, mod, mod_row, g_pre, w_main, wvt, bg_row, bg_col, wc, rope_tabs, gain, *, tm, gm, latent):
    bsz, t, d = x.shape
    n_tiles = t // tm
    hb = tm // SUBLANES
    n_hblk = t // SUBLANES
    n = w_main.shape[1]
    mod_blk = (lambda b: b // SUBLANES) if mod_row is None else (lambda b: mod_row // SUBLANES)

    def const(shape):
        return pl.BlockSpec(shape, lambda i, b: (0,) * len(shape))

    in_specs = [
        pl.BlockSpec((1, tm, d), lambda i, b: (b, i, 0)),
        pl.BlockSpec((1, SUBLANES, d), lambda i, b: (b, jnp.maximum(i * hb - 1, 0), 0)),
        pl.BlockSpec((1, SUBLANES, d), lambda i, b: (b, jnp.minimum((i + 1) * hb, n_hblk - 1), 0)),
        pl.BlockSpec((SUBLANES, d), lambda i, b: (mod_blk(b), 0)),
        pl.BlockSpec((SUBLANES, d), lambda i, b: (mod_blk(b), 1)),
        const((1, d)), const((d, n)), const((ATT_KV_WIDTH + ML_WIDTH + ML_GATES, d)),
        const((1, LANES)), const((ML_GATES, 1)), const((3, 2 * ML_QK_WIDTH)),
    ]
    args = [x, x, x, mod, mod, g_pre, w_main, wvt, bg_row, bg_col, wc]
    tok = lambda width, dt: (pl.BlockSpec((1, tm, width), lambda i, b: (b, i, 0)),
                             jax.ShapeDtypeStruct((bsz, t, width), dt))
    chunked = lambda rows, dt: (pl.BlockSpec((1, tm // ML_CHUNK, rows, ML_CHUNK), lambda i, b: (b, i, 0, 0)),
                                jax.ShapeDtypeStruct((bsz, t // ML_CHUNK, rows, ML_CHUNK), dt))
    outs = []
    if latent:
        in_specs += [pl.BlockSpec((tm, LANES), lambda i, b: (i, 0))] * 3 + [const((1, ML_WIDTH))]
        args += list(rope_tabs) + [gain]
        outs.append(tok(ATT_WIDTH, BF16))
    outs.append(tok(ATT_KV_WIDTH, BF16))
    outs.append((pl.BlockSpec((1, LANES, tm), lambda i, b: (b, 0, i)),
                 jax.ShapeDtypeStruct((bsz, ATT_KV_WIDTH, t), BF16)))
    outs.append(tok(2 * ML_QK_WIDTH, BF16))
    outs.append(chunked(ML_WIDTH, BF16))
    if latent:
        outs.append(tok(ML_WIDTH, BF16))
    outs.append(tok(LANES, F32))
    outs.append(chunked(ML_GATES, F32))
    return pl.pallas_call(
        functools.partial(_inproj_kernel, tm=tm, gm=gm, n_tiles=n_tiles, latent=latent, mod_row=mod_row),
        grid=(n_tiles, bsz),
        in_specs=in_specs,
        out_specs=[o[0] for o in outs],
        out_shape=[o[1] for o in outs],
        compiler_params=pltpu.CompilerParams(dimension_semantics=("arbitrary", "arbitrary"),
                                             vmem_limit_bytes=VMEM_LIMIT),
        name="inproj_latent" if latent else "inproj_context",
    )(*args)


def _attn_kernel(q_ref, k_ref, kx_ref, vt_ref, vx_ref, sink_ref, o_ref):
    blk = ATT_BLOCK
    lane = lax.broadcasted_iota(jnp.int32, (blk, LANES), 1)
    zero = jnp.zeros((blk, LANES), BF16)
    half_groups = ATT_UNIT_GROUPS
    n_slots = ATT_KV_HEADS * half_groups
    n_parts = ATT_GROUP // half_groups
    sink = [sink_ref[:, part * n_slots * blk:(part + 1) * n_slots * blk] for part in range(n_parts)]

    def stack_heads(q, half):
        parts = []
        for g in range(half * half_groups, (half + 1) * half_groups):
            slab = q[:, g * LANES:(g + 1) * LANES]
            parts.append(jnp.where(lane < ATT_HEAD_DIM, slab, zero))
            parts.append(jnp.where(lane >= ATT_HEAD_DIM, slab, zero))
        return jnp.concatenate(parts, axis=0)

    key = lax.broadcasted_iota(jnp.int32, (blk, blk), 0)
    qry = lax.broadcasted_iota(jnp.int32, (blk, blk), 1)
    ninf = jnp.full((blk, blk), -jnp.inf, F32)
    bias_prev = jnp.where(key >= qry, 0.0, ninf)
    bias_next = jnp.where(key <= qry, 0.0, ninf)
    slots = lambda b: jnp.concatenate([b] * n_slots, axis=1)

    def scores(qs, k_prev, k_cur, k_next):
        return [_dot_nt(k_prev, qs), _dot_nt(k_cur, qs), _dot_nt(k_next, qs), _dot_nt(kx_ref[0], qs)]

    def softmax(s, b_prev, b_next, sink_h):
        s = jnp.concatenate([s[0] + slots(b_prev), s[1], s[2] + slots(b_next), s[3]], axis=0)
        m = jnp.maximum(sink_h, jnp.max(s, axis=0, keepdims=True))
        return jnp.exp2(s - m).astype(BF16), jnp.exp2(sink_h - m)

    n_keys = 3 * blk + kx_ref.shape[1]
    ones_rows = jnp.ones((ATT_ONES_ROWS, n_keys), BF16)

    def weighted_values(p, v_prev, v_cur, v_next):
        vt = jnp.concatenate([v_prev, v_cur, v_next, vx_ref[0]], axis=1)
        return _dot(jnp.concatenate([vt, ones_rows], axis=0), p)

    dim = lax.broadcasted_iota(jnp.int32, (LANES, blk), 0)

    def emit(rows, half, ot, p_sink):
        ot = ot[:LANES] * (1.0 / (ot[LANES:LANES + 1] + p_sink))
        for j in range(half_groups):
            g = half * half_groups + j
            a = ot[:, (2 * j) * blk:(2 * j + 1) * blk]
            b = ot[:, (2 * j + 1) * blk:(2 * j + 2) * blk]
            o_ref[0, rows, g * LANES:(g + 1) * LANES] = jnp.where(dim < ATT_HEAD_DIM, a, b).T.astype(BF16)

    nblk = q_ref.shape[1] // blk
    rows_of = [slice(b * blk, (b + 1) * blk) for b in range(nblk)]
    k_blocks = [k_ref[0, r, :] for r in rows_of]
    v_blocks = [vt_ref[0, :, r] for r in rows_of]
    near = lambda blocks, b: (blocks[max(b - 1, 0)], blocks[b], blocks[min(b + 1, nblk - 1)])
    bias = [(ninf if b == 0 else bias_prev, ninf if b == nblk - 1 else bias_next) for b in range(nblk)]
    units = [(b, half) for b in range(nblk) for half in range(n_parts)]

    def stage_scores(u):
        b, half = u
        return scores(stack_heads(q_ref[0, rows_of[b], :], half), *near(k_blocks, b))

    def stage_softmax(u, s):
        b, half = u
        return softmax(s, *bias[b], sink[half])

    def stage_values(u, p):
        return weighted_values(p, *near(v_blocks, u[0]))

    def stage_emit(u, ot, p_sink):
        emit(rows_of[u[0]], u[1], ot, p_sink)

    n_units = len(units)
    s, p, ot = {}, {}, {}
    s[0] = stage_scores(units[0])
    for t in range(n_units):
        if t + 1 < n_units:
            s[t + 1] = stage_scores(units[t + 1])
        p[t] = stage_softmax(units[t], s.pop(t))
        if t >= 1:
            ot[t - 1] = stage_values(units[t - 1], p[t - 1][0])
        if t >= 2:
            stage_emit(units[t - 2], ot.pop(t - 2), p.pop(t - 2)[1])
    ot[n_units - 1] = stage_values(units[n_units - 1], p[n_units - 1][0])
    for t in (n_units - 2, n_units - 1):
        stage_emit(units[t], ot.pop(t), p.pop(t)[1])


def _attention(q, k, vt, kx, vxt, sink_row):
    bsz, s, _ = q.shape
    l = kx.shape[1]
    per_b = lambda a: pl.BlockSpec((1,) + a.shape[1:], lambda b: (b, 0, 0))
    vx_spec = pl.BlockSpec((1, ATT_KV_WIDTH, l), lambda b: (b // CTX_PER_STEP, 0, b % CTX_PER_STEP))
    return pl.pallas_call(
        _attn_kernel,
        grid=(bsz,),
        in_specs=[per_b(q), per_b(k), per_b(kx), per_b(vt), vx_spec,
                  pl.BlockSpec(sink_row.shape, lambda b: (0, 0))],
        out_specs=pl.BlockSpec((1, s, ATT_WIDTH), lambda b: (b, 0, 0)),
        out_shape=jax.ShapeDtypeStruct((bsz, s, ATT_WIDTH), BF16),
        compiler_params=pltpu.CompilerParams(dimension_semantics=("arbitrary",),
                                             vmem_limit_bytes=VMEM_LIMIT),
        name="attention",
    )(q, k, kx, vt, vxt, sink_row)


BF16_SUBLANES = 16
ML_STATE_ROWS = ML_V_DIM + BF16_SUBLANES


def _mlstm_kernel(qkx_ref, vtx_ref, ox_ref, gcx_ref, grx_ref, qkc_ref, vtc_ref, grc_ref,
                  out_ref, sin_ref, st_ref, ucol_ref, mrun_ref, wint_ref, floor_ref,
                  wkey_ref, decay_ref, *, ncx, ncc):
    lc = ML_CHUNK
    qkw = ML_QK_WIDTH
    sr = ML_STATE_ROWS
    ng = ML_GATES
    hg = ML_GATES // 2
    st_ref[...] = jnp.zeros(st_ref.shape, F32)

    rr = lax.broadcasted_iota(jnp.int32, (lc, lc), 0)
    cc = lax.broadcasted_iota(jnp.int32, (lc, lc), 1)
    tril = rr >= cc
    triu = rr <= cc
    tril_b = jnp.where(tril, 1.0, 0.0).astype(BF16)
    triu_b = jnp.where(triu, 1.0, 0.0).astype(BF16)
    head_of_lane = lax.broadcasted_iota(jnp.int32, (lc, qkw), 1) // ML_QK_DIM
    head_of_state_lane = lax.broadcasted_iota(jnp.int32, (1, qkw), 1) // ML_QK_DIM

    def split(v):
        hi = v.astype(BF16)
        return hi, (v - hi.astype(F32)).astype(BF16)

    def gate_rows(g):
        n16 = g.shape[0]
        fwd_row = (lax.broadcasted_iota(jnp.int32, g.shape, 0) & (ng - 1)) < hg
        lane = lax.broadcasted_iota(jnp.int32, g.shape, 1)
        hi, lo = split(_log_sigmoid(g))
        cat = jnp.concatenate([hi, lo], axis=0)
        bu = _dot(cat, triu_b)
        bl = _dot(cat, tril_b)
        b = jnp.where(fwd_row, bu[:n16] + bu[n16:], bl[:n16] + bl[n16:])
        b = pltpu.roll(b, n16 - ML_HEADS, axis=0)
        u = g - b
        run_f = run_b = u
        k = 1
        while k < lc:
            run_f = jnp.maximum(run_f, jnp.where(lane >= k, pltpu.roll(run_f, k, axis=1), -jnp.inf))
            run_b = jnp.maximum(run_b, jnp.where(lane < lc - k, pltpu.roll(run_b, lc - k, axis=1), -jnp.inf))
            k *= 2
        run = jnp.where(fwd_row, run_f, run_b)

        def at_end(a):
            return jnp.where(fwd_row, jnp.broadcast_to(a[:, lc - 1:lc], a.shape),
                             jnp.broadcast_to(a[:, 0:1], a.shape))

        return u, b, run, at_end(b), at_end(run)

    def derived(u, b, run, b_tot, run_end, m_in):
        m_run = jnp.maximum(run, m_in)
        m_out = b_tot + jnp.maximum(run_end, m_in)
        return (m_run * LOG2_E, jnp.exp(m_in - m_run), jnp.exp(-(b + m_run)),
                jnp.exp(b_tot + u - m_out), jnp.exp(b_tot + m_in - m_out))

    gc_rows = gate_rows(grc_ref[0].reshape(ncc * ng, lc))
    gx_rows = gate_rows(grx_ref[0].reshape(ncx * ng, lc))

    def scan_m(rows, n, m_f, m_b):
        _, _, _, b_tot, run_end = rows
        part = lambda a, c, d: a[c * ng + d * hg:c * ng + (d + 1) * hg]
        ins_f, ins_b = [], [None] * n
        for c in range(n):
            ins_f.append(m_f)
            m_f = part(b_tot, c, 0) + jnp.maximum(part(run_end, c, 0), m_f)
        for c in reversed(range(n)):
            ins_b[c] = m_b
            m_b = part(b_tot, c, 1) + jnp.maximum(part(run_end, c, 1), m_b)
        return jnp.concatenate([x for c in range(n) for x in (ins_f[c], ins_b[c])], axis=0), m_f, m_b

    m0 = jnp.zeros((hg, lc), F32)
    m_in_c, m_f, m_b = scan_m(gc_rows, ncc, m0, m0)
    m_in_x, _, _ = scan_m(gx_rows, ncx, m_f, m_b)
    _, _, _, wkey_c, decay_c = derived(*gc_rows, m_in_c)
    for ref, val in zip((mrun_ref, wint_ref, floor_ref, wkey_ref, decay_ref), derived(*gx_rows, m_in_x)):
        ref[...] = val

    fwd_cat = (lax.broadcasted_iota(jnp.int32, (lc, 2 * LANES), 1) & (ng - 1)) < hg

    def token_major_u(j):
        rows = pl.ds(pl.multiple_of(j * lc, lc), lc)
        gcol = gcx_ref[0, rows, :]
        hi, lo = split(_log_sigmoid(gcol))
        cat = jnp.concatenate([hi, lo], axis=1)
        zero = jnp.zeros_like(cat)
        bb = _dot(jnp.concatenate([tril_b, triu_b], axis=1),
                  jnp.concatenate([jnp.where(fwd_cat, cat, zero), jnp.where(fwd_cat, zero, cat)], axis=0))
        b = bb[:, :LANES] + bb[:, LANES:]
        ucol_ref[rows, :] = (gcol - pltpu.roll(b, LANES - ML_HEADS, axis=1)) * LOG2_E

    def advance(dirn, k4, vt, w_key, decay):
        st = st_ref[dirn]
        lhs, rhs, decay_row = [], [], None
        for h in range(ML_HEADS):
            c = hg * dirn + h
            wk = w_key[c:c + 1].astype(BF16)
            lhs.append(jnp.concatenate([vt[h * ML_V_DIM:(h + 1) * ML_V_DIM] * wk,
                                        jnp.broadcast_to(wk, (BF16_SUBLANES, lc))], axis=0))
            rhs.append(jnp.where(head_of_lane == h, k4, jnp.zeros_like(k4)))
            d_h = jnp.concatenate([decay[c:c + 1]] * (qkw // lc), axis=1)
            decay_row = d_h if decay_row is None else jnp.where(head_of_state_lane == h, d_h, decay_row)
        upd = _dot(jnp.concatenate(lhs, axis=1), jnp.concatenate(rhs, axis=0))
        st_ref[dirn] = decay_row * st + upd

    for j in range(ncc):
        for dirn, cj in ((0, j), (1, ncc - 1 - j)):
            grows = slice(cj * ng, (cj + 1) * ng)
            advance(dirn, qkc_ref[0, cj * lc:(cj + 1) * lc, qkw:], vtc_ref[0, cj], wkey_c[grows], decay_c[grows])

    def scan_body(j, carry):
        for dirn, cj in ((0, j), (1, ncx - 1 - j)):
            rows = pl.ds(pl.multiple_of(cj * lc, lc), lc)
            grows = pl.ds(pl.multiple_of(cj * ng, ng), ng)
            sin_ref[dirn, cj] = st_ref[dirn].astype(BF16)
            advance(dirn, qkx_ref[0, rows, qkw:], vtx_ref[0, cj], wkey_ref[grows, :], decay_ref[grows, :])
        token_major_u(j)
        return carry

    lax.fori_loop(0, ncx, scan_body, 0, unroll=16)


    def out_body(j, carry):
        rows = pl.ds(pl.multiple_of(j * lc, lc), lc)
        grows = pl.ds(pl.multiple_of(j * ng, ng), ng)
        q4 = qkx_ref[0, rows, :qkw]
        k4 = qkx_ref[0, rows, qkw:]
        vt = vtx_ref[0, j]
        u_col = ucol_ref[rows, :]
        m_run, w_int, floor = mrun_ref[grows, :], wint_ref[grows, :], floor_ref[grows, :]
        zero = jnp.zeros_like(q4)
        qs = jnp.concatenate([jnp.where(head_of_lane == h, q4, zero) for h in range(ML_HEADS)], axis=0)
        both = _dot_nt(jnp.concatenate([k4, sin_ref[0, j], sin_ref[1, j]], axis=0), qs)
        qk_t = both[:lc]
        for h in range(ML_HEADS):
            cols_h = slice(h * lc, (h + 1) * lc)
            s_dirs = []
            for dirn in range(2):
                c = hg * dirn + h
                valid = triu if dirn == 0 else tril
                e = jnp.exp2(jnp.where(valid, u_col[:, c:c + 1] - m_run[c:c + 1], -jnp.inf))
                s_dirs.append(qk_t[:, cols_h] * e)
            nums = _dot(vt[h * ML_V_DIM:(h + 1) * ML_V_DIM],
                        jnp.concatenate([s.astype(BF16) for s in s_dirs], axis=1))
            hs = None
            for dirn in range(2):
                c = hg * dirn + h
                inter = both[lc + dirn * sr:lc + (dirn + 1) * sr, cols_h]
                num = nums[:, dirn * lc:(dirn + 1) * lc] + w_int[c:c + 1] * inter[:ML_V_DIM]
                den = (jnp.sum(s_dirs[dirn], axis=0, keepdims=True)
                       + w_int[c:c + 1] * inter[ML_V_DIM:ML_V_DIM + 1])
                hv = num * (1.0 / jnp.maximum(jnp.abs(den), floor[c:c + 1]))
                hs = hv if hs is None else hs + hv
            cols = slice(h * ML_V_DIM, (h + 1) * ML_V_DIM)
            hn = hs * lax.rsqrt(jnp.mean(hs * hs, axis=0, keepdims=True) + EPS)
            out_ref[0, rows, cols] = (hn.T * ox_ref[0, rows, cols].astype(F32)).astype(BF16)
        return carry

    lax.fori_loop(0, ncx, out_body, 0, unroll=16)


def _mlstm(qkx, vtx, ox, gcx, grx, qkc, vtc, grc):
    bsz, s, _ = qkx.shape
    ncx, ncc = s // ML_CHUNK, qkc.shape[1] // ML_CHUNK
    per_b = lambda a: pl.BlockSpec((1,) + a.shape[1:], lambda b: (b,) + (0,) * (a.ndim - 1))
    ins = [qkx, vtx, ox, gcx, grx, qkc, vtc, grc]
    return pl.pallas_call(
        functools.partial(_mlstm_kernel, ncx=ncx, ncc=ncc),
        grid=(bsz,),
        in_specs=[per_b(a) for a in ins],
        out_specs=pl.BlockSpec((1, s, ML_WIDTH), lambda b: (b, 0, 0)),
        out_shape=jax.ShapeDtypeStruct((bsz, s, ML_WIDTH), BF16),
        scratch_shapes=[pltpu.VMEM((2, ncx, ML_STATE_ROWS, ML_QK_WIDTH), BF16),
                        pltpu.VMEM((2, ML_STATE_ROWS, ML_QK_WIDTH), F32),
                        pltpu.VMEM((s, LANES), F32)]
                       + [pltpu.VMEM((ncx * ML_GATES, ML_CHUNK), F32)] * 5,
        compiler_params=pltpu.CompilerParams(dimension_semantics=("arbitrary",),
                                             vmem_limit_bytes=VMEM_LIMIT),
        name="mlstm",
    )(*ins)


def _out_ffn_kernel(x_ref, att_ref, ml_ref, gtm_ref, shf_ref, scf_ref, gtf_ref,
                    gpm_ref, gpf_ref, gqf_ref, woa_ref, wom_ref, wfi_ref, wfo_ref, o_ref, *, hidden,
                    tiles_per_batch):
    r_mod = pl.ds((pl.program_id(0) // tiles_per_batch) % SUBLANES, 1)
    gtm, shf, scf, gtf = gtm_ref[r_mod, :], shf_ref[r_mod, :], scf_ref[r_mod, :], gtf_ref[r_mod, :]

    def group(g):
        r = slice(g * FFN_ROWS, (g + 1) * FFN_ROWS)
        mix = _dot(att_ref[r, :], woa_ref[...]) + _dot(ml_ref[r, :], wom_ref[...])
        yield
        x1 = x_ref[r, :] + gtm * _rms(mix, gpm_ref[...])
        h = (_rms(x1, gpf_ref[...]) * (1.0 + scf) + shf).astype(BF16)
        gu = _dot(h, wfi_ref[...])
        yield
        act = (_silu(gu[:, :hidden]) * gu[:, hidden:]).astype(BF16)
        fx = _dot(act, wfo_ref[...])
        yield
        o_ref[r, :] = x1 + gtf * _rms(fx, gqf_ref[...])

    done = object()
    waiting = [group(g) for g in range(x_ref.shape[0] // FFN_ROWS)]
    active, rounds = [], 0
    while waiting or active:
        if rounds % FFN_PAIR_PERIOD == 0:
            active += [waiting.pop(0) for _ in range(min(2, len(waiting)))]
        active = [g for g in active if next(g, done) is not done]
        rounds += 1


def _out_ffn(x2, att2, ml2, mod, g_post_mix, g_pre_ffn, g_post_ffn, woa, wom, wfi, wfo, *, tiles_per_batch):
    t, d = x2.shape
    tm = FFN_TM
    hidden = wfo.shape[0]
    resident = lambda a: pl.BlockSpec(a.shape, lambda i: (0,) * a.ndim, pipeline_mode=pl.Buffered(1))
    mod_spec = lambda k: pl.BlockSpec((SUBLANES, d), lambda i: (i // tiles_per_batch // SUBLANES, k))
    row = pl.BlockSpec((1, d), lambda i: (0, 0))
    return pl.pallas_call(
        functools.partial(_out_ffn_kernel, hidden=hidden, tiles_per_batch=tiles_per_batch),
        grid=(t // tm,),
        in_specs=[pl.BlockSpec((tm, d), lambda i: (i, 0)),
                  pl.BlockSpec((tm, ATT_WIDTH), lambda i: (i, 0)),
                  pl.BlockSpec((tm, ML_WIDTH), lambda i: (i, 0)),
                  mod_spec(2), mod_spec(3), mod_spec(4), mod_spec(5), row, row, row,
                  resident(woa), resident(wom), resident(wfi), resident(wfo)],
        out_specs=pl.BlockSpec((tm, d), lambda i: (i, 0)),
        out_shape=jax.ShapeDtypeStruct((t, d), F32),
        compiler_params=pltpu.CompilerParams(dimension_semantics=("arbitrary",),
                                             vmem_limit_bytes=VMEM_LIMIT),
        name="out_ffn",
    )(x2, att2, ml2, mod, mod, mod, mod, g_post_mix, g_pre_ffn, g_post_ffn, woa, wom, wfi, wfo)


def _rope_tables(n_tokens):
    pos = jnp.arange(n_tokens)
    row = (pos // GRID_W).astype(F32)
    col = (pos % GRID_W).astype(F32)
    inv_freq = jnp.power(ROPE_BASE, -jnp.arange(ROPE_PAIR, dtype=F32) / ROPE_PAIR)
    ang_r = row[:, None] * inv_freq
    ang_c = col[:, None] * inv_freq
    z = jnp.zeros_like(ang_r)
    reps = LANES // ATT_HEAD_DIM
    cos = jnp.tile(jnp.concatenate([jnp.cos(ang_r)] * 2 + [jnp.cos(ang_c)] * 2, axis=1), (1, reps))
    sin_lo = jnp.tile(jnp.concatenate([-jnp.sin(ang_r), z, -jnp.sin(ang_c), z], axis=1), (1, reps))
    sin_hi = jnp.tile(jnp.concatenate([z, jnp.sin(ang_r), z, jnp.sin(ang_c)], axis=1), (1, reps))
    return cos, sin_lo, sin_hi


def _permute_heads(w, axis):
    shape = w.shape
    grouped = shape[:axis] + (ATT_KV_HEADS, ATT_GROUP, ATT_HEAD_DIM) + shape[axis + 1:]
    return jnp.swapaxes(w.reshape(grouped), axis, axis + 1).reshape(shape)


def kernel(x, c, ctx, c_ctx, w_ada, b_ada, g_pre_mix, w_in, w_conv_qk, b_gates, attn_sink,
           g_mlstm_out, w_out, g_post_mix, g_pre_ffn, w_ffn_in, w_ffn_out, g_post_ffn):
    bsz, s, d = x.shape
    l = ctx.shape[1]
    assert w_ada.shape[0] == 1, "single-layer block"
    assert bsz < MOD_ROWS and s % INPROJ_TM == 0 and l % ML_CHUNK == 0 and s % FFN_TM == 0
    assert bsz % CTX_PER_STEP == 0 and l % LANES == 0

    mod = _ada(c, c_ctx, w_ada[0], b_ada)

    w = w_in[0]
    o_q, o_k, o_v = 0, ATT_WIDTH, ATT_WIDTH + ATT_KV_WIDTH
    o_mq = o_v + ATT_KV_WIDTH
    o_mv = o_mq + 2 * ML_QK_WIDTH
    o_mo = o_mv + ML_WIDTH
    o_mg = o_mo + ML_WIDTH
    w_q = _permute_heads(w[:, o_q:o_k], 1) * (ATT_HEAD_DIM ** -0.5 * LOG2_E)
    w_g = jnp.pad(w[:, o_mg:], ((0, 0), (0, LANES - ML_GATES)))
    shared = [w[:, o_k:o_v], w_g, w[:, o_mq:o_mv]]
    w_lat = jnp.concatenate([w_q] + shared + [w[:, o_mo:o_mg]], axis=1).astype(BF16)
    wvt = jnp.concatenate([w[:, o_v:o_mq], w[:, o_mv:o_mo], w[:, o_mg:]], axis=1).T.astype(BF16)
    bg_row = jnp.pad(b_gates, ((0, 0), (0, LANES - ML_GATES)))
    bg_col = b_gates.reshape(ML_GATES, 1)
    wc = w_conv_qk[0]

    q, k, vt, qkx, vtx, ox, gcx, grx = _inproj(
        x, mod, None, g_pre_mix, w_lat, wvt, bg_row, bg_col, wc, _rope_tables(s), g_mlstm_out,
        tm=INPROJ_TM, gm=INPROJ_ROWS, latent=True)
    nctx = bsz // CTX_PER_STEP
    kc, vct, qkc, vtc, _, grc = _inproj(
        ctx.reshape(nctx, CTX_PER_STEP * l, d), mod, bsz, g_pre_mix, w_lat, wvt, bg_row, bg_col, wc, None, None,
        tm=CTX_PER_STEP * l, gm=l, latent=False)
    kc = kc.reshape(bsz, l, ATT_KV_WIDTH)
    qkc = qkc.reshape(bsz, l, 2 * ML_QK_WIDTH)
    vtc = vtc.reshape((bsz, l // ML_CHUNK) + vtc.shape[2:])
    grc = grc.reshape((bsz, l // ML_CHUNK) + grc.shape[2:])

    sink_row = jnp.repeat(attn_sink[0][jnp.array(_HEAD_PERM)] * LOG2_E, ATT_BLOCK)[None, :]
    att = _attention(q, k, vt, kc, vct, sink_row)
    ml = _mlstm(qkx, vtx, ox, gcx, grx, qkc, vtc, grc)

    wo = w_out[0]
    woa = _permute_heads(wo[:ATT_WIDTH], 0).astype(BF16)
    wom = wo[ATT_WIDTH:].astype(BF16)
    out = _out_ffn(x.reshape(bsz * s, d), att.reshape(bsz * s, ATT_WIDTH), ml.reshape(bsz * s, ML_WIDTH),
                   mod, g_post_mix, g_pre_ffn, g_post_ffn, woa, wom,
                   w_ffn_in[0].astype(BF16), w_ffn_out[0].astype(BF16), tiles_per_batch=s // FFN_TM)
    return out.reshape(bsz, s, d)
```
